```python
import math
import jax, jax.numpy as jnp
from jax import lax
import numpy as np

D_MODEL = 1024
BATCH = 8
SEQ = 2048
DEPTH = 4
DEC_BATCH = 128
DEC_SEQ = 4
PAST_LEN = 8192
PAGE_SIZE = 128

HEAD_DIM = 64
MIX_WIDTH = D_MODEL
ATTN_SCALE = HEAD_DIM ** -0.5
H_A = MIX_WIDTH // 4 // HEAD_DIM
HKV_A = H_A // 2
WIN_A = 128
H_C = MIX_WIDTH // 4 // HEAD_DIM
HKV_C = H_C
C_BRANCHES = ((128, 1), (512, 4), (2048, 16))
WIN_C_MAX = 2048
D_INNER_B = MIX_WIDTH // 2
SSM_HEAD_DIM = 64
H_B = D_INNER_B // SSM_HEAD_DIM
SSM_GROUPS = 2
D_STATE = 128
CONV_K = 4
CONV_DIM = D_INNER_B + 2 * SSM_GROUPS * D_STATE
SSD_CHUNK = 128
D_FF = 4 * D_MODEL
NUM_BUCKETS = 32
REL_MAX_DIST = 2048
BLOCK = 128
EPS = 1e-6
IN_SPLITS = (H_A * HEAD_DIM, HKV_A * HEAD_DIM, HKV_A * HEAD_DIM,
             H_C * HEAD_DIM, HKV_C * HEAD_DIM, HKV_C * HEAD_DIM,
             D_INNER_B, CONV_DIM, H_B)
IN_COLS = sum(IN_SPLITS)

kernel_name = "hymba_swa_ssd_dilated_decoder_step"


def rms_normalize(x):
    xf = x.astype(jnp.float32)
    return xf * lax.rsqrt(jnp.mean(xf * xf, axis=-1, keepdims=True) + EPS)


def rmsnorm(x, g):
    return (rms_normalize(x) * g.astype(jnp.float32)).astype(x.dtype)


def t5_bucket(dist):
    max_exact = NUM_BUCKETS // 2
    d = jnp.maximum(dist, 0)
    df = jnp.maximum(d, 1).astype(jnp.float32)
    large = max_exact + (jnp.log(df / max_exact) / math.log(REL_MAX_DIST / max_exact)
                         * (NUM_BUCKETS - max_exact)).astype(jnp.int32)
    return jnp.where(d < max_exact, d, jnp.minimum(large, NUM_BUCKETS - 1))


def rel_bias(dist, table):
    return jnp.moveaxis(table[t5_bucket(dist)].astype(jnp.float32), -1, 0)


def softmax_with_sink(logits, valid, sink):
    masked = jnp.where(valid, logits, -jnp.inf)
    m = jnp.max(masked, axis=-1, keepdims=True)
    if sink is not None:
        m = jnp.maximum(m, sink)
    p = jnp.exp(masked - m)
    den = jnp.sum(p, axis=-1, keepdims=True)
    if sink is not None:
        den = den + jnp.exp(sink - m)
    return p / den, (m + jnp.log(den))[..., 0]


def banded_attention(q, k, v, window, dil, table, sink):
    assert window <= BLOCK
    n, L, H, dh = q.shape
    hkv = k.shape[2]
    g = H // hkv
    nb = -(-L // BLOCK)
    lp = nb * BLOCK
    padw = ((0, 0), (0, lp - L), (0, 0), (0, 0))
    qb = jnp.pad(q, padw).astype(jnp.float32).reshape(n, nb, BLOCK, hkv, g, dh)

    def with_prev(x):
        xb = jnp.pad(x, padw).astype(jnp.float32).reshape(n, nb, BLOCK, hkv, dh)
        prev = jnp.pad(xb, ((0, 0), (1, 0), (0, 0), (0, 0), (0, 0)))[:, :nb]
        return jnp.concatenate([prev, xb], axis=2)

    kb = with_prev(k)
    vb = with_prev(v)
    qi = jnp.arange(BLOCK)[:, None]
    kj = jnp.arange(2 * BLOCK)[None, :]
    dist = BLOCK + qi - kj
    kpos = jnp.arange(nb)[:, None, None] * BLOCK - BLOCK + kj
    valid = (dist >= 0) & (dist <= window) & (kpos >= 0)
    bias = rel_bias(dist * dil, table).reshape(hkv, g, BLOCK, 2 * BLOCK)
    logits = jnp.einsum('nbqhgd,nbshd->nbhgqs', qb, kb) + bias
    sk = None if sink is None else sink.astype(jnp.float32).reshape(hkv, g, 1, 1)
    probs, lse = softmax_with_sink(logits, valid[None, :, None, None], sk)
    out = jnp.einsum('nbhgqs,nbshd->nbqhgd', probs, vb).reshape(n, lp, H, dh)[:, :L]
    lse = lse.transpose(0, 1, 4, 2, 3).reshape(n, lp, H)[:, :L]
    return out, lse


def dilated_banded_attention(q, k, v, window, dil, table):
    n, L = q.shape[:2]

    def to_res(x):
        return x.reshape(n, L // dil, dil, *x.shape[2:]).swapaxes(1, 2).reshape(n * dil, L // dil, *x.shape[2:])

    def from_res(x):
        return x.reshape(n, dil, L // dil, *x.shape[2:]).swapaxes(1, 2).reshape(n, L, *x.shape[2:])

    out, lse = banded_attention(to_res(q), to_res(k), to_res(v), window // dil, dil, table, None)
    return from_res(out), from_res(lse)


def gathered_window_attention(q, kbuf, vbuf, knew, vnew, window, dil, table, sink):
    n, T, H, dh = q.shape
    hkv = knew.shape[2]
    g = H // hkv
    kc = jnp.concatenate([kbuf, knew.astype(kbuf.dtype)], axis=1)
    vc = jnp.concatenate([vbuf, vnew.astype(vbuf.dtype)], axis=1)
    lbuf = kbuf.shape[1]
    steps = jnp.arange(window // dil + 1)
    idx = lbuf + jnp.arange(T)[:, None] - steps[None, :] * dil
    valid = idx >= 0
    idx = jnp.maximum(idx, 0)
    kg = kc[:, idx].astype(jnp.float32)
    vg = vc[:, idx].astype(jnp.float32)
    qg = q.astype(jnp.float32).reshape(n, T, hkv, g, dh)
    bias = rel_bias(steps * dil, table).reshape(hkv, g, -1)
    logits = jnp.einsum('nthgd,ntshd->nthgs', qg, kg) + bias
    sk = None if sink is None else sink.astype(jnp.float32).reshape(hkv, g, 1)
    probs, lse = softmax_with_sink(logits, valid[None, :, None, None, :], sk)
    out = jnp.einsum('nthgs,ntshd->nthgd', probs, vg).reshape(n, T, H, dh)
    return out, lse.reshape(n, T, H)


def causal_dwconv(xbc, prev, w, b):
    xpad = jnp.concatenate([prev.astype(xbc.dtype), xbc], axis=1)
    y = lax.conv_general_dilated(xpad, w[:, None, :].astype(xbc.dtype), (1,), 'VALID',
                                 dimension_numbers=('NWC', 'WIO', 'NWC'),
                                 feature_group_count=CONV_DIM)
    return y + b.astype(xbc.dtype), xpad[:, -(CONV_K - 1):]


def ssd_chunked(x, dt, A, Bm, Cm, h0):
    n, L, H, P = x.shape
    f32 = jnp.float32
    rep = H // Bm.shape[2]
    Bh = jnp.repeat(Bm.astype(f32), rep, axis=2)
    Ch = jnp.repeat(Cm.astype(f32), rep, axis=2)
    Q = min(SSD_CHUNK, L)
    nc = -(-L // Q)
    pad = nc * Q - L

    def chunks(t):
        t = jnp.pad(t, ((0, 0), (0, pad)) + ((0, 0),) * (t.ndim - 2))
        return t.reshape(n, nc, Q, *t.shape[2:])

    xc = chunks(x.astype(f32) * dt[..., None])
    ac = chunks(dt * A)
    Bc = chunks(Bh)
    Cc = chunks(Ch)
    acum = jnp.cumsum(ac, axis=2)
    causal = jnp.tril(jnp.ones((Q, Q), dtype=bool))
    seg = acum[:, :, :, None, :] - acum[:, :, None, :, :]
    decay = jnp.exp(jnp.where(causal[None, None, :, :, None], seg, -jnp.inf))
    scores = jnp.einsum('nclhd,ncshd->nclsh', Cc, Bc) * decay
    y_diag = jnp.einsum('nclsh,ncshp->nclhp', scores, xc)
    to_end = jnp.exp(acum[:, :, -1:, :] - acum)
    states = jnp.einsum('nclhd,nclhp->nchpd', Bc * to_end[..., None], xc)
    chunk_decay = jnp.exp(acum[:, :, -1, :])

    def step(h, inp):
        dec, st = inp
        return h * dec[..., None, None] + st, h

    h_last, h_prev = lax.scan(step, h0.astype(f32),
                              (chunk_decay.swapaxes(0, 1), states.swapaxes(0, 1)))
    h_prev = h_prev.swapaxes(0, 1)
    y_off = jnp.einsum('nclhd,nchpd->nclhp', Cc * jnp.exp(acum)[..., None], h_prev)
    y = (y_diag + y_off).reshape(n, nc * Q, H, P)[:, :L]
    return y, h_last


def hybrid_layer(h, W, l, past):
    n, L, _ = h.shape
    f32 = jnp.float32
    u = rmsnorm(h, W['norm_mix_g'][l])
    proj = u @ W['w_in'][l]
    offs = []
    acc = 0
    for s in IN_SPLITS[:-1]:
        acc += s
        offs.append(acc)
    qa, ka, va, qc, kc, vc, z, xbc, dt_raw = jnp.split(proj, offs, axis=-1)
    tab_a = W['rel_bias'][:, :H_A]
    tab_c = W['rel_bias'][:, H_A:]

    qa = rmsnorm(qa.reshape(n, L, H_A, HEAD_DIM), W['a_q_norm_g'][l]) * ATTN_SCALE
    ka = rmsnorm(ka.reshape(n, L, HKV_A, HEAD_DIM), W['a_k_norm_g'][l])
    va = va.reshape(n, L, HKV_A, HEAD_DIM)
    if past is None:
        oa, _ = banded_attention(qa, ka, va, WIN_A, 1, tab_a, W['a_sinks'][l])
    else:
        oa, _ = gathered_window_attention(qa, past[0], past[1], ka, va, WIN_A, 1, tab_a, W['a_sinks'][l])

    qc = rmsnorm(qc.reshape(n, L, H_C, HEAD_DIM), W['c_q_norm_g'][l]) * ATTN_SCALE
    kc = rmsnorm(kc.reshape(n, L, HKV_C, HEAD_DIM), W['c_k_norm_g'][l])
    vc = vc.reshape(n, L, HKV_C, HEAD_DIM)
    outs, lses = [], []
    for window, dil in C_BRANCHES:
        if past is None:
            o, s = dilated_banded_attention(qc, kc, vc, window, dil, tab_c)
        else:
            o, s = gathered_window_attention(qc, past[2], past[3], kc, vc, window, dil, tab_c, None)
        outs.append(o)
        lses.append(s)
    wts = jax.nn.softmax(jnp.stack(lses, axis=-1), axis=-1)
    oc = jnp.einsum('nlhbd,nlhb->nlhd', jnp.stack(outs, axis=-2), wts)

    conv_prev = jnp.zeros((n, CONV_K - 1, CONV_DIM), h.dtype) if past is None else past[5]
    h0 = jnp.zeros((n, H_B, SSM_HEAD_DIM, D_STATE), f32) if past is None else past[4]
    xbc_c, conv_new = causal_dwconv(xbc, conv_prev, W['conv_w'][l], W['conv_b'][l])
    xbc_c = jax.nn.silu(xbc_c)
    xs, bm, cm = jnp.split(xbc_c, [D_INNER_B, D_INNER_B + SSM_GROUPS * D_STATE], axis=-1)
    xs = xs.reshape(n, L, H_B, SSM_HEAD_DIM)
    dt = jax.nn.softplus(dt_raw.astype(f32) + W['dt_bias'][l].astype(f32))
    A = -jnp.exp(W['a_log'][l].astype(f32))
    y, h_last = ssd_chunked(xs, dt, A, bm.reshape(n, L, SSM_GROUPS, D_STATE),
                            cm.reshape(n, L, SSM_GROUPS, D_STATE), h0)
    y = y + W['d_skip'][l].astype(f32)[:, None] * xs.astype(f32)
    gated = (y.reshape(n, L, D_INNER_B) * jax.nn.silu(z.astype(f32))).reshape(n, L, SSM_GROUPS, -1)
    ob = (rms_normalize(gated).reshape(n, L, D_INNER_B) * W['ssm_norm_g'][l].astype(f32)).astype(h.dtype)

    mixed = jnp.concatenate([oa.reshape(n, L, -1).astype(h.dtype), ob,
                             oc.reshape(n, L, -1).astype(h.dtype)], axis=-1)
    h = h + mixed @ W['w_out'][l]

    u2 = rmsnorm(h, W['norm_mlp_g'][l])
    h = h + jnp.square(jax.nn.relu(u2 @ W['w_up'][l])) @ W['w_down'][l]

    if past is None:
        la = min(WIN_A, L)
        lc = min(WIN_C_MAX, L)
        st = (ka[:, L - la:], va[:, L - la:], kc[:, L - lc:], vc[:, L - lc:],
              h_last.astype(h.dtype), conv_new)
    else:
        st = (ka, va, kc, vc, h_last.astype(h.dtype), conv_new)
    return h, st


def run_trunk(x, W, past):
    h = x
    per_layer = []
    for l in range(DEPTH):
        past_l = None if past is None else tuple(c[l] for c in past)
        h, st = hybrid_layer(h, W, l, past_l)
        per_layer.append(st)
    stacked = tuple(jnp.stack([st[i] for st in per_layer]) for i in range(6))
    return h, stacked


def setup_inputs(seed: int = 0) -> dict:
    key = jax.random.key(seed)
    ks = jax.random.split(key, 32)
    f32 = jnp.float32

    def nrm(k, shape, scale):
        return jax.random.normal(k, shape, f32) * scale

    def gain(k, shape):
        return 1.0 + 0.05 * jax.random.normal(k, shape, f32)

    la = min(WIN_A, PAST_LEN)
    lc = min(WIN_C_MAX, PAST_LEN)
    dt0 = jnp.exp(jax.random.uniform(ks[20], (DEPTH, H_B), f32, math.log(1e-3), math.log(1e-1)))
    dt_bias = dt0 + jnp.log(-jnp.expm1(-dt0))
    a_log = jnp.log(jax.random.uniform(ks[21], (DEPTH, H_B), f32, 1.0, 16.0))
    return {
        'x_prompt': nrm(ks[0], (BATCH, SEQ, D_MODEL), 1.0),
        'x_sample': nrm(ks[1], (DEC_BATCH, DEC_SEQ, D_MODEL), 1.0),
        'cache_a_k': nrm(ks[2], (DEPTH, DEC_BATCH, la, HKV_A, HEAD_DIM), 1.0),
        'cache_a_v': nrm(ks[3], (DEPTH, DEC_BATCH, la, HKV_A, HEAD_DIM), 1.0),
        'cache_c_k': nrm(ks[4], (DEPTH, DEC_BATCH, lc, HKV_C, HEAD_DIM), 1.0),
        'cache_c_v': nrm(ks[5], (DEPTH, DEC_BATCH, lc, HKV_C, HEAD_DIM), 1.0),
        'state_ssm': nrm(ks[6], (DEPTH, DEC_BATCH, H_B, SSM_HEAD_DIM, D_STATE), 0.1),
        'state_conv': nrm(ks[7], (DEPTH, DEC_BATCH, CONV_K - 1, CONV_DIM), 1.0),
        'norm_mix_g': gain(ks[8], (DEPTH, D_MODEL)),
        'w_in': nrm(ks[9], (DEPTH, D_MODEL, IN_COLS), D_MODEL ** -0.5),
        'a_q_norm_g': gain(ks[10], (DEPTH, HEAD_DIM)),
        'a_k_norm_g': gain(ks[11], (DEPTH, HEAD_DIM)),
        'a_sinks': nrm(ks[12], (DEPTH, H_A), 0.5),
        'c_q_norm_g': gain(ks[13], (DEPTH, HEAD_DIM)),
        'c_k_norm_g': gain(ks[14], (DEPTH, HEAD_DIM)),
        'rel_bias': nrm(ks[15], (NUM_BUCKETS, H_A + H_C), 0.5),
        'conv_w': nrm(ks[16], (DEPTH, CONV_K, CONV_DIM), CONV_K ** -0.5),
        'conv_b': nrm(ks[17], (DEPTH, CONV_DIM), 0.02),
        'dt_bias': dt_bias,
        'a_log': a_log,
        'd_skip': gain(ks[18], (DEPTH, H_B)),
        'ssm_norm_g': gain(ks[19], (DEPTH, D_INNER_B)),
        'w_out': nrm(ks[22], (DEPTH, MIX_WIDTH, D_MODEL), 0.5 * MIX_WIDTH ** -0.5),
        'norm_mlp_g': gain(ks[23], (DEPTH, D_MODEL)),
        'w_up': nrm(ks[24], (DEPTH, D_MODEL, D_FF), D_MODEL ** -0.5),
        'w_down': nrm(ks[25], (DEPTH, D_FF, D_MODEL), 0.5 * D_FF ** -0.5),
    }


def reference(x_prompt, x_sample, cache_a_k, cache_a_v, cache_c_k, cache_c_v, state_ssm, state_conv,
              norm_mix_g, w_in, a_q_norm_g, a_k_norm_g, a_sinks, c_q_norm_g, c_k_norm_g, rel_bias,
              conv_w, conv_b, dt_bias, a_log, d_skip, ssm_norm_g, w_out, norm_mlp_g, w_up, w_down):
    W = {
        'norm_mix_g': norm_mix_g, 'w_in': w_in,
        'a_q_norm_g': a_q_norm_g, 'a_k_norm_g': a_k_norm_g, 'a_sinks': a_sinks,
        'c_q_norm_g': c_q_norm_g, 'c_k_norm_g': c_k_norm_g, 'rel_bias': rel_bias,
        'conv_w': conv_w, 'conv_b': conv_b, 'dt_bias': dt_bias, 'a_log': a_log,
        'd_skip': d_skip, 'ssm_norm_g': ssm_norm_g, 'w_out': w_out,
        'norm_mlp_g': norm_mlp_g, 'w_up': w_up, 'w_down': w_down,
    }
    y_prompt, st_p = run_trunk(x_prompt, W, None)
    past = (cache_a_k, cache_a_v, cache_c_k, cache_c_v, state_ssm, state_conv)
    y_sample, st_s = run_trunk(x_sample, W, past)
    p_a_k, p_a_v, p_c_k, p_c_v, p_ssm, p_conv = st_p
    s_a_k, s_a_v, s_c_k, s_c_v, s_ssm, s_conv = st_s
    return (y_prompt, y_sample, p_a_k, p_a_v, p_c_k, p_c_v, p_ssm, p_conv,
            s_a_k, s_a_v, s_c_k, s_c_v, s_ssm, s_conv)
```

```python
import functools
import math

import jax
import jax.numpy as jnp
from jax import lax
from jax.experimental import pallas as pl
from jax.experimental.pallas import tpu as pltpu

F32 = jnp.float32
BF16 = jnp.bfloat16
HIGHEST = lax.Precision.HIGHEST

D_MODEL = 1024
HEAD_DIM = 64
LANES = 128
SUBLANES = 8
BLOCK = 128
WIN = 128
D_INNER = 512
D_STATE = 128
N_SSM_HEADS = 8
CONV_DIM = 1024
D_FF = 4096
NUM_BUCKETS = 32
REL_MAX_DIST = 2048
EPS = 1e-6
ATTN_SCALE = HEAD_DIM ** -0.5
NEG = -1e30
C_DILS = (1, 4, 16)
SROWS = 8
SVALID = 4

COL_QA, COL_KA, COL_VA, COL_QC, COL_KC, COL_VC, COL_Z, COL_XBC, COL_DT, COL_END = (
    0, 256, 384, 512, 768, 1024, 1280, 1792, 2816, 2944)
IN_COLS = 2824
VMEM_LIMIT = 56 * 1024 * 1024


def _cparams(sem):
    return pltpu.CompilerParams(dimension_semantics=sem, vmem_limit_bytes=VMEM_LIMIT)


def _dot(a, b):
    return jnp.dot(a, b, preferred_element_type=F32)


def _dot_t(a, b):
    return lax.dot_general(a, b, (((1,), (1,)), ((), ())), preferred_element_type=F32)


def _dot_t0(a, b):
    return lax.dot_general(a, b, (((0,), (0,)), ((), ())), preferred_element_type=F32)


def _lane_iota(shape):
    return lax.broadcasted_iota(jnp.int32, shape, len(shape) - 1)


def _silu(x):
    return x * (1.0 / (1.0 + jnp.exp(-x)))


def _softplus(x):
    return jnp.maximum(x, 0.0) + jnp.log(1.0 + jnp.exp(-jnp.abs(x)))


def _bias_kernel(tab_ref, bkt_ref, o_ref, *, heads, col0):
    bkt = bkt_ref[...]
    for h in range(heads):
        acc = jnp.full(bkt.shape, NEG, F32)
        for b in range(NUM_BUCKETS):
            acc = jnp.where(bkt == b, tab_ref[b, col0 + h], acc)
        o_ref[h] = acc


def _expand_bias(table, bkt, heads, col0):
    r, c = bkt.shape
    return pl.pallas_call(
        functools.partial(_bias_kernel, heads=heads, col0=col0),
        out_shape=jax.ShapeDtypeStruct((heads, r, c), F32),
        in_specs=[pl.BlockSpec(memory_space=pltpu.SMEM),
                  pl.BlockSpec((r, c), lambda: (0, 0))],
        out_specs=pl.BlockSpec((heads, r, c), lambda: (0, 0, 0)),
        name="bias_expand",
    )(table, bkt)


def _t5_bucket(dist):
    max_exact = NUM_BUCKETS // 2
    d = jnp.maximum(dist, 0)
    df = jnp.maximum(d, 1).astype(F32)
    large = max_exact + (jnp.log(df / max_exact) / math.log(REL_MAX_DIST / max_exact)
                         * (NUM_BUCKETS - max_exact)).astype(jnp.int32)
    return jnp.where(d < max_exact, d, jnp.minimum(large, NUM_BUCKETS - 1))


def _masked_bucket(dist, valid):
    return jnp.where(valid, _t5_bucket(dist), -1).astype(jnp.int32)


def _head_norm(slab, gain, bd):
    sq = slab * slab
    hi = sq.astype(BF16)
    lo = (sq - hi.astype(F32)).astype(BF16)
    ss = _dot(hi, bd) + _dot(lo, bd)
    return slab * lax.rsqrt(ss * (1.0 / HEAD_DIM) + EPS) * gain


def _in_proj_kernel(h_ref, g_ref, w_ref, gain_ref, bd_ref,
                    qa_ref, ka_ref, va_ref, qc_ref, kc_ref, vc_ref, z_ref, xbc_ref, dt_ref):
    x = h_ref[...]
    ms = jnp.mean(x * x, axis=-1, keepdims=True)
    u = (x * lax.rsqrt(ms + EPS) * g_ref[...]).astype(BF16)
    bd = bd_ref[...]

    def proj(c0, c1):
        return _dot(u, w_ref[:, c0:c1])

    def normed(c0, c1, out_ref):
        p = proj(c0, c1)
        for s in range((c1 - c0) // LANES):
            sl = slice(s * LANES, (s + 1) * LANES)
            out_ref[:, sl] = _head_norm(p[:, sl], gain_ref[:, c0 + s * LANES:c0 + (s + 1) * LANES], bd)

    normed(COL_QA, COL_KA, qa_ref)
    normed(COL_KA, COL_VA, ka_ref)
    va_ref[...] = proj(COL_VA, COL_QC)
    normed(COL_QC, COL_KC, qc_ref)
    normed(COL_KC, COL_VC, kc_ref)
    vc_ref[...] = proj(COL_VC, COL_Z)
    z_ref[...] = proj(COL_Z, COL_XBC)
    xbc_ref[...] = proj(COL_XBC, COL_DT)
    dt_ref[...] = proj(COL_DT, COL_END)


def _in_proj(h, g, w, gain, bd, tm):
    t = h.shape[0]
    widths = (256, 128, 128, 256, 256, 256, 512, 1024, 128)
    const = lambda i: (0, 0)
    return pl.pallas_call(
        _in_proj_kernel,
        grid=(t // tm,),
        out_shape=[jax.ShapeDtypeStruct((t, w_), F32) for w_ in widths],
        in_specs=[pl.BlockSpec((tm, D_MODEL), lambda i: (i, 0)),
                  pl.BlockSpec((1, D_MODEL), const),
                  pl.BlockSpec((D_MODEL, COL_END), const),
                  pl.BlockSpec((1, D_MODEL), const),
                  pl.BlockSpec((LANES, LANES), const)],
        out_specs=[pl.BlockSpec((tm, w_), lambda i: (i, 0)) for w_ in widths],
        compiler_params=_cparams(("parallel",)),
        name="in_proj",
    )(h, g, w, gain, bd)


def _out_mlp_kernel(h_ref, oa_ref, ob_ref, oc_ref, wo_ref, g_ref, wu_ref, wd_ref, o_ref):
    acc = _dot(oa_ref[...].astype(BF16), wo_ref[0:256, :])
    acc += _dot(ob_ref[...].astype(BF16), wo_ref[256:768, :])
    acc += _dot(oc_ref[...].astype(BF16), wo_ref[768:1024, :])
    h2 = h_ref[...] + acc
    ms = jnp.mean(h2 * h2, axis=-1, keepdims=True)
    u = (h2 * lax.rsqrt(ms + EPS) * g_ref[...]).astype(BF16)
    ff_chunk = 1024
    mlp = jnp.zeros_like(h2)
    for c in range(D_FF // ff_chunk):
        a = _dot(u, wu_ref[:, c * ff_chunk:(c + 1) * ff_chunk])
        a = jnp.square(jnp.maximum(a, 0.0)).astype(BF16)
        mlp += _dot(a, wd_ref[c * ff_chunk:(c + 1) * ff_chunk, :])
    o_ref[...] = h2 + mlp


def _out_mlp(h, oa, ob, oc, wo, g, wu, wd, tm):
    t = h.shape[0]
    const = lambda i: (0, 0)
    row = lambda i: (i, 0)
    return pl.pallas_call(
        _out_mlp_kernel,
        grid=(t // tm,),
        out_shape=jax.ShapeDtypeStruct((t, D_MODEL), F32),
        in_specs=[pl.BlockSpec((tm, D_MODEL), row),
                  pl.BlockSpec((tm, 256), row),
                  pl.BlockSpec((tm, 512), row),
                  pl.BlockSpec((tm, 256), row),
                  pl.BlockSpec((D_MODEL, D_MODEL), const),
                  pl.BlockSpec((1, D_MODEL), const),
                  pl.BlockSpec((D_MODEL, D_FF), const),
                  pl.BlockSpec((D_FF, D_MODEL), const)],
        out_specs=pl.BlockSpec((tm, D_MODEL), row),
        compiler_params=_cparams(("parallel",)),
        name="out_mlp",
    )(h, oa, ob, oc, wo, g, wu, wd)


def _attn_a_prompt_kernel(sink_ref, q_ref, kc_ref, kp_ref, vc_ref, vp_ref, bias_ref, o_ref):
    first = pl.program_id(1) == 0
    lane = _lane_iota((BLOCK, LANES))
    col = _lane_iota((BLOCK, 2 * BLOCK))
    lo = lane < HEAD_DIM
    k = jnp.concatenate([kp_ref[...], kc_ref[...]], axis=0)
    v = jnp.concatenate([vp_ref[...], vc_ref[...]], axis=0)
    k_sw = pltpu.roll(k, HEAD_DIM, 1).astype(BF16)
    v_sw = pltpu.roll(v, HEAD_DIM, 1).astype(BF16)
    k = k.astype(BF16)
    v = v.astype(BF16)
    outs = []
    for h in range(4):
        slab = q_ref[:, (h // 2) * LANES:(h // 2 + 1) * LANES]
        qm = jnp.where(lo if h % 2 == 0 else ~lo, slab, 0.0).astype(BF16)
        same = h in (0, 3)
        s = _dot_t(qm, k if same else k_sw) + bias_ref[h]
        s = jnp.where(jnp.logical_and(first, col < BLOCK), NEG, s)
        sink = sink_ref[h]
        m = jnp.maximum(jnp.max(s, axis=-1, keepdims=True), sink)
        p = jnp.exp(s - m)
        den = jnp.sum(p, axis=-1, keepdims=True) + jnp.exp(sink - m)
        outs.append(_dot(p.astype(BF16), v if same else v_sw) / den)
    o_ref[:, 0:LANES] = jnp.where(lo, outs[0], outs[1])
    o_ref[:, LANES:2 * LANES] = jnp.where(lo, outs[2], outs[3])


def _attn_a_prompt(q, k, v, bias, sinks, n, nb):
    t = q.shape[0]
    cur = lambda i, b: (i * nb + b, 0)
    prev = lambda i, b: (i * nb + jnp.maximum(b - 1, 0), 0)
    return pl.pallas_call(
        _attn_a_prompt_kernel,
        grid=(n, nb),
        out_shape=jax.ShapeDtypeStruct((t, 256), F32),
        in_specs=[pl.BlockSpec(memory_space=pltpu.SMEM),
                  pl.BlockSpec((BLOCK, 256), cur),
                  pl.BlockSpec((BLOCK, LANES), cur),
                  pl.BlockSpec((BLOCK, LANES), prev),
                  pl.BlockSpec((BLOCK, LANES), cur),
                  pl.BlockSpec((BLOCK, LANES), prev),
                  pl.BlockSpec((4, BLOCK, 2 * BLOCK), lambda i, b: (0, 0, 0))],
        out_specs=pl.BlockSpec((BLOCK, 256), cur),
        compiler_params=_cparams(("parallel", "parallel")),
        name="attn_a_prompt",
    )(sinks, q, k, k, v, v, bias)


def _attn_c_prompt_kernel(q_ref, k_ref, v_ref, bias_ref, o_ref, ob_scr, lse_scr, *, seq):
    lane = _lane_iota((BLOCK, LANES))
    lo = lane < HEAD_DIM

    def block(br, dil, qs, ps, first, has_prev):
        q = q_ref[pl.ds(qs, BLOCK, stride=dil), :]
        kc = k_ref[pl.ds(qs, BLOCK, stride=dil), :].astype(BF16)
        vc = v_ref[pl.ds(qs, BLOCK, stride=dil), :].astype(BF16)
        if has_prev:
            kp = k_ref[pl.ds(ps, BLOCK, stride=dil), :].astype(BF16)
            vp = v_ref[pl.ds(ps, BLOCK, stride=dil), :].astype(BF16)
        outs, lses = [], []
        for hh in range(2):
            qm = jnp.where(lo if hh == 0 else ~lo, q, 0.0).astype(BF16)
            s_c = _dot_t(qm, kc) + bias_ref[br, hh, :, BLOCK:2 * BLOCK]
            m = jnp.max(s_c, axis=-1, keepdims=True)
            if has_prev:
                s_p = _dot_t(qm, kp) + bias_ref[br, hh, :, 0:BLOCK]
                s_p = jnp.where(first, NEG, s_p)
                m = jnp.maximum(m, jnp.max(s_p, axis=-1, keepdims=True))
            p_c = jnp.exp(s_c - m)
            den = jnp.sum(p_c, axis=-1, keepdims=True)
            o = _dot(p_c.astype(BF16), vc)
            if has_prev:
                p_p = jnp.exp(s_p - m)
                den = den + jnp.sum(p_p, axis=-1, keepdims=True)
                o = o + _dot(p_p.astype(BF16), vp)
            outs.append(o / den)
            lses.append(jnp.broadcast_to(m + jnp.log(den), (BLOCK, LANES)))
        ob_scr[br, pl.ds(qs, BLOCK, stride=dil), :] = jnp.where(lo, outs[0], outs[1])
        lse_scr[br, pl.ds(qs, BLOCK, stride=dil), :] = jnp.where(lo, lses[0], lses[1])

    for br, dil in enumerate(C_DILS):
        nb = seq // dil // BLOCK
        span = BLOCK * dil

        def body(i, carry, br=br, dil=dil, nb=nb, span=span):
            res = i % dil
            b = i // dil
            qs = res + b * span
            ps = res + jnp.maximum(b - 1, 0) * span
            block(br, dil, qs, ps, b == 0, nb > 1)
            return carry

        lax.fori_loop(0, dil * nb, body, 0)

    def merge(i, carry):
        rows = pl.ds(pl.multiple_of(i * BLOCK, BLOCK), BLOCK)
        l0, l1, l2 = lse_scr[0, rows, :], lse_scr[1, rows, :], lse_scr[2, rows, :]
        m = jnp.maximum(jnp.maximum(l0, l1), l2)
        w0, w1, w2 = jnp.exp(l0 - m), jnp.exp(l1 - m), jnp.exp(l2 - m)
        num = w0 * ob_scr[0, rows, :] + w1 * ob_scr[1, rows, :] + w2 * ob_scr[2, rows, :]
        o_ref[rows, :] = num / (w0 + w1 + w2)
        return carry

    lax.fori_loop(0, seq // BLOCK, merge, 0)


def _attn_c_prompt(q, k, v, bias, n, seq):
    t = q.shape[0]
    blk = pl.BlockSpec((seq, LANES), lambda i, hp: (i, hp))
    return pl.pallas_call(
        functools.partial(_attn_c_prompt_kernel, seq=seq),
        grid=(n, 2),
        out_shape=jax.ShapeDtypeStruct((t, 256), F32),
        in_specs=[blk, blk, blk,
                  pl.BlockSpec((3, 2, BLOCK, 2 * BLOCK), lambda i, hp: (0, hp, 0, 0))],
        out_specs=blk,
        scratch_shapes=[pltpu.VMEM((3, seq, LANES), F32), pltpu.VMEM((3, seq, LANES), F32)],
        compiler_params=_cparams(("parallel", "parallel")),
        name="attn_c_prompt",
    )(q, k, v, bias)


def _gated_group_norm(y, z, ng):
    gated = y * _silu(z)
    parts = []
    for grp in range(2):
        gsl = gated[:, grp * 256:(grp + 1) * 256]
        ms = jnp.mean(gsl * gsl, axis=-1, keepdims=True)
        parts.append(gsl * lax.rsqrt(ms + EPS))
    return jnp.concatenate(parts, axis=1) * ng


def _ssd_prompt_kernel(xbc_ref, z_ref, dt_ref, h0_ref, prev_ref, cw_ref, cb_ref, dtb_ref, alog_ref,
                       dsk_ref, ng_ref, tri_ref, ob_ref, hl_ref, hst, xbuf):
    c = pl.program_id(1)

    @pl.when(c == 0)
    def _():
        hst[...] = h0_ref[...]
        xbuf[0:SUBLANES, :] = prev_ref[...]

    x = xbc_ref[...]
    xbuf[SUBLANES:SUBLANES + BLOCK, :] = x
    y = cb_ref[...] + cw_ref[3:4, :] * x
    for kk in range(3):
        y = y + cw_ref[kk:kk + 1, :] * xbuf[SUBLANES - 3 + kk:SUBLANES - 3 + kk + BLOCK, :]
    xbuf[0:SUBLANES, :] = x[BLOCK - SUBLANES:BLOCK, :]
    xa = _silu(y)
    xs = xa[:, 0:D_INNER]
    dt = _softplus(dt_ref[...] + dtb_ref[...])
    a = dt * (-jnp.exp(alog_ref[...]))
    acum = jnp.dot(tri_ref[...], a, precision=HIGHEST, preferred_element_type=F32)
    acum_t = acum.T
    dt_t = dt.T
    eac = jnp.exp(acum)
    last = acum[BLOCK - 1:BLOCK, :]
    w_end = jnp.exp(last - acum) * dt
    e_last = jnp.exp(last)
    row = lax.broadcasted_iota(jnp.int32, (BLOCK, BLOCK), 0)
    lane = _lane_iota((BLOCK, BLOCK))
    causal = lane <= row
    lo = lane < HEAD_DIM
    top = row < HEAD_DIM
    y_parts = []
    for j in range(N_SSM_HEADS // 2):
        g = j // 2
        bm = xa[:, D_INNER + g * D_STATE:D_INNER + (g + 1) * D_STATE].astype(BF16)
        cm = xa[:, D_INNER + 2 * D_STATE + g * D_STATE:D_INNER + 2 * D_STATE + (g + 1) * D_STATE].astype(BF16)
        cb = _dot_t(cm, bm)
        xs_pair = xs[:, j * LANES:(j + 1) * LANES]
        xs_b = xs_pair.astype(BF16)
        yd = []
        for hh in range(2):
            h = 2 * j + hh
            seg = acum[:, h:h + 1] - acum_t[h:h + 1, :]
            dec = jnp.exp(jnp.where(causal, seg, NEG)) * dt_t[h:h + 1, :]
            yd.append(_dot((cb * dec).astype(BF16), xs_b))
        h0_, h1_ = 2 * j, 2 * j + 1
        hp = hst[h0_:h0_ + 2].reshape(BLOCK, D_STATE)
        e_pair = jnp.where(lo, eac[:, h0_:h0_ + 1], eac[:, h1_:h1_ + 1])
        y_parts.append(jnp.where(lo, yd[0], yd[1]) + _dot_t(cm, hp.astype(BF16)) * e_pair)
        w_pair = jnp.where(lo, w_end[:, h0_:h0_ + 1], w_end[:, h1_:h1_ + 1])
        s_new = _dot_t0((xs_pair * w_pair).astype(BF16), bm)
        cd_pair = jnp.where(top, e_last[:, h0_:h0_ + 1], e_last[:, h1_:h1_ + 1])
        hst[h0_:h0_ + 2] = (hp * cd_pair + s_new).reshape(2, HEAD_DIM, D_STATE)
    y = jnp.concatenate(y_parts, axis=1) + dsk_ref[...] * xs
    ob_ref[...] = _gated_group_norm(y, z_ref[...], ng_ref[...])

    @pl.when(c == pl.num_programs(1) - 1)
    def _():
        hl_ref[...] = hst[...]


def _ssd_prompt(xbc, z, dt, h0, prev8, cw8, cb, dtb, alog, dsk, ng, tri, n, nc):
    t = xbc.shape[0]
    row = lambda i, c: (i * nc + c, 0)
    const = lambda i, c: (0, 0)
    return pl.pallas_call(
        _ssd_prompt_kernel,
        grid=(n, nc),
        out_shape=[jax.ShapeDtypeStruct((t, D_INNER), F32),
                   jax.ShapeDtypeStruct((n, N_SSM_HEADS, HEAD_DIM, D_STATE), F32)],
        in_specs=[pl.BlockSpec((BLOCK, CONV_DIM), row),
                  pl.BlockSpec((BLOCK, D_INNER), row),
                  pl.BlockSpec((BLOCK, LANES), row),
                  pl.BlockSpec((None, N_SSM_HEADS, HEAD_DIM, D_STATE), lambda i, c: (i, 0, 0, 0)),
                  pl.BlockSpec((None, SUBLANES, CONV_DIM), lambda i, c: (i, 0, 0)),
                  pl.BlockSpec((SUBLANES, CONV_DIM), const),
                  pl.BlockSpec((1, CONV_DIM), const),
                  pl.BlockSpec((1, LANES), const),
                  pl.BlockSpec((1, LANES), const),
                  pl.BlockSpec((1, D_INNER), const),
                  pl.BlockSpec((1, D_INNER), const),
                  pl.BlockSpec((BLOCK, BLOCK), const)],
        out_specs=[pl.BlockSpec((BLOCK, D_INNER), row),
                   pl.BlockSpec((None, N_SSM_HEADS, HEAD_DIM, D_STATE), lambda i, c: (i, 0, 0, 0))],
        scratch_shapes=[pltpu.VMEM((N_SSM_HEADS, HEAD_DIM, D_STATE), F32),
                        pltpu.VMEM((SUBLANES + BLOCK, CONV_DIM), F32)],
        compiler_params=_cparams(("parallel", "arbitrary")),
        name="ssd_prompt",
    )(xbc, z, dt, h0, prev8, cw8, cb, dtb, alog, dsk, ng, tri)


def _attn_a_sample_kernel(sink_ref, q_ref, kn_ref, vn_ref, kc_ref, vc_ref, bc_ref, bn_ref, o_ref, *, nbs):
    rb = nbs * SROWS
    lo = _lane_iota((rb, LANES)) < HEAD_DIM
    s0, s1 = q_ref[:, 0:LANES], q_ref[:, LANES:2 * LANES]
    per_seq = lambda v: v.reshape(nbs, SROWS, LANES)
    qm = jnp.concatenate([
        per_seq(jnp.where(lo, s0, 0.0)),
        per_seq(jnp.where(lo, pltpu.roll(s0, HEAD_DIM, 1), 0.0)),
        per_seq(jnp.where(lo, 0.0, pltpu.roll(s1, HEAD_DIM, 1))),
        per_seq(jnp.where(lo, 0.0, s1))], axis=1).astype(BF16)
    kc = kc_ref[...].astype(BF16)
    vc = vc_ref[...].astype(BF16)
    kn = kn_ref[...].reshape(nbs, SROWS, LANES).astype(BF16)
    vn = vn_ref[...].reshape(nbs, SROWS, LANES).astype(BF16)
    s_c = jnp.einsum('nqd,nkd->nqk', qm, kc, preferred_element_type=F32) + bc_ref[...]
    s_n = jnp.einsum('nqd,nkd->nqk', qm, kn, preferred_element_type=F32) + bn_ref[...]
    hrow = lax.broadcasted_iota(jnp.int32, (nbs, 4 * SROWS, 1), 1) // SROWS
    sink = jnp.where(hrow == 0, sink_ref[0], jnp.where(hrow == 1, sink_ref[1],
                     jnp.where(hrow == 2, sink_ref[2], sink_ref[3])))
    m = jnp.maximum(jnp.max(s_c, axis=-1, keepdims=True), jnp.max(s_n, axis=-1, keepdims=True))
    m = jnp.maximum(m, sink)
    p_c = jnp.exp(s_c - m)
    p_n = jnp.exp(s_n - m)
    den = jnp.sum(p_c, axis=-1, keepdims=True) + jnp.sum(p_n, axis=-1, keepdims=True) + jnp.exp(sink - m)
    o = (jnp.einsum('nqk,nkd->nqd', p_c.astype(BF16), vc, preferred_element_type=F32)
         + jnp.einsum('nqk,nkd->nqd', p_n.astype(BF16), vn, preferred_element_type=F32)) / den
    o0, o1, o2, o3 = (o[:, i * SROWS:(i + 1) * SROWS, :].reshape(rb, LANES) for i in range(4))
    o_ref[:, 0:LANES] = jnp.where(lo, o0, pltpu.roll(o1, HEAD_DIM, 1))
    o_ref[:, LANES:2 * LANES] = jnp.where(lo, pltpu.roll(o2, HEAD_DIM, 1), o3)


def _attn_a_sample(q8, k8, v8, cache_k, cache_v, bias_c, bias_n, sinks, nbs):
    ns = cache_k.shape[0]
    tok = lambda i: (i, 0)
    seq3 = lambda i: (i, 0, 0)
    c3 = lambda i: (0, 0, 0)
    return pl.pallas_call(
        functools.partial(_attn_a_sample_kernel, nbs=nbs),
        grid=(ns // nbs,),
        out_shape=jax.ShapeDtypeStruct((ns * SROWS, 256), F32),
        in_specs=[pl.BlockSpec(memory_space=pltpu.SMEM),
                  pl.BlockSpec((nbs * SROWS, 256), tok),
                  pl.BlockSpec((nbs * SROWS, LANES), tok),
                  pl.BlockSpec((nbs * SROWS, LANES), tok),
                  pl.BlockSpec((nbs, WIN, LANES), seq3),
                  pl.BlockSpec((nbs, WIN, LANES), seq3),
                  pl.BlockSpec((1, 4 * SROWS, WIN), c3),
                  pl.BlockSpec((1, 4 * SROWS, SROWS), c3)],
        out_specs=pl.BlockSpec((nbs * SROWS, 256), tok),
        compiler_params=_cparams(("parallel",)),
        name="attn_a_sample",
    )(sinks, q8, k8, v8, cache_k, cache_v, bias_c, bias_n)


R1_ROWS = 512
R3_ROWS = 128


def _attn_c_sample_kernel(q_ref, kn_ref, vn_ref, k1_ref, v1_ref, k3_ref, v3_ref,
                          b1_ref, b2_ref, b3_ref, bn_ref, o_ref, *, nbs):
    rows = 4 * SROWS
    lane = _lane_iota((nbs, SROWS, 256))
    in_head = [jnp.logical_and(lane >= h * HEAD_DIM, lane < (h + 1) * HEAD_DIM) for h in range(4)]
    q = q_ref[...].reshape(nbs, SROWS, 256)
    qbd = jnp.concatenate([jnp.where(in_head[h], q, 0.0) for h in range(4)], axis=1).astype(BF16)
    kn = kn_ref[...].reshape(nbs, SROWS, 256).astype(BF16)
    vn = vn_ref[...].reshape(nbs, SROWS, 256).astype(BF16)
    k1 = k1_ref[...].astype(BF16)
    v1 = v1_ref[...].astype(BF16)

    def qk(a, b):
        return jnp.einsum('nqd,nkd->nqk', a, b, preferred_element_type=F32)

    def pv(a, b):
        return jnp.einsum('nqk,nkd->nqd', a.astype(BF16), b, preferred_element_type=F32)

    s1 = qk(qbd, k1)
    sn = qk(qbd, kn)
    tok = lax.broadcasted_iota(jnp.int32, (nbs, rows, 1), 1) % SROWS
    s3 = jnp.zeros((nbs, rows, R3_ROWS), F32)
    for t in range(4):
        k3t = k3_ref[:, :, t * 256:(t + 1) * 256].astype(BF16)
        s3 = s3 + jnp.where(tok == SVALID + t, qk(qbd, k3t), 0.0)

    def softmax_parts(parts):
        m = None
        for s in parts:
            mm = jnp.max(s, axis=-1, keepdims=True)
            m = mm if m is None else jnp.maximum(m, mm)
        ps = [jnp.exp(s - m) for s in parts]
        den = None
        for p in ps:
            dd = jnp.sum(p, axis=-1, keepdims=True)
            den = dd if den is None else den + dd
        return ps, den, m + jnp.log(den)

    (p1, p1n), d1, l1 = softmax_parts([s1 + b1_ref[...], sn + bn_ref[0:1]])
    o1 = (pv(p1, v1) + pv(p1n, vn)) / d1
    (p2, p2n), d2, l2 = softmax_parts([s1 + b2_ref[...], sn + bn_ref[1:2]])
    o2 = (pv(p2, v1) + pv(p2n, vn)) / d2
    (p3, p3n), d3, l3 = softmax_parts([s3 + b3_ref[...], sn + bn_ref[2:3]])
    o3 = pv(p3n, vn)
    for t in range(4):
        v3t = v3_ref[:, :, t * 256:(t + 1) * 256].astype(BF16)
        o3 = o3 + jnp.where(tok == SVALID + t, pv(p3, v3t), 0.0)
    o3 = o3 / d3
    m = jnp.maximum(jnp.maximum(l1, l2), l3)
    w1, w2, w3 = jnp.exp(l1 - m), jnp.exp(l2 - m), jnp.exp(l3 - m)
    o = (w1 * o1 + w2 * o2 + w3 * o3) / (w1 + w2 + w3)
    out = jnp.zeros((nbs, SROWS, 256), F32)
    for h in range(4):
        out = out + jnp.where(in_head[h], o[:, h * SROWS:(h + 1) * SROWS, :], 0.0)
    o_ref[...] = out.reshape(nbs * SROWS, 256)


def _attn_c_sample(q8, k8, v8, cache_k, cache_v, b1, b2, b3, bn, nbs):
    ns, lc = cache_k.shape[0], cache_k.shape[1]
    k3 = cache_k.reshape(ns, lc // 16, 16 * 256)
    v3 = cache_v.reshape(ns, lc // 16, 16 * 256)
    tok = lambda i: (i, 0)
    last = lambda i: (i, lc // R1_ROWS - 1, 0)
    res = lambda i: (i, 0, 0)
    c3 = lambda i: (0, 0, 0)
    return pl.pallas_call(
        functools.partial(_attn_c_sample_kernel, nbs=nbs),
        grid=(ns // nbs,),
        out_shape=jax.ShapeDtypeStruct((ns * SROWS, 256), F32),
        in_specs=[pl.BlockSpec((nbs * SROWS, 256), tok),
                  pl.BlockSpec((nbs * SROWS, 256), tok),
                  pl.BlockSpec((nbs * SROWS, 256), tok),
                  pl.BlockSpec((nbs, R1_ROWS, 256), last),
                  pl.BlockSpec((nbs, R1_ROWS, 256), last),
                  pl.BlockSpec((nbs, R3_ROWS, 4 * 256), res),
                  pl.BlockSpec((nbs, R3_ROWS, 4 * 256), res),
                  pl.BlockSpec((1, 4 * SROWS, R1_ROWS), c3),
                  pl.BlockSpec((1, 4 * SROWS, R1_ROWS), c3),
                  pl.BlockSpec((1, 4 * SROWS, R3_ROWS), c3),
                  pl.BlockSpec((3, 4 * SROWS, SROWS), c3)],
        out_specs=pl.BlockSpec((nbs * SROWS, 256), tok),
        compiler_params=_cparams(("parallel",)),
        name="attn_c_sample",
    )(q8, k8, v8, cache_k, cache_v, k3, v3, b1, b2, b3, bn)


def _ssd_sample_pre_kernel(xbc_ref, prev_ref, dt_ref, cw_ref, cb_ref, dtb_ref, alog_ref, dsk_ref, ex_ref,
                           yd_ref, eac_ref, xw_ref, bm_ref, cm_ref, cd_ref):
    r = xbc_ref.shape[0]
    rid = lax.broadcasted_iota(jnp.int32, (r, 1), 0) % SROWS
    x = jnp.where(jnp.logical_and(rid >= 1, rid < SVALID), prev_ref[...], xbc_ref[...])
    y = cb_ref[...] + cw_ref[3:4, :] * x
    for d in range(1, 4):
        y = y + cw_ref[3 - d:4 - d, :] * pltpu.roll(x, d, 0)
    xa = _silu(y)
    xs = xa[:, 0:D_INNER]
    bm = xa[:, D_INNER:D_INNER + 2 * D_STATE]
    cm = xa[:, D_INNER + 2 * D_STATE:CONV_DIM]
    valid = rid >= SVALID
    dt = jnp.where(valid, _softplus(dt_ref[...] + dtb_ref[...]), 0.0)
    a = dt * (-jnp.exp(alog_ref[...]))
    acum = a
    rem = jnp.zeros_like(a)
    for d in range(1, 4):
        acum = acum + jnp.where(rid - d >= SVALID, pltpu.roll(a, d, 0), 0.0)
        rem = rem + jnp.where(rid + d < SROWS, pltpu.roll(a, r - d, 0), 0.0)
    ex = ex_ref[...]

    def expand(v):
        return jnp.dot(v, ex, precision=HIGHEST, preferred_element_type=F32)

    lane = _lane_iota((r, LANES))
    y_acc = dsk_ref[...] * xs
    for d in range(4):
        ok = rid - d >= SVALID
        bsh = bm if d == 0 else pltpu.roll(bm, d, 0)
        cb0 = jnp.sum(cm[:, 0:D_STATE] * bsh[:, 0:D_STATE], axis=-1, keepdims=True)
        cb1 = jnp.sum(cm[:, D_STATE:] * bsh[:, D_STATE:], axis=-1, keepdims=True)
        cbh = jnp.where(lane < N_SSM_HEADS // 2, cb0, cb1)
        if d == 0:
            coef = cbh * dt
            xsh = xs
        else:
            dec = jnp.exp(jnp.where(ok, acum - pltpu.roll(acum, d, 0), NEG))
            coef = cbh * dec * pltpu.roll(dt, d, 0)
            xsh = pltpu.roll(xs, d, 0)
        y_acc = y_acc + expand(jnp.where(ok, coef, 0.0)) * xsh
    yd_ref[...] = y_acc
    eac_ref[...] = expand(jnp.exp(acum))
    xw_ref[...] = xs * expand(dt * jnp.exp(rem))
    bm_ref[...] = bm
    cm_ref[...] = cm
    cd_ref[...] = jnp.exp(acum + rem)


def _ssd_sample_pre(xbc8, prev8, dt8, cw8, cb, dtb, alog, dsk, ex, rb):
    r = xbc8.shape[0]
    row = lambda i: (i, 0)
    const = lambda i: (0, 0)
    widths = (D_INNER, D_INNER, D_INNER, 2 * D_STATE, 2 * D_STATE, LANES)
    return pl.pallas_call(
        _ssd_sample_pre_kernel,
        grid=(r // rb,),
        out_shape=[jax.ShapeDtypeStruct((r, w_), F32) for w_ in widths],
        in_specs=[pl.BlockSpec((rb, CONV_DIM), row),
                  pl.BlockSpec((rb, CONV_DIM), row),
                  pl.BlockSpec((rb, LANES), row),
                  pl.BlockSpec((SUBLANES, CONV_DIM), const),
                  pl.BlockSpec((1, CONV_DIM), const),
                  pl.BlockSpec((1, LANES), const),
                  pl.BlockSpec((1, LANES), const),
                  pl.BlockSpec((1, D_INNER), const),
                  pl.BlockSpec((LANES, D_INNER), const)],
        out_specs=[pl.BlockSpec((rb, w_), row) for w_ in widths],
        compiler_params=_cparams(("parallel",)),
        name="ssd_sample_pre",
    )(xbc8, prev8, dt8, cw8, cb, dtb, alog, dsk, ex)


def _ssd_sample_state_kernel(cd_ref, yd_ref, eac_ref, xw_ref, bm_ref, cm_ref, z_ref, h0_ref, ng_ref,
                             ob_ref, hn_ref, *, nbs):
    base = pl.program_id(0) * nbs
    yo = []
    for n in range(nbs):
        rows = slice(n * SROWS, (n + 1) * SROWS)
        parts = []
        for g in range(2):
            hp = h0_ref[n, 4 * g:4 * g + 4].reshape(4 * HEAD_DIM, D_STATE)
            cmg = cm_ref[rows, g * D_STATE:(g + 1) * D_STATE].astype(BF16)
            bmg = bm_ref[rows, g * D_STATE:(g + 1) * D_STATE].astype(BF16)
            parts.append(_dot_t(cmg, hp.astype(BF16)))
            s_new = _dot_t0(xw_ref[rows, g * 256:(g + 1) * 256].astype(BF16), bmg)
            for hh in range(4):
                h = 4 * g + hh
                hn_ref[n, h] = (hp[hh * HEAD_DIM:(hh + 1) * HEAD_DIM, :] * cd_ref[base + n, h]
                                + s_new[hh * HEAD_DIM:(hh + 1) * HEAD_DIM, :])
        yo.append(jnp.concatenate(parts, axis=1))
    y = yd_ref[...] + eac_ref[...] * jnp.concatenate(yo, axis=0)
    ob_ref[...] = _gated_group_norm(y, z_ref[...], ng_ref[...])


def _ssd_sample_state(cd, yd, eac, xw, bm, cm, z8, h0, ng, nbs):
    ns = h0.shape[0]
    row = lambda i: (i, 0)
    st = lambda i: (i, 0, 0, 0)
    rb = nbs * SROWS
    return pl.pallas_call(
        functools.partial(_ssd_sample_state_kernel, nbs=nbs),
        grid=(ns // nbs,),
        out_shape=[jax.ShapeDtypeStruct((ns * SROWS, D_INNER), F32),
                   jax.ShapeDtypeStruct(h0.shape, F32)],
        in_specs=[pl.BlockSpec(memory_space=pltpu.SMEM),
                  pl.BlockSpec((rb, D_INNER), row),
                  pl.BlockSpec((rb, D_INNER), row),
                  pl.BlockSpec((rb, D_INNER), row),
                  pl.BlockSpec((rb, 2 * D_STATE), row),
                  pl.BlockSpec((rb, 2 * D_STATE), row),
                  pl.BlockSpec((rb, D_INNER), row),
                  pl.BlockSpec((nbs, N_SSM_HEADS, HEAD_DIM, D_STATE), st),
                  pl.BlockSpec((1, D_INNER), lambda i: (0, 0))],
        out_specs=[pl.BlockSpec((rb, D_INNER), row),
                   pl.BlockSpec((nbs, N_SSM_HEADS, HEAD_DIM, D_STATE), st)],
        compiler_params=_cparams(("parallel",)),
        name="ssd_sample_state",
    )(cd, yd, eac, xw, bm, cm, z8, h0, ng)


def _prompt_buckets(dil):
    qi = jnp.arange(BLOCK)[:, None]
    kj = jnp.arange(2 * BLOCK)[None, :]
    dist = BLOCK + qi - kj
    return _masked_bucket(dist * dil, (dist >= 0) & (dist <= WIN))


def _sample_row_tokens():
    r = jnp.arange(4 * SROWS) % SROWS
    return jnp.where(r >= SVALID, r - SVALID, -1)[:, None]


def _sample_buckets_a():
    t = _sample_row_tokens()
    j = jnp.arange(WIN)[None, :]
    dist = WIN + t - j
    bc = jnp.where(t >= 0, _masked_bucket(dist, (dist >= 0) & (dist <= WIN)), 0)
    r2 = jnp.arange(SROWS)[None, :]
    dn = t - (r2 - SVALID)
    bn = jnp.where(t >= 0, _masked_bucket(dn, (r2 >= SVALID) & (dn >= 0)), 0)
    return bc.astype(jnp.int32), bn.astype(jnp.int32)


def _sample_buckets_c():
    t = _sample_row_tokens()
    j = jnp.arange(R1_ROWS)[None, :]
    dist = R1_ROWS + t - j
    b1 = jnp.where(t >= 0, _masked_bucket(dist, (dist >= 0) & (dist <= 128)), 0)
    b2 = jnp.where(t >= 0, _masked_bucket(dist, (dist >= 0) & (dist <= 512) & (dist % 4 == 0)), 0)
    j3 = jnp.arange(R3_ROWS)[None, :]
    b3 = jnp.where(t >= 0, _masked_bucket(16 * (R3_ROWS - j3) + 0 * t, j3 >= 0), 0)
    r2 = jnp.arange(SROWS)[None, :]
    dn = t - (r2 - SVALID)
    new_ok = (r2 >= SVALID) & (dn >= 0)
    n1 = jnp.where(t >= 0, _masked_bucket(dn, new_ok), 0)
    n23 = jnp.where(t >= 0, _masked_bucket(dn, new_ok & (dn == 0)), 0)
    i32 = lambda v: v.astype(jnp.int32)
    return i32(b1), i32(b2), i32(b3), i32(n1), i32(n23)


def _rows_by_head(bias):
    h = bias.shape[0]
    parts = [bias[i, i * SROWS:(i + 1) * SROWS] for i in range(h)]
    return jnp.concatenate(parts, axis=0)[None]


def _pad_rows8(x):
    n, r, c = x.shape
    return jnp.pad(x, ((0, 0), (SROWS - r, 0), (0, 0)))


def kernel(x_prompt, x_sample, cache_a_k, cache_a_v, cache_c_k, cache_c_v, state_ssm, state_conv,
           norm_mix_g, w_in, a_q_norm_g, a_k_norm_g, a_sinks, c_q_norm_g, c_k_norm_g, rel_bias,
           conv_w, conv_b, dt_bias, a_log, d_skip, ssm_norm_g, w_out, norm_mlp_g, w_up, w_down):
    depth = w_in.shape[0]
    n, seq, _ = x_prompt.shape
    ns, ts, _ = x_sample.shape
    assert ts == SROWS - SVALID and seq % (BLOCK * 16) == 0
    nb = seq // BLOCK

    w_in_b = jnp.pad(w_in, ((0, 0), (0, 0), (0, COL_END - IN_COLS))).astype(BF16)
    w_out_b, w_up_b, w_down_b = w_out.astype(BF16), w_up.astype(BF16), w_down.astype(BF16)
    ones = jnp.ones((depth, 128), F32)
    gain = jnp.concatenate([jnp.tile(a_q_norm_g, (1, 4)) * ATTN_SCALE, jnp.tile(a_k_norm_g, (1, 2)), ones,
                            jnp.tile(c_q_norm_g, (1, 4)) * ATTN_SCALE, jnp.tile(c_k_norm_g, (1, 4))], axis=1)
    lane_grp = jnp.arange(LANES) // HEAD_DIM
    bd = (lane_grp[:, None] == lane_grp[None, :]).astype(BF16)
    tri = (jnp.arange(BLOCK)[None, :] <= jnp.arange(BLOCK)[:, None]).astype(F32)
    ex = (jnp.arange(LANES)[:, None] == (jnp.arange(D_INNER) // HEAD_DIM)[None, :]).astype(F32)
    cw8 = jnp.pad(conv_w, ((0, 0), (0, SUBLANES - conv_w.shape[1]), (0, 0)))
    pad_h = lambda v: jnp.pad(v, ((0, 0), (0, LANES - N_SSM_HEADS)))
    dtb, alog = pad_h(dt_bias), pad_h(a_log)
    dsk = jnp.repeat(d_skip, HEAD_DIM, axis=1)

    bias_a = _expand_bias(rel_bias, _prompt_buckets(1), 4, 0)
    bias_c = jnp.stack([_expand_bias(rel_bias, _prompt_buckets(d), 4, 4) for d in C_DILS])
    sa_c, sa_n = _sample_buckets_a()
    sbias_a_c = _rows_by_head(_expand_bias(rel_bias, sa_c, 4, 0))
    sbias_a_n = _rows_by_head(_expand_bias(rel_bias, sa_n, 4, 0))
    sc1, sc2, sc3, sn1, sn23 = _sample_buckets_c()
    sb1 = _rows_by_head(_expand_bias(rel_bias, sc1, 4, 4))
    sb2 = _rows_by_head(_expand_bias(rel_bias, sc2, 4, 4))
    sb3 = _rows_by_head(_expand_bias(rel_bias, sc3, 4, 4))
    sbn1 = _rows_by_head(_expand_bias(rel_bias, sn1, 4, 4))
    sbn23 = _rows_by_head(_expand_bias(rel_bias, sn23, 4, 4))
    sbn = jnp.concatenate([sbn1, sbn23, sbn23], axis=0)

    hp = x_prompt.reshape(n * seq, D_MODEL)
    hs = _pad_rows8(x_sample).reshape(ns * SROWS, D_MODEL)
    h0_zero = jnp.zeros((n, N_SSM_HEADS, HEAD_DIM, D_STATE), F32)
    prev_zero = jnp.zeros((n, SUBLANES, CONV_DIM), F32)
    tm_p = 512
    tm_s = min(512, ns * SROWS)
    nbs_a = min(16, ns)
    nbs_c = min(4, ns)
    nbs_b = min(8, ns)

    st_p = [[] for _ in range(6)]
    st_s = [[] for _ in range(6)]
    for l in range(depth):
        vec = lambda p: p[l][None, :]
        qa, ka, va, qc, kc, vc, z, xbc, dt = _in_proj(hp, vec(norm_mix_g), w_in_b[l], vec(gain), bd, tm_p)
        oa = _attn_a_prompt(qa, ka, va, bias_a, a_sinks[l], n, nb)
        oc = _attn_c_prompt(qc, kc, vc, bias_c, n, seq)
        ob, h_last = _ssd_prompt(xbc, z, dt, h0_zero, prev_zero, cw8[l], vec(conv_b), vec(dtb), vec(alog),
                                 vec(dsk), vec(ssm_norm_g), tri, n, nb)
        hp = _out_mlp(hp, oa, ob, oc, w_out_b[l], vec(norm_mlp_g), w_up_b[l], w_down_b[l], tm_p)
        la = min(WIN, seq)
        st_p[0].append(ka.reshape(n, seq, 2, HEAD_DIM)[:, seq - la:])
        st_p[1].append(va.reshape(n, seq, 2, HEAD_DIM)[:, seq - la:])
        st_p[2].append(kc.reshape(n, seq, 4, HEAD_DIM))
        st_p[3].append(vc.reshape(n, seq, 4, HEAD_DIM))
        st_p[4].append(h_last)
        st_p[5].append(xbc.reshape(n, seq, CONV_DIM)[:, seq - 3:])

        qa, ka, va, qc, kc, vc, z, xbc, dt = _in_proj(hs, vec(norm_mix_g), w_in_b[l], vec(gain), bd, tm_s)
        oa = _attn_a_sample(qa, ka, va, cache_a_k[l].reshape(ns, WIN, LANES), cache_a_v[l].reshape(ns, WIN, LANES),
                            sbias_a_c, sbias_a_n, a_sinks[l], nbs_a)
        lc = cache_c_k.shape[2]
        oc = _attn_c_sample(qc, kc, vc, cache_c_k[l].reshape(ns, lc, 256), cache_c_v[l].reshape(ns, lc, 256),
                            sb1, sb2, sb3, sbn, nbs_c)
        prev8 = jnp.pad(state_conv[l], ((0, 0), (1, SROWS - 4), (0, 0))).reshape(ns * SROWS, CONV_DIM)
        yd, eac, xw, bm, cm, cd = _ssd_sample_pre(xbc, prev8, dt, cw8[l], vec(conv_b), vec(dtb), vec(alog),
                                                  vec(dsk), ex, min(256, ns * SROWS))
        cd_s = cd.reshape(ns, SROWS, LANES)[:, SROWS - 1, :N_SSM_HEADS]
        ob, h_new = _ssd_sample_state(cd_s, yd, eac, xw, bm, cm, z, state_ssm[l], vec(ssm_norm_g), nbs_b)
        hs = _out_mlp(hs, oa, ob, oc, w_out_b[l], vec(norm_mlp_g), w_up_b[l], w_down_b[l], tm_s)
        new = lambda v, hh: v.reshape(ns, SROWS, hh, HEAD_DIM)[:, SVALID:]
        st_s[0].append(new(ka, 2))
        st_s[1].append(new(va, 2))
        st_s[2].append(new(kc, 4))
        st_s[3].append(new(vc, 4))
        st_s[4].append(h_new)
        st_s[5].append(xbc.reshape(ns, SROWS, CONV_DIM)[:, SROWS - 3:])

    y_prompt = hp.reshape(n, seq, D_MODEL)
    y_sample = hs.reshape(ns, SROWS, D_MODEL)[:, SVALID:]
    stack = lambda parts: tuple(jnp.stack(p) for p in parts)
    return (y_prompt, y_sample) + stack(st_p) + stack(st_s)
```

```python
import functools
import math

import jax
import jax.numpy as jnp
from jax import lax
from jax.experimental import pallas as pl
from jax.experimental.pallas import tpu as pltpu

F32 = jnp.float32
BF16 = jnp.bfloat16
HIGHEST = lax.Precision.HIGHEST

D_MODEL = 1024
HEAD_DIM = 64
LANES = 128
SUBLANES = 8
BLOCK = 128
WIN = 128
D_INNER = 512
D_STATE = 128
N_SSM_HEADS = 8
CONV_DIM = 1024
D_FF = 4096
NUM_BUCKETS = 32
REL_MAX_DIST = 2048
EPS = 1e-6
ATTN_SCALE = HEAD_DIM ** -0.5
NEG = -1e30
C_BRANCHES = ((128, 1), (512, 4), (2048, 16))
C_DILS = tuple(d for _, d in C_BRANCHES)
SROWS = 8
SVALID = 4
NEAR = 512

COL_QA, COL_KA, COL_VA, COL_QC, COL_KC, COL_VC, COL_Z, COL_XBC, COL_DT, COL_END = (
    0, 256, 384, 512, 768, 1024, 1280, 1792, 2816, 2944)
IN_COLS = 2824
VMEM_LIMIT = 56 * 1024 * 1024


def _cparams(sem):
    return pltpu.CompilerParams(dimension_semantics=sem, vmem_limit_bytes=VMEM_LIMIT)


def _dot(a, b):
    return jnp.dot(a, b, preferred_element_type=F32)


def _dot_t(a, b):
    return lax.dot_general(a, b, (((1,), (1,)), ((), ())), preferred_element_type=F32)


def _dot_t0(a, b):
    return lax.dot_general(a, b, (((0,), (0,)), ((), ())), preferred_element_type=F32)


def _lane_iota(shape):
    return lax.broadcasted_iota(jnp.int32, shape, len(shape) - 1)


def _silu(x):
    return x * (1.0 / (1.0 + jnp.exp(-x)))


def _softplus(x):
    return jnp.maximum(x, 0.0) + jnp.log(1.0 + jnp.exp(-jnp.abs(x)))


def _stacked(body, n_in, bufs):
    if bufs is None:
        return body, [], [], 0
    k = len(bufs)

    def wrapped(*refs):
        return body(*refs[:n_in], *refs[n_in + k:])

    return wrapped, [pl.BlockSpec(memory_space=pl.ANY)] * k, list(bufs), k


def _alias_map(n_in, k, n_out):
    return {n_in + i: n_out - k + i for i in range(k)}


def _bias_kernel(tab_ref, bkt_ref, o_ref, *, heads, col0):
    bkt = bkt_ref[...]
    for h in range(heads):
        acc = jnp.full(bkt.shape, NEG, F32)
        for b in range(NUM_BUCKETS):
            acc = jnp.where(bkt == b, tab_ref[b, col0 + h], acc)
        o_ref[h] = acc


def _expand_bias(table, bkt, heads, col0):
    r, c = bkt.shape
    return pl.pallas_call(
        functools.partial(_bias_kernel, heads=heads, col0=col0),
        out_shape=jax.ShapeDtypeStruct((heads, r, c), F32),
        in_specs=[pl.BlockSpec(memory_space=pltpu.SMEM),
                  pl.BlockSpec((r, c), lambda: (0, 0))],
        out_specs=pl.BlockSpec((heads, r, c), lambda: (0, 0, 0)),
        name="bias_expand",
    )(table, bkt)


def _t5_bucket(dist):
    max_exact = NUM_BUCKETS // 2
    d = jnp.maximum(dist, 0)
    df = jnp.maximum(d, 1).astype(F32)
    large = max_exact + (jnp.log(df / max_exact) / math.log(REL_MAX_DIST / max_exact)
                         * (NUM_BUCKETS - max_exact)).astype(jnp.int32)
    return jnp.where(d < max_exact, d, jnp.minimum(large, NUM_BUCKETS - 1))


def _masked_bucket(dist, valid):
    return jnp.where(valid, _t5_bucket(dist), -1).astype(jnp.int32)


def _head_norm(slab, gain, bd):
    sq = slab * slab
    hi = sq.astype(BF16)
    lo = (sq - hi.astype(F32)).astype(BF16)
    ss = _dot(hi, bd) + _dot(lo, bd)
    return slab * lax.rsqrt(ss * (1.0 / HEAD_DIM) + EPS) * gain


def _in_proj_kernel(h_ref, g_ref, w_ref, gain_ref, bd_ref,
                    qa_ref, ka_ref, va_ref, qc_ref, kc_ref, vc_ref, z_ref, xbc_ref, dt_ref):
    x = h_ref[...]
    ms = jnp.mean(x * x, axis=-1, keepdims=True)
    u = (x * lax.rsqrt(ms + EPS) * g_ref[...]).astype(BF16)
    bd = bd_ref[...]

    def proj(c0, c1):
        return _dot(u, w_ref[:, c0:c1])

    def normed(c0, c1, out_ref):
        p = proj(c0, c1)
        for s in range((c1 - c0) // LANES):
            sl = slice(s * LANES, (s + 1) * LANES)
            out_ref[:, sl] = _head_norm(p[:, sl], gain_ref[:, c0 + s * LANES:c0 + (s + 1) * LANES], bd)

    normed(COL_QA, COL_KA, qa_ref)
    normed(COL_KA, COL_VA, ka_ref)
    va_ref[...] = proj(COL_VA, COL_QC)
    normed(COL_QC, COL_KC, qc_ref)
    normed(COL_KC, COL_VC, kc_ref)
    vc_ref[...] = proj(COL_VC, COL_Z)
    z_ref[...] = proj(COL_Z, COL_XBC)
    xbc_ref[...] = proj(COL_XBC, COL_DT)
    dt_ref[...] = proj(COL_DT, COL_END)


def _in_proj(h, g, w, gain, bd, l, tm):
    t = h.shape[0]
    widths = (256, 128, 128, 256, 256, 256, 512, 1024, 128)
    layer = lambda i: (l, 0, 0)
    return pl.pallas_call(
        _in_proj_kernel,
        grid=(t // tm,),
        out_shape=[jax.ShapeDtypeStruct((t, w_), F32) for w_ in widths],
        in_specs=[pl.BlockSpec((tm, D_MODEL), lambda i: (i, 0)),
                  pl.BlockSpec((None, 1, D_MODEL), layer),
                  pl.BlockSpec((None, D_MODEL, COL_END), layer),
                  pl.BlockSpec((None, 1, D_MODEL), layer),
                  pl.BlockSpec((LANES, LANES), lambda i: (0, 0))],
        out_specs=[pl.BlockSpec((tm, w_), lambda i: (i, 0)) for w_ in widths],
        compiler_params=_cparams(("parallel",)),
        name="in_proj",
    )(h, g, w, gain, bd)


def _out_mlp_kernel(h_ref, oa_ref, ob_ref, oc_ref, wo_ref, g_ref, wu_ref, wd_ref, o_ref):
    acc = _dot(oa_ref[...].astype(BF16), wo_ref[0:256, :])
    acc += _dot(ob_ref[...].astype(BF16), wo_ref[256:768, :])
    acc += _dot(oc_ref[...].astype(BF16), wo_ref[768:1024, :])
    h2 = h_ref[...] + acc
    ms = jnp.mean(h2 * h2, axis=-1, keepdims=True)
    u = (h2 * lax.rsqrt(ms + EPS) * g_ref[...]).astype(BF16)
    ff_chunk = 1024
    mlp = jnp.zeros_like(h2)
    for c in range(D_FF // ff_chunk):
        a = _dot(u, wu_ref[:, c * ff_chunk:(c + 1) * ff_chunk])
        a = jnp.square(jnp.maximum(a, 0.0)).astype(BF16)
        mlp += _dot(a, wd_ref[c * ff_chunk:(c + 1) * ff_chunk, :])
    o_ref[...] = h2 + mlp


def _out_mlp(h, oa, ob, oc, wo, g, wu, wd, l, tm):
    t = h.shape[0]
    layer = lambda i: (l, 0, 0)
    row = lambda i: (i, 0)
    return pl.pallas_call(
        _out_mlp_kernel,
        grid=(t // tm,),
        out_shape=jax.ShapeDtypeStruct((t, D_MODEL), F32),
        in_specs=[pl.BlockSpec((tm, D_MODEL), row),
                  pl.BlockSpec((tm, 256), row),
                  pl.BlockSpec((tm, 512), row),
                  pl.BlockSpec((tm, 256), row),
                  pl.BlockSpec((None, D_MODEL, D_MODEL), layer),
                  pl.BlockSpec((None, 1, D_MODEL), layer),
                  pl.BlockSpec((None, D_MODEL, D_FF), layer),
                  pl.BlockSpec((None, D_FF, D_MODEL), layer)],
        out_specs=pl.BlockSpec((tm, D_MODEL), row),
        compiler_params=_cparams(("parallel",)),
        name="out_mlp",
    )(h, oa, ob, oc, wo, g, wu, wd)


def _attn_a_prompt_kernel(sink_ref, q_ref, kc_ref, kp_ref, vc_ref, vp_ref, bias_ref, o_ref, kt_ref, vt_ref, *, l):
    first = pl.program_id(1) == 0
    lane = _lane_iota((BLOCK, LANES))
    col = _lane_iota((BLOCK, 2 * BLOCK))
    lo = lane < HEAD_DIM
    k = jnp.concatenate([kp_ref[...], kc_ref[...]], axis=0)
    v = jnp.concatenate([vp_ref[...], vc_ref[...]], axis=0)
    k_sw = pltpu.roll(k, HEAD_DIM, 1).astype(BF16)
    v_sw = pltpu.roll(v, HEAD_DIM, 1).astype(BF16)
    k = k.astype(BF16)
    v = v.astype(BF16)
    outs = []
    for h in range(4):
        slab = q_ref[:, (h // 2) * LANES:(h // 2 + 1) * LANES]
        qm = jnp.where(lo if h % 2 == 0 else ~lo, slab, 0.0).astype(BF16)
        same = h in (0, 3)
        s = _dot_t(qm, k if same else k_sw) + bias_ref[h]
        s = jnp.where(jnp.logical_and(first, col < BLOCK), NEG, s)
        sink = sink_ref[l, h]
        m = jnp.maximum(jnp.max(s, axis=-1, keepdims=True), sink)
        p = jnp.exp(s - m)
        den = jnp.sum(p, axis=-1, keepdims=True) + jnp.exp(sink - m)
        outs.append(_dot(p.astype(BF16), v if same else v_sw) / den)
    o_ref[:, 0:LANES] = jnp.where(lo, outs[0], outs[1])
    o_ref[:, LANES:2 * LANES] = jnp.where(lo, outs[2], outs[3])

    @pl.when(pl.program_id(1) == pl.num_programs(1) - 1)
    def _():
        kt_ref[...] = kc_ref[...].T
        vt_ref[...] = vc_ref[...].T


def _attn_a_prompt(q, k, v, bias, sinks, l, depth, n, nb, bufs):
    t = q.shape[0]
    cur = lambda i, b: (i * nb + b, 0)
    prev = lambda i, b: (i * nb + jnp.maximum(b - 1, 0), 0)
    n_in = 7
    body, xspecs, xargs, k_alias = _stacked(functools.partial(_attn_a_prompt_kernel, l=l), n_in, bufs)
    st_shape = jax.ShapeDtypeStruct((depth, n, LANES, WIN), F32)
    st_spec = pl.BlockSpec((None, None, LANES, WIN), lambda i, b: (l, i, 0, 0))
    return pl.pallas_call(
        body,
        grid=(n, nb),
        out_shape=[jax.ShapeDtypeStruct((t, 256), F32), st_shape, st_shape],
        in_specs=[pl.BlockSpec(memory_space=pltpu.SMEM),
                  pl.BlockSpec((BLOCK, 256), cur),
                  pl.BlockSpec((BLOCK, LANES), cur),
                  pl.BlockSpec((BLOCK, LANES), prev),
                  pl.BlockSpec((BLOCK, LANES), cur),
                  pl.BlockSpec((BLOCK, LANES), prev),
                  pl.BlockSpec((4, BLOCK, 2 * BLOCK), lambda i, b: (0, 0, 0))] + xspecs,
        out_specs=[pl.BlockSpec((BLOCK, 256), cur), st_spec, st_spec],
        input_output_aliases=_alias_map(n_in, k_alias, 3) if k_alias else {},
        compiler_params=_cparams(("parallel", "arbitrary")),
        name="attn_a_prompt",
    )(sinks, q, k, k, v, v, bias, *xargs)


def _attn_c_prompt_kernel(q_ref, k_ref, v_ref, bias_ref, o_ref, kt_ref, vt_ref, ob_scr, lse_scr, *, seq):
    lane = _lane_iota((BLOCK, LANES))
    lo = lane < HEAD_DIM
    kt_ref[...] = k_ref[...].T
    vt_ref[...] = v_ref[...].T

    def block(br, dil, qs, ps, first, has_prev):
        q = q_ref[pl.ds(qs, BLOCK, stride=dil), :]
        kc = k_ref[pl.ds(qs, BLOCK, stride=dil), :].astype(BF16)
        vc = v_ref[pl.ds(qs, BLOCK, stride=dil), :].astype(BF16)
        if has_prev:
            kp = k_ref[pl.ds(ps, BLOCK, stride=dil), :].astype(BF16)
            vp = v_ref[pl.ds(ps, BLOCK, stride=dil), :].astype(BF16)
        outs, lses = [], []
        for hh in range(2):
            qm = jnp.where(lo if hh == 0 else ~lo, q, 0.0).astype(BF16)
            s_c = _dot_t(qm, kc) + bias_ref[br, hh, :, BLOCK:2 * BLOCK]
            m = jnp.max(s_c, axis=-1, keepdims=True)
            if has_prev:
                s_p = _dot_t(qm, kp) + bias_ref[br, hh, :, 0:BLOCK]
                s_p = jnp.where(first, NEG, s_p)
                m = jnp.maximum(m, jnp.max(s_p, axis=-1, keepdims=True))
            p_c = jnp.exp(s_c - m)
            den = jnp.sum(p_c, axis=-1, keepdims=True)
            o = _dot(p_c.astype(BF16), vc)
            if has_prev:
                p_p = jnp.exp(s_p - m)
                den = den + jnp.sum(p_p, axis=-1, keepdims=True)
                o = o + _dot(p_p.astype(BF16), vp)
            outs.append(o / den)
            lses.append(jnp.broadcast_to(m + jnp.log(den), (BLOCK, LANES)))
        ob_scr[br, pl.ds(qs, BLOCK, stride=dil), :] = jnp.where(lo, outs[0], outs[1])
        lse_scr[br, pl.ds(qs, BLOCK, stride=dil), :] = jnp.where(lo, lses[0], lses[1])

    for br, dil in enumerate(C_DILS):
        nb = seq // dil // BLOCK
        span = BLOCK * dil

        def body(i, carry, br=br, dil=dil, nb=nb, span=span):
            res = i % dil
            b = i // dil
            qs = res + b * span
            ps = res + jnp.maximum(b - 1, 0) * span
            block(br, dil, qs, ps, b == 0, nb > 1)
            return carry

        lax.fori_loop(0, dil * nb, body, 0)

    def merge(i, carry):
        rows = pl.ds(pl.multiple_of(i * BLOCK, BLOCK), BLOCK)
        l0, l1, l2 = lse_scr[0, rows, :], lse_scr[1, rows, :], lse_scr[2, rows, :]
        m = jnp.maximum(jnp.maximum(l0, l1), l2)
        w0, w1, w2 = jnp.exp(l0 - m), jnp.exp(l1 - m), jnp.exp(l2 - m)
        num = w0 * ob_scr[0, rows, :] + w1 * ob_scr[1, rows, :] + w2 * ob_scr[2, rows, :]
        o_ref[rows, :] = num / (w0 + w1 + w2)
        return carry

    lax.fori_loop(0, seq // BLOCK, merge, 0)


def _attn_c_prompt(q, k, v, bias, l, depth, n, seq, bufs):
    t = q.shape[0]
    blk = pl.BlockSpec((seq, LANES), lambda i, hp: (i, hp))
    n_in = 4
    body, xspecs, xargs, k_alias = _stacked(functools.partial(_attn_c_prompt_kernel, seq=seq), n_in, bufs)
    st_shape = jax.ShapeDtypeStruct((depth, n, 256, seq), F32)
    st_spec = pl.BlockSpec((None, None, LANES, seq), lambda i, hp: (l, i, hp, 0))
    return pl.pallas_call(
        body,
        grid=(n, 2),
        out_shape=[jax.ShapeDtypeStruct((t, 256), F32), st_shape, st_shape],
        in_specs=[blk, blk, blk,
                  pl.BlockSpec((3, 2, BLOCK, 2 * BLOCK), lambda i, hp: (0, hp, 0, 0))] + xspecs,
        out_specs=[blk, st_spec, st_spec],
        scratch_shapes=[pltpu.VMEM((3, seq, LANES), F32), pltpu.VMEM((3, seq, LANES), F32)],
        input_output_aliases=_alias_map(n_in, k_alias, 3) if k_alias else {},
        compiler_params=_cparams(("parallel", "parallel")),
        name="attn_c_prompt",
    )(q, k, v, bias, *xargs)


def _gated_group_norm(y, z, ng):
    gated = y * _silu(z)
    parts = []
    for grp in range(2):
        gsl = gated[:, grp * 256:(grp + 1) * 256]
        ms = jnp.mean(gsl * gsl, axis=-1, keepdims=True)
        parts.append(gsl * lax.rsqrt(ms + EPS))
    return jnp.concatenate(parts, axis=1) * ng


def _ssd_prompt_kernel(xbc_ref, z_ref, dt_ref, cw_ref, cb_ref, dtb_ref, alog_ref,
                       dsk_ref, ng_ref, tri_ref, ob_ref, hl_ref, hst, xbuf):
    c = pl.program_id(1)

    @pl.when(c == 0)
    def _():
        hst[...] = jnp.zeros_like(hst)
        xbuf[0:SUBLANES, :] = jnp.zeros((SUBLANES, CONV_DIM), F32)

    x = xbc_ref[...]
    xbuf[SUBLANES:SUBLANES + BLOCK, :] = x
    y = cb_ref[...] + cw_ref[3:4, :] * x
    for kk in range(3):
        y = y + cw_ref[kk:kk + 1, :] * xbuf[SUBLANES - 3 + kk:SUBLANES - 3 + kk + BLOCK, :]
    xbuf[0:SUBLANES, :] = x[BLOCK - SUBLANES:BLOCK, :]
    xa = _silu(y)
    xs = xa[:, 0:D_INNER]
    dt = _softplus(dt_ref[...] + dtb_ref[...])
    a = dt * (-jnp.exp(alog_ref[...]))
    acum = jnp.dot(tri_ref[...], a, precision=HIGHEST, preferred_element_type=F32)
    acum_t = acum.T
    dt_t = dt.T
    eac = jnp.exp(acum)
    last = acum[BLOCK - 1:BLOCK, :]
    w_end = jnp.exp(last - acum) * dt
    e_last = jnp.exp(last)
    row = lax.broadcasted_iota(jnp.int32, (BLOCK, BLOCK), 0)
    lane = _lane_iota((BLOCK, BLOCK))
    causal = lane <= row
    lo = lane < HEAD_DIM
    top = row < HEAD_DIM
    y_parts = []
    for j in range(N_SSM_HEADS // 2):
        g = j // 2
        bm = xa[:, D_INNER + g * D_STATE:D_INNER + (g + 1) * D_STATE].astype(BF16)
        cm = xa[:, D_INNER + 2 * D_STATE + g * D_STATE:D_INNER + 2 * D_STATE + (g + 1) * D_STATE].astype(BF16)
        cb = _dot_t(cm, bm)
        xs_pair = xs[:, j * LANES:(j + 1) * LANES]
        xs_b = xs_pair.astype(BF16)
        yd = []
        for hh in range(2):
            h = 2 * j + hh
            seg = acum[:, h:h + 1] - acum_t[h:h + 1, :]
            dec = jnp.exp(jnp.where(causal, seg, NEG)) * dt_t[h:h + 1, :]
            yd.append(_dot((cb * dec).astype(BF16), xs_b))
        h0_, h1_ = 2 * j, 2 * j + 1
        hp = hst[h0_:h0_ + 2].reshape(BLOCK, D_STATE)
        e_pair = jnp.where(lo, eac[:, h0_:h0_ + 1], eac[:, h1_:h1_ + 1])
        y_parts.append(jnp.where(lo, yd[0], yd[1]) + _dot_t(cm, hp.astype(BF16)) * e_pair)
        w_pair = jnp.where(lo, w_end[:, h0_:h0_ + 1], w_end[:, h1_:h1_ + 1])
        s_new = _dot_t0((xs_pair * w_pair).astype(BF16), bm)
        cd_pair = jnp.where(top, e_last[:, h0_:h0_ + 1], e_last[:, h1_:h1_ + 1])
        hst[h0_:h0_ + 2] = (hp * cd_pair + s_new).reshape(2, HEAD_DIM, D_STATE)
    y = jnp.concatenate(y_parts, axis=1) + dsk_ref[...] * xs
    ob_ref[...] = _gated_group_norm(y, z_ref[...], ng_ref[...])

    @pl.when(c == pl.num_programs(1) - 1)
    def _():
        hl_ref[...] = hst[...]


def _ssd_prompt(xbc, z, dt, cw8, cb, dtb, alog, dsk, ng, tri, l, depth, n, nc, bufs):
    t = xbc.shape[0]
    row = lambda i, c: (i * nc + c, 0)
    layer = lambda i, c: (l, 0, 0)
    n_in = 10
    body, xspecs, xargs, k_alias = _stacked(_ssd_prompt_kernel, n_in, bufs)
    st = (N_SSM_HEADS, HEAD_DIM, D_STATE)
    return pl.pallas_call(
        body,
        grid=(n, nc),
        out_shape=[jax.ShapeDtypeStruct((t, D_INNER), F32),
                   jax.ShapeDtypeStruct((depth, n) + st, F32)],
        in_specs=[pl.BlockSpec((BLOCK, CONV_DIM), row),
                  pl.BlockSpec((BLOCK, D_INNER), row),
                  pl.BlockSpec((BLOCK, LANES), row),
                  pl.BlockSpec((None, SUBLANES, CONV_DIM), layer),
                  pl.BlockSpec((None, 1, CONV_DIM), layer),
                  pl.BlockSpec((None, 1, LANES), layer),
                  pl.BlockSpec((None, 1, LANES), layer),
                  pl.BlockSpec((None, 1, D_INNER), layer),
                  pl.BlockSpec((None, 1, D_INNER), layer),
                  pl.BlockSpec((BLOCK, BLOCK), lambda i, c: (0, 0))] + xspecs,
        out_specs=[pl.BlockSpec((BLOCK, D_INNER), row),
                   pl.BlockSpec((None, None) + st, lambda i, c: (l, i, 0, 0, 0))],
        scratch_shapes=[pltpu.VMEM(st, F32),
                        pltpu.VMEM((SUBLANES + BLOCK, CONV_DIM), F32)],
        input_output_aliases=_alias_map(n_in, k_alias, 2) if k_alias else {},
        compiler_params=_cparams(("parallel", "arbitrary")),
        name="ssd_prompt",
    )(xbc, z, dt, cw8, cb, dtb, alog, dsk, ng, tri, *xargs)


def _attn_a_sample_kernel(sink_ref, q_ref, kn_ref, vn_ref, kt_ref, vt_ref, bc_ref, bn_ref, o_ref, *, nbs, l):
    rb = nbs * SROWS
    lo = _lane_iota((rb, LANES)) < HEAD_DIM
    s0, s1 = q_ref[:, 0:LANES], q_ref[:, LANES:2 * LANES]
    per_seq = lambda v: v.reshape(nbs, SROWS, LANES)
    qm = jnp.concatenate([
        per_seq(jnp.where(lo, s0, 0.0)),
        per_seq(jnp.where(lo, pltpu.roll(s0, HEAD_DIM, 1), 0.0)),
        per_seq(jnp.where(lo, 0.0, pltpu.roll(s1, HEAD_DIM, 1))),
        per_seq(jnp.where(lo, 0.0, s1))], axis=1).astype(BF16)
    kt = kt_ref[...].reshape(nbs, LANES, WIN).astype(BF16)
    vt = vt_ref[...].reshape(nbs, LANES, WIN).astype(BF16)
    kn = kn_ref[...].reshape(nbs, SROWS, LANES).astype(BF16)
    vn = vn_ref[...].reshape(nbs, SROWS, LANES).astype(BF16)
    s_c = jnp.einsum('nqd,ndk->nqk', qm, kt, preferred_element_type=F32) + bc_ref[...]
    s_n = jnp.einsum('nqd,nkd->nqk', qm, kn, preferred_element_type=F32) + bn_ref[...]
    hrow = lax.broadcasted_iota(jnp.int32, (nbs, 4 * SROWS, 1), 1) // SROWS
    sink = jnp.where(hrow == 0, sink_ref[l, 0], jnp.where(hrow == 1, sink_ref[l, 1],
                     jnp.where(hrow == 2, sink_ref[l, 2], sink_ref[l, 3])))
    m = jnp.maximum(jnp.max(s_c, axis=-1, keepdims=True), jnp.max(s_n, axis=-1, keepdims=True))
    m = jnp.maximum(m, sink)
    p_c = jnp.exp(s_c - m)
    p_n = jnp.exp(s_n - m)
    den = jnp.sum(p_c, axis=-1, keepdims=True) + jnp.sum(p_n, axis=-1, keepdims=True) + jnp.exp(sink - m)
    o = (jnp.einsum('nqk,ndk->nqd', p_c.astype(BF16), vt, preferred_element_type=F32)
         + jnp.einsum('nqk,nkd->nqd', p_n.astype(BF16), vn, preferred_element_type=F32)) / den
    o0, o1, o2, o3 = (o[:, i * SROWS:(i + 1) * SROWS, :].reshape(rb, LANES) for i in range(4))
    o_ref[:, 0:LANES] = jnp.where(lo, o0, pltpu.roll(o1, HEAD_DIM, 1))
    o_ref[:, LANES:2 * LANES] = jnp.where(lo, pltpu.roll(o2, HEAD_DIM, 1), o3)


def _attn_a_sample(q8, k8, v8, cache_kt, cache_vt, bias_c, bias_n, sinks, l, nbs):
    ns = cache_kt.shape[1]
    tok = lambda i: (i, 0)
    cache = pl.BlockSpec((None, nbs, 2, HEAD_DIM, WIN), lambda i: (l, i, 0, 0, 0))
    c3 = lambda i: (0, 0, 0)
    return pl.pallas_call(
        functools.partial(_attn_a_sample_kernel, nbs=nbs, l=l),
        grid=(ns // nbs,),
        out_shape=jax.ShapeDtypeStruct((ns * SROWS, 256), F32),
        in_specs=[pl.BlockSpec(memory_space=pltpu.SMEM),
                  pl.BlockSpec((nbs * SROWS, 256), tok),
                  pl.BlockSpec((nbs * SROWS, LANES), tok),
                  pl.BlockSpec((nbs * SROWS, LANES), tok),
                  cache, cache,
                  pl.BlockSpec((1, 4 * SROWS, WIN), c3),
                  pl.BlockSpec((1, 4 * SROWS, SROWS), c3)],
        out_specs=pl.BlockSpec((nbs * SROWS, 256), tok),
        compiler_params=_cparams(("parallel",)),
        name="attn_a_sample",
    )(sinks, q8, k8, v8, cache_kt, cache_vt, bias_c, bias_n)


def _softmax_parts(parts):
    m = None
    for s in parts:
        mm = jnp.max(s, axis=-1, keepdims=True)
        m = mm if m is None else jnp.maximum(m, mm)
    ps = [jnp.exp(s - m) for s in parts]
    den = None
    for p in ps:
        dd = jnp.sum(p, axis=-1, keepdims=True)
        den = dd if den is None else den + dd
    return ps, den, m + jnp.log(den)


def _attn_c_sample_kernel(q_ref, kn_ref, vn_ref, kt_ref, vt_ref, b12_ref, b3_ref, bn_ref, o_ref, *, nbs, lc):
    lo = _lane_iota((SROWS, LANES)) < HEAD_DIM
    pr = 2 * SROWS
    for n in range(nbs):
        rows = slice(n * SROWS, (n + 1) * SROWS)
        for j in range(2):
            lanes = slice(j * LANES, (j + 1) * LANES)
            qp = q_ref[rows, lanes]
            lhs = jnp.concatenate([jnp.where(lo, qp, 0.0), jnp.where(lo, 0.0, qp)], axis=0).astype(BF16)
            ktp = kt_ref[n, 2 * j:2 * j + 2].reshape(LANES, lc).astype(BF16)
            vtp = vt_ref[n, 2 * j:2 * j + 2].reshape(LANES, lc).astype(BF16)
            knp = kn_ref[rows, lanes].astype(BF16)
            vnp = vn_ref[rows, lanes].astype(BF16)
            s = _dot(lhs, ktp)
            sn = _dot_t(lhs, knp)
            s_near = s[:, lc - NEAR:]
            (p1, p1n), d1, l1 = _softmax_parts([s_near + b12_ref[0, j], sn + bn_ref[0, j]])
            (p2, p2n), d2, l2 = _softmax_parts([s_near + b12_ref[1, j], sn + bn_ref[1, j]])
            (p3, p3n), d3, l3 = _softmax_parts([s + b3_ref[j], sn + bn_ref[2, j]])
            o12 = _dot_t(jnp.concatenate([p1, p2], axis=0).astype(BF16), vtp[:, lc - NEAR:])
            o1 = (o12[0:pr] + _dot(p1n.astype(BF16), vnp)) / d1
            o2 = (o12[pr:2 * pr] + _dot(p2n.astype(BF16), vnp)) / d2
            o3 = (_dot_t(p3.astype(BF16), vtp) + _dot(p3n.astype(BF16), vnp)) / d3
            m = jnp.maximum(jnp.maximum(l1, l2), l3)
            w1, w2, w3 = jnp.exp(l1 - m), jnp.exp(l2 - m), jnp.exp(l3 - m)
            o = (w1 * o1 + w2 * o2 + w3 * o3) / (w1 + w2 + w3)
            o_ref[rows, lanes] = jnp.where(lo, o[0:SROWS], o[SROWS:pr])


def _attn_c_sample(q8, k8, v8, cache_kt, cache_vt, b12, b3, bn, l, nbs):
    ns, lc = cache_kt.shape[1], cache_kt.shape[4]
    tok = lambda i: (i, 0)
    cache = pl.BlockSpec((None, nbs, 4, HEAD_DIM, lc), lambda i: (l, i, 0, 0, 0))
    c3 = lambda i: (0, 0, 0)
    c4 = lambda i: (0, 0, 0, 0)
    return pl.pallas_call(
        functools.partial(_attn_c_sample_kernel, nbs=nbs, lc=lc),
        grid=(ns // nbs,),
        out_shape=jax.ShapeDtypeStruct((ns * SROWS, 256), F32),
        in_specs=[pl.BlockSpec((nbs * SROWS, 256), tok),
                  pl.BlockSpec((nbs * SROWS, 256), tok),
                  pl.BlockSpec((nbs * SROWS, 256), tok),
                  cache, cache,
                  pl.BlockSpec((2, 2, 2 * SROWS, NEAR), c4),
                  pl.BlockSpec((2, 2 * SROWS, lc), c3),
                  pl.BlockSpec((3, 2, 2 * SROWS, SROWS), c4)],
        out_specs=pl.BlockSpec((nbs * SROWS, 256), tok),
        compiler_params=_cparams(("parallel",)),
        name="attn_c_sample",
    )(q8, k8, v8, cache_kt, cache_vt, b12, b3, bn)


def _ssd_sample_pre_kernel(xbc_ref, prev_ref, dt_ref, cw_ref, cb_ref, dtb_ref, alog_ref, dsk_ref, ex_ref,
                           yd_ref, eac_ref, xw_ref, bm_ref, cm_ref, cd_ref):
    r = xbc_ref.shape[0]
    rid = lax.broadcasted_iota(jnp.int32, (r, 1), 0) % SROWS
    x = jnp.where(jnp.logical_and(rid >= 1, rid < SVALID), prev_ref[...], xbc_ref[...])
    y = cb_ref[...] + cw_ref[3:4, :] * x
    for d in range(1, 4):
        y = y + cw_ref[3 - d:4 - d, :] * pltpu.roll(x, d, 0)
    xa = _silu(y)
    xs = xa[:, 0:D_INNER]
    bm = xa[:, D_INNER:D_INNER + 2 * D_STATE]
    cm = xa[:, D_INNER + 2 * D_STATE:CONV_DIM]
    valid = rid >= SVALID
    dt = jnp.where(valid, _softplus(dt_ref[...] + dtb_ref[...]), 0.0)
    a = dt * (-jnp.exp(alog_ref[...]))
    acum = a
    rem = jnp.zeros_like(a)
    for d in range(1, 4):
        acum = acum + jnp.where(rid - d >= SVALID, pltpu.roll(a, d, 0), 0.0)
        rem = rem + jnp.where(rid + d < SROWS, pltpu.roll(a, r - d, 0), 0.0)
    ex = ex_ref[...]

    def expand(v):
        return jnp.dot(v, ex, precision=HIGHEST, preferred_element_type=F32)

    lane = _lane_iota((r, LANES))
    y_acc = dsk_ref[...] * xs
    for d in range(4):
        ok = rid - d >= SVALID
        bsh = bm if d == 0 else pltpu.roll(bm, d, 0)
        cb0 = jnp.sum(cm[:, 0:D_STATE] * bsh[:, 0:D_STATE], axis=-1, keepdims=True)
        cb1 = jnp.sum(cm[:, D_STATE:] * bsh[:, D_STATE:], axis=-1, keepdims=True)
        cbh = jnp.where(lane < N_SSM_HEADS // 2, cb0, cb1)
        if d == 0:
            coef = cbh * dt
            xsh = xs
        else:
            dec = jnp.exp(jnp.where(ok, acum - pltpu.roll(acum, d, 0), NEG))
            coef = cbh * dec * pltpu.roll(dt, d, 0)
            xsh = pltpu.roll(xs, d, 0)
        y_acc = y_acc + expand(jnp.where(ok, coef, 0.0)) * xsh
    yd_ref[...] = y_acc
    eac_ref[...] = expand(jnp.exp(acum))
    xw_ref[...] = xs * expand(dt * jnp.exp(rem))
    bm_ref[...] = bm
    cm_ref[...] = cm
    cd_ref[...] = jnp.exp(acum + rem)


def _ssd_sample_pre(xbc8, prev8, dt8, cw8, cb, dtb, alog, dsk, ex, l, rb):
    r = xbc8.shape[0]
    row = lambda i: (i, 0)
    layer = lambda i: (l, 0, 0)
    widths = (D_INNER, D_INNER, D_INNER, 2 * D_STATE, 2 * D_STATE, LANES)
    return pl.pallas_call(
        _ssd_sample_pre_kernel,
        grid=(r // rb,),
        out_shape=[jax.ShapeDtypeStruct((r, w_), F32) for w_ in widths],
        in_specs=[pl.BlockSpec((rb, CONV_DIM), row),
                  pl.BlockSpec((rb, CONV_DIM), row),
                  pl.BlockSpec((rb, LANES), row),
                  pl.BlockSpec((None, SUBLANES, CONV_DIM), layer),
                  pl.BlockSpec((None, 1, CONV_DIM), layer),
                  pl.BlockSpec((None, 1, LANES), layer),
                  pl.BlockSpec((None, 1, LANES), layer),
                  pl.BlockSpec((None, 1, D_INNER), layer),
                  pl.BlockSpec((LANES, D_INNER), lambda i: (0, 0))],
        out_specs=[pl.BlockSpec((rb, w_), row) for w_ in widths],
        compiler_params=_cparams(("parallel",)),
        name="ssd_sample_pre",
    )(xbc8, prev8, dt8, cw8, cb, dtb, alog, dsk, ex)


def _ssd_sample_state_kernel(cd_ref, yd_ref, eac_ref, xw_ref, bm_ref, cm_ref, z_ref, h0_ref, ng_ref,
                             ob_ref, hn_ref, *, nbs):
    base = pl.program_id(0) * nbs
    yo = []
    for n in range(nbs):
        rows = slice(n * SROWS, (n + 1) * SROWS)
        parts = []
        for g in range(2):
            hp = h0_ref[n, 4 * g:4 * g + 4].reshape(4 * HEAD_DIM, D_STATE)
            cmg = cm_ref[rows, g * D_STATE:(g + 1) * D_STATE].astype(BF16)
            bmg = bm_ref[rows, g * D_STATE:(g + 1) * D_STATE].astype(BF16)
            parts.append(_dot_t(cmg, hp.astype(BF16)))
            s_new = _dot_t0(xw_ref[rows, g * 256:(g + 1) * 256].astype(BF16), bmg)
            for hh in range(4):
                h = 4 * g + hh
                hn_ref[n, h] = (hp[hh * HEAD_DIM:(hh + 1) * HEAD_DIM, :] * cd_ref[base + n, h]
                                + s_new[hh * HEAD_DIM:(hh + 1) * HEAD_DIM, :])
        yo.append(jnp.concatenate(parts, axis=1))
    y = yd_ref[...] + eac_ref[...] * jnp.concatenate(yo, axis=0)
    ob_ref[...] = _gated_group_norm(y, z_ref[...], ng_ref[...])


def _ssd_sample_state(cd, yd, eac, xw, bm, cm, z8, state, ng, l, depth, nbs, bufs):
    ns = state.shape[1]
    row = lambda i: (i, 0)
    st_spec = pl.BlockSpec((None, nbs, N_SSM_HEADS, HEAD_DIM, D_STATE), lambda i: (l, i, 0, 0, 0))
    rb = nbs * SROWS
    n_in = 9
    body, xspecs, xargs, k_alias = _stacked(functools.partial(_ssd_sample_state_kernel, nbs=nbs), n_in, bufs)
    return pl.pallas_call(
        body,
        grid=(ns // nbs,),
        out_shape=[jax.ShapeDtypeStruct((ns * SROWS, D_INNER), F32),
                   jax.ShapeDtypeStruct(state.shape, F32)],
        in_specs=[pl.BlockSpec(memory_space=pltpu.SMEM),
                  pl.BlockSpec((rb, D_INNER), row),
                  pl.BlockSpec((rb, D_INNER), row),
                  pl.BlockSpec((rb, D_INNER), row),
                  pl.BlockSpec((rb, 2 * D_STATE), row),
                  pl.BlockSpec((rb, 2 * D_STATE), row),
                  pl.BlockSpec((rb, D_INNER), row),
                  st_spec,
                  pl.BlockSpec((None, 1, D_INNER), lambda i: (l, 0, 0))] + xspecs,
        out_specs=[pl.BlockSpec((rb, D_INNER), row), st_spec],
        input_output_aliases=_alias_map(n_in, k_alias, 2) if k_alias else {},
        compiler_params=_cparams(("parallel",)),
        name="ssd_sample_state",
    )(cd, yd, eac, xw, bm, cm, z8, state, ng, *xargs)


def _prompt_buckets(dil):
    qi = jnp.arange(BLOCK)[:, None]
    kj = jnp.arange(2 * BLOCK)[None, :]
    dist = BLOCK + qi - kj
    return _masked_bucket(dist * dil, (dist >= 0) & (dist <= WIN))


def _sample_row_tokens(rows):
    r = jnp.arange(rows) % SROWS
    return jnp.where(r >= SVALID, r - SVALID, -1)[:, None]


def _new_row_buckets(t, same_token_only):
    r2 = jnp.arange(SROWS)[None, :]
    dn = t - (r2 - SVALID)
    ok = (r2 >= SVALID) & (dn >= 0)
    if same_token_only:
        ok = ok & (dn == 0)
    return jnp.where(t >= 0, _masked_bucket(dn, ok), 0).astype(jnp.int32)


def _sample_buckets_a():
    t = _sample_row_tokens(4 * SROWS)
    j = jnp.arange(WIN)[None, :]
    dist = WIN + t - j
    bc = jnp.where(t >= 0, _masked_bucket(dist, (dist >= 0) & (dist <= WIN)), 0)
    return bc.astype(jnp.int32), _new_row_buckets(t, False)


def _sample_buckets_c(lc, window, dil, span):
    t = _sample_row_tokens(2 * SROWS)
    pos = jnp.arange(lc - span, lc)[None, :]
    dist = lc + t - pos
    ok = (dist >= 0) & (dist <= window) & (dist % dil == 0)
    return jnp.where(t >= 0, _masked_bucket(dist, ok), 0).astype(jnp.int32)


def _rows_by_head(bias, group):
    h, r, _ = bias.shape
    rb = r // group
    return jnp.stack([jnp.concatenate([bias[g * group + i, i * rb:(i + 1) * rb] for i in range(group)], axis=0)
                      for g in range(h // group)])


def kernel(x_prompt, x_sample, cache_a_k, cache_a_v, cache_c_k, cache_c_v, state_ssm, state_conv,
           norm_mix_g, w_in, a_q_norm_g, a_k_norm_g, a_sinks, c_q_norm_g, c_k_norm_g, rel_bias,
           conv_w, conv_b, dt_bias, a_log, d_skip, ssm_norm_g, w_out, norm_mlp_g, w_up, w_down):
    depth = w_in.shape[0]
    n, seq, _ = x_prompt.shape
    ns, ts, _ = x_sample.shape
    lc = cache_c_k.shape[2]
    assert ts == SROWS - SVALID and seq % (BLOCK * 16) == 0
    assert cache_a_k.shape[2] == WIN and all(w <= lc for w, _ in C_BRANCHES)
    nb = seq // BLOCK

    w_in_b = jnp.pad(w_in, ((0, 0), (0, 0), (0, COL_END - IN_COLS))).astype(BF16)
    w_out_b, w_up_b, w_down_b = w_out.astype(BF16), w_up.astype(BF16), w_down.astype(BF16)
    ones = jnp.ones((depth, 128), F32)
    gain = jnp.concatenate([jnp.tile(a_q_norm_g, (1, 4)) * ATTN_SCALE, jnp.tile(a_k_norm_g, (1, 2)), ones,
                            jnp.tile(c_q_norm_g, (1, 4)) * ATTN_SCALE, jnp.tile(c_k_norm_g, (1, 4))], axis=1)
    lane_grp = jnp.arange(LANES) // HEAD_DIM
    bd = (lane_grp[:, None] == lane_grp[None, :]).astype(BF16)
    tri = (jnp.arange(BLOCK)[None, :] <= jnp.arange(BLOCK)[:, None]).astype(F32)
    ex = (jnp.arange(LANES)[:, None] == (jnp.arange(D_INNER) // HEAD_DIM)[None, :]).astype(F32)
    cw8 = jnp.pad(conv_w, ((0, 0), (0, SUBLANES - conv_w.shape[1]), (0, 0)))
    pad_h = lambda v: jnp.pad(v, ((0, 0), (0, LANES - N_SSM_HEADS)))[:, None, :]
    vec = lambda v: v[:, None, :]
    dtb, alog = pad_h(dt_bias), pad_h(a_log)
    dsk = vec(jnp.repeat(d_skip, HEAD_DIM, axis=1))
    g_mix, g_mlp, gain, cb, ng = vec(norm_mix_g), vec(norm_mlp_g), vec(gain), vec(conv_b), vec(ssm_norm_g)

    bias_a = _expand_bias(rel_bias, _prompt_buckets(1), 4, 0)
    bias_c = jnp.stack([_expand_bias(rel_bias, _prompt_buckets(d), 4, 4) for d in C_DILS])
    sa_c, sa_n = _sample_buckets_a()
    sbias_a_c = _rows_by_head(_expand_bias(rel_bias, sa_c, 4, 0), 4)
    sbias_a_n = _rows_by_head(_expand_bias(rel_bias, sa_n, 4, 0), 4)
    pair_bias = lambda bkt: _rows_by_head(_expand_bias(rel_bias, bkt, 4, 4), 2)
    (w1, d1), (w2, d2), (w3, d3) = C_BRANCHES
    sb12 = jnp.stack([pair_bias(_sample_buckets_c(lc, w1, d1, NEAR)), pair_bias(_sample_buckets_c(lc, w2, d2, NEAR))])
    sb3 = pair_bias(_sample_buckets_c(lc, w3, d3, lc))
    tok2 = _sample_row_tokens(2 * SROWS)
    sbn_all, sbn_same = pair_bias(_new_row_buckets(tok2, False)), pair_bias(_new_row_buckets(tok2, True))
    sbn = jnp.stack([sbn_all, sbn_same, sbn_same])

    cak_t, cav_t = cache_a_k.transpose(0, 1, 3, 4, 2), cache_a_v.transpose(0, 1, 3, 4, 2)
    cck_t, ccv_t = cache_c_k.transpose(0, 1, 3, 4, 2), cache_c_v.transpose(0, 1, 3, 4, 2)

    hp = x_prompt.reshape(n * seq, D_MODEL)
    hs = jnp.pad(x_sample, ((0, 0), (SVALID, 0), (0, 0))).reshape(ns * SROWS, D_MODEL)
    tm_p = 512
    tm_s = min(512, ns * SROWS)
    nbs_a = min(16, ns)
    nbs_c = min(2, ns)
    nbs_b = min(8, ns)

    p_ak = p_ck = p_ssm = s_ssm = None
    p_conv, s_small = [], [[] for _ in range(5)]
    for l in range(depth):
        qa, ka, va, qc, kc, vc, z, xbc, dt = _in_proj(hp, g_mix, w_in_b, gain, bd, l, tm_p)
        oa, *p_ak = _attn_a_prompt(qa, ka, va, bias_a, a_sinks, l, depth, n, nb, p_ak)
        oc, *p_ck = _attn_c_prompt(qc, kc, vc, bias_c, l, depth, n, seq, p_ck)
        ob, *p_ssm = _ssd_prompt(xbc, z, dt, cw8, cb, dtb, alog, dsk, ng, tri, l, depth, n, nb, p_ssm)
        hp = _out_mlp(hp, oa, ob, oc, w_out_b, g_mlp, w_up_b, w_down_b, l, tm_p)
        p_conv.append(xbc.reshape(n, seq, CONV_DIM)[:, seq - 3:])

        qa, ka, va, qc, kc, vc, z, xbc, dt = _in_proj(hs, g_mix, w_in_b, gain, bd, l, tm_s)
        oa = _attn_a_sample(qa, ka, va, cak_t, cav_t, sbias_a_c, sbias_a_n, a_sinks, l, nbs_a)
        oc = _attn_c_sample(qc, kc, vc, cck_t, ccv_t, sb12, sb3, sbn, l, nbs_c)
        prev8 = jnp.pad(state_conv[l], ((0, 0), (1, SROWS - 4), (0, 0))).reshape(ns * SROWS, CONV_DIM)
        yd, eac, xw, bm, cm, cd = _ssd_sample_pre(xbc, prev8, dt, cw8, cb, dtb, alog, dsk, ex, l,
                                                  min(256, ns * SROWS))
        cd_s = cd.reshape(ns, SROWS, LANES)[:, SROWS - 1, :N_SSM_HEADS]
        ob, *s_ssm = _ssd_sample_state(cd_s, yd, eac, xw, bm, cm, z, state_ssm, ng, l, depth, nbs_b, s_ssm)
        hs = _out_mlp(hs, oa, ob, oc, w_out_b, g_mlp, w_up_b, w_down_b, l, tm_s)
        new = lambda v, hh: v.reshape(ns, SROWS, hh, HEAD_DIM)[:, SVALID:]
        for dst, val in zip(s_small, (new(ka, 2), new(va, 2), new(kc, 4), new(vc, 4),
                                      xbc.reshape(ns, SROWS, CONV_DIM)[:, SROWS - 3:])):
            dst.append(val)

    y_prompt = hp.reshape(n, seq, D_MODEL)
    y_sample = hs.reshape(ns, SROWS, D_MODEL)[:, SVALID:]
    unminor = lambda buf, heads: buf.reshape(depth, n, heads, HEAD_DIM, -1).transpose(0, 1, 4, 2, 3)
    p_state = (unminor(p_ak[0], 2), unminor(p_ak[1], 2), unminor(p_ck[0], 4), unminor(p_ck[1], 4),
               p_ssm[0], jnp.stack(p_conv))
    s_state = tuple(jnp.stack(v) for v in s_small[:4]) + (s_ssm[0], jnp.stack(s_small[4]))
    return (y_prompt, y_sample) + p_state + s_state
```

```python
import functools
import math

import jax
import jax.numpy as jnp
from jax import lax
from jax.experimental import pallas as pl
from jax.experimental.pallas import tpu as pltpu

F32 = jnp.float32
BF16 = jnp.bfloat16
HIGHEST = lax.Precision.HIGHEST

D_MODEL = 1024
HEAD_DIM = 64
LANES = 128
SUBLANES = 8
BLOCK = 128
WIN = 128
D_INNER = 512
D_STATE = 128
N_SSM_HEADS = 8
CONV_DIM = 1024
D_FF = 4096
NUM_BUCKETS = 32
REL_MAX_DIST = 2048
EPS = 1e-6
ATTN_SCALE = HEAD_DIM ** -0.5
NEG = -1e30
C_BRANCHES = ((128, 1), (512, 4), (2048, 16))
C_DILS = tuple(d for _, d in C_BRANCHES)
SROWS = 8
SVALID = 4
NEAR = 512
C_UNROLL = 4
A_UNROLL = 2

COL_QA, COL_KA, COL_VA, COL_QC, COL_KC, COL_VC, COL_Z, COL_XBC, COL_DT, COL_END = (
    0, 256, 384, 512, 768, 1024, 1280, 1792, 2816, 2944)
IN_COLS = 2824
VMEM_LIMIT = 56 * 1024 * 1024


def _cparams(sem):
    return pltpu.CompilerParams(dimension_semantics=sem, vmem_limit_bytes=VMEM_LIMIT)


def _dot(a, b):
    return jnp.dot(a, b, preferred_element_type=F32)


def _dot_t(a, b):
    return lax.dot_general(a, b, (((1,), (1,)), ((), ())), preferred_element_type=F32)


def _dot_t0(a, b):
    return lax.dot_general(a, b, (((0,), (0,)), ((), ())), preferred_element_type=F32)


def _lane_iota(shape):
    return lax.broadcasted_iota(jnp.int32, shape, len(shape) - 1)


def _silu(x):
    return x * (1.0 / (1.0 + jnp.exp(-x)))


def _softplus(x):
    return jnp.maximum(x, 0.0) + jnp.log(1.0 + jnp.exp(-jnp.abs(x)))


def _stacked(body, n_in, bufs):
    if bufs is None:
        return body, [], [], 0
    k = len(bufs)

    def wrapped(*refs):
        return body(*refs[:n_in], *refs[n_in + k:])

    return wrapped, [pl.BlockSpec(memory_space=pl.ANY)] * k, list(bufs), k


def _alias_map(n_in, k, n_out):
    return {n_in + i: n_out - k + i for i in range(k)}


def _bias_kernel(tab_ref, bkt_ref, o_ref, *, heads, col0):
    bkt = bkt_ref[...]
    for h in range(heads):
        acc = jnp.full(bkt.shape, NEG, F32)
        for b in range(NUM_BUCKETS):
            acc = jnp.where(bkt == b, tab_ref[b, col0 + h], acc)
        o_ref[h] = acc


def _expand_bias(table, bkt, heads, col0):
    r, c = bkt.shape
    return pl.pallas_call(
        functools.partial(_bias_kernel, heads=heads, col0=col0),
        out_shape=jax.ShapeDtypeStruct((heads, r, c), F32),
        in_specs=[pl.BlockSpec(memory_space=pltpu.SMEM),
                  pl.BlockSpec((r, c), lambda: (0, 0))],
        out_specs=pl.BlockSpec((heads, r, c), lambda: (0, 0, 0)),
        name="bias_expand",
    )(table, bkt)


def _t5_bucket(dist):
    max_exact = NUM_BUCKETS // 2
    d = jnp.maximum(dist, 0)
    df = jnp.maximum(d, 1).astype(F32)
    large = max_exact + (jnp.log(df / max_exact) / math.log(REL_MAX_DIST / max_exact)
                         * (NUM_BUCKETS - max_exact)).astype(jnp.int32)
    return jnp.where(d < max_exact, d, jnp.minimum(large, NUM_BUCKETS - 1))


def _masked_bucket(dist, valid):
    return jnp.where(valid, _t5_bucket(dist), -1).astype(jnp.int32)


def _head_norm(slab, gain, bd):
    sq = slab * slab
    hi = sq.astype(BF16)
    lo = (sq - hi.astype(F32)).astype(BF16)
    ss = _dot(hi, bd) + _dot(lo, bd)
    return slab * lax.rsqrt(ss * (1.0 / HEAD_DIM) + EPS) * gain


def _in_proj_kernel(h_ref, g_ref, w_ref, gain_ref, bd_ref,
                    qa_ref, ka_ref, va_ref, qc_ref, kc_ref, vc_ref, z_ref, xbc_ref, dt_ref):
    x = h_ref[...]
    ms = jnp.mean(x * x, axis=-1, keepdims=True)
    u = (x * lax.rsqrt(ms + EPS) * g_ref[...]).astype(BF16)
    bd = bd_ref[...]

    def proj(c0, c1):
        return _dot(u, w_ref[:, c0:c1])

    def normed(c0, c1, out_ref):
        p = proj(c0, c1)
        for s in range((c1 - c0) // LANES):
            sl = slice(s * LANES, (s + 1) * LANES)
            out_ref[:, sl] = _head_norm(p[:, sl], gain_ref[:, c0 + s * LANES:c0 + (s + 1) * LANES], bd)

    normed(COL_QA, COL_KA, qa_ref)
    normed(COL_KA, COL_VA, ka_ref)
    va_ref[...] = proj(COL_VA, COL_QC)
    normed(COL_QC, COL_KC, qc_ref)
    normed(COL_KC, COL_VC, kc_ref)
    vc_ref[...] = proj(COL_VC, COL_Z)
    z_ref[...] = proj(COL_Z, COL_XBC)
    xbc_ref[...] = proj(COL_XBC, COL_DT)
    dt_ref[...] = proj(COL_DT, COL_END)


def _in_proj(h, g, w, gain, bd, l, tm):
    t = h.shape[0]
    widths = (256, 128, 128, 256, 256, 256, 512, 1024, 128)
    layer = lambda i: (l, 0, 0)
    return pl.pallas_call(
        _in_proj_kernel,
        grid=(t // tm,),
        out_shape=[jax.ShapeDtypeStruct((t, w_), F32) for w_ in widths],
        in_specs=[pl.BlockSpec((tm, D_MODEL), lambda i: (i, 0)),
                  pl.BlockSpec((None, 1, D_MODEL), layer),
                  pl.BlockSpec((None, D_MODEL, COL_END), layer),
                  pl.BlockSpec((None, 1, D_MODEL), layer),
                  pl.BlockSpec((LANES, LANES), lambda i: (0, 0))],
        out_specs=[pl.BlockSpec((tm, w_), lambda i: (i, 0)) for w_ in widths],
        compiler_params=_cparams(("parallel",)),
        name="in_proj",
    )(h, g, w, gain, bd)


def _out_mlp_kernel(h_ref, oa_ref, ob_ref, oc_ref, wo_ref, g_ref, wu_ref, wd_ref, o_ref):
    acc = _dot(oa_ref[...].astype(BF16), wo_ref[0:256, :])
    acc += _dot(ob_ref[...].astype(BF16), wo_ref[256:768, :])
    acc += _dot(oc_ref[...].astype(BF16), wo_ref[768:1024, :])
    h2 = h_ref[...] + acc
    ms = jnp.mean(h2 * h2, axis=-1, keepdims=True)
    u = (h2 * lax.rsqrt(ms + EPS) * g_ref[...]).astype(BF16)
    ff_chunk = 1024
    mlp = jnp.zeros_like(h2)
    for c in range(D_FF // ff_chunk):
        a = _dot(u, wu_ref[:, c * ff_chunk:(c + 1) * ff_chunk])
        a = jnp.square(jnp.maximum(a, 0.0)).astype(BF16)
        mlp += _dot(a, wd_ref[c * ff_chunk:(c + 1) * ff_chunk, :])
    o_ref[...] = h2 + mlp


def _out_mlp(h, oa, ob, oc, wo, g, wu, wd, l, tm):
    t = h.shape[0]
    layer = lambda i: (l, 0, 0)
    row = lambda i: (i, 0)
    return pl.pallas_call(
        _out_mlp_kernel,
        grid=(t // tm,),
        out_shape=jax.ShapeDtypeStruct((t, D_MODEL), F32),
        in_specs=[pl.BlockSpec((tm, D_MODEL), row),
                  pl.BlockSpec((tm, 256), row),
                  pl.BlockSpec((tm, 512), row),
                  pl.BlockSpec((tm, 256), row),
                  pl.BlockSpec((None, D_MODEL, D_MODEL), layer),
                  pl.BlockSpec((None, 1, D_MODEL), layer),
                  pl.BlockSpec((None, D_MODEL, D_FF), layer),
                  pl.BlockSpec((None, D_FF, D_MODEL), layer)],
        out_specs=pl.BlockSpec((tm, D_MODEL), row),
        compiler_params=_cparams(("parallel",)),
        name="out_mlp",
    )(h, oa, ob, oc, wo, g, wu, wd)


def _attn_a_prompt_kernel(sink_ref, q_ref, k_ref, v_ref, bias_ref, o_ref, kt_ref, vt_ref,
                          k_st, k_sw, v_st, v_sw, *, l, seq):
    lo = _lane_iota((BLOCK, LANES)) < HEAD_DIM
    k = k_ref[...]
    v = v_ref[...]
    k_st[...] = k.astype(BF16)
    v_st[...] = v.astype(BF16)
    k_sw[...] = pltpu.roll(k, HEAD_DIM, 1).astype(BF16)
    v_sw[...] = pltpu.roll(v, HEAD_DIM, 1).astype(BF16)
    kt_ref[...] = k_ref[seq - WIN:seq, :].T
    vt_ref[...] = v_ref[seq - WIN:seq, :].T

    def body(g, carry):
        scores, where = [], []
        for u in range(A_UNROLL):
            b = g * A_UNROLL + u
            cur = pl.ds(pl.multiple_of(b * BLOCK, BLOCK), BLOCK)
            prev = pl.ds(pl.multiple_of(jnp.maximum(b - 1, 0) * BLOCK, BLOCK), BLOCK)
            where.append((cur, prev))
            for h in range(4):
                slab = q_ref[cur, (h // 2) * LANES:(h // 2 + 1) * LANES]
                qm = jnp.where(lo if h % 2 == 0 else ~lo, slab, 0.0).astype(BF16)
                kk = k_st if h in (0, 3) else k_sw
                s_c = _dot_t(qm, kk[cur, :]) + bias_ref[h, :, BLOCK:2 * BLOCK]
                s_p = jnp.where(b == 0, NEG, _dot_t(qm, kk[prev, :]) + bias_ref[h, :, 0:BLOCK])
                scores.append((s_c, s_p))
        probs = []
        for i, (s_c, s_p) in enumerate(scores):
            sink = sink_ref[l, i % 4]
            m = jnp.maximum(jnp.max(jnp.maximum(s_c, s_p), axis=-1, keepdims=True), sink)
            p_c, p_p = jnp.exp(s_c - m), jnp.exp(s_p - m)
            den = jnp.sum(p_c + p_p, axis=-1, keepdims=True) + jnp.exp(sink - m)
            probs.append((p_c.astype(BF16), p_p.astype(BF16), den))
        for u, (cur, prev) in enumerate(where):
            outs = []
            for h in range(4):
                p_c, p_p, den = probs[4 * u + h]
                vv = v_st if h in (0, 3) else v_sw
                outs.append((_dot(p_c, vv[cur, :]) + _dot(p_p, vv[prev, :])) / den)
            o_ref[cur, 0:LANES] = jnp.where(lo, outs[0], outs[1])
            o_ref[cur, LANES:2 * LANES] = jnp.where(lo, outs[2], outs[3])
        return carry

    lax.fori_loop(0, seq // BLOCK // A_UNROLL, body, 0)


def _attn_a_prompt(q, k, v, bias, sinks, l, depth, n, seq, bufs):
    t = q.shape[0]
    n_in = 5
    body, xspecs, xargs, k_alias = _stacked(functools.partial(_attn_a_prompt_kernel, l=l, seq=seq), n_in, bufs)
    st_shape = jax.ShapeDtypeStruct((depth, n, LANES, WIN), F32)
    st_spec = pl.BlockSpec((None, None, LANES, WIN), lambda i: (l, i, 0, 0))
    tok = lambda w: pl.BlockSpec((seq, w), lambda i: (i, 0))
    return pl.pallas_call(
        body,
        grid=(n,),
        out_shape=[jax.ShapeDtypeStruct((t, 256), F32), st_shape, st_shape],
        in_specs=[pl.BlockSpec(memory_space=pltpu.SMEM), tok(256), tok(LANES), tok(LANES),
                  pl.BlockSpec((4, BLOCK, 2 * BLOCK), lambda i: (0, 0, 0))] + xspecs,
        out_specs=[tok(256), st_spec, st_spec],
        scratch_shapes=[pltpu.VMEM((seq, LANES), BF16)] * 4,
        input_output_aliases=_alias_map(n_in, k_alias, 3) if k_alias else {},
        compiler_params=_cparams(("parallel",)),
        name="attn_a_prompt",
    )(sinks, q, k, v, bias, *xargs)


def _attn_c_prompt_kernel(q_ref, k_ref, v_ref, bias_ref, o_ref, kt_ref, vt_ref, ob_scr, lse_scr, *, seq):
    lane = _lane_iota((BLOCK, LANES))
    lo = lane < HEAD_DIM
    kt_ref[...] = k_ref[...].T
    vt_ref[...] = v_ref[...].T

    def blocks(br, dil, where, has_prev):
        rows = lambda ref, s: ref[pl.ds(s, BLOCK, stride=dil), :]
        scores, vals = [], []
        for qs, ps, first in where:
            q = rows(q_ref, qs)
            kc = rows(k_ref, qs).astype(BF16)
            kp = rows(k_ref, ps).astype(BF16) if has_prev else None
            vals.append((rows(v_ref, qs).astype(BF16), rows(v_ref, ps).astype(BF16) if has_prev else None))
            for hh in range(2):
                qm = jnp.where(lo if hh == 0 else ~lo, q, 0.0).astype(BF16)
                s_c = _dot_t(qm, kc) + bias_ref[br, hh, :, BLOCK:2 * BLOCK]
                s_p = None
                if has_prev:
                    s_p = jnp.where(first, NEG, _dot_t(qm, kp) + bias_ref[br, hh, :, 0:BLOCK])
                scores.append((s_c, s_p))
        probs = []
        for s_c, s_p in scores:
            if has_prev:
                m = jnp.max(jnp.maximum(s_c, s_p), axis=-1, keepdims=True)
                p_c, p_p = jnp.exp(s_c - m), jnp.exp(s_p - m)
                den = jnp.sum(p_c + p_p, axis=-1, keepdims=True)
                probs.append((p_c.astype(BF16), p_p.astype(BF16), den, m))
            else:
                m = jnp.max(s_c, axis=-1, keepdims=True)
                p_c = jnp.exp(s_c - m)
                probs.append((p_c.astype(BF16), None, jnp.sum(p_c, axis=-1, keepdims=True), m))
        for u, (qs, _, _) in enumerate(where):
            outs, lses = [], []
            for hh in range(2):
                p_c, p_p, den, m = probs[2 * u + hh]
                vc, vp = vals[u]
                o = _dot(p_c, vc)
                if has_prev:
                    o = o + _dot(p_p, vp)
                outs.append(o / den)
                lses.append(jnp.broadcast_to(m + jnp.log(den), (BLOCK, LANES)))
            ob_scr[br, pl.ds(qs, BLOCK, stride=dil), :] = jnp.where(lo, outs[0], outs[1])
            lse_scr[br, pl.ds(qs, BLOCK, stride=dil), :] = jnp.where(lo, lses[0], lses[1])

    for br, dil in enumerate(C_DILS):
        nb = seq // dil // BLOCK
        span = BLOCK * dil

        def body(g, carry, br=br, dil=dil, nb=nb, span=span):
            where = []
            for u in range(C_UNROLL):
                i = g * C_UNROLL + u
                res = i % dil
                b = i // dil
                where.append((res + b * span, res + jnp.maximum(b - 1, 0) * span, b == 0))
            blocks(br, dil, where, nb > 1)
            return carry

        lax.fori_loop(0, dil * nb // C_UNROLL, body, 0)

    def merge(i, carry):
        rows = pl.ds(pl.multiple_of(i * BLOCK, BLOCK), BLOCK)
        l0, l1, l2 = lse_scr[0, rows, :], lse_scr[1, rows, :], lse_scr[2, rows, :]
        m = jnp.maximum(jnp.maximum(l0, l1), l2)
        w0, w1, w2 = jnp.exp(l0 - m), jnp.exp(l1 - m), jnp.exp(l2 - m)
        num = w0 * ob_scr[0, rows, :] + w1 * ob_scr[1, rows, :] + w2 * ob_scr[2, rows, :]
        o_ref[rows, :] = num / (w0 + w1 + w2)
        return carry

    lax.fori_loop(0, seq // BLOCK, merge, 0)


def _attn_c_prompt(q, k, v, bias, l, depth, n, seq, bufs):
    t = q.shape[0]
    blk = pl.BlockSpec((seq, LANES), lambda i, hp: (i, hp))
    n_in = 4
    body, xspecs, xargs, k_alias = _stacked(functools.partial(_attn_c_prompt_kernel, seq=seq), n_in, bufs)
    st_shape = jax.ShapeDtypeStruct((depth, n, 256, seq), F32)
    st_spec = pl.BlockSpec((None, None, LANES, seq), lambda i, hp: (l, i, hp, 0))
    return pl.pallas_call(
        body,
        grid=(n, 2),
        out_shape=[jax.ShapeDtypeStruct((t, 256), F32), st_shape, st_shape],
        in_specs=[blk, blk, blk,
                  pl.BlockSpec((3, 2, BLOCK, 2 * BLOCK), lambda i, hp: (0, hp, 0, 0))] + xspecs,
        out_specs=[blk, st_spec, st_spec],
        scratch_shapes=[pltpu.VMEM((3, seq, LANES), F32), pltpu.VMEM((3, seq, LANES), F32)],
        input_output_aliases=_alias_map(n_in, k_alias, 3) if k_alias else {},
        compiler_params=_cparams(("parallel", "parallel")),
        name="attn_c_prompt",
    )(q, k, v, bias, *xargs)


def _gated_group_norm(y, z, ng):
    gated = y * _silu(z)
    parts = []
    for grp in range(2):
        gsl = gated[:, grp * 256:(grp + 1) * 256]
        ms = jnp.mean(gsl * gsl, axis=-1, keepdims=True)
        parts.append(gsl * lax.rsqrt(ms + EPS))
    return jnp.concatenate(parts, axis=1) * ng


def _ssd_prompt_kernel(xbc_ref, z_ref, dt_ref, cw_ref, cb_ref, dtb_ref, alog_ref,
                       dsk_ref, ng_ref, tri_ref, ob_ref, hl_ref, hst, xbuf):
    c = pl.program_id(1)

    @pl.when(c == 0)
    def _():
        hst[...] = jnp.zeros_like(hst)
        xbuf[0:SUBLANES, :] = jnp.zeros((SUBLANES, CONV_DIM), F32)

    x = xbc_ref[...]
    xbuf[SUBLANES:SUBLANES + BLOCK, :] = x
    y = cb_ref[...] + cw_ref[3:4, :] * x
    for kk in range(3):
        y = y + cw_ref[kk:kk + 1, :] * xbuf[SUBLANES - 3 + kk:SUBLANES - 3 + kk + BLOCK, :]
    xbuf[0:SUBLANES, :] = x[BLOCK - SUBLANES:BLOCK, :]
    xa = _silu(y)
    xs = xa[:, 0:D_INNER]
    dt = _softplus(dt_ref[...] + dtb_ref[...])
    a = dt * (-jnp.exp(alog_ref[...]))
    acum = jnp.dot(tri_ref[...], a, precision=HIGHEST, preferred_element_type=F32)
    acum_t = acum.T
    dt_t = dt.T
    eac = jnp.exp(acum)
    last = acum[BLOCK - 1:BLOCK, :]
    w_end = jnp.exp(last - acum) * dt
    e_last = jnp.exp(last)
    row = lax.broadcasted_iota(jnp.int32, (BLOCK, BLOCK), 0)
    lane = _lane_iota((BLOCK, BLOCK))
    causal = lane <= row
    lo = lane < HEAD_DIM
    top = row < HEAD_DIM
    y_parts = []
    for j in range(N_SSM_HEADS // 2):
        g = j // 2
        bm = xa[:, D_INNER + g * D_STATE:D_INNER + (g + 1) * D_STATE].astype(BF16)
        cm = xa[:, D_INNER + 2 * D_STATE + g * D_STATE:D_INNER + 2 * D_STATE + (g + 1) * D_STATE].astype(BF16)
        cb = _dot_t(cm, bm)
        xs_pair = xs[:, j * LANES:(j + 1) * LANES]
        xs_b = xs_pair.astype(BF16)
        yd = []
        for hh in range(2):
            h = 2 * j + hh
            seg = acum[:, h:h + 1] - acum_t[h:h + 1, :]
            dec = jnp.exp(jnp.where(causal, seg, NEG)) * dt_t[h:h + 1, :]
            yd.append(_dot((cb * dec).astype(BF16), xs_b))
        h0_, h1_ = 2 * j, 2 * j + 1
        hp = hst[h0_:h0_ + 2].reshape(BLOCK, D_STATE)
        e_pair = jnp.where(lo, eac[:, h0_:h0_ + 1], eac[:, h1_:h1_ + 1])
        y_parts.append(jnp.where(lo, yd[0], yd[1]) + _dot_t(cm, hp.astype(BF16)) * e_pair)
        w_pair = jnp.where(lo, w_end[:, h0_:h0_ + 1], w_end[:, h1_:h1_ + 1])
        s_new = _dot_t0((xs_pair * w_pair).astype(BF16), bm)
        cd_pair = jnp.where(top, e_last[:, h0_:h0_ + 1], e_last[:, h1_:h1_ + 1])
        hst[h0_:h0_ + 2] = (hp * cd_pair + s_new).reshape(2, HEAD_DIM, D_STATE)
    y = jnp.concatenate(y_parts, axis=1) + dsk_ref[...] * xs
    ob_ref[...] = _gated_group_norm(y, z_ref[...], ng_ref[...])

    @pl.when(c == pl.num_programs(1) - 1)
    def _():
        hl_ref[...] = hst[...]


def _ssd_prompt(xbc, z, dt, cw8, cb, dtb, alog, dsk, ng, tri, l, depth, n, nc, bufs):
    t = xbc.shape[0]
    row = lambda i, c: (i * nc + c, 0)
    layer = lambda i, c: (l, 0, 0)
    n_in = 10
    body, xspecs, xargs, k_alias = _stacked(_ssd_prompt_kernel, n_in, bufs)
    st = (N_SSM_HEADS, HEAD_DIM, D_STATE)
    return pl.pallas_call(
        body,
        grid=(n, nc),
        out_shape=[jax.ShapeDtypeStruct((t, D_INNER), F32),
                   jax.ShapeDtypeStruct((depth, n) + st, F32)],
        in_specs=[pl.BlockSpec((BLOCK, CONV_DIM), row),
                  pl.BlockSpec((BLOCK, D_INNER), row),
                  pl.BlockSpec((BLOCK, LANES), row),
                  pl.BlockSpec((None, SUBLANES, CONV_DIM), layer),
                  pl.BlockSpec((None, 1, CONV_DIM), layer),
                  pl.BlockSpec((None, 1, LANES), layer),
                  pl.BlockSpec((None, 1, LANES), layer),
                  pl.BlockSpec((None, 1, D_INNER), layer),
                  pl.BlockSpec((None, 1, D_INNER), layer),
                  pl.BlockSpec((BLOCK, BLOCK), lambda i, c: (0, 0))] + xspecs,
        out_specs=[pl.BlockSpec((BLOCK, D_INNER), row),
                   pl.BlockSpec((None, None) + st, lambda i, c: (l, i, 0, 0, 0))],
        scratch_shapes=[pltpu.VMEM(st, F32),
                        pltpu.VMEM((SUBLANES + BLOCK, CONV_DIM), F32)],
        input_output_aliases=_alias_map(n_in, k_alias, 2) if k_alias else {},
        compiler_params=_cparams(("parallel", "arbitrary")),
        name="ssd_prompt",
    )(xbc, z, dt, cw8, cb, dtb, alog, dsk, ng, tri, *xargs)


def _attn_a_sample_kernel(sink_ref, q_ref, kn_ref, vn_ref, kt_ref, vt_ref, bc_ref, bn_ref, o_ref, *, nbs, l):
    rb = nbs * SROWS
    lo = _lane_iota((rb, LANES)) < HEAD_DIM
    s0, s1 = q_ref[:, 0:LANES], q_ref[:, LANES:2 * LANES]
    per_seq = lambda v: v.reshape(nbs, SROWS, LANES)
    qm = jnp.concatenate([
        per_seq(jnp.where(lo, s0, 0.0)),
        per_seq(jnp.where(lo, pltpu.roll(s0, HEAD_DIM, 1), 0.0)),
        per_seq(jnp.where(lo, 0.0, pltpu.roll(s1, HEAD_DIM, 1))),
        per_seq(jnp.where(lo, 0.0, s1))], axis=1).astype(BF16)
    kt = kt_ref[...].reshape(nbs, LANES, WIN).astype(BF16)
    vt = vt_ref[...].reshape(nbs, LANES, WIN).astype(BF16)
    kn = kn_ref[...].reshape(nbs, SROWS, LANES).astype(BF16)
    vn = vn_ref[...].reshape(nbs, SROWS, LANES).astype(BF16)
    s_c = jnp.einsum('nqd,ndk->nqk', qm, kt, preferred_element_type=F32) + bc_ref[...]
    s_n = jnp.einsum('nqd,nkd->nqk', qm, kn, preferred_element_type=F32) + bn_ref[...]
    hrow = lax.broadcasted_iota(jnp.int32, (nbs, 4 * SROWS, 1), 1) // SROWS
    sink = jnp.where(hrow == 0, sink_ref[l, 0], jnp.where(hrow == 1, sink_ref[l, 1],
                     jnp.where(hrow == 2, sink_ref[l, 2], sink_ref[l, 3])))
    m = jnp.maximum(jnp.max(s_c, axis=-1, keepdims=True), jnp.max(s_n, axis=-1, keepdims=True))
    m = jnp.maximum(m, sink)
    p_c = jnp.exp(s_c - m)
    p_n = jnp.exp(s_n - m)
    den = jnp.sum(p_c, axis=-1, keepdims=True) + jnp.sum(p_n, axis=-1, keepdims=True) + jnp.exp(sink - m)
    o = (jnp.einsum('nqk,ndk->nqd', p_c.astype(BF16), vt, preferred_element_type=F32)
         + jnp.einsum('nqk,nkd->nqd', p_n.astype(BF16), vn, preferred_element_type=F32)) / den
    o0, o1, o2, o3 = (o[:, i * SROWS:(i + 1) * SROWS, :].reshape(rb, LANES) for i in range(4))
    o_ref[:, 0:LANES] = jnp.where(lo, o0, pltpu.roll(o1, HEAD_DIM, 1))
    o_ref[:, LANES:2 * LANES] = jnp.where(lo, pltpu.roll(o2, HEAD_DIM, 1), o3)


def _attn_a_sample(q8, k8, v8, cache_kt, cache_vt, bias_c, bias_n, sinks, l, nbs):
    ns = cache_kt.shape[1]
    tok = lambda i: (i, 0)
    cache = pl.BlockSpec((None, nbs, 2, HEAD_DIM, WIN), lambda i: (l, i, 0, 0, 0))
    c3 = lambda i: (0, 0, 0)
    return pl.pallas_call(
        functools.partial(_attn_a_sample_kernel, nbs=nbs, l=l),
        grid=(ns // nbs,),
        out_shape=jax.ShapeDtypeStruct((ns * SROWS, 256), F32),
        in_specs=[pl.BlockSpec(memory_space=pltpu.SMEM),
                  pl.BlockSpec((nbs * SROWS, 256), tok),
                  pl.BlockSpec((nbs * SROWS, LANES), tok),
                  pl.BlockSpec((nbs * SROWS, LANES), tok),
                  cache, cache,
                  pl.BlockSpec((1, 4 * SROWS, WIN), c3),
                  pl.BlockSpec((1, 4 * SROWS, SROWS), c3)],
        out_specs=pl.BlockSpec((nbs * SROWS, 256), tok),
        compiler_params=_cparams(("parallel",)),
        name="attn_a_sample",
    )(sinks, q8, k8, v8, cache_kt, cache_vt, bias_c, bias_n)


def _softmax_parts(parts):
    m = None
    for s in parts:
        mm = jnp.max(s, axis=-1, keepdims=True)
        m = mm if m is None else jnp.maximum(m, mm)
    ps = [jnp.exp(s - m) for s in parts]
    den = None
    for p in ps:
        dd = jnp.sum(p, axis=-1, keepdims=True)
        den = dd if den is None else den + dd
    return ps, den, m + jnp.log(den)


def _attn_c_sample_kernel(q_ref, kn_ref, vn_ref, kt_ref, vt_ref, b12_ref, b3_ref, bn_ref, o_ref, *, nbs, lc):
    lo = _lane_iota((SROWS, LANES)) < HEAD_DIM
    pr = 2 * SROWS
    for n in range(nbs):
        rows = slice(n * SROWS, (n + 1) * SROWS)
        for j in range(2):
            lanes = slice(j * LANES, (j + 1) * LANES)
            qp = q_ref[rows, lanes]
            lhs = jnp.concatenate([jnp.where(lo, qp, 0.0), jnp.where(lo, 0.0, qp)], axis=0).astype(BF16)
            ktp = kt_ref[n, 2 * j:2 * j + 2].reshape(LANES, lc).astype(BF16)
            vtp = vt_ref[n, 2 * j:2 * j + 2].reshape(LANES, lc).astype(BF16)
            knp = kn_ref[rows, lanes].astype(BF16)
            vnp = vn_ref[rows, lanes].astype(BF16)
            s = _dot(lhs, ktp)
            sn = _dot_t(lhs, knp)
            s_near = s[:, lc - NEAR:]
            (p1, p1n), d1, l1 = _softmax_parts([s_near + b12_ref[0, j], sn + bn_ref[0, j]])
            (p2, p2n), d2, l2 = _softmax_parts([s_near + b12_ref[1, j], sn + bn_ref[1, j]])
            (p3, p3n), d3, l3 = _softmax_parts([s + b3_ref[j], sn + bn_ref[2, j]])
            o12 = _dot_t(jnp.concatenate([p1, p2], axis=0).astype(BF16), vtp[:, lc - NEAR:])
            o1 = (o12[0:pr] + _dot(p1n.astype(BF16), vnp)) / d1
            o2 = (o12[pr:2 * pr] + _dot(p2n.astype(BF16), vnp)) / d2
            o3 = (_dot_t(p3.astype(BF16), vtp) + _dot(p3n.astype(BF16), vnp)) / d3
            m = jnp.maximum(jnp.maximum(l1, l2), l3)
            w1, w2, w3 = jnp.exp(l1 - m), jnp.exp(l2 - m), jnp.exp(l3 - m)
            o = (w1 * o1 + w2 * o2 + w3 * o3) / (w1 + w2 + w3)
            o_ref[rows, lanes] = jnp.where(lo, o[0:SROWS], o[SROWS:pr])


def _attn_c_sample(q8, k8, v8, cache_kt, cache_vt, b12, b3, bn, l, nbs):
    ns, lc = cache_kt.shape[1], cache_kt.shape[4]
    tok = lambda i: (i, 0)
    cache = pl.BlockSpec((None, nbs, 4, HEAD_DIM, lc), lambda i: (l, i, 0, 0, 0))
    c3 = lambda i: (0, 0, 0)
    c4 = lambda i: (0, 0, 0, 0)
    return pl.pallas_call(
        functools.partial(_attn_c_sample_kernel, nbs=nbs, lc=lc),
        grid=(ns // nbs,),
        out_shape=jax.ShapeDtypeStruct((ns * SROWS, 256), F32),
        in_specs=[pl.BlockSpec((nbs * SROWS, 256), tok),
                  pl.BlockSpec((nbs * SROWS, 256), tok),
                  pl.BlockSpec((nbs * SROWS, 256), tok),
                  cache, cache,
                  pl.BlockSpec((2, 2, 2 * SROWS, NEAR), c4),
                  pl.BlockSpec((2, 2 * SROWS, lc), c3),
                  pl.BlockSpec((3, 2, 2 * SROWS, SROWS), c4)],
        out_specs=pl.BlockSpec((nbs * SROWS, 256), tok),
        compiler_params=_cparams(("parallel",)),
        name="attn_c_sample",
    )(q8, k8, v8, cache_kt, cache_vt, b12, b3, bn)


def _ssd_sample_pre_kernel(xbc_ref, prev_ref, dt_ref, cw_ref, cb_ref, dtb_ref, alog_ref, dsk_ref, ex_ref,
                           yd_ref, eac_ref, xw_ref, bm_ref, cm_ref, cd_ref):
    r = xbc_ref.shape[0]
    rid = lax.broadcasted_iota(jnp.int32, (r, 1), 0) % SROWS
    x = jnp.where(jnp.logical_and(rid >= 1, rid < SVALID), prev_ref[...], xbc_ref[...])
    y = cb_ref[...] + cw_ref[3:4, :] * x
    for d in range(1, 4):
        y = y + cw_ref[3 - d:4 - d, :] * pltpu.roll(x, d, 0)
    xa = _silu(y)
    xs = xa[:, 0:D_INNER]
    bm = xa[:, D_INNER:D_INNER + 2 * D_STATE]
    cm = xa[:, D_INNER + 2 * D_STATE:CONV_DIM]
    valid = rid >= SVALID
    dt = jnp.where(valid, _softplus(dt_ref[...] + dtb_ref[...]), 0.0)
    a = dt * (-jnp.exp(alog_ref[...]))
    acum = a
    rem = jnp.zeros_like(a)
    for d in range(1, 4):
        acum = acum + jnp.where(rid - d >= SVALID, pltpu.roll(a, d, 0), 0.0)
        rem = rem + jnp.where(rid + d < SROWS, pltpu.roll(a, r - d, 0), 0.0)
    ex = ex_ref[...]

    def expand(v):
        return jnp.dot(v, ex, precision=HIGHEST, preferred_element_type=F32)

    lane = _lane_iota((r, LANES))
    y_acc = dsk_ref[...] * xs
    for d in range(4):
        ok = rid - d >= SVALID
        bsh = bm if d == 0 else pltpu.roll(bm, d, 0)
        cb0 = jnp.sum(cm[:, 0:D_STATE] * bsh[:, 0:D_STATE], axis=-1, keepdims=True)
        cb1 = jnp.sum(cm[:, D_STATE:] * bsh[:, D_STATE:], axis=-1, keepdims=True)
        cbh = jnp.where(lane < N_SSM_HEADS // 2, cb0, cb1)
        if d == 0:
            coef = cbh * dt
            xsh = xs
        else:
            dec = jnp.exp(jnp.where(ok, acum - pltpu.roll(acum, d, 0), NEG))
            coef = cbh * dec * pltpu.roll(dt, d, 0)
            xsh = pltpu.roll(xs, d, 0)
        y_acc = y_acc + expand(jnp.where(ok, coef, 0.0)) * xsh
    yd_ref[...] = y_acc
    eac_ref[...] = expand(jnp.exp(acum))
    xw_ref[...] = xs * expand(dt * jnp.exp(rem))
    bm_ref[...] = bm
    cm_ref[...] = cm
    cd_ref[...] = jnp.exp(acum + rem)


def _ssd_sample_pre(xbc8, prev8, dt8, cw8, cb, dtb, alog, dsk, ex, l, rb):
    r = xbc8.shape[0]
    row = lambda i: (i, 0)
    layer = lambda i: (l, 0, 0)
    widths = (D_INNER, D_INNER, D_INNER, 2 * D_STATE, 2 * D_STATE, LANES)
    return pl.pallas_call(
        _ssd_sample_pre_kernel,
        grid=(r // rb,),
        out_shape=[jax.ShapeDtypeStruct((r, w_), F32) for w_ in widths],
        in_specs=[pl.BlockSpec((rb, CONV_DIM), row),
                  pl.BlockSpec((rb, CONV_DIM), row),
                  pl.BlockSpec((rb, LANES), row),
                  pl.BlockSpec((None, SUBLANES, CONV_DIM), layer),
                  pl.BlockSpec((None, 1, CONV_DIM), layer),
                  pl.BlockSpec((None, 1, LANES), layer),
                  pl.BlockSpec((None, 1, LANES), layer),
                  pl.BlockSpec((None, 1, D_INNER), layer),
                  pl.BlockSpec((LANES, D_INNER), lambda i: (0, 0))],
        out_specs=[pl.BlockSpec((rb, w_), row) for w_ in widths],
        compiler_params=_cparams(("parallel",)),
        name="ssd_sample_pre",
    )(xbc8, prev8, dt8, cw8, cb, dtb, alog, dsk, ex)


def _ssd_sample_state_kernel(cd_ref, yd_ref, eac_ref, xw_ref, bm_ref, cm_ref, z_ref, h0_ref, ng_ref,
                             ob_ref, hn_ref, *, nbs):
    base = pl.program_id(0) * nbs
    yo = []
    for n in range(nbs):
        rows = slice(n * SROWS, (n + 1) * SROWS)
        parts = []
        for g in range(2):
            hp = h0_ref[n, 4 * g:4 * g + 4].reshape(4 * HEAD_DIM, D_STATE)
            cmg = cm_ref[rows, g * D_STATE:(g + 1) * D_STATE].astype(BF16)
            bmg = bm_ref[rows, g * D_STATE:(g + 1) * D_STATE].astype(BF16)
            parts.append(_dot_t(cmg, hp.astype(BF16)))
            s_new = _dot_t0(xw_ref[rows, g * 256:(g + 1) * 256].astype(BF16), bmg)
            for hh in range(4):
                h = 4 * g + hh
                hn_ref[n, h] = (hp[hh * HEAD_DIM:(hh + 1) * HEAD_DIM, :] * cd_ref[base + n, h]
                                + s_new[hh * HEAD_DIM:(hh + 1) * HEAD_DIM, :])
        yo.append(jnp.concatenate(parts, axis=1))
    y = yd_ref[...] + eac_ref[...] * jnp.concatenate(yo, axis=0)
    ob_ref[...] = _gated_group_norm(y, z_ref[...], ng_ref[...])


def _ssd_sample_state(cd, yd, eac, xw, bm, cm, z8, state, ng, l, depth, nbs, bufs):
    ns = state.shape[1]
    row = lambda i: (i, 0)
    st_spec = pl.BlockSpec((None, nbs, N_SSM_HEADS, HEAD_DIM, D_STATE), lambda i: (l, i, 0, 0, 0))
    rb = nbs * SROWS
    n_in = 9
    body, xspecs, xargs, k_alias = _stacked(functools.partial(_ssd_sample_state_kernel, nbs=nbs), n_in, bufs)
    return pl.pallas_call(
        body,
        grid=(ns // nbs,),
        out_shape=[jax.ShapeDtypeStruct((ns * SROWS, D_INNER), F32),
                   jax.ShapeDtypeStruct(state.shape, F32)],
        in_specs=[pl.BlockSpec(memory_space=pltpu.SMEM),
                  pl.BlockSpec((rb, D_INNER), row),
                  pl.BlockSpec((rb, D_INNER), row),
                  pl.BlockSpec((rb, D_INNER), row),
                  pl.BlockSpec((rb, 2 * D_STATE), row),
                  pl.BlockSpec((rb, 2 * D_STATE), row),
                  pl.BlockSpec((rb, D_INNER), row),
                  st_spec,
                  pl.BlockSpec((None, 1, D_INNER), lambda i: (l, 0, 0))] + xspecs,
        out_specs=[pl.BlockSpec((rb, D_INNER), row), st_spec],
        input_output_aliases=_alias_map(n_in, k_alias, 2) if k_alias else {},
        compiler_params=_cparams(("parallel",)),
        name="ssd_sample_state",
    )(cd, yd, eac, xw, bm, cm, z8, state, ng, *xargs)


def _prompt_buckets(dil):
    qi = jnp.arange(BLOCK)[:, None]
    kj = jnp.arange(2 * BLOCK)[None, :]
    dist = BLOCK + qi - kj
    return _masked_bucket(dist * dil, (dist >= 0) & (dist <= WIN))


def _sample_row_tokens(rows):
    r = jnp.arange(rows) % SROWS
    return jnp.where(r >= SVALID, r - SVALID, -1)[:, None]


def _new_row_buckets(t, same_token_only):
    r2 = jnp.arange(SROWS)[None, :]
    dn = t - (r2 - SVALID)
    ok = (r2 >= SVALID) & (dn >= 0)
    if same_token_only:
        ok = ok & (dn == 0)
    return jnp.where(t >= 0, _masked_bucket(dn, ok), 0).astype(jnp.int32)


def _sample_buckets_a():
    t = _sample_row_tokens(4 * SROWS)
    j = jnp.arange(WIN)[None, :]
    dist = WIN + t - j
    bc = jnp.where(t >= 0, _masked_bucket(dist, (dist >= 0) & (dist <= WIN)), 0)
    return bc.astype(jnp.int32), _new_row_buckets(t, False)


def _sample_buckets_c(lc, window, dil, span):
    t = _sample_row_tokens(2 * SROWS)
    pos = jnp.arange(lc - span, lc)[None, :]
    dist = lc + t - pos
    ok = (dist >= 0) & (dist <= window) & (dist % dil == 0)
    return jnp.where(t >= 0, _masked_bucket(dist, ok), 0).astype(jnp.int32)


def _rows_by_head(bias, group):
    h, r, _ = bias.shape
    rb = r // group
    return jnp.stack([jnp.concatenate([bias[g * group + i, i * rb:(i + 1) * rb] for i in range(group)], axis=0)
                      for g in range(h // group)])


def kernel(x_prompt, x_sample, cache_a_k, cache_a_v, cache_c_k, cache_c_v, state_ssm, state_conv,
           norm_mix_g, w_in, a_q_norm_g, a_k_norm_g, a_sinks, c_q_norm_g, c_k_norm_g, rel_bias,
           conv_w, conv_b, dt_bias, a_log, d_skip, ssm_norm_g, w_out, norm_mlp_g, w_up, w_down):
    depth = w_in.shape[0]
    n, seq, _ = x_prompt.shape
    ns, ts, _ = x_sample.shape
    lc = cache_c_k.shape[2]
    assert ts == SROWS - SVALID and seq % (BLOCK * 16) == 0
    assert cache_a_k.shape[2] == WIN and all(w <= lc for w, _ in C_BRANCHES)
    nb = seq // BLOCK

    w_in_b = jnp.pad(w_in, ((0, 0), (0, 0), (0, COL_END - IN_COLS))).astype(BF16)
    w_out_b, w_up_b, w_down_b = w_out.astype(BF16), w_up.astype(BF16), w_down.astype(BF16)
    ones = jnp.ones((depth, 128), F32)
    gain = jnp.concatenate([jnp.tile(a_q_norm_g, (1, 4)) * ATTN_SCALE, jnp.tile(a_k_norm_g, (1, 2)), ones,
                            jnp.tile(c_q_norm_g, (1, 4)) * ATTN_SCALE, jnp.tile(c_k_norm_g, (1, 4))], axis=1)
    lane_grp = jnp.arange(LANES) // HEAD_DIM
    bd = (lane_grp[:, None] == lane_grp[None, :]).astype(BF16)
    tri = (jnp.arange(BLOCK)[None, :] <= jnp.arange(BLOCK)[:, None]).astype(F32)
    ex = (jnp.arange(LANES)[:, None] == (jnp.arange(D_INNER) // HEAD_DIM)[None, :]).astype(F32)
    cw8 = jnp.pad(conv_w, ((0, 0), (0, SUBLANES - conv_w.shape[1]), (0, 0)))
    pad_h = lambda v: jnp.pad(v, ((0, 0), (0, LANES - N_SSM_HEADS)))[:, None, :]
    vec = lambda v: v[:, None, :]
    dtb, alog = pad_h(dt_bias), pad_h(a_log)
    dsk = vec(jnp.repeat(d_skip, HEAD_DIM, axis=1))
    g_mix, g_mlp, gain, cb, ng = vec(norm_mix_g), vec(norm_mlp_g), vec(gain), vec(conv_b), vec(ssm_norm_g)

    bias_a = _expand_bias(rel_bias, _prompt_buckets(1), 4, 0)
    bias_c = jnp.stack([_expand_bias(rel_bias, _prompt_buckets(d), 4, 4) for d in C_DILS])
    sa_c, sa_n = _sample_buckets_a()
    sbias_a_c = _rows_by_head(_expand_bias(rel_bias, sa_c, 4, 0), 4)
    sbias_a_n = _rows_by_head(_expand_bias(rel_bias, sa_n, 4, 0), 4)
    pair_bias = lambda bkt: _rows_by_head(_expand_bias(rel_bias, bkt, 4, 4), 2)
    (w1, d1), (w2, d2), (w3, d3) = C_BRANCHES
    sb12 = jnp.stack([pair_bias(_sample_buckets_c(lc, w1, d1, NEAR)), pair_bias(_sample_buckets_c(lc, w2, d2, NEAR))])
    sb3 = pair_bias(_sample_buckets_c(lc, w3, d3, lc))
    tok2 = _sample_row_tokens(2 * SROWS)
    sbn_all, sbn_same = pair_bias(_new_row_buckets(tok2, False)), pair_bias(_new_row_buckets(tok2, True))
    sbn = jnp.stack([sbn_all, sbn_same, sbn_same])

    cak_t, cav_t = cache_a_k.transpose(0, 1, 3, 4, 2), cache_a_v.transpose(0, 1, 3, 4, 2)
    cck_t, ccv_t = cache_c_k.transpose(0, 1, 3, 4, 2), cache_c_v.transpose(0, 1, 3, 4, 2)

    hp = x_prompt.reshape(n * seq, D_MODEL)
    hs = jnp.pad(x_sample, ((0, 0), (SVALID, 0), (0, 0))).reshape(ns * SROWS, D_MODEL)
    tm_p = 512
    tm_s = min(512, ns * SROWS)
    nbs_a = min(16, ns)
    nbs_c = min(2, ns)
    nbs_b = min(8, ns)

    p_ak = p_ck = p_ssm = s_ssm = None
    p_conv, s_small = [], [[] for _ in range(5)]
    for l in range(depth):
        qa, ka, va, qc, kc, vc, z, xbc, dt = _in_proj(hp, g_mix, w_in_b, gain, bd, l, tm_p)
        oa, *p_ak = _attn_a_prompt(qa, ka, va, bias_a, a_sinks, l, depth, n, seq, p_ak)
        oc, *p_ck = _attn_c_prompt(qc, kc, vc, bias_c, l, depth, n, seq, p_ck)
        ob, *p_ssm = _ssd_prompt(xbc, z, dt, cw8, cb, dtb, alog, dsk, ng, tri, l, depth, n, nb, p_ssm)
        hp = _out_mlp(hp, oa, ob, oc, w_out_b, g_mlp, w_up_b, w_down_b, l, tm_p)
        p_conv.append(xbc.reshape(n, seq, CONV_DIM)[:, seq - 3:])

        qa, ka, va, qc, kc, vc, z, xbc, dt = _in_proj(hs, g_mix, w_in_b, gain, bd, l, tm_s)
        oa = _attn_a_sample(qa, ka, va, cak_t, cav_t, sbias_a_c, sbias_a_n, a_sinks, l, nbs_a)
        oc = _attn_c_sample(qc, kc, vc, cck_t, ccv_t, sb12, sb3, sbn, l, nbs_c)
        prev8 = jnp.pad(state_conv[l], ((0, 0), (1, SROWS - 4), (0, 0))).reshape(ns * SROWS, CONV_DIM)
        yd, eac, xw, bm, cm, cd = _ssd_sample_pre(xbc, prev8, dt, cw8, cb, dtb, alog, dsk, ex, l,
                                                  min(256, ns * SROWS))
        cd_s = cd.reshape(ns, SROWS, LANES)[:, SROWS - 1, :N_SSM_HEADS]
        ob, *s_ssm = _ssd_sample_state(cd_s, yd, eac, xw, bm, cm, z, state_ssm, ng, l, depth, nbs_b, s_ssm)
        hs = _out_mlp(hs, oa, ob, oc, w_out_b, g_mlp, w_up_b, w_down_b, l, tm_s)
        new = lambda v, hh: v.reshape(ns, SROWS, hh, HEAD_DIM)[:, SVALID:]
        for dst, val in zip(s_small, (new(ka, 2), new(va, 2), new(kc, 4), new(vc, 4),
                                      xbc.reshape(ns, SROWS, CONV_DIM)[:, SROWS - 3:])):
            dst.append(val)

    y_prompt = hp.reshape(n, seq, D_MODEL)
    y_sample = hs.reshape(ns, SROWS, D_MODEL)[:, SVALID:]
    unminor = lambda buf, heads: buf.reshape(depth, n, heads, HEAD_DIM, -1).transpose(0, 1, 4, 2, 3)
    p_state = (unminor(p_ak[0], 2), unminor(p_ak[1], 2), unminor(p_ck[0], 4), unminor(p_ck[1], 4),
               p_ssm[0], jnp.stack(p_conv))
    s_state = tuple(jnp.stack(v) for v in s_small[:4]) + (s_ssm[0], jnp.stack(s_small[4]))
    return (y_prompt, y_sample) + p_state + s_state
```

```python
import functools
import math

import jax
import jax.numpy as jnp
from jax import lax
from jax.experimental import pallas as pl
from jax.experimental.pallas import tpu as pltpu

F32 = jnp.float32
BF16 = jnp.bfloat16
HIGHEST = lax.Precision.HIGHEST

D_MODEL = 1024
HEAD_DIM = 64
LANES = 128
SUBLANES = 8
BLOCK = 128
WIN = 128
D_INNER = 512
D_STATE = 128
N_SSM_HEADS = 8
CONV_DIM = 1024
D_FF = 4096
NUM_BUCKETS = 32
REL_MAX_DIST = 2048
EPS = 1e-6
ATTN_SCALE = HEAD_DIM ** -0.5
NEG = -1e30
C_BRANCHES = ((128, 1), (512, 4), (2048, 16))
C_DILS = tuple(d for _, d in C_BRANCHES)
SROWS = 8
SVALID = 4
NEAR = 512
C_UNROLL = 4
A_UNROLL = 2
SSD_SUB = 2

COL_QA, COL_KA, COL_VA, COL_QC, COL_KC, COL_VC, COL_Z, COL_XBC, COL_DT, COL_END = (
    0, 256, 384, 512, 768, 1024, 1280, 1792, 2816, 2944)
IN_COLS = 2824
VMEM_LIMIT = 56 * 1024 * 1024


def _cparams(sem):
    return pltpu.CompilerParams(dimension_semantics=sem, vmem_limit_bytes=VMEM_LIMIT)


def _dot(a, b):
    return jnp.dot(a, b, preferred_element_type=F32)


def _dot_t(a, b):
    return lax.dot_general(a, b, (((1,), (1,)), ((), ())), preferred_element_type=F32)


def _dot_t0(a, b):
    return lax.dot_general(a, b, (((0,), (0,)), ((), ())), preferred_element_type=F32)


def _lane_iota(shape):
    return lax.broadcasted_iota(jnp.int32, shape, len(shape) - 1)


def _silu(x):
    return x * (1.0 / (1.0 + jnp.exp(-x)))


def _softplus(x):
    return jnp.maximum(x, 0.0) + jnp.log(1.0 + jnp.exp(-jnp.abs(x)))


def _stacked(body, n_in, bufs):
    if bufs is None:
        return body, [], [], 0
    k = len(bufs)

    def wrapped(*refs):
        return body(*refs[:n_in], *refs[n_in + k:])

    return wrapped, [pl.BlockSpec(memory_space=pl.ANY)] * k, list(bufs), k


def _alias_map(n_in, k, n_out):
    return {n_in + i: n_out - k + i for i in range(k)}


def _bias_kernel(tab_ref, bkt_ref, o_ref, *, heads, col0):
    bkt = bkt_ref[...]
    for h in range(heads):
        acc = jnp.full(bkt.shape, NEG, F32)
        for b in range(NUM_BUCKETS):
            acc = jnp.where(bkt == b, tab_ref[b, col0 + h], acc)
        o_ref[h] = acc


def _expand_bias(table, bkt, heads, col0):
    r, c = bkt.shape
    return pl.pallas_call(
        functools.partial(_bias_kernel, heads=heads, col0=col0),
        out_shape=jax.ShapeDtypeStruct((heads, r, c), F32),
        in_specs=[pl.BlockSpec(memory_space=pltpu.SMEM),
                  pl.BlockSpec((r, c), lambda: (0, 0))],
        out_specs=pl.BlockSpec((heads, r, c), lambda: (0, 0, 0)),
        name="bias_expand",
    )(table, bkt)


def _t5_bucket(dist):
    max_exact = NUM_BUCKETS // 2
    d = jnp.maximum(dist, 0)
    df = jnp.maximum(d, 1).astype(F32)
    large = max_exact + (jnp.log(df / max_exact) / math.log(REL_MAX_DIST / max_exact)
                         * (NUM_BUCKETS - max_exact)).astype(jnp.int32)
    return jnp.where(d < max_exact, d, jnp.minimum(large, NUM_BUCKETS - 1))


def _masked_bucket(dist, valid):
    return jnp.where(valid, _t5_bucket(dist), -1).astype(jnp.int32)


def _head_norm(slab, gain, bd):
    sq = slab * slab
    hi = sq.astype(BF16)
    lo = (sq - hi.astype(F32)).astype(BF16)
    ss = _dot(hi, bd) + _dot(lo, bd)
    return slab * lax.rsqrt(ss * (1.0 / HEAD_DIM) + EPS) * gain


def _in_proj_kernel(h_ref, g_ref, w_ref, gain_ref, bd_ref,
                    qa_ref, ka_ref, va_ref, qc_ref, kc_ref, vc_ref, z_ref, xbc_ref, dt_ref):
    x = h_ref[...]
    ms = jnp.mean(x * x, axis=-1, keepdims=True)
    u = (x * lax.rsqrt(ms + EPS) * g_ref[...]).astype(BF16)
    bd = bd_ref[...]

    def proj(c0, c1):
        return _dot(u, w_ref[:, c0:c1])

    to_norm = [(COL_QA, COL_KA, qa_ref), (COL_KA, COL_VA, ka_ref), (COL_QC, COL_KC, qc_ref), (COL_KC, COL_VC, kc_ref)]
    raw = [proj(COL_QA, COL_KA), proj(COL_KA, COL_QC), proj(COL_QC, COL_KC), proj(COL_KC, COL_VC)]
    va_ref[...] = raw[1][:, COL_VA - COL_KA:]
    vc_ref[...] = proj(COL_VC, COL_Z)
    z_ref[...] = proj(COL_Z, COL_XBC)
    for (c0, c1, out_ref), p in zip(to_norm, raw):
        for s in range((c1 - c0) // LANES):
            sl = slice(s * LANES, (s + 1) * LANES)
            out_ref[:, sl] = _head_norm(p[:, sl], gain_ref[:, c0 + s * LANES:c0 + (s + 1) * LANES], bd)
    xbc_ref[...] = proj(COL_XBC, COL_DT)
    dt_ref[...] = proj(COL_DT, COL_END)


def _in_proj(h, g, w, gain, bd, l, tm):
    t = h.shape[0]
    widths = (256, 128, 128, 256, 256, 256, 512, 1024, 128)
    layer = lambda i: (l, 0, 0)
    return pl.pallas_call(
        _in_proj_kernel,
        grid=(t // tm,),
        out_shape=[jax.ShapeDtypeStruct((t, w_), F32) for w_ in widths],
        in_specs=[pl.BlockSpec((tm, D_MODEL), lambda i: (i, 0)),
                  pl.BlockSpec((None, 1, D_MODEL), layer),
                  pl.BlockSpec((None, D_MODEL, COL_END), layer),
                  pl.BlockSpec((None, 1, D_MODEL), layer),
                  pl.BlockSpec((LANES, LANES), lambda i: (0, 0))],
        out_specs=[pl.BlockSpec((tm, w_), lambda i: (i, 0)) for w_ in widths],
        compiler_params=_cparams(("parallel",)),
        name="in_proj",
    )(h, g, w, gain, bd)


def _out_mlp_kernel(h_ref, oa_ref, ob_ref, oc_ref, wo_ref, g_ref, wu_ref, wd_ref, o_ref):
    acc = _dot(oa_ref[...].astype(BF16), wo_ref[0:256, :])
    acc += _dot(ob_ref[...].astype(BF16), wo_ref[256:768, :])
    acc += _dot(oc_ref[...].astype(BF16), wo_ref[768:1024, :])
    h2 = h_ref[...] + acc
    ms = jnp.mean(h2 * h2, axis=-1, keepdims=True)
    u = (h2 * lax.rsqrt(ms + EPS) * g_ref[...]).astype(BF16)
    ff_chunk = 1024
    mlp = jnp.zeros_like(h2)
    for c in range(D_FF // ff_chunk):
        a = _dot(u, wu_ref[:, c * ff_chunk:(c + 1) * ff_chunk])
        a = jnp.square(jnp.maximum(a, 0.0)).astype(BF16)
        mlp += _dot(a, wd_ref[c * ff_chunk:(c + 1) * ff_chunk, :])
    o_ref[...] = h2 + mlp


def _out_mlp(h, oa, ob, oc, wo, g, wu, wd, l, tm):
    t = h.shape[0]
    layer = lambda i: (l, 0, 0)
    row = lambda i: (i, 0)
    return pl.pallas_call(
        _out_mlp_kernel,
        grid=(t // tm,),
        out_shape=jax.ShapeDtypeStruct((t, D_MODEL), F32),
        in_specs=[pl.BlockSpec((tm, D_MODEL), row),
                  pl.BlockSpec((tm, 256), row),
                  pl.BlockSpec((tm, 512), row),
                  pl.BlockSpec((tm, 256), row),
                  pl.BlockSpec((None, D_MODEL, D_MODEL), layer),
                  pl.BlockSpec((None, 1, D_MODEL), layer),
                  pl.BlockSpec((None, D_MODEL, D_FF), layer),
                  pl.BlockSpec((None, D_FF, D_MODEL), layer)],
        out_specs=pl.BlockSpec((tm, D_MODEL), row),
        compiler_params=_cparams(("parallel",)),
        name="out_mlp",
    )(h, oa, ob, oc, wo, g, wu, wd)


def _attn_a_prompt_kernel(sink_ref, q_ref, k_ref, v_ref, bias_ref, o_ref, kt_ref, vt_ref,
                          k_st, k_sw, v_st, v_sw, *, l, seq):
    lo = _lane_iota((BLOCK, LANES)) < HEAD_DIM
    k = k_ref[...]
    v = v_ref[...]
    k_st[...] = k.astype(BF16)
    v_st[...] = v.astype(BF16)
    k_sw[...] = pltpu.roll(k, HEAD_DIM, 1).astype(BF16)
    v_sw[...] = pltpu.roll(v, HEAD_DIM, 1).astype(BF16)
    kt_ref[...] = k_ref[seq - WIN:seq, :].T
    vt_ref[...] = v_ref[seq - WIN:seq, :].T

    def body(g, carry):
        scores, where = [], []
        for u in range(A_UNROLL):
            b = g * A_UNROLL + u
            cur = pl.ds(pl.multiple_of(b * BLOCK, BLOCK), BLOCK)
            prev = pl.ds(pl.multiple_of(jnp.maximum(b - 1, 0) * BLOCK, BLOCK), BLOCK)
            where.append((cur, prev))
            for h in range(4):
                slab = q_ref[cur, (h // 2) * LANES:(h // 2 + 1) * LANES]
                qm = jnp.where(lo if h % 2 == 0 else ~lo, slab, 0.0).astype(BF16)
                kk = k_st if h in (0, 3) else k_sw
                s_c = _dot_t(qm, kk[cur, :]) + bias_ref[h, :, BLOCK:2 * BLOCK]
                s_p = jnp.where(b == 0, NEG, _dot_t(qm, kk[prev, :]) + bias_ref[h, :, 0:BLOCK])
                scores.append((s_c, s_p))
        probs = []
        for i, (s_c, s_p) in enumerate(scores):
            sink = sink_ref[l, i % 4]
            m = jnp.maximum(jnp.max(jnp.maximum(s_c, s_p), axis=-1, keepdims=True), sink)
            p_c, p_p = jnp.exp(s_c - m), jnp.exp(s_p - m)
            den = jnp.sum(p_c + p_p, axis=-1, keepdims=True) + jnp.exp(sink - m)
            probs.append((p_c.astype(BF16), p_p.astype(BF16), den))
        for u, (cur, prev) in enumerate(where):
            outs = []
            for h in range(4):
                p_c, p_p, den = probs[4 * u + h]
                vv = v_st if h in (0, 3) else v_sw
                outs.append((_dot(p_c, vv[cur, :]) + _dot(p_p, vv[prev, :])) / den)
            o_ref[cur, 0:LANES] = jnp.where(lo, outs[0], outs[1])
            o_ref[cur, LANES:2 * LANES] = jnp.where(lo, outs[2], outs[3])
        return carry

    lax.fori_loop(0, seq // BLOCK // A_UNROLL, body, 0)


def _attn_a_prompt(q, k, v, bias, sinks, l, depth, n, seq, bufs):
    t = q.shape[0]
    n_in = 5
    body, xspecs, xargs, k_alias = _stacked(functools.partial(_attn_a_prompt_kernel, l=l, seq=seq), n_in, bufs)
    st_shape = jax.ShapeDtypeStruct((depth, n, LANES, WIN), F32)
    st_spec = pl.BlockSpec((None, None, LANES, WIN), lambda i: (l, i, 0, 0))
    tok = lambda w: pl.BlockSpec((seq, w), lambda i: (i, 0))
    return pl.pallas_call(
        body,
        grid=(n,),
        out_shape=[jax.ShapeDtypeStruct((t, 256), F32), st_shape, st_shape],
        in_specs=[pl.BlockSpec(memory_space=pltpu.SMEM), tok(256), tok(LANES), tok(LANES),
                  pl.BlockSpec((4, BLOCK, 2 * BLOCK), lambda i: (0, 0, 0))] + xspecs,
        out_specs=[tok(256), st_spec, st_spec],
        scratch_shapes=[pltpu.VMEM((seq, LANES), BF16)] * 4,
        input_output_aliases=_alias_map(n_in, k_alias, 3) if k_alias else {},
        compiler_params=_cparams(("parallel",)),
        name="attn_a_prompt",
    )(sinks, q, k, v, bias, *xargs)


def _attn_c_prompt_kernel(q_ref, k_ref, v_ref, bias_ref, o_ref, kt_ref, vt_ref, ob_scr, lse_scr, *, seq):
    lane = _lane_iota((BLOCK, LANES))
    lo = lane < HEAD_DIM
    kt_ref[...] = k_ref[...].T
    vt_ref[...] = v_ref[...].T

    def blocks(br, dil, where, has_prev):
        rows = lambda ref, s: ref[pl.ds(s, BLOCK, stride=dil), :]
        scores, vals = [], []
        for qs, ps, first in where:
            q = rows(q_ref, qs)
            kc = rows(k_ref, qs).astype(BF16)
            kp = rows(k_ref, ps).astype(BF16) if has_prev else None
            vals.append((rows(v_ref, qs).astype(BF16), rows(v_ref, ps).astype(BF16) if has_prev else None))
            for hh in range(2):
                qm = jnp.where(lo if hh == 0 else ~lo, q, 0.0).astype(BF16)
                s_c = _dot_t(qm, kc) + bias_ref[br, hh, :, BLOCK:2 * BLOCK]
                s_p = None
                if has_prev:
                    s_p = jnp.where(first, NEG, _dot_t(qm, kp) + bias_ref[br, hh, :, 0:BLOCK])
                scores.append((s_c, s_p))
        probs = []
        for s_c, s_p in scores:
            if has_prev:
                m = jnp.max(jnp.maximum(s_c, s_p), axis=-1, keepdims=True)
                p_c, p_p = jnp.exp(s_c - m), jnp.exp(s_p - m)
                den = jnp.sum(p_c + p_p, axis=-1, keepdims=True)
                probs.append((p_c.astype(BF16), p_p.astype(BF16), den, m))
            else:
                m = jnp.max(s_c, axis=-1, keepdims=True)
                p_c = jnp.exp(s_c - m)
                probs.append((p_c.astype(BF16), None, jnp.sum(p_c, axis=-1, keepdims=True), m))
        for u, (qs, _, _) in enumerate(where):
            outs, lses = [], []
            for hh in range(2):
                p_c, p_p, den, m = probs[2 * u + hh]
                vc, vp = vals[u]
                o = _dot(p_c, vc)
                if has_prev:
                    o = o + _dot(p_p, vp)
                outs.append(o / den)
                lses.append(jnp.broadcast_to(m + jnp.log(den), (BLOCK, LANES)))
            ob_scr[br, pl.ds(qs, BLOCK, stride=dil), :] = jnp.where(lo, outs[0], outs[1])
            lse_scr[br, pl.ds(qs, BLOCK, stride=dil), :] = jnp.where(lo, lses[0], lses[1])

    for br, dil in enumerate(C_DILS):
        nb = seq // dil // BLOCK
        span = BLOCK * dil

        def body(g, carry, br=br, dil=dil, nb=nb, span=span):
            where = []
            for u in range(C_UNROLL):
                i = g * C_UNROLL + u
                res = i % dil
                b = i // dil
                where.append((res + b * span, res + jnp.maximum(b - 1, 0) * span, b == 0))
            blocks(br, dil, where, nb > 1)
            return carry

        lax.fori_loop(0, dil * nb // C_UNROLL, body, 0)

    def merge(i, carry):
        rows = pl.ds(pl.multiple_of(i * BLOCK, BLOCK), BLOCK)
        l0, l1, l2 = lse_scr[0, rows, :], lse_scr[1, rows, :], lse_scr[2, rows, :]
        m = jnp.maximum(jnp.maximum(l0, l1), l2)
        w0, w1, w2 = jnp.exp(l0 - m), jnp.exp(l1 - m), jnp.exp(l2 - m)
        num = w0 * ob_scr[0, rows, :] + w1 * ob_scr[1, rows, :] + w2 * ob_scr[2, rows, :]
        o_ref[rows, :] = num / (w0 + w1 + w2)
        return carry

    lax.fori_loop(0, seq // BLOCK, merge, 0)


def _attn_c_prompt(q, k, v, bias, l, depth, n, seq, bufs):
    t = q.shape[0]
    blk = pl.BlockSpec((seq, LANES), lambda i, hp: (i, hp))
    n_in = 4
    body, xspecs, xargs, k_alias = _stacked(functools.partial(_attn_c_prompt_kernel, seq=seq), n_in, bufs)
    st_shape = jax.ShapeDtypeStruct((depth, n, 256, seq), F32)
    st_spec = pl.BlockSpec((None, None, LANES, seq), lambda i, hp: (l, i, hp, 0))
    return pl.pallas_call(
        body,
        grid=(n, 2),
        out_shape=[jax.ShapeDtypeStruct((t, 256), F32), st_shape, st_shape],
        in_specs=[blk, blk, blk,
                  pl.BlockSpec((3, 2, BLOCK, 2 * BLOCK), lambda i, hp: (0, hp, 0, 0))] + xspecs,
        out_specs=[blk, st_spec, st_spec],
        scratch_shapes=[pltpu.VMEM((3, seq, LANES), F32), pltpu.VMEM((3, seq, LANES), F32)],
        input_output_aliases=_alias_map(n_in, k_alias, 3) if k_alias else {},
        compiler_params=_cparams(("parallel", "parallel")),
        name="attn_c_prompt",
    )(q, k, v, bias, *xargs)


def _gated_group_norm(y, z, ng):
    gated = y * _silu(z)
    parts = []
    for grp in range(2):
        gsl = gated[:, grp * 256:(grp + 1) * 256]
        ms = jnp.mean(gsl * gsl, axis=-1, keepdims=True)
        parts.append(gsl * lax.rsqrt(ms + EPS))
    return jnp.concatenate(parts, axis=1) * ng


def _ssd_prompt_kernel(xbc_ref, z_ref, dt_ref, cw_ref, cb_ref, dtb_ref, alog_ref,
                       dsk_ref, ng_ref, tri_ref, ob_ref, hl_ref, hst, xbuf):
    c = pl.program_id(1)

    @pl.when(c == 0)
    def _():
        hst[...] = jnp.zeros_like(hst)
        xbuf[0:SUBLANES, :] = jnp.zeros((SUBLANES, CONV_DIM), F32)

    rows = SSD_SUB * BLOCK
    x = xbc_ref[...]
    xbuf[SUBLANES:SUBLANES + rows, :] = x
    y = cb_ref[...] + cw_ref[3:4, :] * x
    for kk in range(3):
        y = y + cw_ref[kk:kk + 1, :] * xbuf[SUBLANES - 3 + kk:SUBLANES - 3 + kk + rows, :]
    xbuf[0:SUBLANES, :] = x[rows - SUBLANES:rows, :]
    xa_all = _silu(y)
    dt_all = _softplus(dt_ref[...] + dtb_ref[...])
    a_all = dt_all * (-jnp.exp(alog_ref[...]))
    row = lax.broadcasted_iota(jnp.int32, (BLOCK, BLOCK), 0)
    lane = _lane_iota((BLOCK, BLOCK))
    causal = lane <= row
    lo = lane < HEAD_DIM
    top = row < HEAD_DIM
    n_pairs = N_SSM_HEADS // 2
    for sub in range(SSD_SUB):
        rs = slice(sub * BLOCK, (sub + 1) * BLOCK)
        xa, dt = xa_all[rs], dt_all[rs]
        xs = xa[:, 0:D_INNER]
        acum = jnp.dot(tri_ref[...], a_all[rs], precision=HIGHEST, preferred_element_type=F32)
        bms = [xa[:, D_INNER + g * D_STATE:D_INNER + (g + 1) * D_STATE].astype(BF16) for g in range(2)]
        cms = [xa[:, D_INNER + (2 + g) * D_STATE:D_INNER + (3 + g) * D_STATE].astype(BF16) for g in range(2)]
        cbs = [_dot_t(cms[g], bms[g]) for g in range(2)]
        hps = [hst[2 * j:2 * j + 2].reshape(BLOCK, D_STATE) for j in range(n_pairs)]
        y_off = [_dot_t(cms[j // 2], hps[j].astype(BF16)) for j in range(n_pairs)]
        last = acum[BLOCK - 1:BLOCK, :]
        w_end = jnp.exp(last - acum) * dt
        e_last = jnp.exp(last)
        xs_pairs = [xs[:, j * LANES:(j + 1) * LANES] for j in range(n_pairs)]
        for j in range(n_pairs):
            h0_, h1_ = 2 * j, 2 * j + 1
            w_pair = jnp.where(lo, w_end[:, h0_:h0_ + 1], w_end[:, h1_:h1_ + 1])
            s_new = _dot_t0((xs_pairs[j] * w_pair).astype(BF16), bms[j // 2])
            cd_pair = jnp.where(top, e_last[:, h0_:h0_ + 1], e_last[:, h1_:h1_ + 1])
            hst[h0_:h0_ + 2] = (hps[j] * cd_pair + s_new).reshape(2, HEAD_DIM, D_STATE)
        acum_t = acum.T
        dt_t = dt.T
        eac = jnp.exp(acum)
        y_parts = []
        for j in range(n_pairs):
            xs_b = xs_pairs[j].astype(BF16)
            yd = []
            for hh in range(2):
                h = 2 * j + hh
                seg = acum[:, h:h + 1] - acum_t[h:h + 1, :]
                dec = jnp.exp(jnp.where(causal, seg, NEG)) * dt_t[h:h + 1, :]
                yd.append(_dot((cbs[j // 2] * dec).astype(BF16), xs_b))
            e_pair = jnp.where(lo, eac[:, 2 * j:2 * j + 1], eac[:, 2 * j + 1:2 * j + 2])
            y_parts.append(jnp.where(lo, yd[0], yd[1]) + y_off[j] * e_pair)
        y = jnp.concatenate(y_parts, axis=1) + dsk_ref[...] * xs
        ob_ref[rs, :] = _gated_group_norm(y, z_ref[rs, :], ng_ref[...])

    @pl.when(c == pl.num_programs(1) - 1)
    def _():
        hl_ref[...] = hst[...]


def _ssd_prompt(xbc, z, dt, cw8, cb, dtb, alog, dsk, ng, tri, l, depth, n, nc, bufs):
    t = xbc.shape[0]
    nc = nc // SSD_SUB
    rows = SSD_SUB * BLOCK
    row = lambda i, c: (i * nc + c, 0)
    layer = lambda i, c: (l, 0, 0)
    n_in = 10
    body, xspecs, xargs, k_alias = _stacked(_ssd_prompt_kernel, n_in, bufs)
    st = (N_SSM_HEADS, HEAD_DIM, D_STATE)
    return pl.pallas_call(
        body,
        grid=(n, nc),
        out_shape=[jax.ShapeDtypeStruct((t, D_INNER), F32),
                   jax.ShapeDtypeStruct((depth, n) + st, F32)],
        in_specs=[pl.BlockSpec((rows, CONV_DIM), row),
                  pl.BlockSpec((rows, D_INNER), row),
                  pl.BlockSpec((rows, LANES), row),
                  pl.BlockSpec((None, SUBLANES, CONV_DIM), layer),
                  pl.BlockSpec((None, 1, CONV_DIM), layer),
                  pl.BlockSpec((None, 1, LANES), layer),
                  pl.BlockSpec((None, 1, LANES), layer),
                  pl.BlockSpec((None, 1, D_INNER), layer),
                  pl.BlockSpec((None, 1, D_INNER), layer),
                  pl.BlockSpec((BLOCK, BLOCK), lambda i, c: (0, 0))] + xspecs,
        out_specs=[pl.BlockSpec((rows, D_INNER), row),
                   pl.BlockSpec((None, None) + st, lambda i, c: (l, i, 0, 0, 0))],
        scratch_shapes=[pltpu.VMEM(st, F32),
                        pltpu.VMEM((SUBLANES + rows, CONV_DIM), F32)],
        input_output_aliases=_alias_map(n_in, k_alias, 2) if k_alias else {},
        compiler_params=_cparams(("parallel", "arbitrary")),
        name="ssd_prompt",
    )(xbc, z, dt, cw8, cb, dtb, alog, dsk, ng, tri, *xargs)


def _attn_a_sample_kernel(sink_ref, q_ref, kn_ref, vn_ref, kt_ref, vt_ref, bc_ref, bn_ref, o_ref, *, nbs, l):
    rb = nbs * SROWS
    lo = _lane_iota((rb, LANES)) < HEAD_DIM
    s0, s1 = q_ref[:, 0:LANES], q_ref[:, LANES:2 * LANES]
    per_seq = lambda v: v.reshape(nbs, SROWS, LANES)
    qm = jnp.concatenate([
        per_seq(jnp.where(lo, s0, 0.0)),
        per_seq(jnp.where(lo, pltpu.roll(s0, HEAD_DIM, 1), 0.0)),
        per_seq(jnp.where(lo, 0.0, pltpu.roll(s1, HEAD_DIM, 1))),
        per_seq(jnp.where(lo, 0.0, s1))], axis=1).astype(BF16)
    kt = kt_ref[...].reshape(nbs, LANES, WIN).astype(BF16)
    vt = vt_ref[...].reshape(nbs, LANES, WIN).astype(BF16)
    kn = kn_ref[...].reshape(nbs, SROWS, LANES).astype(BF16)
    vn = vn_ref[...].reshape(nbs, SROWS, LANES).astype(BF16)
    s_c = jnp.einsum('nqd,ndk->nqk', qm, kt, preferred_element_type=F32) + bc_ref[...]
    s_n = jnp.einsum('nqd,nkd->nqk', qm, kn, preferred_element_type=F32) + bn_ref[...]
    hrow = lax.broadcasted_iota(jnp.int32, (nbs, 4 * SROWS, 1), 1) // SROWS
    sink = jnp.where(hrow == 0, sink_ref[l, 0], jnp.where(hrow == 1, sink_ref[l, 1],
                     jnp.where(hrow == 2, sink_ref[l, 2], sink_ref[l, 3])))
    m = jnp.maximum(jnp.max(s_c, axis=-1, keepdims=True), jnp.max(s_n, axis=-1, keepdims=True))
    m = jnp.maximum(m, sink)
    p_c = jnp.exp(s_c - m)
    p_n = jnp.exp(s_n - m)
    den = jnp.sum(p_c, axis=-1, keepdims=True) + jnp.sum(p_n, axis=-1, keepdims=True) + jnp.exp(sink - m)
    o = (jnp.einsum('nqk,ndk->nqd', p_c.astype(BF16), vt, preferred_element_type=F32)
         + jnp.einsum('nqk,nkd->nqd', p_n.astype(BF16), vn, preferred_element_type=F32)) / den
    o0, o1, o2, o3 = (o[:, i * SROWS:(i + 1) * SROWS, :].reshape(rb, LANES) for i in range(4))
    o_ref[:, 0:LANES] = jnp.where(lo, o0, pltpu.roll(o1, HEAD_DIM, 1))
    o_ref[:, LANES:2 * LANES] = jnp.where(lo, pltpu.roll(o2, HEAD_DIM, 1), o3)


def _attn_a_sample(q8, k8, v8, cache_kt, cache_vt, bias_c, bias_n, sinks, l, nbs):
    ns = cache_kt.shape[1]
    tok = lambda i: (i, 0)
    cache = pl.BlockSpec((None, nbs, 2, HEAD_DIM, WIN), lambda i: (l, i, 0, 0, 0))
    c3 = lambda i: (0, 0, 0)
    return pl.pallas_call(
        functools.partial(_attn_a_sample_kernel, nbs=nbs, l=l),
        grid=(ns // nbs,),
        out_shape=jax.ShapeDtypeStruct((ns * SROWS, 256), F32),
        in_specs=[pl.BlockSpec(memory_space=pltpu.SMEM),
                  pl.BlockSpec((nbs * SROWS, 256), tok),
                  pl.BlockSpec((nbs * SROWS, LANES), tok),
                  pl.BlockSpec((nbs * SROWS, LANES), tok),
                  cache, cache,
                  pl.BlockSpec((1, 4 * SROWS, WIN), c3),
                  pl.BlockSpec((1, 4 * SROWS, SROWS), c3)],
        out_specs=pl.BlockSpec((nbs * SROWS, 256), tok),
        compiler_params=_cparams(("parallel",)),
        name="attn_a_sample",
    )(sinks, q8, k8, v8, cache_kt, cache_vt, bias_c, bias_n)


def _softmax_parts(parts):
    m = None
    for s in parts:
        mm = jnp.max(s, axis=-1, keepdims=True)
        m = mm if m is None else jnp.maximum(m, mm)
    ps = [jnp.exp(s - m) for s in parts]
    den = None
    for p in ps:
        dd = jnp.sum(p, axis=-1, keepdims=True)
        den = dd if den is None else den + dd
    return ps, den, m + jnp.log(den)


def _attn_c_sample_kernel(q_ref, kn_ref, vn_ref, kt_ref, vt_ref, b12_ref, b3_ref, bn_ref, o_ref, *, nbs, lc):
    lo = _lane_iota((SROWS, LANES)) < HEAD_DIM
    pr = 2 * SROWS
    units = [(n, j, slice(n * SROWS, (n + 1) * SROWS), slice(j * LANES, (j + 1) * LANES))
             for n in range(nbs) for j in range(2)]
    scores = []
    for n, j, rows, lanes in units:
        qp = q_ref[rows, lanes]
        lhs = jnp.concatenate([jnp.where(lo, qp, 0.0), jnp.where(lo, 0.0, qp)], axis=0).astype(BF16)
        ktp = kt_ref[n, 2 * j:2 * j + 2].reshape(LANES, lc).astype(BF16)
        scores.append((_dot(lhs, ktp), _dot_t(lhs, kn_ref[rows, lanes].astype(BF16))))
    probs = []
    for (n, j, rows, lanes), (s, sn) in zip(units, scores):
        s_near = s[:, lc - NEAR:]
        (p1, p1n), d1, l1 = _softmax_parts([s_near + b12_ref[0, j], sn + bn_ref[0, j]])
        (p2, p2n), d2, l2 = _softmax_parts([s_near + b12_ref[1, j], sn + bn_ref[1, j]])
        (p3, p3n), d3, l3 = _softmax_parts([s + b3_ref[j], sn + bn_ref[2, j]])
        m = jnp.maximum(jnp.maximum(l1, l2), l3)
        w1, w2, w3 = jnp.exp(l1 - m), jnp.exp(l2 - m), jnp.exp(l3 - m)
        wsum = w1 + w2 + w3
        probs.append((jnp.concatenate([p1, p2], axis=0).astype(BF16), p3.astype(BF16),
                      (p1n.astype(BF16), p2n.astype(BF16), p3n.astype(BF16)),
                      (w1 / (d1 * wsum), w2 / (d2 * wsum), w3 / (d3 * wsum))))
    for (n, j, rows, lanes), (p12, p3, pn, wts) in zip(units, probs):
        vtp = vt_ref[n, 2 * j:2 * j + 2].reshape(LANES, lc).astype(BF16)
        vnp = vn_ref[rows, lanes].astype(BF16)
        o12 = _dot_t(p12, vtp[:, lc - NEAR:])
        o = ((o12[0:pr] + _dot(pn[0], vnp)) * wts[0] + (o12[pr:2 * pr] + _dot(pn[1], vnp)) * wts[1]
             + (_dot_t(p3, vtp) + _dot(pn[2], vnp)) * wts[2])
        o_ref[rows, lanes] = jnp.where(lo, o[0:SROWS], o[SROWS:pr])


def _attn_c_sample(q8, k8, v8, cache_kt, cache_vt, b12, b3, bn, l, nbs):
    ns, lc = cache_kt.shape[1], cache_kt.shape[4]
    tok = lambda i: (i, 0)
    cache = pl.BlockSpec((None, nbs, 4, HEAD_DIM, lc), lambda i: (l, i, 0, 0, 0))
    c3 = lambda i: (0, 0, 0)
    c4 = lambda i: (0, 0, 0, 0)
    return pl.pallas_call(
        functools.partial(_attn_c_sample_kernel, nbs=nbs, lc=lc),
        grid=(ns // nbs,),
        out_shape=jax.ShapeDtypeStruct((ns * SROWS, 256), F32),
        in_specs=[pl.BlockSpec((nbs * SROWS, 256), tok),
                  pl.BlockSpec((nbs * SROWS, 256), tok),
                  pl.BlockSpec((nbs * SROWS, 256), tok),
                  cache, cache,
                  pl.BlockSpec((2, 2, 2 * SROWS, NEAR), c4),
                  pl.BlockSpec((2, 2 * SROWS, lc), c3),
                  pl.BlockSpec((3, 2, 2 * SROWS, SROWS), c4)],
        out_specs=pl.BlockSpec((nbs * SROWS, 256), tok),
        compiler_params=_cparams(("parallel",)),
        name="attn_c_sample",
    )(q8, k8, v8, cache_kt, cache_vt, b12, b3, bn)


def _ssd_sample_pre_kernel(xbc_ref, prev_ref, dt_ref, cw_ref, cb_ref, dtb_ref, alog_ref, dsk_ref, ex_ref,
                           yd_ref, eac_ref, xw_ref, bm_ref, cm_ref, cd_ref):
    r = xbc_ref.shape[0]
    rid = lax.broadcasted_iota(jnp.int32, (r, 1), 0) % SROWS
    x = jnp.where(jnp.logical_and(rid >= 1, rid < SVALID), prev_ref[...], xbc_ref[...])
    y = cb_ref[...] + cw_ref[3:4, :] * x
    for d in range(1, 4):
        y = y + cw_ref[3 - d:4 - d, :] * pltpu.roll(x, d, 0)
    xa = _silu(y)
    xs = xa[:, 0:D_INNER]
    bm = xa[:, D_INNER:D_INNER + 2 * D_STATE]
    cm = xa[:, D_INNER + 2 * D_STATE:CONV_DIM]
    valid = rid >= SVALID
    dt = jnp.where(valid, _softplus(dt_ref[...] + dtb_ref[...]), 0.0)
    a = dt * (-jnp.exp(alog_ref[...]))
    acum = a
    rem = jnp.zeros_like(a)
    for d in range(1, 4):
        acum = acum + jnp.where(rid - d >= SVALID, pltpu.roll(a, d, 0), 0.0)
        rem = rem + jnp.where(rid + d < SROWS, pltpu.roll(a, r - d, 0), 0.0)
    ex = ex_ref[...]

    def expand(v):
        return jnp.dot(v, ex, precision=HIGHEST, preferred_element_type=F32)

    lane = _lane_iota((r, LANES))
    y_acc = dsk_ref[...] * xs
    for d in range(4):
        ok = rid - d >= SVALID
        bsh = bm if d == 0 else pltpu.roll(bm, d, 0)
        cb0 = jnp.sum(cm[:, 0:D_STATE] * bsh[:, 0:D_STATE], axis=-1, keepdims=True)
        cb1 = jnp.sum(cm[:, D_STATE:] * bsh[:, D_STATE:], axis=-1, keepdims=True)
        cbh = jnp.where(lane < N_SSM_HEADS // 2, cb0, cb1)
        if d == 0:
            coef = cbh * dt
            xsh = xs
        else:
            dec = jnp.exp(jnp.where(ok, acum - pltpu.roll(acum, d, 0), NEG))
            coef = cbh * dec * pltpu.roll(dt, d, 0)
            xsh = pltpu.roll(xs, d, 0)
        y_acc = y_acc + expand(jnp.where(ok, coef, 0.0)) * xsh
    yd_ref[...] = y_acc
    eac_ref[...] = expand(jnp.exp(acum))
    xw_ref[...] = xs * expand(dt * jnp.exp(rem))
    bm_ref[...] = bm
    cm_ref[...] = cm
    cd_ref[...] = jnp.exp(acum + rem)


def _ssd_sample_pre(xbc8, prev8, dt8, cw8, cb, dtb, alog, dsk, ex, l, rb):
    r = xbc8.shape[0]
    row = lambda i: (i, 0)
    layer = lambda i: (l, 0, 0)
    widths = (D_INNER, D_INNER, D_INNER, 2 * D_STATE, 2 * D_STATE, LANES)
    return pl.pallas_call(
        _ssd_sample_pre_kernel,
        grid=(r // rb,),
        out_shape=[jax.ShapeDtypeStruct((r, w_), F32) for w_ in widths],
        in_specs=[pl.BlockSpec((rb, CONV_DIM), row),
                  pl.BlockSpec((rb, CONV_DIM), row),
                  pl.BlockSpec((rb, LANES), row),
                  pl.BlockSpec((None, SUBLANES, CONV_DIM), layer),
                  pl.BlockSpec((None, 1, CONV_DIM), layer),
                  pl.BlockSpec((None, 1, LANES), layer),
                  pl.BlockSpec((None, 1, LANES), layer),
                  pl.BlockSpec((None, 1, D_INNER), layer),
                  pl.BlockSpec((LANES, D_INNER), lambda i: (0, 0))],
        out_specs=[pl.BlockSpec((rb, w_), row) for w_ in widths],
        compiler_params=_cparams(("parallel",)),
        name="ssd_sample_pre",
    )(xbc8, prev8, dt8, cw8, cb, dtb, alog, dsk, ex)


def _ssd_sample_state_kernel(cd_ref, yd_ref, eac_ref, xw_ref, bm_ref, cm_ref, z_ref, h0_ref, ng_ref,
                             ob_ref, hn_ref, *, nbs):
    base = pl.program_id(0) * nbs
    yo = []
    for n in range(nbs):
        rows = slice(n * SROWS, (n + 1) * SROWS)
        parts = []
        for g in range(2):
            hp = h0_ref[n, 4 * g:4 * g + 4].reshape(4 * HEAD_DIM, D_STATE)
            cmg = cm_ref[rows, g * D_STATE:(g + 1) * D_STATE].astype(BF16)
            bmg = bm_ref[rows, g * D_STATE:(g + 1) * D_STATE].astype(BF16)
            parts.append(_dot_t(cmg, hp.astype(BF16)))
            s_new = _dot_t0(xw_ref[rows, g * 256:(g + 1) * 256].astype(BF16), bmg)
            for hh in range(4):
                h = 4 * g + hh
                hn_ref[n, h] = (hp[hh * HEAD_DIM:(hh + 1) * HEAD_DIM, :] * cd_ref[base + n, h]
                                + s_new[hh * HEAD_DIM:(hh + 1) * HEAD_DIM, :])
        yo.append(jnp.concatenate(parts, axis=1))
    y = yd_ref[...] + eac_ref[...] * jnp.concatenate(yo, axis=0)
    ob_ref[...] = _gated_group_norm(y, z_ref[...], ng_ref[...])


def _ssd_sample_state(cd, yd, eac, xw, bm, cm, z8, state, ng, l, depth, nbs, bufs):
    ns = state.shape[1]
    row = lambda i: (i, 0)
    st_spec = pl.BlockSpec((None, nbs, N_SSM_HEADS, HEAD_DIM, D_STATE), lambda i: (l, i, 0, 0, 0))
    rb = nbs * SROWS
    n_in = 9
    body, xspecs, xargs, k_alias = _stacked(functools.partial(_ssd_sample_state_kernel, nbs=nbs), n_in, bufs)
    return pl.pallas_call(
        body,
        grid=(ns // nbs,),
        out_shape=[jax.ShapeDtypeStruct((ns * SROWS, D_INNER), F32),
                   jax.ShapeDtypeStruct(state.shape, F32)],
        in_specs=[pl.BlockSpec(memory_space=pltpu.SMEM),
                  pl.BlockSpec((rb, D_INNER), row),
                  pl.BlockSpec((rb, D_INNER), row),
                  pl.BlockSpec((rb, D_INNER), row),
                  pl.BlockSpec((rb, 2 * D_STATE), row),
                  pl.BlockSpec((rb, 2 * D_STATE), row),
                  pl.BlockSpec((rb, D_INNER), row),
                  st_spec,
                  pl.BlockSpec((None, 1, D_INNER), lambda i: (l, 0, 0))] + xspecs,
        out_specs=[pl.BlockSpec((rb, D_INNER), row), st_spec],
        input_output_aliases=_alias_map(n_in, k_alias, 2) if k_alias else {},
        compiler_params=_cparams(("parallel",)),
        name="ssd_sample_state",
    )(cd, yd, eac, xw, bm, cm, z8, state, ng, *xargs)


def _prompt_buckets(dil):
    qi = jnp.arange(BLOCK)[:, None]
    kj = jnp.arange(2 * BLOCK)[None, :]
    dist = BLOCK + qi - kj
    return _masked_bucket(dist * dil, (dist >= 0) & (dist <= WIN))


def _sample_row_tokens(rows):
    r = jnp.arange(rows) % SROWS
    return jnp.where(r >= SVALID, r - SVALID, -1)[:, None]


def _new_row_buckets(t, same_token_only):
    r2 = jnp.arange(SROWS)[None, :]
    dn = t - (r2 - SVALID)
    ok = (r2 >= SVALID) & (dn >= 0)
    if same_token_only:
        ok = ok & (dn == 0)
    return jnp.where(t >= 0, _masked_bucket(dn, ok), 0).astype(jnp.int32)


def _sample_buckets_a():
    t = _sample_row_tokens(4 * SROWS)
    j = jnp.arange(WIN)[None, :]
    dist = WIN + t - j
    bc = jnp.where(t >= 0, _masked_bucket(dist, (dist >= 0) & (dist <= WIN)), 0)
    return bc.astype(jnp.int32), _new_row_buckets(t, False)


def _sample_buckets_c(lc, window, dil, span):
    t = _sample_row_tokens(2 * SROWS)
    pos = jnp.arange(lc - span, lc)[None, :]
    dist = lc + t - pos
    ok = (dist >= 0) & (dist <= window) & (dist % dil == 0)
    return jnp.where(t >= 0, _masked_bucket(dist, ok), 0).astype(jnp.int32)


def _rows_by_head(bias, group):
    h, r, _ = bias.shape
    rb = r // group
    return jnp.stack([jnp.concatenate([bias[g * group + i, i * rb:(i + 1) * rb] for i in range(group)], axis=0)
                      for g in range(h // group)])


def kernel(x_prompt, x_sample, cache_a_k, cache_a_v, cache_c_k, cache_c_v, state_ssm, state_conv,
           norm_mix_g, w_in, a_q_norm_g, a_k_norm_g, a_sinks, c_q_norm_g, c_k_norm_g, rel_bias,
           conv_w, conv_b, dt_bias, a_log, d_skip, ssm_norm_g, w_out, norm_mlp_g, w_up, w_down):
    depth = w_in.shape[0]
    n, seq, _ = x_prompt.shape
    ns, ts, _ = x_sample.shape
    lc = cache_c_k.shape[2]
    assert ts == SROWS - SVALID and seq % (BLOCK * 16) == 0
    assert cache_a_k.shape[2] == WIN and all(w <= lc for w, _ in C_BRANCHES)
    nb = seq // BLOCK

    w_in_b = jnp.pad(w_in, ((0, 0), (0, 0), (0, COL_END - IN_COLS))).astype(BF16)
    w_out_b, w_up_b, w_down_b = w_out.astype(BF16), w_up.astype(BF16), w_down.astype(BF16)
    ones = jnp.ones((depth, 128), F32)
    gain = jnp.concatenate([jnp.tile(a_q_norm_g, (1, 4)) * ATTN_SCALE, jnp.tile(a_k_norm_g, (1, 2)), ones,
                            jnp.tile(c_q_norm_g, (1, 4)) * ATTN_SCALE, jnp.tile(c_k_norm_g, (1, 4))], axis=1)
    lane_grp = jnp.arange(LANES) // HEAD_DIM
    bd = (lane_grp[:, None] == lane_grp[None, :]).astype(BF16)
    tri = (jnp.arange(BLOCK)[None, :] <= jnp.arange(BLOCK)[:, None]).astype(F32)
    ex = (jnp.arange(LANES)[:, None] == (jnp.arange(D_INNER) // HEAD_DIM)[None, :]).astype(F32)
    cw8 = jnp.pad(conv_w, ((0, 0), (0, SUBLANES - conv_w.shape[1]), (0, 0)))
    pad_h = lambda v: jnp.pad(v, ((0, 0), (0, LANES - N_SSM_HEADS)))[:, None, :]
    vec = lambda v: v[:, None, :]
    dtb, alog = pad_h(dt_bias), pad_h(a_log)
    dsk = vec(jnp.repeat(d_skip, HEAD_DIM, axis=1))
    g_mix, g_mlp, gain, cb, ng = vec(norm_mix_g), vec(norm_mlp_g), vec(gain), vec(conv_b), vec(ssm_norm_g)

    bias_a = _expand_bias(rel_bias, _prompt_buckets(1), 4, 0)
    bias_c = jnp.stack([_expand_bias(rel_bias, _prompt_buckets(d), 4, 4) for d in C_DILS])
    sa_c, sa_n = _sample_buckets_a()
    sbias_a_c = _rows_by_head(_expand_bias(rel_bias, sa_c, 4, 0), 4)
    sbias_a_n = _rows_by_head(_expand_bias(rel_bias, sa_n, 4, 0), 4)
    pair_bias = lambda bkt: _rows_by_head(_expand_bias(rel_bias, bkt, 4, 4), 2)
    (w1, d1), (w2, d2), (w3, d3) = C_BRANCHES
    sb12 = jnp.stack([pair_bias(_sample_buckets_c(lc, w1, d1, NEAR)), pair_bias(_sample_buckets_c(lc, w2, d2, NEAR))])
    sb3 = pair_bias(_sample_buckets_c(lc, w3, d3, lc))
    tok2 = _sample_row_tokens(2 * SROWS)
    sbn_all, sbn_same = pair_bias(_new_row_buckets(tok2, False)), pair_bias(_new_row_buckets(tok2, True))
    sbn = jnp.stack([sbn_all, sbn_same, sbn_same])

    cak_t, cav_t = cache_a_k.transpose(0, 1, 3, 4, 2), cache_a_v.transpose(0, 1, 3, 4, 2)
    cck_t, ccv_t = cache_c_k.transpose(0, 1, 3, 4, 2), cache_c_v.transpose(0, 1, 3, 4, 2)

    hp = x_prompt.reshape(n * seq, D_MODEL)
    hs = jnp.pad(x_sample, ((0, 0), (SVALID, 0), (0, 0))).reshape(ns * SROWS, D_MODEL)
    tm_p = 512
    tm_s = min(512, ns * SROWS)
    nbs_a = min(16, ns)
    nbs_c = min(2, ns)
    nbs_b = min(8, ns)

    p_ak = p_ck = p_ssm = s_ssm = None
    p_conv, s_small = [], [[] for _ in range(5)]
    for l in range(depth):
        qa, ka, va, qc, kc, vc, z, xbc, dt = _in_proj(hp, g_mix, w_in_b, gain, bd, l, tm_p)
        oa, *p_ak = _attn_a_prompt(qa, ka, va, bias_a, a_sinks, l, depth, n, seq, p_ak)
        oc, *p_ck = _attn_c_prompt(qc, kc, vc, bias_c, l, depth, n, seq, p_ck)
        ob, *p_ssm = _ssd_prompt(xbc, z, dt, cw8, cb, dtb, alog, dsk, ng, tri, l, depth, n, nb, p_ssm)
        hp = _out_mlp(hp, oa, ob, oc, w_out_b, g_mlp, w_up_b, w_down_b, l, tm_p)
        p_conv.append(xbc.reshape(n, seq, CONV_DIM)[:, seq - 3:])

        qa, ka, va, qc, kc, vc, z, xbc, dt = _in_proj(hs, g_mix, w_in_b, gain, bd, l, tm_s)
        oa = _attn_a_sample(qa, ka, va, cak_t, cav_t, sbias_a_c, sbias_a_n, a_sinks, l, nbs_a)
        oc = _attn_c_sample(qc, kc, vc, cck_t, ccv_t, sb12, sb3, sbn, l, nbs_c)
        prev8 = jnp.pad(state_conv[l], ((0, 0), (1, SROWS - 4), (0, 0))).reshape(ns * SROWS, CONV_DIM)
        yd, eac, xw, bm, cm, cd = _ssd_sample_pre(xbc, prev8, dt, cw8, cb, dtb, alog, dsk, ex, l,
                                                  min(256, ns * SROWS))
        cd_s = cd.reshape(ns, SROWS, LANES)[:, SROWS - 1, :N_SSM_HEADS]
        ob, *s_ssm = _ssd_sample_state(cd_s, yd, eac, xw, bm, cm, z, state_ssm, ng, l, depth, nbs_b, s_ssm)
        hs = _out_mlp(hs, oa, ob, oc, w_out_b, g_mlp, w_up_b, w_down_b, l, tm_s)
        new = lambda v, hh: v.reshape(ns, SROWS, hh, HEAD_DIM)[:, SVALID:]
        for dst, val in zip(s_small, (new(ka, 2), new(va, 2), new(kc, 4), new(vc, 4),
                                      xbc.reshape(ns, SROWS, CONV_DIM)[:, SROWS - 3:])):
            dst.append(val)

    y_prompt = hp.reshape(n, seq, D_MODEL)
    y_sample = hs.reshape(ns, SROWS, D_MODEL)[:, SVALID:]
    unminor = lambda buf, heads: buf.reshape(depth, n, heads, HEAD_DIM, -1).transpose(0, 1, 4, 2, 3)
    p_state = (unminor(p_ak[0], 2), unminor(p_ak[1], 2), unminor(p_ck[0], 4), unminor(p_ck[1], 4),
               p_ssm[0], jnp.stack(p_conv))
    s_state = tuple(jnp.stack(v) for v in s_small[:4]) + (s_ssm[0], jnp.stack(s_small[4]))
    return (y_prompt, y_sample) + p_state + s_state
```

```python
import functools
import math

import jax
import jax.numpy as jnp
import numpy as np
from jax import lax
from jax.experimental import pallas as pl
from jax.experimental.pallas import tpu as pltpu

F32 = jnp.float32
BF16 = jnp.bfloat16
HIGHEST = lax.Precision.HIGHEST

D_MODEL = 1024
HEAD_DIM = 64
LANES = 128
SUBLANES = 8
BLOCK = 128
WIN = 128
D_INNER = 512
D_STATE = 128
N_SSM_HEADS = 8
CONV_DIM = 1024
D_FF = 4096
NUM_BUCKETS = 32
REL_MAX_DIST = 2048
EPS = 1e-6
ATTN_SCALE = HEAD_DIM ** -0.5
LOG2E = math.log2(math.e)
NEG = -1e30
C_BRANCHES = ((128, 1), (512, 4), (2048, 16))
C_DILS = tuple(d for _, d in C_BRANCHES)
SROWS = 8
SVALID = 4
NEAR = 512
C_UNROLL = 4
A_UNROLL = 2
SSD_SUB = 2

COL_QA, COL_KA, COL_VA, COL_QC, COL_KC, COL_VC, COL_Z, COL_XBC, COL_DT, COL_END = (
    0, 256, 384, 512, 768, 1024, 1280, 1792, 2816, 2944)
IN_COLS = 2824
VMEM_LIMIT = 56 * 1024 * 1024


def _cparams(sem):
    return pltpu.CompilerParams(dimension_semantics=sem, vmem_limit_bytes=VMEM_LIMIT)


def _dot(a, b):
    return jnp.dot(a, b, preferred_element_type=F32)


def _dot_t(a, b):
    return lax.dot_general(a, b, (((1,), (1,)), ((), ())), preferred_element_type=F32)


def _dot_t0(a, b):
    return lax.dot_general(a, b, (((0,), (0,)), ((), ())), preferred_element_type=F32)


def _lane_iota(shape):
    return lax.broadcasted_iota(jnp.int32, shape, len(shape) - 1)


def _silu(x):
    return x * (1.0 / (1.0 + jnp.exp(-x)))


def _softplus(x):
    return jnp.maximum(x, 0.0) + jnp.log(1.0 + jnp.exp(-jnp.abs(x)))


def _stacked(body, n_in, bufs):
    if bufs is None:
        return body, [], [], 0
    k = len(bufs)

    def wrapped(*refs):
        return body(*refs[:n_in], *refs[n_in + k:])

    return wrapped, [pl.BlockSpec(memory_space=pl.ANY)] * k, list(bufs), k


def _alias_map(n_in, k, n_out):
    return {n_in + i: n_out - k + i for i in range(k)}


def _bias_kernel(tab_ref, *refs, segments):
    n = len(segments)
    for bkt_ref, o_ref, segs in zip(refs[:n], refs[n:], segments):
        for r0, r1, col in segs:
            bkt = bkt_ref[r0:r1, :]
            acc = jnp.full(bkt.shape, NEG, F32)
            for b in range(NUM_BUCKETS):
                acc = jnp.where(bkt == b, tab_ref[b, col], acc)
            o_ref[r0:r1, :] = acc


def _expand_biases(table, items):
    bkts = [jnp.asarray(b) for b, _ in items]
    full = lambda b: pl.BlockSpec(b.shape, lambda: (0, 0))
    return pl.pallas_call(
        functools.partial(_bias_kernel, segments=[s for _, s in items]),
        out_shape=[jax.ShapeDtypeStruct(b.shape, F32) for b in bkts],
        in_specs=[pl.BlockSpec(memory_space=pltpu.SMEM)] + [full(b) for b in bkts],
        out_specs=[full(b) for b in bkts],
        name="bias_expand",
    )(table, *bkts)


def _t5_bucket(dist):
    max_exact = NUM_BUCKETS // 2
    d = np.maximum(dist, 0)
    df = np.maximum(d, 1).astype(np.float32)
    ratio = np.log(df / np.float32(max_exact)) / np.float32(math.log(REL_MAX_DIST / max_exact))
    large = max_exact + (ratio * np.float32(NUM_BUCKETS - max_exact)).astype(np.int32)
    return np.where(d < max_exact, d, np.minimum(large, NUM_BUCKETS - 1))


def _masked_bucket(dist, valid):
    return np.where(valid, _t5_bucket(dist), -1).astype(np.int32)


def _head_norm(slab, gain, bd):
    sq = slab * slab
    hi = sq.astype(BF16)
    lo = (sq - hi.astype(F32)).astype(BF16)
    ss = _dot(hi, bd) + _dot(lo, bd)
    return slab * lax.rsqrt(ss * (1.0 / HEAD_DIM) + EPS) * gain


def _in_proj_kernel(h_ref, g_ref, w_ref, gain_ref, bd_ref,
                    qa_ref, ka_ref, va_ref, qc_ref, kc_ref, vc_ref, z_ref, xbc_ref, dt_ref):
    x = h_ref[...]
    ms = jnp.mean(x * x, axis=-1, keepdims=True)
    u = (x * lax.rsqrt(ms + EPS) * g_ref[...]).astype(BF16)
    bd = bd_ref[...]

    def proj(c0, c1):
        return _dot(u, w_ref[:, c0:c1])

    to_norm = [(COL_QA, COL_KA, qa_ref), (COL_KA, COL_VA, ka_ref), (COL_QC, COL_KC, qc_ref), (COL_KC, COL_VC, kc_ref)]
    raw = [proj(COL_QA, COL_KA), proj(COL_KA, COL_QC), proj(COL_QC, COL_KC), proj(COL_KC, COL_VC)]
    va_ref[...] = raw[1][:, COL_VA - COL_KA:]
    vc_ref[...] = proj(COL_VC, COL_Z)
    z_ref[...] = proj(COL_Z, COL_XBC)
    for (c0, c1, out_ref), p in zip(to_norm, raw):
        for s in range((c1 - c0) // LANES):
            sl = slice(s * LANES, (s + 1) * LANES)
            out_ref[:, sl] = _head_norm(p[:, sl], gain_ref[:, c0 + s * LANES:c0 + (s + 1) * LANES], bd)
    xbc_ref[...] = proj(COL_XBC, COL_DT)
    dt_ref[...] = proj(COL_DT, COL_END)


def _in_proj(h, g, w, gain, bd, l, tm):
    t = h.shape[0]
    widths = (256, 128, 128, 256, 256, 256, 512, 1024, 128)
    layer = lambda i: (l, 0, 0)
    return pl.pallas_call(
        _in_proj_kernel,
        grid=(t // tm,),
        out_shape=[jax.ShapeDtypeStruct((t, w_), F32) for w_ in widths],
        in_specs=[pl.BlockSpec((tm, D_MODEL), lambda i: (i, 0)),
                  pl.BlockSpec((None, 1, D_MODEL), layer),
                  pl.BlockSpec((None, D_MODEL, COL_END), layer),
                  pl.BlockSpec((None, 1, D_MODEL), layer),
                  pl.BlockSpec((LANES, LANES), lambda i: (0, 0))],
        out_specs=[pl.BlockSpec((tm, w_), lambda i: (i, 0)) for w_ in widths],
        compiler_params=_cparams(("parallel",)),
        name="in_proj",
    )(h, g, w, gain, bd)


def _out_mlp_kernel(h_ref, oa_ref, ob_ref, oc_ref, wo_ref, g_ref, wu_ref, wd_ref, o_ref):
    acc = _dot(oa_ref[...].astype(BF16), wo_ref[0:256, :])
    acc += _dot(ob_ref[...].astype(BF16), wo_ref[256:768, :])
    acc += _dot(oc_ref[...].astype(BF16), wo_ref[768:1024, :])
    h2 = h_ref[...] + acc
    ms = jnp.mean(h2 * h2, axis=-1, keepdims=True)
    u = (h2 * lax.rsqrt(ms + EPS) * g_ref[...]).astype(BF16)
    ff_chunk = 1024
    mlp = jnp.zeros_like(h2)
    for c in range(D_FF // ff_chunk):
        a = _dot(u, wu_ref[:, c * ff_chunk:(c + 1) * ff_chunk])
        a = jnp.square(jnp.maximum(a, 0.0)).astype(BF16)
        mlp += _dot(a, wd_ref[c * ff_chunk:(c + 1) * ff_chunk, :])
    o_ref[...] = h2 + mlp


def _out_mlp(h, oa, ob, oc, wo, g, wu, wd, l, tm):
    t = h.shape[0]
    layer = lambda i: (l, 0, 0)
    row = lambda i: (i, 0)
    return pl.pallas_call(
        _out_mlp_kernel,
        grid=(t // tm,),
        out_shape=jax.ShapeDtypeStruct((t, D_MODEL), F32),
        in_specs=[pl.BlockSpec((tm, D_MODEL), row),
                  pl.BlockSpec((tm, 256), row),
                  pl.BlockSpec((tm, 512), row),
                  pl.BlockSpec((tm, 256), row),
                  pl.BlockSpec((None, D_MODEL, D_MODEL), layer),
                  pl.BlockSpec((None, 1, D_MODEL), layer),
                  pl.BlockSpec((None, D_MODEL, D_FF), layer),
                  pl.BlockSpec((None, D_FF, D_MODEL), layer)],
        out_specs=pl.BlockSpec((tm, D_MODEL), row),
        compiler_params=_cparams(("parallel",)),
        name="out_mlp",
    )(h, oa, ob, oc, wo, g, wu, wd)


def _attn_a_prompt_kernel(sink_ref, q_ref, k_ref, v_ref, bias_ref, o_ref, kt_ref, vt_ref,
                          k_st, k_sw, v_st, v_sw, *, l, seq):
    lo = _lane_iota((BLOCK, LANES)) < HEAD_DIM
    k = k_ref[...]
    v = v_ref[...]
    for st in (k_st, k_sw, v_st, v_sw):
        st[0:BLOCK, :] = jnp.zeros((BLOCK, LANES), BF16)
    k_st[BLOCK:, :] = k.astype(BF16)
    v_st[BLOCK:, :] = v.astype(BF16)
    k_sw[BLOCK:, :] = pltpu.roll(k, HEAD_DIM, 1).astype(BF16)
    v_sw[BLOCK:, :] = pltpu.roll(v, HEAD_DIM, 1).astype(BF16)
    kt_ref[...] = k_ref[seq - WIN:seq, :].T
    vt_ref[...] = v_ref[seq - WIN:seq, :].T

    def body(g, carry):
        scores, where = [], []
        for u in range(A_UNROLL):
            b = g * A_UNROLL + u
            cur = pl.ds(pl.multiple_of(b * BLOCK, BLOCK), BLOCK)
            both = pl.ds(pl.multiple_of(b * BLOCK, BLOCK), 2 * BLOCK)
            variant = jnp.where(b == 0, 1, 0)
            where.append((cur, both))
            for h in range(4):
                slab = q_ref[cur, (h // 2) * LANES:(h // 2 + 1) * LANES]
                qm = jnp.where(lo if h % 2 == 0 else ~lo, slab, 0.0).astype(BF16)
                kk = k_st if h in (0, 3) else k_sw
                scores.append(_dot_t(qm, kk[both, :]) + bias_ref[h, variant])
        probs = []
        for i, s in enumerate(scores):
            sink = sink_ref[l, i % 4]
            m = jnp.maximum(jnp.max(jnp.maximum(s[:, 0:BLOCK], s[:, BLOCK:2 * BLOCK]), axis=-1, keepdims=True), sink)
            p = jnp.exp2(s - m)
            den = jnp.sum(p[:, 0:BLOCK] + p[:, BLOCK:2 * BLOCK], axis=-1, keepdims=True) + jnp.exp2(sink - m)
            probs.append((p.astype(BF16), den))
        for u, (cur, both) in enumerate(where):
            outs = []
            for h in range(4):
                p, den = probs[4 * u + h]
                vv = v_st if h in (0, 3) else v_sw
                outs.append(_dot(p, vv[both, :]) / den)
            o_ref[cur, 0:LANES] = jnp.where(lo, outs[0], outs[1])
            o_ref[cur, LANES:2 * LANES] = jnp.where(lo, outs[2], outs[3])
        return carry

    lax.fori_loop(0, seq // BLOCK // A_UNROLL, body, 0)


def _attn_a_prompt(q, k, v, bias, sinks, l, depth, n, seq, bufs):
    t = q.shape[0]
    n_in = 5
    body, xspecs, xargs, k_alias = _stacked(functools.partial(_attn_a_prompt_kernel, l=l, seq=seq), n_in, bufs)
    st_shape = jax.ShapeDtypeStruct((depth, n, LANES, WIN), F32)
    st_spec = pl.BlockSpec((None, None, LANES, WIN), lambda i: (l, i, 0, 0))
    tok = lambda w: pl.BlockSpec((seq, w), lambda i: (i, 0))
    return pl.pallas_call(
        body,
        grid=(n,),
        out_shape=[jax.ShapeDtypeStruct((t, 256), F32), st_shape, st_shape],
        in_specs=[pl.BlockSpec(memory_space=pltpu.SMEM), tok(256), tok(LANES), tok(LANES),
                  pl.BlockSpec((4, 2, BLOCK, 2 * BLOCK), lambda i: (0, 0, 0, 0))] + xspecs,
        out_specs=[tok(256), st_spec, st_spec],
        scratch_shapes=[pltpu.VMEM((BLOCK + seq, LANES), BF16)] * 4,
        input_output_aliases=_alias_map(n_in, k_alias, 3) if k_alias else {},
        compiler_params=_cparams(("parallel",)),
        name="attn_a_prompt",
    )(sinks, q, k, v, bias, *xargs)


def _attn_c_prompt_kernel(q_ref, k_ref, v_ref, bias_ref, o_ref, kt_ref, vt_ref, ob_scr, lse_scr, *, seq):
    lane = _lane_iota((BLOCK, LANES))
    lo = lane < HEAD_DIM
    kt_ref[...] = k_ref[...].T
    vt_ref[...] = v_ref[...].T

    def blocks(br, dil, where, has_prev):
        rows = lambda ref, s: ref[pl.ds(s, BLOCK, stride=dil), :]
        scores, vals = [], []
        for qs, ps, first in where:
            q = rows(q_ref, qs)
            if has_prev:
                k2 = jnp.concatenate([rows(k_ref, ps), rows(k_ref, qs)], axis=0).astype(BF16)
                vals.append(jnp.concatenate([rows(v_ref, ps), rows(v_ref, qs)], axis=0).astype(BF16))
                variant = jnp.where(first, 1, 0)
            else:
                k2 = rows(k_ref, qs).astype(BF16)
                vals.append(rows(v_ref, qs).astype(BF16))
            for hh in range(2):
                qm = jnp.where(lo if hh == 0 else ~lo, q, 0.0).astype(BF16)
                bias = bias_ref[br, hh, variant] if has_prev else bias_ref[br, hh, 0, :, BLOCK:2 * BLOCK]
                scores.append(_dot_t(qm, k2) + bias)
        probs = []
        for s in scores:
            sm = jnp.maximum(s[:, 0:BLOCK], s[:, BLOCK:2 * BLOCK]) if has_prev else s
            m = jnp.max(sm, axis=-1, keepdims=True)
            p = jnp.exp2(s - m)
            pm = p[:, 0:BLOCK] + p[:, BLOCK:2 * BLOCK] if has_prev else p
            probs.append((p.astype(BF16), m, jnp.sum(pm, axis=-1, keepdims=True)))
        for u, (qs, _, _) in enumerate(where):
            outs, lses = [], []
            for hh in range(2):
                p, m, den = probs[2 * u + hh]
                outs.append(_dot(p, vals[u]) / den)
                lses.append(jnp.broadcast_to(m + jnp.log2(den), (BLOCK, LANES)))
            ob_scr[br, pl.ds(qs, BLOCK, stride=dil), :] = jnp.where(lo, outs[0], outs[1])
            lse_scr[br, pl.ds(qs, BLOCK, stride=dil), :] = jnp.where(lo, lses[0], lses[1])

    for br, dil in enumerate(C_DILS):
        nb = seq // dil // BLOCK
        span = BLOCK * dil

        def body(g, carry, br=br, dil=dil, nb=nb, span=span):
            where = []
            for u in range(C_UNROLL):
                i = g * C_UNROLL + u
                res = i % dil
                b = i // dil
                where.append((res + b * span, res + jnp.maximum(b - 1, 0) * span, b == 0))
            blocks(br, dil, where, nb > 1)
            return carry

        lax.fori_loop(0, dil * nb // C_UNROLL, body, 0)

    def merge(i, carry):
        rows = pl.ds(pl.multiple_of(i * BLOCK, BLOCK), BLOCK)
        l0, l1, l2 = lse_scr[0, rows, :], lse_scr[1, rows, :], lse_scr[2, rows, :]
        m = jnp.maximum(jnp.maximum(l0, l1), l2)
        w0, w1, w2 = jnp.exp2(l0 - m), jnp.exp2(l1 - m), jnp.exp2(l2 - m)
        num = w0 * ob_scr[0, rows, :] + w1 * ob_scr[1, rows, :] + w2 * ob_scr[2, rows, :]
        o_ref[rows, :] = num / (w0 + w1 + w2)
        return carry

    lax.fori_loop(0, seq // BLOCK, merge, 0)


def _attn_c_prompt(q, k, v, bias, l, depth, n, seq, bufs):
    t = q.shape[0]
    blk = pl.BlockSpec((seq, LANES), lambda i, hp: (i, hp))
    n_in = 4
    body, xspecs, xargs, k_alias = _stacked(functools.partial(_attn_c_prompt_kernel, seq=seq), n_in, bufs)
    st_shape = jax.ShapeDtypeStruct((depth, n, 256, seq), F32)
    st_spec = pl.BlockSpec((None, None, LANES, seq), lambda i, hp: (l, i, hp, 0))
    return pl.pallas_call(
        body,
        grid=(n, 2),
        out_shape=[jax.ShapeDtypeStruct((t, 256), F32), st_shape, st_shape],
        in_specs=[blk, blk, blk,
                  pl.BlockSpec((3, 2, 2, BLOCK, 2 * BLOCK), lambda i, hp: (0, hp, 0, 0, 0))] + xspecs,
        out_specs=[blk, st_spec, st_spec],
        scratch_shapes=[pltpu.VMEM((3, seq, LANES), F32), pltpu.VMEM((3, seq, LANES), F32)],
        input_output_aliases=_alias_map(n_in, k_alias, 3) if k_alias else {},
        compiler_params=_cparams(("parallel", "parallel")),
        name="attn_c_prompt",
    )(q, k, v, bias, *xargs)


def _gated_group_norm(y, z, ng):
    gated = y * _silu(z)
    parts = []
    for grp in range(2):
        gsl = gated[:, grp * 256:(grp + 1) * 256]
        ms = jnp.mean(gsl * gsl, axis=-1, keepdims=True)
        parts.append(gsl * lax.rsqrt(ms + EPS))
    return jnp.concatenate(parts, axis=1) * ng


def _ssd_prompt_kernel(xbc_ref, z_ref, dt_ref, cw_ref, cb_ref, dtb_ref, alog_ref,
                       dsk_ref, ng_ref, tri_ref, ob_ref, hl_ref, hst, xbuf):
    c = pl.program_id(1)

    @pl.when(c == 0)
    def _():
        hst[...] = jnp.zeros_like(hst)
        xbuf[0:SUBLANES, :] = jnp.zeros((SUBLANES, CONV_DIM), F32)

    rows = SSD_SUB * BLOCK
    x = xbc_ref[...]
    xbuf[SUBLANES:SUBLANES + rows, :] = x
    y = cb_ref[...] + cw_ref[3:4, :] * x
    for kk in range(3):
        y = y + cw_ref[kk:kk + 1, :] * xbuf[SUBLANES - 3 + kk:SUBLANES - 3 + kk + rows, :]
    xbuf[0:SUBLANES, :] = x[rows - SUBLANES:rows, :]
    xa_all = _silu(y)
    dt_all = _softplus(dt_ref[...] + dtb_ref[...])
    a_all = dt_all * (-jnp.exp(alog_ref[...]))
    row = lax.broadcasted_iota(jnp.int32, (BLOCK, BLOCK), 0)
    lane = _lane_iota((BLOCK, BLOCK))
    causal = lane <= row
    lo = lane < HEAD_DIM
    top = row < HEAD_DIM
    n_pairs = N_SSM_HEADS // 2
    for sub in range(SSD_SUB):
        rs = slice(sub * BLOCK, (sub + 1) * BLOCK)
        xa, dt = xa_all[rs], dt_all[rs]
        xs = xa[:, 0:D_INNER]
        acum = jnp.dot(tri_ref[...], a_all[rs], precision=HIGHEST, preferred_element_type=F32)
        bms = [xa[:, D_INNER + g * D_STATE:D_INNER + (g + 1) * D_STATE].astype(BF16) for g in range(2)]
        cms = [xa[:, D_INNER + (2 + g) * D_STATE:D_INNER + (3 + g) * D_STATE].astype(BF16) for g in range(2)]
        cbs = [_dot_t(cms[g], bms[g]) for g in range(2)]
        hps = [hst[2 * j:2 * j + 2].reshape(BLOCK, D_STATE) for j in range(n_pairs)]
        y_off = [_dot_t(cms[j // 2], hps[j].astype(BF16)) for j in range(n_pairs)]
        last = acum[BLOCK - 1:BLOCK, :]
        w_end = jnp.exp(last - acum) * dt
        e_last = jnp.exp(last)
        xs_pairs = [xs[:, j * LANES:(j + 1) * LANES] for j in range(n_pairs)]
        for j in range(n_pairs):
            h0_, h1_ = 2 * j, 2 * j + 1
            w_pair = jnp.where(lo, w_end[:, h0_:h0_ + 1], w_end[:, h1_:h1_ + 1])
            s_new = _dot_t0((xs_pairs[j] * w_pair).astype(BF16), bms[j // 2])
            cd_pair = jnp.where(top, e_last[:, h0_:h0_ + 1], e_last[:, h1_:h1_ + 1])
            hst[h0_:h0_ + 2] = (hps[j] * cd_pair + s_new).reshape(2, HEAD_DIM, D_STATE)
        acum_t = acum.T
        dt_t = dt.T
        eac = jnp.exp(acum)
        y_parts = []
        for j in range(n_pairs):
            xs_b = xs_pairs[j].astype(BF16)
            yd = []
            for hh in range(2):
                h = 2 * j + hh
                seg = acum[:, h:h + 1] - acum_t[h:h + 1, :]
                dec = jnp.exp(jnp.where(causal, seg, NEG)) * dt_t[h:h + 1, :]
                yd.append(_dot((cbs[j // 2] * dec).astype(BF16), xs_b))
            e_pair = jnp.where(lo, eac[:, 2 * j:2 * j + 1], eac[:, 2 * j + 1:2 * j + 2])
            y_parts.append(jnp.where(lo, yd[0], yd[1]) + y_off[j] * e_pair)
        y = jnp.concatenate(y_parts, axis=1) + dsk_ref[...] * xs
        ob_ref[rs, :] = _gated_group_norm(y, z_ref[rs, :], ng_ref[...])

    @pl.when(c == pl.num_programs(1) - 1)
    def _():
        hl_ref[...] = hst[...]


def _ssd_prompt(xbc, z, dt, cw8, cb, dtb, alog, dsk, ng, tri, l, depth, n, nc, bufs):
    t = xbc.shape[0]
    nc = nc // SSD_SUB
    rows = SSD_SUB * BLOCK
    row = lambda i, c: (i * nc + c, 0)
    layer = lambda i, c: (l, 0, 0)
    n_in = 10
    body, xspecs, xargs, k_alias = _stacked(_ssd_prompt_kernel, n_in, bufs)
    st = (N_SSM_HEADS, HEAD_DIM, D_STATE)
    return pl.pallas_call(
        body,
        grid=(n, nc),
        out_shape=[jax.ShapeDtypeStruct((t, D_INNER), F32),
                   jax.ShapeDtypeStruct((depth, n) + st, F32)],
        in_specs=[pl.BlockSpec((rows, CONV_DIM), row),
                  pl.BlockSpec((rows, D_INNER), row),
                  pl.BlockSpec((rows, LANES), row),
                  pl.BlockSpec((None, SUBLANES, CONV_DIM), layer),
                  pl.BlockSpec((None, 1, CONV_DIM), layer),
                  pl.BlockSpec((None, 1, LANES), layer),
                  pl.BlockSpec((None, 1, LANES), layer),
                  pl.BlockSpec((None, 1, D_INNER), layer),
                  pl.BlockSpec((None, 1, D_INNER), layer),
                  pl.BlockSpec((BLOCK, BLOCK), lambda i, c: (0, 0))] + xspecs,
        out_specs=[pl.BlockSpec((rows, D_INNER), row),
                   pl.BlockSpec((None, None) + st, lambda i, c: (l, i, 0, 0, 0))],
        scratch_shapes=[pltpu.VMEM(st, F32),
                        pltpu.VMEM((SUBLANES + rows, CONV_DIM), F32)],
        input_output_aliases=_alias_map(n_in, k_alias, 2) if k_alias else {},
        compiler_params=_cparams(("parallel", "arbitrary")),
        name="ssd_prompt",
    )(xbc, z, dt, cw8, cb, dtb, alog, dsk, ng, tri, *xargs)


def _attn_a_sample_kernel(sink_ref, q_ref, kn_ref, vn_ref, kt_ref, vt_ref, bc_ref, bn_ref, o_ref, *, nbs, l):
    rb = nbs * SROWS
    lo = _lane_iota((rb, LANES)) < HEAD_DIM
    s0, s1 = q_ref[:, 0:LANES], q_ref[:, LANES:2 * LANES]
    per_seq = lambda v: v.reshape(nbs, SROWS, LANES)
    qm = jnp.concatenate([
        per_seq(jnp.where(lo, s0, 0.0)),
        per_seq(jnp.where(lo, pltpu.roll(s0, HEAD_DIM, 1), 0.0)),
        per_seq(jnp.where(lo, 0.0, pltpu.roll(s1, HEAD_DIM, 1))),
        per_seq(jnp.where(lo, 0.0, s1))], axis=1).astype(BF16)
    kt = kt_ref[...].reshape(nbs, LANES, WIN).astype(BF16)
    vt = vt_ref[...].reshape(nbs, LANES, WIN).astype(BF16)
    kn = kn_ref[...].reshape(nbs, SROWS, LANES).astype(BF16)
    vn = vn_ref[...].reshape(nbs, SROWS, LANES).astype(BF16)
    s_c = jnp.einsum('nqd,ndk->nqk', qm, kt, preferred_element_type=F32) + bc_ref[...]
    s_n = jnp.einsum('nqd,nkd->nqk', qm, kn, preferred_element_type=F32) + bn_ref[...]
    hrow = lax.broadcasted_iota(jnp.int32, (nbs, 4 * SROWS, 1), 1) // SROWS
    sink = jnp.where(hrow == 0, sink_ref[l, 0], jnp.where(hrow == 1, sink_ref[l, 1],
                     jnp.where(hrow == 2, sink_ref[l, 2], sink_ref[l, 3])))
    m = jnp.maximum(jnp.max(s_c, axis=-1, keepdims=True), jnp.max(s_n, axis=-1, keepdims=True))
    m = jnp.maximum(m, sink)
    p_c = jnp.exp2(s_c - m)
    p_n = jnp.exp2(s_n - m)
    den = jnp.sum(p_c, axis=-1, keepdims=True) + jnp.sum(p_n, axis=-1, keepdims=True) + jnp.exp2(sink - m)
    o = (jnp.einsum('nqk,ndk->nqd', p_c.astype(BF16), vt, preferred_element_type=F32)
         + jnp.einsum('nqk,nkd->nqd', p_n.astype(BF16), vn, preferred_element_type=F32)) / den
    o0, o1, o2, o3 = (o[:, i * SROWS:(i + 1) * SROWS, :].reshape(rb, LANES) for i in range(4))
    o_ref[:, 0:LANES] = jnp.where(lo, o0, pltpu.roll(o1, HEAD_DIM, 1))
    o_ref[:, LANES:2 * LANES] = jnp.where(lo, pltpu.roll(o2, HEAD_DIM, 1), o3)


def _attn_a_sample(q8, k8, v8, cache_kt, cache_vt, bias_c, bias_n, sinks, l, nbs):
    ns = cache_kt.shape[1]
    tok = lambda i: (i, 0)
    cache = pl.BlockSpec((None, nbs, 2, HEAD_DIM, WIN), lambda i: (l, i, 0, 0, 0))
    c3 = lambda i: (0, 0, 0)
    return pl.pallas_call(
        functools.partial(_attn_a_sample_kernel, nbs=nbs, l=l),
        grid=(ns // nbs,),
        out_shape=jax.ShapeDtypeStruct((ns * SROWS, 256), F32),
        in_specs=[pl.BlockSpec(memory_space=pltpu.SMEM),
                  pl.BlockSpec((nbs * SROWS, 256), tok),
                  pl.BlockSpec((nbs * SROWS, LANES), tok),
                  pl.BlockSpec((nbs * SROWS, LANES), tok),
                  cache, cache,
                  pl.BlockSpec((1, 4 * SROWS, WIN), c3),
                  pl.BlockSpec((1, 4 * SROWS, SROWS), c3)],
        out_specs=pl.BlockSpec((nbs * SROWS, 256), tok),
        compiler_params=_cparams(("parallel",)),
        name="attn_a_sample",
    )(sinks, q8, k8, v8, cache_kt, cache_vt, bias_c, bias_n)


def _softmax_parts(parts):
    m = None
    for s in parts:
        mm = jnp.max(s, axis=-1, keepdims=True)
        m = mm if m is None else jnp.maximum(m, mm)
    ps = [jnp.exp2(s - m) for s in parts]
    den = None
    for p in ps:
        dd = jnp.sum(p, axis=-1, keepdims=True)
        den = dd if den is None else den + dd
    return ps, den, m + jnp.log2(den)


def _attn_c_sample_kernel(q_ref, kn_ref, vn_ref, kt_ref, vt_ref, b12_ref, b3_ref, bn_ref, o_ref, *, nbs, lc):
    lo = _lane_iota((SROWS, LANES)) < HEAD_DIM
    pr = 2 * SROWS
    units = [(n, j, slice(n * SROWS, (n + 1) * SROWS), slice(j * LANES, (j + 1) * LANES))
             for n in range(nbs) for j in range(2)]
    scores = []
    for n, j, rows, lanes in units:
        qp = q_ref[rows, lanes]
        lhs = jnp.concatenate([jnp.where(lo, qp, 0.0), jnp.where(lo, 0.0, qp)], axis=0).astype(BF16)
        ktp = kt_ref[n, 2 * j:2 * j + 2].reshape(LANES, lc).astype(BF16)
        scores.append((_dot(lhs, ktp), _dot_t(lhs, kn_ref[rows, lanes].astype(BF16))))
    probs = []
    for (n, j, rows, lanes), (s, sn) in zip(units, scores):
        s_near = s[:, lc - NEAR:]
        (p1, p1n), d1, l1 = _softmax_parts([s_near + b12_ref[0, j], sn + bn_ref[0, j]])
        (p2, p2n), d2, l2 = _softmax_parts([s_near + b12_ref[1, j], sn + bn_ref[1, j]])
        (p3, p3n), d3, l3 = _softmax_parts([s + b3_ref[j], sn + bn_ref[2, j]])
        m = jnp.maximum(jnp.maximum(l1, l2), l3)
        w1, w2, w3 = jnp.exp2(l1 - m), jnp.exp2(l2 - m), jnp.exp2(l3 - m)
        wsum = w1 + w2 + w3
        probs.append((jnp.concatenate([p1, p2], axis=0).astype(BF16), p3.astype(BF16),
                      (p1n.astype(BF16), p2n.astype(BF16), p3n.astype(BF16)),
                      (w1 / (d1 * wsum), w2 / (d2 * wsum), w3 / (d3 * wsum))))
    for (n, j, rows, lanes), (p12, p3, pn, wts) in zip(units, probs):
        vtp = vt_ref[n, 2 * j:2 * j + 2].reshape(LANES, lc).astype(BF16)
        vnp = vn_ref[rows, lanes].astype(BF16)
        o12 = _dot_t(p12, vtp[:, lc - NEAR:])
        o = ((o12[0:pr] + _dot(pn[0], vnp)) * wts[0] + (o12[pr:2 * pr] + _dot(pn[1], vnp)) * wts[1]
             + (_dot_t(p3, vtp) + _dot(pn[2], vnp)) * wts[2])
        o_ref[rows, lanes] = jnp.where(lo, o[0:SROWS], o[SROWS:pr])


def _attn_c_sample(q8, k8, v8, cache_kt, cache_vt, b12, b3, bn, l, nbs):
    ns, lc = cache_kt.shape[1], cache_kt.shape[4]
    tok = lambda i: (i, 0)
    cache = pl.BlockSpec((None, nbs, 4, HEAD_DIM, lc), lambda i: (l, i, 0, 0, 0))
    c3 = lambda i: (0, 0, 0)
    c4 = lambda i: (0, 0, 0, 0)
    return pl.pallas_call(
        functools.partial(_attn_c_sample_kernel, nbs=nbs, lc=lc),
        grid=(ns // nbs,),
        out_shape=jax.ShapeDtypeStruct((ns * SROWS, 256), F32),
        in_specs=[pl.BlockSpec((nbs * SROWS, 256), tok),
                  pl.BlockSpec((nbs * SROWS, 256), tok),
                  pl.BlockSpec((nbs * SROWS, 256), tok),
                  cache, cache,
                  pl.BlockSpec((2, 2, 2 * SROWS, NEAR), c4),
                  pl.BlockSpec((2, 2 * SROWS, lc), c3),
                  pl.BlockSpec((3, 2, 2 * SROWS, SROWS), c4)],
        out_specs=pl.BlockSpec((nbs * SROWS, 256), tok),
        compiler_params=_cparams(("parallel",)),
        name="attn_c_sample",
    )(q8, k8, v8, cache_kt, cache_vt, b12, b3, bn)


def _ssd_sample_pre_kernel(xbc_ref, prev_ref, dt_ref, cw_ref, cb_ref, dtb_ref, alog_ref, dsk_ref, ex_ref,
                           yd_ref, eac_ref, xw_ref, bm_ref, cm_ref, cd_ref):
    r = xbc_ref.shape[0]
    rid = lax.broadcasted_iota(jnp.int32, (r, 1), 0) % SROWS
    x = jnp.where(jnp.logical_and(rid >= 1, rid < SVALID), prev_ref[...], xbc_ref[...])
    y = cb_ref[...] + cw_ref[3:4, :] * x
    for d in range(1, 4):
        y = y + cw_ref[3 - d:4 - d, :] * pltpu.roll(x, d, 0)
    xa = _silu(y)
    xs = xa[:, 0:D_INNER]
    bm = xa[:, D_INNER:D_INNER + 2 * D_STATE]
    cm = xa[:, D_INNER + 2 * D_STATE:CONV_DIM]
    valid = rid >= SVALID
    dt = jnp.where(valid, _softplus(dt_ref[...] + dtb_ref[...]), 0.0)
    a = dt * (-jnp.exp(alog_ref[...]))
    acum = a
    rem = jnp.zeros_like(a)
    for d in range(1, 4):
        acum = acum + jnp.where(rid - d >= SVALID, pltpu.roll(a, d, 0), 0.0)
        rem = rem + jnp.where(rid + d < SROWS, pltpu.roll(a, r - d, 0), 0.0)
    ex = ex_ref[...]

    def expand(v):
        return jnp.dot(v, ex, precision=HIGHEST, preferred_element_type=F32)

    lane = _lane_iota((r, LANES))
    y_acc = dsk_ref[...] * xs
    for d in range(4):
        ok = rid - d >= SVALID
        bsh = bm if d == 0 else pltpu.roll(bm, d, 0)
        cb0 = jnp.sum(cm[:, 0:D_STATE] * bsh[:, 0:D_STATE], axis=-1, keepdims=True)
        cb1 = jnp.sum(cm[:, D_STATE:] * bsh[:, D_STATE:], axis=-1, keepdims=True)
        cbh = jnp.where(lane < N_SSM_HEADS // 2, cb0, cb1)
        if d == 0:
            coef = cbh * dt
            xsh = xs
        else:
            dec = jnp.exp(jnp.where(ok, acum - pltpu.roll(acum, d, 0), NEG))
            coef = cbh * dec * pltpu.roll(dt, d, 0)
            xsh = pltpu.roll(xs, d, 0)
        y_acc = y_acc + expand(jnp.where(ok, coef, 0.0)) * xsh
    yd_ref[...] = y_acc
    eac_ref[...] = expand(jnp.exp(acum))
    xw_ref[...] = xs * expand(dt * jnp.exp(rem))
    bm_ref[...] = bm
    cm_ref[...] = cm
    cd_ref[...] = jnp.exp(acum + rem)


def _ssd_sample_pre(xbc8, prev8, dt8, cw8, cb, dtb, alog, dsk, ex, l, rb):
    r = xbc8.shape[0]
    row = lambda i: (i, 0)
    layer = lambda i: (l, 0, 0)
    widths = (D_INNER, D_INNER, D_INNER, 2 * D_STATE, 2 * D_STATE, LANES)
    return pl.pallas_call(
        _ssd_sample_pre_kernel,
        grid=(r // rb,),
        out_shape=[jax.ShapeDtypeStruct((r, w_), F32) for w_ in widths],
        in_specs=[pl.BlockSpec((rb, CONV_DIM), row),
                  pl.BlockSpec((None, rb, CONV_DIM), lambda i: (l, i, 0)),
                  pl.BlockSpec((rb, LANES), row),
                  pl.BlockSpec((None, SUBLANES, CONV_DIM), layer),
                  pl.BlockSpec((None, 1, CONV_DIM), layer),
                  pl.BlockSpec((None, 1, LANES), layer),
                  pl.BlockSpec((None, 1, LANES), layer),
                  pl.BlockSpec((None, 1, D_INNER), layer),
                  pl.BlockSpec((LANES, D_INNER), lambda i: (0, 0))],
        out_specs=[pl.BlockSpec((rb, w_), row) for w_ in widths],
        compiler_params=_cparams(("parallel",)),
        name="ssd_sample_pre",
    )(xbc8, prev8, dt8, cw8, cb, dtb, alog, dsk, ex)


def _ssd_sample_state_kernel(cd_ref, yd_ref, eac_ref, xw_ref, bm_ref, cm_ref, z_ref, h0_ref, ng_ref,
                             ob_ref, hn_ref, *, nbs):
    base = pl.program_id(0) * nbs
    yo = []
    for n in range(nbs):
        rows = slice(n * SROWS, (n + 1) * SROWS)
        parts = []
        for g in range(2):
            hp = h0_ref[n, 4 * g:4 * g + 4].reshape(4 * HEAD_DIM, D_STATE)
            cmg = cm_ref[rows, g * D_STATE:(g + 1) * D_STATE].astype(BF16)
            bmg = bm_ref[rows, g * D_STATE:(g + 1) * D_STATE].astype(BF16)
            parts.append(_dot_t(cmg, hp.astype(BF16)))
            s_new = _dot_t0(xw_ref[rows, g * 256:(g + 1) * 256].astype(BF16), bmg)
            for hh in range(4):
                h = 4 * g + hh
                hn_ref[n, h] = (hp[hh * HEAD_DIM:(hh + 1) * HEAD_DIM, :] * cd_ref[base + n, h]
                                + s_new[hh * HEAD_DIM:(hh + 1) * HEAD_DIM, :])
        yo.append(jnp.concatenate(parts, axis=1))
    y = yd_ref[...] + eac_ref[...] * jnp.concatenate(yo, axis=0)
    ob_ref[...] = _gated_group_norm(y, z_ref[...], ng_ref[...])


def _ssd_sample_state(cd, yd, eac, xw, bm, cm, z8, state, ng, l, depth, nbs, bufs):
    ns = state.shape[1]
    row = lambda i: (i, 0)
    st_spec = pl.BlockSpec((None, nbs, N_SSM_HEADS, HEAD_DIM, D_STATE), lambda i: (l, i, 0, 0, 0))
    rb = nbs * SROWS
    n_in = 9
    body, xspecs, xargs, k_alias = _stacked(functools.partial(_ssd_sample_state_kernel, nbs=nbs), n_in, bufs)
    return pl.pallas_call(
        body,
        grid=(ns // nbs,),
        out_shape=[jax.ShapeDtypeStruct((ns * SROWS, D_INNER), F32),
                   jax.ShapeDtypeStruct(state.shape, F32)],
        in_specs=[pl.BlockSpec(memory_space=pltpu.SMEM),
                  pl.BlockSpec((rb, D_INNER), row),
                  pl.BlockSpec((rb, D_INNER), row),
                  pl.BlockSpec((rb, D_INNER), row),
                  pl.BlockSpec((rb, 2 * D_STATE), row),
                  pl.BlockSpec((rb, 2 * D_STATE), row),
                  pl.BlockSpec((rb, D_INNER), row),
                  st_spec,
                  pl.BlockSpec((None, 1, D_INNER), lambda i: (l, 0, 0))] + xspecs,
        out_specs=[pl.BlockSpec((rb, D_INNER), row), st_spec],
        input_output_aliases=_alias_map(n_in, k_alias, 2) if k_alias else {},
        compiler_params=_cparams(("parallel",)),
        name="ssd_sample_state",
    )(cd, yd, eac, xw, bm, cm, z8, state, ng, *xargs)


def _prompt_bias_item(dil, col0):
    qi = np.arange(BLOCK)[:, None]
    kj = np.arange(2 * BLOCK)[None, :]
    dist = BLOCK + qi - kj
    ok = (dist >= 0) & (dist <= WIN)
    variants = [_masked_bucket(dist * dil, ok), _masked_bucket(dist * dil, ok & (kj >= BLOCK))]
    segs = [((2 * h + v) * BLOCK, (2 * h + v + 1) * BLOCK, col0 + h) for h in range(4) for v in range(2)]
    return np.concatenate(variants * 4, axis=0), segs


def _sample_bias_item(dist_of, ok_of, cols, col0):
    t = np.arange(SROWS)[:, None] - SVALID
    bkt = np.where(t >= 0, _masked_bucket(dist_of(t, cols), ok_of(t, cols)), 0).astype(np.int32)
    segs = [(h * SROWS, (h + 1) * SROWS, col0 + h) for h in range(4)]
    return np.concatenate([bkt] * 4, axis=0), segs


def _cache_bias_item(length, window, dil, span, col0):
    dist_of = lambda t, pos: length + t - pos
    ok_of = lambda t, pos: (dist_of(t, pos) >= 0) & (dist_of(t, pos) <= window) & (dist_of(t, pos) % dil == 0)
    return _sample_bias_item(dist_of, ok_of, np.arange(length - span, length)[None, :], col0)


def _new_rows_bias_item(same_token_only, col0):
    dist_of = lambda t, r: t - (r - SVALID)
    if same_token_only:
        ok_of = lambda t, r: (r >= SVALID) & (dist_of(t, r) == 0)
    else:
        ok_of = lambda t, r: (r >= SVALID) & (dist_of(t, r) >= 0)
    return _sample_bias_item(dist_of, ok_of, np.arange(SROWS)[None, :], col0)


def kernel(x_prompt, x_sample, cache_a_k, cache_a_v, cache_c_k, cache_c_v, state_ssm, state_conv,
           norm_mix_g, w_in, a_q_norm_g, a_k_norm_g, a_sinks, c_q_norm_g, c_k_norm_g, rel_bias,
           conv_w, conv_b, dt_bias, a_log, d_skip, ssm_norm_g, w_out, norm_mlp_g, w_up, w_down):
    depth = w_in.shape[0]
    n, seq, _ = x_prompt.shape
    ns, ts, _ = x_sample.shape
    lc = cache_c_k.shape[2]
    assert ts == SROWS - SVALID and seq % (BLOCK * 16) == 0
    assert cache_a_k.shape[2] == WIN and all(w <= lc for w, _ in C_BRANCHES)
    nb = seq // BLOCK

    w_in_b = jnp.pad(w_in, ((0, 0), (0, 0), (0, COL_END - IN_COLS))).astype(BF16)
    w_out_b, w_up_b, w_down_b = w_out.astype(BF16), w_up.astype(BF16), w_down.astype(BF16)
    ones = jnp.ones((depth, 128), F32)
    q_scale = ATTN_SCALE * LOG2E
    gain = jnp.concatenate([jnp.tile(a_q_norm_g, (1, 4)) * q_scale, jnp.tile(a_k_norm_g, (1, 2)), ones,
                            jnp.tile(c_q_norm_g, (1, 4)) * q_scale, jnp.tile(c_k_norm_g, (1, 4))], axis=1)
    rel_bias = rel_bias * LOG2E
    a_sinks = a_sinks * LOG2E
    lane_grp = jnp.arange(LANES) // HEAD_DIM
    bd = (lane_grp[:, None] == lane_grp[None, :]).astype(BF16)
    tri = (jnp.arange(BLOCK)[None, :] <= jnp.arange(BLOCK)[:, None]).astype(F32)
    ex = (jnp.arange(LANES)[:, None] == (jnp.arange(D_INNER) // HEAD_DIM)[None, :]).astype(F32)
    cw8 = jnp.pad(conv_w, ((0, 0), (0, SUBLANES - conv_w.shape[1]), (0, 0)))
    pad_h = lambda v: jnp.pad(v, ((0, 0), (0, LANES - N_SSM_HEADS)))[:, None, :]
    vec = lambda v: v[:, None, :]
    dtb, alog = pad_h(dt_bias), pad_h(a_log)
    dsk = vec(jnp.repeat(d_skip, HEAD_DIM, axis=1))
    g_mix, g_mlp, gain, cb, ng = vec(norm_mix_g), vec(norm_mlp_g), vec(gain), vec(conv_b), vec(ssm_norm_g)

    (w1, d1), (w2, d2), (w3, d3) = C_BRANCHES
    items = ([_prompt_bias_item(1, 0)] + [_prompt_bias_item(d, 4) for d in C_DILS]
             + [_cache_bias_item(WIN, WIN, 1, WIN, 0), _new_rows_bias_item(False, 0)]
             + [_cache_bias_item(lc, w1, d1, NEAR, 4), _cache_bias_item(lc, w2, d2, NEAR, 4),
                _cache_bias_item(lc, w3, d3, lc, 4), _new_rows_bias_item(False, 4), _new_rows_bias_item(True, 4)])
    pa, pc1, pc4, pc16, sa_c, sa_n, sc1, sc2, sc3, sn_all, sn_same = _expand_biases(rel_bias, items)
    by_variant = lambda b: b.reshape(4, 2, BLOCK, 2 * BLOCK)
    by_pair = lambda b: b.reshape(2, 2 * SROWS, b.shape[-1])
    bias_a = by_variant(pa)
    bias_c = jnp.stack([by_variant(pc1), by_variant(pc4), by_variant(pc16)])
    sbias_a_c, sbias_a_n = sa_c[None], sa_n[None]
    sb12 = jnp.stack([by_pair(sc1), by_pair(sc2)])
    sb3 = by_pair(sc3)
    sbn = jnp.stack([by_pair(sn_all), by_pair(sn_same), by_pair(sn_same)])

    cak_t, cav_t = cache_a_k.transpose(0, 1, 3, 4, 2), cache_a_v.transpose(0, 1, 3, 4, 2)
    cck_t, ccv_t = cache_c_k.transpose(0, 1, 3, 4, 2), cache_c_v.transpose(0, 1, 3, 4, 2)

    hp = x_prompt.reshape(n * seq, D_MODEL)
    hs = jnp.pad(x_sample, ((0, 0), (SVALID, 0), (0, 0))).reshape(ns * SROWS, D_MODEL)
    tm_p = 512
    tm_s = min(512, ns * SROWS)
    nbs_a = min(16, ns)
    nbs_c = min(2, ns)
    nbs_b = min(16, ns)
    prev8 = jnp.pad(state_conv, ((0, 0), (0, 0), (1, SROWS - 4), (0, 0))).reshape(depth, ns * SROWS, CONV_DIM)

    p_ak = p_ck = p_ssm = s_ssm = None
    p_conv, s_small = [], [[] for _ in range(5)]
    for l in range(depth):
        qa, ka, va, qc, kc, vc, z, xbc, dt = _in_proj(hp, g_mix, w_in_b, gain, bd, l, tm_p)
        oa, *p_ak = _attn_a_prompt(qa, ka, va, bias_a, a_sinks, l, depth, n, seq, p_ak)
        oc, *p_ck = _attn_c_prompt(qc, kc, vc, bias_c, l, depth, n, seq, p_ck)
        ob, *p_ssm = _ssd_prompt(xbc, z, dt, cw8, cb, dtb, alog, dsk, ng, tri, l, depth, n, nb, p_ssm)
        hp = _out_mlp(hp, oa, ob, oc, w_out_b, g_mlp, w_up_b, w_down_b, l, tm_p)
        p_conv.append(xbc.reshape(n, seq, CONV_DIM)[:, seq - 3:])

        qa, ka, va, qc, kc, vc, z, xbc, dt = _in_proj(hs, g_mix, w_in_b, gain, bd, l, tm_s)
        oa = _attn_a_sample(qa, ka, va, cak_t, cav_t, sbias_a_c, sbias_a_n, a_sinks, l, nbs_a)
        oc = _attn_c_sample(qc, kc, vc, cck_t, ccv_t, sb12, sb3, sbn, l, nbs_c)
        yd, eac, xw, bm, cm, cd = _ssd_sample_pre(xbc, prev8, dt, cw8, cb, dtb, alog, dsk, ex, l,
                                                  min(256, ns * SROWS))
        cd_s = cd.reshape(ns, SROWS, LANES)[:, SROWS - 1, :N_SSM_HEADS]
        ob, *s_ssm = _ssd_sample_state(cd_s, yd, eac, xw, bm, cm, z, state_ssm, ng, l, depth, nbs_b, s_ssm)
        hs = _out_mlp(hs, oa, ob, oc, w_out_b, g_mlp, w_up_b, w_down_b, l, tm_s)
        for dst, val in zip(s_small, (ka, va, kc, vc, xbc)):
            dst.append(val)

    y_prompt = hp.reshape(n, seq, D_MODEL)
    y_sample = hs.reshape(ns, SROWS, D_MODEL)[:, SVALID:]
    unminor = lambda buf, heads: buf.reshape(depth, n, heads, HEAD_DIM, -1).transpose(0, 1, 4, 2, 3)
    p_state = (unminor(p_ak[0], 2), unminor(p_ak[1], 2), unminor(p_ck[0], 4), unminor(p_ck[1], 4),
               p_ssm[0], jnp.stack(p_conv))
    new = lambda vals, first, tail: jnp.stack(vals).reshape((depth, ns, SROWS) + tail)[:, :, first:]
    s_state = tuple(new(s_small[i], SVALID, (hh, HEAD_DIM)) for i, hh in enumerate((2, 2, 4, 4))) + (
        s_ssm[0], new(s_small[4], SROWS - 3, (CONV_DIM,)))
    return (y_prompt, y_sample) + p_state + s_state
```

```python
import functools
import math

import jax
import jax.numpy as jnp
import numpy as np
from jax import lax
from jax.experimental import pallas as pl
from jax.experimental.pallas import tpu as pltpu

F32 = jnp.float32
BF16 = jnp.bfloat16
HIGHEST = lax.Precision.HIGHEST

D_MODEL = 1024
HEAD_DIM = 64
LANES = 128
SUBLANES = 8
BLOCK = 128
WIN = 128
D_INNER = 512
D_STATE = 128
N_SSM_HEADS = 8
CONV_DIM = 1024
D_FF = 4096
NUM_BUCKETS = 32
REL_MAX_DIST = 2048
EPS = 1e-6
ATTN_SCALE = HEAD_DIM ** -0.5
LOG2E = math.log2(math.e)
NEG = -1e30
C_BRANCHES = ((128, 1), (512, 4), (2048, 16))
C_DILS = tuple(d for _, d in C_BRANCHES)
SROWS = 8
SVALID = 4
NEAR = 512
C_UNROLL = 8
A_UNROLL = 2
SSD_SUB = 2
IN_PROJ_PARTS = 1

COL_QA, COL_KA, COL_VA, COL_QC, COL_KC, COL_VC, COL_Z, COL_XBC, COL_DT, COL_END = (
    0, 256, 384, 512, 768, 1024, 1280, 1792, 2816, 2944)
IN_COLS = 2824
VMEM_LIMIT = 56 * 1024 * 1024


def _cparams(sem):
    return pltpu.CompilerParams(dimension_semantics=sem, vmem_limit_bytes=VMEM_LIMIT)


def _dot(a, b):
    return jnp.dot(a, b, preferred_element_type=F32)


def _dot_t(a, b):
    return lax.dot_general(a, b, (((1,), (1,)), ((), ())), preferred_element_type=F32)


def _dot_t0(a, b):
    return lax.dot_general(a, b, (((0,), (0,)), ((), ())), preferred_element_type=F32)


def _lane_iota(shape):
    return lax.broadcasted_iota(jnp.int32, shape, len(shape) - 1)


def _silu(x):
    hx = 0.5 * x
    return hx + hx * jnp.tanh(hx)


def _softplus(x):
    return jnp.maximum(x, 0.0) + jnp.log(1.0 + jnp.exp(-jnp.abs(x)))


def _stacked(body, n_in, bufs):
    if bufs is None:
        return body, [], [], 0
    k = len(bufs)

    def wrapped(*refs):
        return body(*refs[:n_in], *refs[n_in + k:])

    return wrapped, [pl.BlockSpec(memory_space=pl.ANY)] * k, list(bufs), k


def _alias_map(n_in, k, n_out):
    return {n_in + i: n_out - k + i for i in range(k)}


def _bias_kernel(tab_ref, *refs, segments):
    n = len(segments)
    for bkt_ref, o_ref, segs in zip(refs[:n], refs[n:], segments):
        for r0, r1, col in segs:
            bkt = bkt_ref[r0:r1, :]
            acc = jnp.full(bkt.shape, NEG, F32)
            for b in range(NUM_BUCKETS):
                acc = jnp.where(bkt == b, tab_ref[b, col], acc)
            o_ref[r0:r1, :] = acc


def _expand_biases(table, items):
    bkts = [jnp.asarray(b) for b, _ in items]
    full = lambda b: pl.BlockSpec(b.shape, lambda: (0, 0))
    return pl.pallas_call(
        functools.partial(_bias_kernel, segments=[s for _, s in items]),
        out_shape=[jax.ShapeDtypeStruct(b.shape, F32) for b in bkts],
        in_specs=[pl.BlockSpec(memory_space=pltpu.SMEM)] + [full(b) for b in bkts],
        out_specs=[full(b) for b in bkts],
        name="bias_expand",
    )(table, *bkts)


def _t5_bucket(dist):
    max_exact = NUM_BUCKETS // 2
    d = np.maximum(dist, 0)
    df = np.maximum(d, 1).astype(np.float32)
    ratio = np.log(df / np.float32(max_exact)) / np.float32(math.log(REL_MAX_DIST / max_exact))
    large = max_exact + (ratio * np.float32(NUM_BUCKETS - max_exact)).astype(np.int32)
    return np.where(d < max_exact, d, np.minimum(large, NUM_BUCKETS - 1))


def _masked_bucket(dist, valid):
    return np.where(valid, _t5_bucket(dist), -1).astype(np.int32)


def _head_norm(p, gain, bd):
    sq = p * p
    hi = sq.astype(BF16)
    lo = (sq - hi.astype(F32)).astype(BF16)
    ss = _dot(hi, bd) + _dot(lo, bd)
    return p * lax.rsqrt(ss * (1.0 / HEAD_DIM) + EPS) * gain


def _in_proj_kernel(h_ref, g_ref, w_ref, gain_ref, bd_ref, cw_ref, cb_ref,
                    qa_ref, ka_ref, va_ref, qc_ref, kc_ref, vc_ref, z_ref, xbc_ref, dt_ref, tail_ref, xbuf,
                    *, conv_tiles):
    if conv_tiles:
        @pl.when(pl.program_id(0) % conv_tiles == 0)
        def _():
            xbuf[0:SUBLANES, :] = jnp.zeros((SUBLANES, CONV_DIM), F32)

    bd = bd_ref[...]
    tm = h_ref.shape[0]
    part = tm // IN_PROJ_PARTS
    for r0 in range(0, tm, part):
        rs = slice(r0, r0 + part)
        x = h_ref[rs, :]
        ms = jnp.mean(x * x, axis=-1, keepdims=True)
        u = (x * lax.rsqrt(ms + EPS) * g_ref[...]).astype(BF16)

        def proj(c0, c1, u=u):
            return _dot(u, w_ref[:, c0:c1])

        xbc = proj(COL_XBC, COL_DT)
        if conv_tiles:
            xbuf[SUBLANES + r0:SUBLANES + r0 + part, :] = xbc
            y = cb_ref[...] + cw_ref[3:4, :] * xbc
            for kk in range(3):
                y = y + cw_ref[kk:kk + 1, :] * xbuf[SUBLANES + r0 - 3 + kk:SUBLANES + r0 - 3 + kk + part, :]
            xbc_ref[rs, :] = _silu(y)
        else:
            xbc_ref[rs, :] = xbc
        if r0 + part == tm:
            tail = xbc[part - SUBLANES:part, :]
            tail_ref[...] = tail
            if conv_tiles:
                xbuf[0:SUBLANES, :] = tail
        groups = [COL_QA, COL_KA, COL_QC, COL_KC]
        raw = [proj(c0, c0 + 256) for c0 in groups]
        va_ref[rs, :] = raw[1][:, COL_VA - COL_KA:]
        vc_ref[rs, :] = proj(COL_VC, COL_Z)
        z_ref[rs, :] = _silu(proj(COL_Z, COL_XBC))
        normed = [_head_norm(p, gain_ref[:, c0:c0 + 256], bd) for c0, p in zip(groups, raw)]
        qa_ref[rs, :] = normed[0]
        ka_ref[rs, :] = normed[1][:, 0:COL_VA - COL_KA]
        qc_ref[rs, :] = normed[2]
        kc_ref[rs, :] = normed[3]
        dt_ref[rs, :] = proj(COL_DT, COL_END)


def _in_proj(h, g, w, gain, bd, cw8, cb, l, tm, conv_tiles):
    t = h.shape[0]
    widths = (256, 128, 128, 256, 256, 256, 512, 1024, 128)
    layer = lambda i: (l, 0, 0)
    n_tail = t // tm // conv_tiles if conv_tiles else t // tm
    per_seq = conv_tiles if conv_tiles else 1
    return pl.pallas_call(
        functools.partial(_in_proj_kernel, conv_tiles=conv_tiles),
        grid=(t // tm,),
        out_shape=[jax.ShapeDtypeStruct((t, w_), F32) for w_ in widths]
        + [jax.ShapeDtypeStruct((n_tail, SUBLANES, CONV_DIM), F32)],
        in_specs=[pl.BlockSpec((tm, D_MODEL), lambda i: (i, 0)),
                  pl.BlockSpec((None, 1, D_MODEL), layer),
                  pl.BlockSpec((None, D_MODEL, COL_END), layer),
                  pl.BlockSpec((None, 1, D_MODEL), layer),
                  pl.BlockSpec((256, 256), lambda i: (0, 0)),
                  pl.BlockSpec((None, SUBLANES, CONV_DIM), layer),
                  pl.BlockSpec((None, 1, CONV_DIM), layer)],
        out_specs=[pl.BlockSpec((tm, w_), lambda i: (i, 0)) for w_ in widths]
        + [pl.BlockSpec((None, SUBLANES, CONV_DIM), lambda i: (i // per_seq, 0, 0))],
        scratch_shapes=[pltpu.VMEM((SUBLANES + tm, CONV_DIM), F32)],
        compiler_params=_cparams(("arbitrary",)),
        name="in_proj",
    )(h, g, w, gain, bd, cw8, cb)


def _out_mlp_kernel(h_ref, oa_ref, ob_ref, oc_ref, wo_ref, g_ref, wu_ref, wd_ref, o_ref):
    acc = _dot(oa_ref[...].astype(BF16), wo_ref[0:256, :])
    acc += _dot(ob_ref[...].astype(BF16), wo_ref[256:768, :])
    acc += _dot(oc_ref[...].astype(BF16), wo_ref[768:1024, :])
    h2 = h_ref[...] + acc
    ms = jnp.mean(h2 * h2, axis=-1, keepdims=True)
    u = (h2 * lax.rsqrt(ms + EPS) * g_ref[...]).astype(BF16)
    ff_chunk = 1024
    mlp = jnp.zeros_like(h2)
    for c in range(D_FF // ff_chunk):
        a = _dot(u, wu_ref[:, c * ff_chunk:(c + 1) * ff_chunk])
        a = jnp.square(jnp.maximum(a, 0.0)).astype(BF16)
        mlp += _dot(a, wd_ref[c * ff_chunk:(c + 1) * ff_chunk, :])
    o_ref[...] = h2 + mlp


def _out_mlp(h, oa, ob, oc, wo, g, wu, wd, l, tm):
    t = h.shape[0]
    layer = lambda i: (l, 0, 0)
    row = lambda i: (i, 0)
    return pl.pallas_call(
        _out_mlp_kernel,
        grid=(t // tm,),
        out_shape=jax.ShapeDtypeStruct((t, D_MODEL), F32),
        in_specs=[pl.BlockSpec((tm, D_MODEL), row),
                  pl.BlockSpec((tm, 256), row),
                  pl.BlockSpec((tm, 512), row),
                  pl.BlockSpec((tm, 256), row),
                  pl.BlockSpec((None, D_MODEL, D_MODEL), layer),
                  pl.BlockSpec((None, 1, D_MODEL), layer),
                  pl.BlockSpec((None, D_MODEL, D_FF), layer),
                  pl.BlockSpec((None, D_FF, D_MODEL), layer)],
        out_specs=pl.BlockSpec((tm, D_MODEL), row),
        compiler_params=_cparams(("parallel",)),
        name="out_mlp",
    )(h, oa, ob, oc, wo, g, wu, wd)


def _attn_a_prompt_kernel(sink_ref, q_ref, k_ref, v_ref, bias_ref, o_ref, kt_ref, vt_ref,
                          k_st, k_sw, v_st, v_sw, *, l, seq):
    lo = _lane_iota((BLOCK, LANES)) < HEAD_DIM
    k = k_ref[...]
    v = v_ref[...]
    for st in (k_st, k_sw, v_st, v_sw):
        st[0:BLOCK, :] = jnp.zeros((BLOCK, LANES), BF16)
    k_st[BLOCK:, :] = k.astype(BF16)
    v_st[BLOCK:, :] = v.astype(BF16)
    k_sw[BLOCK:, :] = pltpu.roll(k, HEAD_DIM, 1).astype(BF16)
    v_sw[BLOCK:, :] = pltpu.roll(v, HEAD_DIM, 1).astype(BF16)
    kt_ref[...] = k_ref[seq - WIN:seq, :].T
    vt_ref[...] = v_ref[seq - WIN:seq, :].T

    def body(g, carry):
        scores, where = [], []
        for u in range(A_UNROLL):
            b = g * A_UNROLL + u
            cur = pl.ds(pl.multiple_of(b * BLOCK, BLOCK), BLOCK)
            both = pl.ds(pl.multiple_of(b * BLOCK, BLOCK), 2 * BLOCK)
            variant = jnp.where(b == 0, 1, 0)
            where.append((cur, both))
            for h in range(4):
                slab = q_ref[cur, (h // 2) * LANES:(h // 2 + 1) * LANES]
                qm = jnp.where(lo if h % 2 == 0 else ~lo, slab, 0.0).astype(BF16)
                kk = k_st if h in (0, 3) else k_sw
                scores.append(_dot_t(qm, kk[both, :]) + bias_ref[h, variant])
        probs = []
        for i, s in enumerate(scores):
            sink = sink_ref[l, i % 4]
            m = jnp.maximum(jnp.max(jnp.maximum(s[:, 0:BLOCK], s[:, BLOCK:2 * BLOCK]), axis=-1, keepdims=True), sink)
            p = jnp.exp2(s - m)
            den = jnp.sum(p[:, 0:BLOCK] + p[:, BLOCK:2 * BLOCK], axis=-1, keepdims=True) + jnp.exp2(sink - m)
            probs.append((p.astype(BF16), den))
        for u, (cur, both) in enumerate(where):
            outs = []
            for h in range(4):
                p, den = probs[4 * u + h]
                vv = v_st if h in (0, 3) else v_sw
                outs.append(_dot(p, vv[both, :]) / den)
            o_ref[cur, 0:LANES] = jnp.where(lo, outs[0], outs[1])
            o_ref[cur, LANES:2 * LANES] = jnp.where(lo, outs[2], outs[3])
        return carry

    lax.fori_loop(0, seq // BLOCK // A_UNROLL, body, 0)


def _attn_a_prompt(q, k, v, bias, sinks, l, depth, n, seq, bufs):
    t = q.shape[0]
    n_in = 5
    body, xspecs, xargs, k_alias = _stacked(functools.partial(_attn_a_prompt_kernel, l=l, seq=seq), n_in, bufs)
    st_shape = jax.ShapeDtypeStruct((depth, n, LANES, WIN), F32)
    st_spec = pl.BlockSpec((None, None, LANES, WIN), lambda i: (l, i, 0, 0))
    tok = lambda w: pl.BlockSpec((seq, w), lambda i: (i, 0))
    return pl.pallas_call(
        body,
        grid=(n,),
        out_shape=[jax.ShapeDtypeStruct((t, 256), F32), st_shape, st_shape],
        in_specs=[pl.BlockSpec(memory_space=pltpu.SMEM), tok(256), tok(LANES), tok(LANES),
                  pl.BlockSpec((4, 2, BLOCK, 2 * BLOCK), lambda i: (0, 0, 0, 0))] + xspecs,
        out_specs=[tok(256), st_spec, st_spec],
        scratch_shapes=[pltpu.VMEM((BLOCK + seq, LANES), BF16)] * 4,
        input_output_aliases=_alias_map(n_in, k_alias, 3) if k_alias else {},
        compiler_params=_cparams(("parallel",)),
        name="attn_a_prompt",
    )(sinks, q, k, v, bias, *xargs)


def _attn_c_prompt_kernel(q_ref, k_ref, v_ref, bias_ref, o_ref, kt_ref, vt_ref, ob_scr, lse_scr, *, seq):
    lane = _lane_iota((BLOCK, LANES))
    lo = lane < HEAD_DIM
    kt_ref[...] = k_ref[...].T
    vt_ref[...] = v_ref[...].T

    def blocks(br, dil, where, has_prev):
        rows = lambda ref, s: ref[pl.ds(s, BLOCK, stride=dil), :]
        scores, vals = [], []
        for qs, ps, first in where:
            q = rows(q_ref, qs)
            if has_prev:
                k2 = jnp.concatenate([rows(k_ref, ps), rows(k_ref, qs)], axis=0).astype(BF16)
                vals.append(jnp.concatenate([rows(v_ref, ps), rows(v_ref, qs)], axis=0).astype(BF16))
                variant = jnp.where(first, 1, 0)
            else:
                k2 = rows(k_ref, qs).astype(BF16)
                vals.append(rows(v_ref, qs).astype(BF16))
            for hh in range(2):
                qm = jnp.where(lo if hh == 0 else ~lo, q, 0.0).astype(BF16)
                bias = bias_ref[br, hh, variant] if has_prev else bias_ref[br, hh, 0, :, BLOCK:2 * BLOCK]
                scores.append(_dot_t(qm, k2) + bias)
        probs = []
        for s in scores:
            sm = jnp.maximum(s[:, 0:BLOCK], s[:, BLOCK:2 * BLOCK]) if has_prev else s
            m = jnp.max(sm, axis=-1, keepdims=True)
            p = jnp.exp2(s - m)
            pm = p[:, 0:BLOCK] + p[:, BLOCK:2 * BLOCK] if has_prev else p
            probs.append((p.astype(BF16), m, jnp.sum(pm, axis=-1, keepdims=True)))
        for u, (qs, _, _) in enumerate(where):
            outs, lses = [], []
            for hh in range(2):
                p, m, den = probs[2 * u + hh]
                outs.append(_dot(p, vals[u]) / den)
                lses.append(jnp.broadcast_to(m + jnp.log2(den), (BLOCK, LANES)))
            ob_scr[br, pl.ds(qs, BLOCK, stride=dil), :] = jnp.where(lo, outs[0], outs[1])
            lse_scr[br, pl.ds(qs, BLOCK, stride=dil), :] = jnp.where(lo, lses[0], lses[1])

    for br, dil in enumerate(C_DILS):
        nb = seq // dil // BLOCK
        span = BLOCK * dil

        def body(g, carry, br=br, dil=dil, nb=nb, span=span):
            where = []
            for u in range(C_UNROLL):
                i = g * C_UNROLL + u
                res = i % dil
                b = i // dil
                where.append((res + b * span, res + jnp.maximum(b - 1, 0) * span, b == 0))
            blocks(br, dil, where, nb > 1)
            return carry

        lax.fori_loop(0, dil * nb // C_UNROLL, body, 0)

    def merge(i, carry):
        rows = pl.ds(pl.multiple_of(i * BLOCK, BLOCK), BLOCK)
        l0, l1, l2 = lse_scr[0, rows, :], lse_scr[1, rows, :], lse_scr[2, rows, :]
        m = jnp.maximum(jnp.maximum(l0, l1), l2)
        w0, w1, w2 = jnp.exp2(l0 - m), jnp.exp2(l1 - m), jnp.exp2(l2 - m)
        num = w0 * ob_scr[0, rows, :] + w1 * ob_scr[1, rows, :] + w2 * ob_scr[2, rows, :]
        o_ref[rows, :] = num / (w0 + w1 + w2)
        return carry

    lax.fori_loop(0, seq // BLOCK, merge, 0)


def _attn_c_prompt(q, k, v, bias, l, depth, n, seq, bufs):
    t = q.shape[0]
    blk = pl.BlockSpec((seq, LANES), lambda i, hp: (i, hp))
    n_in = 4
    body, xspecs, xargs, k_alias = _stacked(functools.partial(_attn_c_prompt_kernel, seq=seq), n_in, bufs)
    st_shape = jax.ShapeDtypeStruct((depth, n, 256, seq), F32)
    st_spec = pl.BlockSpec((None, None, LANES, seq), lambda i, hp: (l, i, hp, 0))
    return pl.pallas_call(
        body,
        grid=(n, 2),
        out_shape=[jax.ShapeDtypeStruct((t, 256), F32), st_shape, st_shape],
        in_specs=[blk, blk, blk,
                  pl.BlockSpec((3, 2, 2, BLOCK, 2 * BLOCK), lambda i, hp: (0, hp, 0, 0, 0))] + xspecs,
        out_specs=[blk, st_spec, st_spec],
        scratch_shapes=[pltpu.VMEM((3, seq, LANES), F32), pltpu.VMEM((3, seq, LANES), F32)],
        input_output_aliases=_alias_map(n_in, k_alias, 3) if k_alias else {},
        compiler_params=_cparams(("parallel", "parallel")),
        name="attn_c_prompt",
    )(q, k, v, bias, *xargs)


def _gated_group_norm(y, gate, ng):
    gated = y * gate
    parts = []
    for grp in range(2):
        gsl = gated[:, grp * 256:(grp + 1) * 256]
        ms = jnp.mean(gsl * gsl, axis=-1, keepdims=True)
        parts.append(gsl * lax.rsqrt(ms + EPS))
    return jnp.concatenate(parts, axis=1) * ng


def _expand_heads(v, ex):
    hi = v.astype(BF16)
    r1 = v - hi.astype(F32)
    mid = r1.astype(BF16)
    lo = (r1 - mid.astype(F32)).astype(BF16)
    return _dot(hi, ex) + _dot(mid, ex) + _dot(lo, ex)


def _ssd_prompt_kernel(xbc_ref, z_ref, dt_ref, dtb_ref, alog_ref, dsk_ref, ng_ref, tri_ref, ex_ref,
                       ob_ref, hl_ref, hst):
    c = pl.program_id(1)

    @pl.when(c == 0)
    def _():
        hst[...] = jnp.zeros_like(hst)

    xa_all = xbc_ref[...]
    dt_all = _softplus(dt_ref[...] + dtb_ref[...])
    a_all = dt_all * (-jnp.exp(alog_ref[...]))
    row = lax.broadcasted_iota(jnp.int32, (BLOCK, BLOCK), 0)
    lane = _lane_iota((BLOCK, BLOCK))
    causal = lane <= row
    lo = lane < HEAD_DIM
    top = row < HEAD_DIM
    n_pairs = N_SSM_HEADS // 2
    for sub in range(SSD_SUB):
        rs = slice(sub * BLOCK, (sub + 1) * BLOCK)
        xa, dt = xa_all[rs], dt_all[rs]
        xs = xa[:, 0:D_INNER]
        acum = jnp.dot(tri_ref[...], a_all[rs], precision=HIGHEST, preferred_element_type=F32)
        bms = [xa[:, D_INNER + g * D_STATE:D_INNER + (g + 1) * D_STATE].astype(BF16) for g in range(2)]
        cms = [xa[:, D_INNER + (2 + g) * D_STATE:D_INNER + (3 + g) * D_STATE].astype(BF16) for g in range(2)]
        cbs = [_dot_t(cms[g], bms[g]) for g in range(2)]
        hps = [hst[2 * j:2 * j + 2].reshape(BLOCK, D_STATE) for j in range(n_pairs)]
        y_off = [_dot_t(cms[j // 2], hps[j].astype(BF16)) for j in range(n_pairs)]
        last = acum[BLOCK - 1:BLOCK, :]
        e_last = jnp.exp(last)
        w_all = _expand_heads(jnp.exp(last - acum) * dt, ex_ref[...])
        e_all = _expand_heads(jnp.exp(acum), ex_ref[...])
        xs_pairs = [xs[:, j * LANES:(j + 1) * LANES] for j in range(n_pairs)]
        for j in range(n_pairs):
            h0_, h1_ = 2 * j, 2 * j + 1
            xw = xs_pairs[j] * w_all[:, j * LANES:(j + 1) * LANES]
            s_new = _dot_t0(xw.astype(BF16), bms[j // 2])
            cd_pair = jnp.where(top, e_last[:, h0_:h0_ + 1], e_last[:, h1_:h1_ + 1])
            hst[h0_:h0_ + 2] = (hps[j] * cd_pair + s_new).reshape(2, HEAD_DIM, D_STATE)
        acum_t = acum.T
        dt_t = dt.T
        y_parts = []
        for j in range(n_pairs):
            xs_b = xs_pairs[j].astype(BF16)
            yd = []
            for hh in range(2):
                h = 2 * j + hh
                seg = acum[:, h:h + 1] - acum_t[h:h + 1, :]
                dec = jnp.exp(jnp.where(causal, seg, NEG)) * dt_t[h:h + 1, :]
                yd.append(_dot((cbs[j // 2] * dec).astype(BF16), xs_b))
            y_parts.append(jnp.where(lo, yd[0], yd[1]) + y_off[j] * e_all[:, j * LANES:(j + 1) * LANES])
        y = jnp.concatenate(y_parts, axis=1) + dsk_ref[...] * xs
        ob_ref[rs, :] = _gated_group_norm(y, z_ref[rs, :], ng_ref[...])

    @pl.when(c == pl.num_programs(1) - 1)
    def _():
        hl_ref[...] = hst[...]


def _ssd_prompt(xbc, z, dt, dtb, alog, dsk, ng, tri, ex, l, depth, n, nc, bufs):
    t = xbc.shape[0]
    nc = nc // SSD_SUB
    rows = SSD_SUB * BLOCK
    row = lambda i, c: (i * nc + c, 0)
    layer = lambda i, c: (l, 0, 0)
    n_in = 9
    body, xspecs, xargs, k_alias = _stacked(_ssd_prompt_kernel, n_in, bufs)
    st = (N_SSM_HEADS, HEAD_DIM, D_STATE)
    return pl.pallas_call(
        body,
        grid=(n, nc),
        out_shape=[jax.ShapeDtypeStruct((t, D_INNER), F32),
                   jax.ShapeDtypeStruct((depth, n) + st, F32)],
        in_specs=[pl.BlockSpec((rows, CONV_DIM), row),
                  pl.BlockSpec((rows, D_INNER), row),
                  pl.BlockSpec((rows, LANES), row),
                  pl.BlockSpec((None, 1, LANES), layer),
                  pl.BlockSpec((None, 1, LANES), layer),
                  pl.BlockSpec((None, 1, D_INNER), layer),
                  pl.BlockSpec((None, 1, D_INNER), layer),
                  pl.BlockSpec((BLOCK, BLOCK), lambda i, c: (0, 0)),
                  pl.BlockSpec((LANES, D_INNER), lambda i, c: (0, 0))] + xspecs,
        out_specs=[pl.BlockSpec((rows, D_INNER), row),
                   pl.BlockSpec((None, None) + st, lambda i, c: (l, i, 0, 0, 0))],
        scratch_shapes=[pltpu.VMEM(st, F32)],
        input_output_aliases=_alias_map(n_in, k_alias, 2) if k_alias else {},
        compiler_params=_cparams(("parallel", "arbitrary")),
        name="ssd_prompt",
    )(xbc, z, dt, dtb, alog, dsk, ng, tri, ex, *xargs)


def _attn_a_sample_kernel(sink_ref, q_ref, kn_ref, vn_ref, kt_ref, vt_ref, bc_ref, bn_ref, o_ref, *, nbs, l):
    rb = nbs * SROWS
    lo = _lane_iota((rb, LANES)) < HEAD_DIM
    s0, s1 = q_ref[:, 0:LANES], q_ref[:, LANES:2 * LANES]
    per_seq = lambda v: v.reshape(nbs, SROWS, LANES)
    qm = jnp.concatenate([
        per_seq(jnp.where(lo, s0, 0.0)),
        per_seq(jnp.where(lo, pltpu.roll(s0, HEAD_DIM, 1), 0.0)),
        per_seq(jnp.where(lo, 0.0, pltpu.roll(s1, HEAD_DIM, 1))),
        per_seq(jnp.where(lo, 0.0, s1))], axis=1).astype(BF16)
    kt = kt_ref[...].reshape(nbs, LANES, WIN).astype(BF16)
    vt = vt_ref[...].reshape(nbs, LANES, WIN).astype(BF16)
    kn = kn_ref[...].reshape(nbs, SROWS, LANES).astype(BF16)
    vn = vn_ref[...].reshape(nbs, SROWS, LANES).astype(BF16)
    s_c = jnp.einsum('nqd,ndk->nqk', qm, kt, preferred_element_type=F32) + bc_ref[...]
    s_n = jnp.einsum('nqd,nkd->nqk', qm, kn, preferred_element_type=F32) + bn_ref[...]
    hrow = lax.broadcasted_iota(jnp.int32, (nbs, 4 * SROWS, 1), 1) // SROWS
    sink = jnp.where(hrow == 0, sink_ref[l, 0], jnp.where(hrow == 1, sink_ref[l, 1],
                     jnp.where(hrow == 2, sink_ref[l, 2], sink_ref[l, 3])))
    m = jnp.maximum(jnp.max(s_c, axis=-1, keepdims=True), jnp.max(s_n, axis=-1, keepdims=True))
    m = jnp.maximum(m, sink)
    p_c = jnp.exp2(s_c - m)
    p_n = jnp.exp2(s_n - m)
    den = jnp.sum(p_c, axis=-1, keepdims=True) + jnp.sum(p_n, axis=-1, keepdims=True) + jnp.exp2(sink - m)
    o = (jnp.einsum('nqk,ndk->nqd', p_c.astype(BF16), vt, preferred_element_type=F32)
         + jnp.einsum('nqk,nkd->nqd', p_n.astype(BF16), vn, preferred_element_type=F32)) / den
    o0, o1, o2, o3 = (o[:, i * SROWS:(i + 1) * SROWS, :].reshape(rb, LANES) for i in range(4))
    o_ref[:, 0:LANES] = jnp.where(lo, o0, pltpu.roll(o1, HEAD_DIM, 1))
    o_ref[:, LANES:2 * LANES] = jnp.where(lo, pltpu.roll(o2, HEAD_DIM, 1), o3)


def _attn_a_sample(q8, k8, v8, cache_kt, cache_vt, bias_c, bias_n, sinks, l, nbs):
    ns = cache_kt.shape[1]
    tok = lambda i: (i, 0)
    cache = pl.BlockSpec((None, nbs, 2, HEAD_DIM, WIN), lambda i: (l, i, 0, 0, 0))
    c3 = lambda i: (0, 0, 0)
    return pl.pallas_call(
        functools.partial(_attn_a_sample_kernel, nbs=nbs, l=l),
        grid=(ns // nbs,),
        out_shape=jax.ShapeDtypeStruct((ns * SROWS, 256), F32),
        in_specs=[pl.BlockSpec(memory_space=pltpu.SMEM),
                  pl.BlockSpec((nbs * SROWS, 256), tok),
                  pl.BlockSpec((nbs * SROWS, LANES), tok),
                  pl.BlockSpec((nbs * SROWS, LANES), tok),
                  cache, cache,
                  pl.BlockSpec((1, 4 * SROWS, WIN), c3),
                  pl.BlockSpec((1, 4 * SROWS, SROWS), c3)],
        out_specs=pl.BlockSpec((nbs * SROWS, 256), tok),
        compiler_params=_cparams(("parallel",)),
        name="attn_a_sample",
    )(sinks, q8, k8, v8, cache_kt, cache_vt, bias_c, bias_n)


def _softmax_parts(parts):
    m = None
    for s in parts:
        mm = jnp.max(s, axis=-1, keepdims=True)
        m = mm if m is None else jnp.maximum(m, mm)
    ps = [jnp.exp2(s - m) for s in parts]
    den = None
    for p in ps:
        dd = jnp.sum(p, axis=-1, keepdims=True)
        den = dd if den is None else den + dd
    return ps, den, m + jnp.log2(den)


def _attn_c_sample_kernel(q_ref, kn_ref, vn_ref, kt_ref, vt_ref, b12_ref, b3_ref, bn_ref, o_ref, *, nbs, lc):
    lo = _lane_iota((SROWS, LANES)) < HEAD_DIM
    pr = 2 * SROWS
    units = [(n, j, slice(n * SROWS, (n + 1) * SROWS), slice(j * LANES, (j + 1) * LANES))
             for n in range(nbs) for j in range(2)]
    scores = []
    for n, j, rows, lanes in units:
        qp = q_ref[rows, lanes]
        lhs = jnp.concatenate([jnp.where(lo, qp, 0.0), jnp.where(lo, 0.0, qp)], axis=0).astype(BF16)
        ktp = kt_ref[n, 2 * j:2 * j + 2].reshape(LANES, lc).astype(BF16)
        scores.append((_dot(lhs, ktp), _dot_t(lhs, kn_ref[rows, lanes].astype(BF16))))
    probs = []
    for (n, j, rows, lanes), (s, sn) in zip(units, scores):
        s_near = s[:, lc - NEAR:]
        (p1, p1n), d1, l1 = _softmax_parts([s_near + b12_ref[0, j], sn + bn_ref[0, j]])
        (p2, p2n), d2, l2 = _softmax_parts([s_near + b12_ref[1, j], sn + bn_ref[1, j]])
        (p3, p3n), d3, l3 = _softmax_parts([s + b3_ref[j], sn + bn_ref[2, j]])
        m = jnp.maximum(jnp.maximum(l1, l2), l3)
        w1, w2, w3 = jnp.exp2(l1 - m), jnp.exp2(l2 - m), jnp.exp2(l3 - m)
        wsum = w1 + w2 + w3
        probs.append((jnp.concatenate([p1, p2], axis=0).astype(BF16), p3.astype(BF16),
                      (p1n.astype(BF16), p2n.astype(BF16), p3n.astype(BF16)),
                      (w1 / (d1 * wsum), w2 / (d2 * wsum), w3 / (d3 * wsum))))
    for (n, j, rows, lanes), (p12, p3, pn, wts) in zip(units, probs):
        vtp = vt_ref[n, 2 * j:2 * j + 2].reshape(LANES, lc).astype(BF16)
        vnp = vn_ref[rows, lanes].astype(BF16)
        o12 = _dot_t(p12, vtp[:, lc - NEAR:])
        o = ((o12[0:pr] + _dot(pn[0], vnp)) * wts[0] + (o12[pr:2 * pr] + _dot(pn[1], vnp)) * wts[1]
             + (_dot_t(p3, vtp) + _dot(pn[2], vnp)) * wts[2])
        o_ref[rows, lanes] = jnp.where(lo, o[0:SROWS], o[SROWS:pr])


def _attn_c_sample(q8, k8, v8, cache_kt, cache_vt, b12, b3, bn, l, nbs):
    ns, lc = cache_kt.shape[1], cache_kt.shape[4]
    tok = lambda i: (i, 0)
    cache = pl.BlockSpec((None, nbs, 4, HEAD_DIM, lc), lambda i: (l, i, 0, 0, 0))
    c3 = lambda i: (0, 0, 0)
    c4 = lambda i: (0, 0, 0, 0)
    return pl.pallas_call(
        functools.partial(_attn_c_sample_kernel, nbs=nbs, lc=lc),
        grid=(ns // nbs,),
        out_shape=jax.ShapeDtypeStruct((ns * SROWS, 256), F32),
        in_specs=[pl.BlockSpec((nbs * SROWS, 256), tok),
                  pl.BlockSpec((nbs * SROWS, 256), tok),
                  pl.BlockSpec((nbs * SROWS, 256), tok),
                  cache, cache,
                  pl.BlockSpec((2, 2, 2 * SROWS, NEAR), c4),
                  pl.BlockSpec((2, 2 * SROWS, lc), c3),
                  pl.BlockSpec((3, 2, 2 * SROWS, SROWS), c4)],
        out_specs=pl.BlockSpec((nbs * SROWS, 256), tok),
        compiler_params=_cparams(("parallel",)),
        name="attn_c_sample",
    )(q8, k8, v8, cache_kt, cache_vt, b12, b3, bn)


def _ssd_sample_pre_kernel(xbc_ref, prev_ref, dt_ref, cw_ref, cb_ref, dtb_ref, alog_ref, dsk_ref, ex_ref,
                           yd_ref, eac_ref, xw_ref, bm_ref, cm_ref, cd_ref):
    r = xbc_ref.shape[0]
    rid = lax.broadcasted_iota(jnp.int32, (r, 1), 0) % SROWS
    x = jnp.where(jnp.logical_and(rid >= 1, rid < SVALID), prev_ref[...], xbc_ref[...])
    y = cb_ref[...] + cw_ref[3:4, :] * x
    for d in range(1, 4):
        y = y + cw_ref[3 - d:4 - d, :] * pltpu.roll(x, d, 0)
    xa = _silu(y)
    xs = xa[:, 0:D_INNER]
    bm = xa[:, D_INNER:D_INNER + 2 * D_STATE]
    cm = xa[:, D_INNER + 2 * D_STATE:CONV_DIM]
    valid = rid >= SVALID
    dt = jnp.where(valid, _softplus(dt_ref[...] + dtb_ref[...]), 0.0)
    a = dt * (-jnp.exp(alog_ref[...]))
    acum = a
    rem = jnp.zeros_like(a)
    for d in range(1, 4):
        acum = acum + jnp.where(rid - d >= SVALID, pltpu.roll(a, d, 0), 0.0)
        rem = rem + jnp.where(rid + d < SROWS, pltpu.roll(a, r - d, 0), 0.0)
    ex = ex_ref[...]

    def expand(v):
        return _expand_heads(v, ex)

    lane = _lane_iota((r, LANES))
    y_acc = dsk_ref[...] * xs
    for d in range(4):
        ok = rid - d >= SVALID
        bsh = bm if d == 0 else pltpu.roll(bm, d, 0)
        cb0 = jnp.sum(cm[:, 0:D_STATE] * bsh[:, 0:D_STATE], axis=-1, keepdims=True)
        cb1 = jnp.sum(cm[:, D_STATE:] * bsh[:, D_STATE:], axis=-1, keepdims=True)
        cbh = jnp.where(lane < N_SSM_HEADS // 2, cb0, cb1)
        if d == 0:
            coef = cbh * dt
            xsh = xs
        else:
            dec = jnp.exp(jnp.where(ok, acum - pltpu.roll(acum, d, 0), NEG))
            coef = cbh * dec * pltpu.roll(dt, d, 0)
            xsh = pltpu.roll(xs, d, 0)
        y_acc = y_acc + expand(jnp.where(ok, coef, 0.0)) * xsh
    yd_ref[...] = y_acc
    eac_ref[...] = expand(jnp.exp(acum))
    xw_ref[...] = xs * expand(dt * jnp.exp(rem))
    bm_ref[...] = bm
    cm_ref[...] = cm
    cd_ref[...] = jnp.exp(acum + rem)


def _ssd_sample_pre(xbc8, prev8, dt8, cw8, cb, dtb, alog, dsk, ex, l, rb):
    r = xbc8.shape[0]
    row = lambda i: (i, 0)
    layer = lambda i: (l, 0, 0)
    widths = (D_INNER, D_INNER, D_INNER, 2 * D_STATE, 2 * D_STATE, LANES)
    return pl.pallas_call(
        _ssd_sample_pre_kernel,
        grid=(r // rb,),
        out_shape=[jax.ShapeDtypeStruct((r, w_), F32) for w_ in widths],
        in_specs=[pl.BlockSpec((rb, CONV_DIM), row),
                  pl.BlockSpec((None, rb, CONV_DIM), lambda i: (l, i, 0)),
                  pl.BlockSpec((rb, LANES), row),
                  pl.BlockSpec((None, SUBLANES, CONV_DIM), layer),
                  pl.BlockSpec((None, 1, CONV_DIM), layer),
                  pl.BlockSpec((None, 1, LANES), layer),
                  pl.BlockSpec((None, 1, LANES), layer),
                  pl.BlockSpec((None, 1, D_INNER), layer),
                  pl.BlockSpec((LANES, D_INNER), lambda i: (0, 0))],
        out_specs=[pl.BlockSpec((rb, w_), row) for w_ in widths],
        compiler_params=_cparams(("parallel",)),
        name="ssd_sample_pre",
    )(xbc8, prev8, dt8, cw8, cb, dtb, alog, dsk, ex)


def _ssd_sample_state_kernel(cd_ref, yd_ref, eac_ref, xw_ref, bm_ref, cm_ref, z_ref, h0_ref, ng_ref,
                             ob_ref, hn_ref, *, nbs):
    base = pl.program_id(0) * nbs
    yo = []
    for n in range(nbs):
        rows = slice(n * SROWS, (n + 1) * SROWS)
        parts = []
        for g in range(2):
            hp = h0_ref[n, 4 * g:4 * g + 4].reshape(4 * HEAD_DIM, D_STATE)
            cmg = cm_ref[rows, g * D_STATE:(g + 1) * D_STATE].astype(BF16)
            bmg = bm_ref[rows, g * D_STATE:(g + 1) * D_STATE].astype(BF16)
            parts.append(_dot_t(cmg, hp.astype(BF16)))
            s_new = _dot_t0(xw_ref[rows, g * 256:(g + 1) * 256].astype(BF16), bmg)
            for hh in range(4):
                h = 4 * g + hh
                hn_ref[n, h] = (hp[hh * HEAD_DIM:(hh + 1) * HEAD_DIM, :] * cd_ref[base + n, h]
                                + s_new[hh * HEAD_DIM:(hh + 1) * HEAD_DIM, :])
        yo.append(jnp.concatenate(parts, axis=1))
    y = yd_ref[...] + eac_ref[...] * jnp.concatenate(yo, axis=0)
    ob_ref[...] = _gated_group_norm(y, z_ref[...], ng_ref[...])


def _ssd_sample_state(cd, yd, eac, xw, bm, cm, z8, state, ng, l, depth, nbs, bufs):
    ns = state.shape[1]
    row = lambda i: (i, 0)
    st_spec = pl.BlockSpec((None, nbs, N_SSM_HEADS, HEAD_DIM, D_STATE), lambda i: (l, i, 0, 0, 0))
    rb = nbs * SROWS
    n_in = 9
    body, xspecs, xargs, k_alias = _stacked(functools.partial(_ssd_sample_state_kernel, nbs=nbs), n_in, bufs)
    return pl.pallas_call(
        body,
        grid=(ns // nbs,),
        out_shape=[jax.ShapeDtypeStruct((ns * SROWS, D_INNER), F32),
                   jax.ShapeDtypeStruct(state.shape, F32)],
        in_specs=[pl.BlockSpec(memory_space=pltpu.SMEM),
                  pl.BlockSpec((rb, D_INNER), row),
                  pl.BlockSpec((rb, D_INNER), row),
                  pl.BlockSpec((rb, D_INNER), row),
                  pl.BlockSpec((rb, 2 * D_STATE), row),
                  pl.BlockSpec((rb, 2 * D_STATE), row),
                  pl.BlockSpec((rb, D_INNER), row),
                  st_spec,
                  pl.BlockSpec((None, 1, D_INNER), lambda i: (l, 0, 0))] + xspecs,
        out_specs=[pl.BlockSpec((rb, D_INNER), row), st_spec],
        input_output_aliases=_alias_map(n_in, k_alias, 2) if k_alias else {},
        compiler_params=_cparams(("parallel",)),
        name="ssd_sample_state",
    )(cd, yd, eac, xw, bm, cm, z8, state, ng, *xargs)


def _prompt_bias_item(dil, col0):
    qi = np.arange(BLOCK)[:, None]
    kj = np.arange(2 * BLOCK)[None, :]
    dist = BLOCK + qi - kj
    ok = (dist >= 0) & (dist <= WIN)
    variants = [_masked_bucket(dist * dil, ok), _masked_bucket(dist * dil, ok & (kj >= BLOCK))]
    segs = [((2 * h + v) * BLOCK, (2 * h + v + 1) * BLOCK, col0 + h) for h in range(4) for v in range(2)]
    return np.concatenate(variants * 4, axis=0), segs


def _sample_bias_item(dist_of, ok_of, cols, col0):
    t = np.arange(SROWS)[:, None] - SVALID
    bkt = np.where(t >= 0, _masked_bucket(dist_of(t, cols), ok_of(t, cols)), 0).astype(np.int32)
    segs = [(h * SROWS, (h + 1) * SROWS, col0 + h) for h in range(4)]
    return np.concatenate([bkt] * 4, axis=0), segs


def _cache_bias_item(length, window, dil, span, col0):
    dist_of = lambda t, pos: length + t - pos
    ok_of = lambda t, pos: (dist_of(t, pos) >= 0) & (dist_of(t, pos) <= window) & (dist_of(t, pos) % dil == 0)
    return _sample_bias_item(dist_of, ok_of, np.arange(length - span, length)[None, :], col0)


def _new_rows_bias_item(same_token_only, col0):
    dist_of = lambda t, r: t - (r - SVALID)
    if same_token_only:
        ok_of = lambda t, r: (r >= SVALID) & (dist_of(t, r) == 0)
    else:
        ok_of = lambda t, r: (r >= SVALID) & (dist_of(t, r) >= 0)
    return _sample_bias_item(dist_of, ok_of, np.arange(SROWS)[None, :], col0)


def kernel(x_prompt, x_sample, cache_a_k, cache_a_v, cache_c_k, cache_c_v, state_ssm, state_conv,
           norm_mix_g, w_in, a_q_norm_g, a_k_norm_g, a_sinks, c_q_norm_g, c_k_norm_g, rel_bias,
           conv_w, conv_b, dt_bias, a_log, d_skip, ssm_norm_g, w_out, norm_mlp_g, w_up, w_down):
    depth = w_in.shape[0]
    n, seq, _ = x_prompt.shape
    ns, ts, _ = x_sample.shape
    lc = cache_c_k.shape[2]
    assert ts == SROWS - SVALID and seq % (BLOCK * 16) == 0
    assert cache_a_k.shape[2] == WIN and all(w <= lc for w, _ in C_BRANCHES)
    nb = seq // BLOCK

    w_in_b = jnp.pad(w_in, ((0, 0), (0, 0), (0, COL_END - IN_COLS))).astype(BF16)
    w_out_b, w_up_b, w_down_b = w_out.astype(BF16), w_up.astype(BF16), w_down.astype(BF16)
    ones = jnp.ones((depth, 128), F32)
    q_scale = ATTN_SCALE * LOG2E
    gain = jnp.concatenate([jnp.tile(a_q_norm_g, (1, 4)) * q_scale, jnp.tile(a_k_norm_g, (1, 2)), ones,
                            jnp.tile(c_q_norm_g, (1, 4)) * q_scale, jnp.tile(c_k_norm_g, (1, 4))], axis=1)
    rel_bias = rel_bias * LOG2E
    a_sinks = a_sinks * LOG2E
    lane_grp = jnp.arange(2 * LANES) // HEAD_DIM
    bd = (lane_grp[:, None] == lane_grp[None, :]).astype(BF16)
    tri = (jnp.arange(BLOCK)[None, :] <= jnp.arange(BLOCK)[:, None]).astype(F32)
    ex = (jnp.arange(LANES)[:, None] == (jnp.arange(D_INNER) // HEAD_DIM)[None, :]).astype(BF16)
    cw8 = jnp.pad(conv_w, ((0, 0), (0, SUBLANES - conv_w.shape[1]), (0, 0)))
    pad_h = lambda v: jnp.pad(v, ((0, 0), (0, LANES - N_SSM_HEADS)))[:, None, :]
    vec = lambda v: v[:, None, :]
    dtb, alog = pad_h(dt_bias), pad_h(a_log)
    dsk = vec(jnp.repeat(d_skip, HEAD_DIM, axis=1))
    g_mix, g_mlp, gain, cb, ng = vec(norm_mix_g), vec(norm_mlp_g), vec(gain), vec(conv_b), vec(ssm_norm_g)

    (w1, d1), (w2, d2), (w3, d3) = C_BRANCHES
    items = ([_prompt_bias_item(1, 0)] + [_prompt_bias_item(d, 4) for d in C_DILS]
             + [_cache_bias_item(WIN, WIN, 1, WIN, 0), _new_rows_bias_item(False, 0)]
             + [_cache_bias_item(lc, w1, d1, NEAR, 4), _cache_bias_item(lc, w2, d2, NEAR, 4),
                _cache_bias_item(lc, w3, d3, lc, 4), _new_rows_bias_item(False, 4), _new_rows_bias_item(True, 4)])
    pa, pc1, pc4, pc16, sa_c, sa_n, sc1, sc2, sc3, sn_all, sn_same = _expand_biases(rel_bias, items)
    by_variant = lambda b: b.reshape(4, 2, BLOCK, 2 * BLOCK)
    by_pair = lambda b: b.reshape(2, 2 * SROWS, b.shape[-1])
    bias_a = by_variant(pa)
    bias_c = jnp.stack([by_variant(pc1), by_variant(pc4), by_variant(pc16)])
    sbias_a_c, sbias_a_n = sa_c[None], sa_n[None]
    sb12 = jnp.stack([by_pair(sc1), by_pair(sc2)])
    sb3 = by_pair(sc3)
    sbn = jnp.stack([by_pair(sn_all), by_pair(sn_same), by_pair(sn_same)])

    cak_t, cav_t = cache_a_k.transpose(0, 1, 3, 4, 2), cache_a_v.transpose(0, 1, 3, 4, 2)
    cck_t, ccv_t = cache_c_k.transpose(0, 1, 3, 4, 2), cache_c_v.transpose(0, 1, 3, 4, 2)

    hp = x_prompt.reshape(n * seq, D_MODEL)
    hs = jnp.pad(x_sample, ((0, 0), (SVALID, 0), (0, 0))).reshape(ns * SROWS, D_MODEL)
    tm_p = 512
    tm_s = min(512, ns * SROWS)
    nbs_a = min(16, ns)
    nbs_c = min(2, ns)
    nbs_b = min(16, ns)
    prev8 = jnp.pad(state_conv, ((0, 0), (0, 0), (1, SROWS - 4), (0, 0))).reshape(depth, ns * SROWS, CONV_DIM)

    p_ak = p_ck = p_ssm = s_ssm = None
    p_conv, s_small = [], [[] for _ in range(5)]
    for l in range(depth):
        qa, ka, va, qc, kc, vc, z, xbc, dt, tail = _in_proj(hp, g_mix, w_in_b, gain, bd, cw8, cb, l, tm_p, seq // tm_p)
        oa, *p_ak = _attn_a_prompt(qa, ka, va, bias_a, a_sinks, l, depth, n, seq, p_ak)
        oc, *p_ck = _attn_c_prompt(qc, kc, vc, bias_c, l, depth, n, seq, p_ck)
        ob, *p_ssm = _ssd_prompt(xbc, z, dt, dtb, alog, dsk, ng, tri, ex, l, depth, n, nb, p_ssm)
        hp = _out_mlp(hp, oa, ob, oc, w_out_b, g_mlp, w_up_b, w_down_b, l, tm_p)
        p_conv.append(tail)

        qa, ka, va, qc, kc, vc, z, xbc, dt, _ = _in_proj(hs, g_mix, w_in_b, gain, bd, cw8, cb, l, tm_s, 0)
        oa = _attn_a_sample(qa, ka, va, cak_t, cav_t, sbias_a_c, sbias_a_n, a_sinks, l, nbs_a)
        oc = _attn_c_sample(qc, kc, vc, cck_t, ccv_t, sb12, sb3, sbn, l, nbs_c)
        yd, eac, xw, bm, cm, cd = _ssd_sample_pre(xbc, prev8, dt, cw8, cb, dtb, alog, dsk, ex, l,
                                                  min(256, ns * SROWS))
        cd_s = cd.reshape(ns, SROWS, LANES)[:, SROWS - 1, :N_SSM_HEADS]
        ob, *s_ssm = _ssd_sample_state(cd_s, yd, eac, xw, bm, cm, z, state_ssm, ng, l, depth, nbs_b, s_ssm)
        hs = _out_mlp(hs, oa, ob, oc, w_out_b, g_mlp, w_up_b, w_down_b, l, tm_s)
        for dst, val in zip(s_small, (ka, va, kc, vc, xbc)):
            dst.append(val)

    y_prompt = hp.reshape(n, seq, D_MODEL)
    y_sample = hs.reshape(ns, SROWS, D_MODEL)[:, SVALID:]
    unminor = lambda buf, heads: buf.reshape(depth, n, heads, HEAD_DIM, -1).transpose(0, 1, 4, 2, 3)
    p_state = (unminor(p_ak[0], 2), unminor(p_ak[1], 2), unminor(p_ck[0], 4), unminor(p_ck[1], 4),
               p_ssm[0], jnp.stack(p_conv)[:, :, SUBLANES - 3:])
    new = lambda vals, first, tail: jnp.stack(vals).reshape((depth, ns, SROWS) + tail)[:, :, first:]
    s_state = tuple(new(s_small[i], SVALID, (hh, HEAD_DIM)) for i, hh in enumerate((2, 2, 4, 4))) + (
        s_ssm[0], new(s_small[4], SROWS - 3, (CONV_DIM,)))
    return (y_prompt, y_sample) + p_state + s_state
```

```python
import functools
import math

import jax
import jax.numpy as jnp
import numpy as np
from jax import lax
from jax.experimental import pallas as pl
from jax.experimental.pallas import tpu as pltpu

F32 = jnp.float32
BF16 = jnp.bfloat16
HIGHEST = lax.Precision.HIGHEST

D_MODEL = 1024
HEAD_DIM = 64
LANES = 128
SUBLANES = 8
BLOCK = 128
WIN = 128
D_INNER = 512
D_STATE = 128
N_SSM_HEADS = 8
CONV_DIM = 1024
D_FF = 4096
NUM_BUCKETS = 32
REL_MAX_DIST = 2048
EPS = 1e-6
ATTN_SCALE = HEAD_DIM ** -0.5
LOG2E = math.log2(math.e)
NEG = -1e30
C_BRANCHES = ((128, 1), (512, 4), (2048, 16))
C_DILS = tuple(d for _, d in C_BRANCHES)
SROWS = 8
SVALID = 4
NEAR = 512
C_UNROLL = 8
A_UNROLL = 4
SSD_SUB = 4
IN_PROJ_PARTS = 1

COL_QA, COL_KA, COL_VA, COL_QC, COL_KC, COL_VC, COL_Z, COL_XBC, COL_DT, COL_END = (
    0, 256, 384, 512, 768, 1024, 1280, 1792, 2816, 2944)
IN_COLS = 2824
VMEM_LIMIT = 56 * 1024 * 1024


def _cparams(sem):
    return pltpu.CompilerParams(dimension_semantics=sem, vmem_limit_bytes=VMEM_LIMIT)


def _dot(a, b):
    return jnp.dot(a, b, preferred_element_type=F32)


def _dot_t(a, b):
    return lax.dot_general(a, b, (((1,), (1,)), ((), ())), preferred_element_type=F32)


def _dot_t0(a, b):
    return lax.dot_general(a, b, (((0,), (0,)), ((), ())), preferred_element_type=F32)


def _lane_iota(shape):
    return lax.broadcasted_iota(jnp.int32, shape, len(shape) - 1)


def _silu(x):
    hx = 0.5 * x
    return hx + hx * jnp.tanh(hx)


def _softplus(x):
    return jnp.maximum(x, 0.0) + jnp.log(1.0 + jnp.exp(-jnp.abs(x)))


def _stacked(body, n_in, bufs):
    if bufs is None:
        return body, [], [], 0
    k = len(bufs)

    def wrapped(*refs):
        return body(*refs[:n_in], *refs[n_in + k:])

    return wrapped, [pl.BlockSpec(memory_space=pl.ANY)] * k, list(bufs), k


def _alias_map(n_in, k, n_out):
    return {n_in + i: n_out - k + i for i in range(k)}


def _bias_kernel(tab_ref, *refs, segments):
    n = len(segments)
    for bkt_ref, o_ref, segs in zip(refs[:n], refs[n:], segments):
        for r0, r1, col in segs:
            bkt = bkt_ref[r0:r1, :]
            acc = jnp.full(bkt.shape, NEG, F32)
            for b in range(NUM_BUCKETS):
                acc = jnp.where(bkt == b, tab_ref[b, col], acc)
            o_ref[r0:r1, :] = acc


def _expand_biases(table, items):
    bkts = [jnp.asarray(b) for b, _ in items]
    full = lambda b: pl.BlockSpec(b.shape, lambda: (0, 0))
    return pl.pallas_call(
        functools.partial(_bias_kernel, segments=[s for _, s in items]),
        out_shape=[jax.ShapeDtypeStruct(b.shape, F32) for b in bkts],
        in_specs=[pl.BlockSpec(memory_space=pltpu.SMEM)] + [full(b) for b in bkts],
        out_specs=[full(b) for b in bkts],
        name="bias_expand",
    )(table, *bkts)


def _t5_bucket(dist):
    max_exact = NUM_BUCKETS // 2
    d = np.maximum(dist, 0)
    df = np.maximum(d, 1).astype(np.float32)
    ratio = np.log(df / np.float32(max_exact)) / np.float32(math.log(REL_MAX_DIST / max_exact))
    large = max_exact + (ratio * np.float32(NUM_BUCKETS - max_exact)).astype(np.int32)
    return np.where(d < max_exact, d, np.minimum(large, NUM_BUCKETS - 1))


def _masked_bucket(dist, valid):
    return np.where(valid, _t5_bucket(dist), -1).astype(np.int32)


def _head_norm(p, gain, bd):
    sq = p * p
    hi = sq.astype(BF16)
    lo = (sq - hi.astype(F32)).astype(BF16)
    ss = _dot(hi, bd) + _dot(lo, bd)
    return p * lax.rsqrt(ss * (1.0 / HEAD_DIM) + EPS) * gain


def _in_proj_kernel(h_ref, g_ref, w_ref, gain_ref, bd_ref, cw_ref, cb_ref,
                    qa_ref, ka_ref, va_ref, qc_ref, kc_ref, vc_ref, z_ref, xbc_ref, dt_ref, tail_ref, xbuf, u_scr,
                    *, conv_tiles):
    if conv_tiles:
        @pl.when(pl.program_id(0) % conv_tiles == 0)
        def _():
            xbuf[0:SUBLANES, :] = jnp.zeros((SUBLANES, CONV_DIM), F32)

    bd = bd_ref[...]
    tm = h_ref.shape[0]
    part = tm // IN_PROJ_PARTS
    for r0 in range(0, tm, part):
        rs = slice(r0, r0 + part)
        x = h_ref[rs, :]
        ms = jnp.mean(x * x, axis=-1, keepdims=True)
        u_scr[rs, :] = (x * lax.rsqrt(ms + EPS) * g_ref[...]).astype(BF16)

        def proj(c0, c1, rs=rs):
            return _dot(u_scr[rs, :], w_ref[:, c0:c1])

        xbc = proj(COL_XBC, COL_DT)
        if conv_tiles:
            xbuf[SUBLANES + r0:SUBLANES + r0 + part, :] = xbc
            y = cb_ref[...] + cw_ref[3:4, :] * xbc
            for kk in range(3):
                y = y + cw_ref[kk:kk + 1, :] * xbuf[SUBLANES + r0 - 3 + kk:SUBLANES + r0 - 3 + kk + part, :]
            xbc_ref[rs, :] = y
        else:
            xbc_ref[rs, :] = xbc
        if r0 + part == tm:
            tail = xbc[part - SUBLANES:part, :]
            tail_ref[...] = tail
            if conv_tiles:
                xbuf[0:SUBLANES, :] = tail
        groups = [COL_QA, COL_KA, COL_QC, COL_KC]
        raw = [proj(c0, c0 + 256) for c0 in groups]
        va_ref[rs, :] = raw[1][:, COL_VA - COL_KA:]
        vc_ref[rs, :] = proj(COL_VC, COL_Z)
        z_ref[rs, :] = proj(COL_Z, COL_XBC)
        normed = [_head_norm(p, gain_ref[:, c0:c0 + 256], bd) for c0, p in zip(groups, raw)]
        qa_ref[rs, :] = normed[0]
        ka_ref[rs, :] = normed[1][:, 0:COL_VA - COL_KA]
        qc_ref[rs, :] = normed[2]
        kc_ref[rs, :] = normed[3]
        dt_ref[rs, :] = proj(COL_DT, COL_END)


def _in_proj(h, g, w, gain, bd, cw8, cb, l, tm, conv_tiles):
    t = h.shape[0]
    widths = (256, 128, 128, 256, 256, 256, 512, 1024, 128)
    layer = lambda i: (l, 0, 0)
    n_tail = t // tm // conv_tiles if conv_tiles else t // tm
    per_seq = conv_tiles if conv_tiles else 1
    return pl.pallas_call(
        functools.partial(_in_proj_kernel, conv_tiles=conv_tiles),
        grid=(t // tm,),
        out_shape=[jax.ShapeDtypeStruct((t, w_), F32) for w_ in widths]
        + [jax.ShapeDtypeStruct((n_tail, SUBLANES, CONV_DIM), F32)],
        in_specs=[pl.BlockSpec((tm, D_MODEL), lambda i: (i, 0)),
                  pl.BlockSpec((None, 1, D_MODEL), layer),
                  pl.BlockSpec((None, D_MODEL, COL_END), layer),
                  pl.BlockSpec((None, 1, D_MODEL), layer),
                  pl.BlockSpec((256, 256), lambda i: (0, 0)),
                  pl.BlockSpec((None, SUBLANES, CONV_DIM), layer),
                  pl.BlockSpec((None, 1, CONV_DIM), layer)],
        out_specs=[pl.BlockSpec((tm, w_), lambda i: (i, 0)) for w_ in widths]
        + [pl.BlockSpec((None, SUBLANES, CONV_DIM), lambda i: (i // per_seq, 0, 0))],
        scratch_shapes=[pltpu.VMEM((SUBLANES + tm, CONV_DIM), F32), pltpu.VMEM((tm, D_MODEL), BF16)],
        compiler_params=_cparams(("arbitrary",)),
        name="in_proj",
    )(h, g, w, gain, bd, cw8, cb)


def _out_mlp_kernel(h_ref, oa_ref, ob_ref, oc_ref, wo_ref, g_ref, wu_ref, wd_ref, o_ref):
    acc = _dot(oa_ref[...].astype(BF16), wo_ref[0:256, :])
    acc += _dot(ob_ref[...].astype(BF16), wo_ref[256:768, :])
    acc += _dot(oc_ref[...].astype(BF16), wo_ref[768:1024, :])
    h2 = h_ref[...] + acc
    ms = jnp.mean(h2 * h2, axis=-1, keepdims=True)
    u = (h2 * lax.rsqrt(ms + EPS) * g_ref[...]).astype(BF16)
    ff_chunk = 1024
    mlp = jnp.zeros_like(h2)
    for c in range(D_FF // ff_chunk):
        a = _dot(u, wu_ref[:, c * ff_chunk:(c + 1) * ff_chunk])
        a = jnp.square(jnp.maximum(a, 0.0)).astype(BF16)
        mlp += _dot(a, wd_ref[c * ff_chunk:(c + 1) * ff_chunk, :])
    o_ref[...] = h2 + mlp


def _out_mlp(h, oa, ob, oc, wo, g, wu, wd, l, tm):
    t = h.shape[0]
    layer = lambda i: (l, 0, 0)
    row = lambda i: (i, 0)
    return pl.pallas_call(
        _out_mlp_kernel,
        grid=(t // tm,),
        out_shape=jax.ShapeDtypeStruct((t, D_MODEL), F32),
        in_specs=[pl.BlockSpec((tm, D_MODEL), row),
                  pl.BlockSpec((tm, 256), row),
                  pl.BlockSpec((tm, 512), row),
                  pl.BlockSpec((tm, 256), row),
                  pl.BlockSpec((None, D_MODEL, D_MODEL), layer),
                  pl.BlockSpec((None, 1, D_MODEL), layer),
                  pl.BlockSpec((None, D_MODEL, D_FF), layer),
                  pl.BlockSpec((None, D_FF, D_MODEL), layer)],
        out_specs=pl.BlockSpec((tm, D_MODEL), row),
        compiler_params=_cparams(("parallel",)),
        name="out_mlp",
    )(h, oa, ob, oc, wo, g, wu, wd)


def _attn_a_prompt_kernel(sink_ref, q_ref, k_ref, v_ref, bias_ref, o_ref, kt_ref, vt_ref,
                          k_st, k_sw, v_st, v_sw, *, l, seq):
    lo = _lane_iota((BLOCK, LANES)) < HEAD_DIM
    k = k_ref[...]
    v = v_ref[...]
    for st in (k_st, k_sw, v_st, v_sw):
        st[0:BLOCK, :] = jnp.zeros((BLOCK, LANES), BF16)
    k_st[BLOCK:, :] = k.astype(BF16)
    v_st[BLOCK:, :] = v.astype(BF16)
    k_sw[BLOCK:, :] = pltpu.roll(k, HEAD_DIM, 1).astype(BF16)
    v_sw[BLOCK:, :] = pltpu.roll(v, HEAD_DIM, 1).astype(BF16)
    kt_ref[...] = k_ref[seq - WIN:seq, :].T
    vt_ref[...] = v_ref[seq - WIN:seq, :].T

    def body(g, carry):
        scores, where = [], []
        for u in range(A_UNROLL):
            b = g * A_UNROLL + u
            cur = pl.ds(pl.multiple_of(b * BLOCK, BLOCK), BLOCK)
            both = pl.ds(pl.multiple_of(b * BLOCK, BLOCK), 2 * BLOCK)
            variant = jnp.where(b == 0, 1, 0)
            where.append((cur, both))
            for h in range(4):
                slab = q_ref[cur, (h // 2) * LANES:(h // 2 + 1) * LANES]
                qm = jnp.where(lo if h % 2 == 0 else ~lo, slab, 0.0).astype(BF16)
                kk = k_st if h in (0, 3) else k_sw
                scores.append(_dot_t(qm, kk[both, :]) + bias_ref[h, variant])
        probs = []
        for i, s in enumerate(scores):
            sink = sink_ref[l, i % 4]
            m = jnp.maximum(jnp.max(jnp.maximum(s[:, 0:BLOCK], s[:, BLOCK:2 * BLOCK]), axis=-1, keepdims=True), sink)
            p = jnp.exp2(s - m)
            den = jnp.sum(p[:, 0:BLOCK] + p[:, BLOCK:2 * BLOCK], axis=-1, keepdims=True) + jnp.exp2(sink - m)
            probs.append((p.astype(BF16), den))
        for u, (cur, both) in enumerate(where):
            outs = []
            for h in range(4):
                p, den = probs[4 * u + h]
                vv = v_st if h in (0, 3) else v_sw
                outs.append(_dot(p, vv[both, :]) / den)
            o_ref[cur, 0:LANES] = jnp.where(lo, outs[0], outs[1])
            o_ref[cur, LANES:2 * LANES] = jnp.where(lo, outs[2], outs[3])
        return carry

    lax.fori_loop(0, seq // BLOCK // A_UNROLL, body, 0)


def _attn_a_prompt(q, k, v, bias, sinks, l, depth, n, seq, bufs):
    t = q.shape[0]
    n_in = 5
    body, xspecs, xargs, k_alias = _stacked(functools.partial(_attn_a_prompt_kernel, l=l, seq=seq), n_in, bufs)
    st_shape = jax.ShapeDtypeStruct((depth, n, LANES, WIN), F32)
    st_spec = pl.BlockSpec((None, None, LANES, WIN), lambda i: (l, i, 0, 0))
    tok = lambda w: pl.BlockSpec((seq, w), lambda i: (i, 0))
    return pl.pallas_call(
        body,
        grid=(n,),
        out_shape=[jax.ShapeDtypeStruct((t, 256), F32), st_shape, st_shape],
        in_specs=[pl.BlockSpec(memory_space=pltpu.SMEM), tok(256), tok(LANES), tok(LANES),
                  pl.BlockSpec((4, 2, BLOCK, 2 * BLOCK), lambda i: (0, 0, 0, 0))] + xspecs,
        out_specs=[tok(256), st_spec, st_spec],
        scratch_shapes=[pltpu.VMEM((BLOCK + seq, LANES), BF16)] * 4,
        input_output_aliases=_alias_map(n_in, k_alias, 3) if k_alias else {},
        compiler_params=_cparams(("parallel",)),
        name="attn_a_prompt",
    )(sinks, q, k, v, bias, *xargs)


def _attn_c_prompt_kernel(q_ref, k_ref, v_ref, bias_ref, o_ref, kt_ref, vt_ref, ob_scr, lse_scr, *, seq):
    lane = _lane_iota((BLOCK, LANES))
    lo = lane < HEAD_DIM
    kt_ref[...] = k_ref[...].T
    vt_ref[...] = v_ref[...].T

    def blocks(br, dil, where, has_prev):
        rows = lambda ref, s: ref[pl.ds(s, BLOCK, stride=dil), :]
        scores, vals = [], []
        for qs, ps, first in where:
            q = rows(q_ref, qs)
            if has_prev:
                k2 = jnp.concatenate([rows(k_ref, ps), rows(k_ref, qs)], axis=0).astype(BF16)
                vals.append(jnp.concatenate([rows(v_ref, ps), rows(v_ref, qs)], axis=0).astype(BF16))
                variant = jnp.where(first, 1, 0)
            else:
                k2 = rows(k_ref, qs).astype(BF16)
                vals.append(rows(v_ref, qs).astype(BF16))
            for hh in range(2):
                qm = jnp.where(lo if hh == 0 else ~lo, q, 0.0).astype(BF16)
                bias = bias_ref[br, hh, variant] if has_prev else bias_ref[br, hh, 0, :, BLOCK:2 * BLOCK]
                scores.append(_dot_t(qm, k2) + bias)
        probs = []
        for s in scores:
            sm = jnp.maximum(s[:, 0:BLOCK], s[:, BLOCK:2 * BLOCK]) if has_prev else s
            m = jnp.max(sm, axis=-1, keepdims=True)
            p = jnp.exp2(s - m)
            pm = p[:, 0:BLOCK] + p[:, BLOCK:2 * BLOCK] if has_prev else p
            probs.append((p.astype(BF16), m, jnp.sum(pm, axis=-1, keepdims=True)))
        for u, (qs, _, _) in enumerate(where):
            outs, lses = [], []
            for hh in range(2):
                p, m, den = probs[2 * u + hh]
                outs.append(_dot(p, vals[u]) / den)
                lses.append(jnp.broadcast_to(m + jnp.log2(den), (BLOCK, LANES)))
            ob_scr[br, pl.ds(qs, BLOCK, stride=dil), :] = jnp.where(lo, outs[0], outs[1])
            lse_scr[br, pl.ds(qs, BLOCK, stride=dil), :] = jnp.where(lo, lses[0], lses[1])

    for br, dil in enumerate(C_DILS):
        nb = seq // dil // BLOCK
        span = BLOCK * dil

        def body(g, carry, br=br, dil=dil, nb=nb, span=span):
            where = []
            for u in range(C_UNROLL):
                i = g * C_UNROLL + u
                res = i % dil
                b = i // dil
                where.append((res + b * span, res + jnp.maximum(b - 1, 0) * span, b == 0))
            blocks(br, dil, where, nb > 1)
            return carry

        lax.fori_loop(0, dil * nb // C_UNROLL, body, 0)

    def merge(i, carry):
        rows = pl.ds(pl.multiple_of(i * BLOCK, BLOCK), BLOCK)
        l0, l1, l2 = lse_scr[0, rows, :], lse_scr[1, rows, :], lse_scr[2, rows, :]
        m = jnp.maximum(jnp.maximum(l0, l1), l2)
        w0, w1, w2 = jnp.exp2(l0 - m), jnp.exp2(l1 - m), jnp.exp2(l2 - m)
        num = w0 * ob_scr[0, rows, :] + w1 * ob_scr[1, rows, :] + w2 * ob_scr[2, rows, :]
        o_ref[rows, :] = num / (w0 + w1 + w2)
        return carry

    lax.fori_loop(0, seq // BLOCK, merge, 0)


def _attn_c_prompt(q, k, v, bias, l, depth, n, seq, bufs):
    t = q.shape[0]
    blk = pl.BlockSpec((seq, LANES), lambda i, hp: (i, hp))
    n_in = 4
    body, xspecs, xargs, k_alias = _stacked(functools.partial(_attn_c_prompt_kernel, seq=seq), n_in, bufs)
    st_shape = jax.ShapeDtypeStruct((depth, n, 256, seq), F32)
    st_spec = pl.BlockSpec((None, None, LANES, seq), lambda i, hp: (l, i, hp, 0))
    return pl.pallas_call(
        body,
        grid=(n, 2),
        out_shape=[jax.ShapeDtypeStruct((t, 256), F32), st_shape, st_shape],
        in_specs=[blk, blk, blk,
                  pl.BlockSpec((3, 2, 2, BLOCK, 2 * BLOCK), lambda i, hp: (0, hp, 0, 0, 0))] + xspecs,
        out_specs=[blk, st_spec, st_spec],
        scratch_shapes=[pltpu.VMEM((3, seq, LANES), F32), pltpu.VMEM((3, seq, LANES), F32)],
        input_output_aliases=_alias_map(n_in, k_alias, 3) if k_alias else {},
        compiler_params=_cparams(("parallel", "parallel")),
        name="attn_c_prompt",
    )(q, k, v, bias, *xargs)


def _gated_group_norm(y, z, ng):
    gated = y * _silu(z)
    parts = []
    for grp in range(2):
        gsl = gated[:, grp * 256:(grp + 1) * 256]
        ms = jnp.mean(gsl * gsl, axis=-1, keepdims=True)
        parts.append(gsl * lax.rsqrt(ms + EPS))
    return jnp.concatenate(parts, axis=1) * ng


def _expand_heads(v, ex):
    hi = v.astype(BF16)
    r1 = v - hi.astype(F32)
    mid = r1.astype(BF16)
    lo = (r1 - mid.astype(F32)).astype(BF16)
    return _dot(hi, ex) + _dot(mid, ex) + _dot(lo, ex)


def _ssd_prompt_kernel(xbc_ref, z_ref, dt_ref, dtb_ref, alog_ref, dsk_ref, ng_ref, tri_ref, ex_ref,
                       ob_ref, hl_ref, hst):
    c = pl.program_id(1)

    @pl.when(c == 0)
    def _():
        hst[...] = jnp.zeros_like(hst)

    xa_all = _silu(xbc_ref[...])
    dt_all = _softplus(dt_ref[...] + dtb_ref[...])
    a_all = dt_all * (-jnp.exp(alog_ref[...]))
    row = lax.broadcasted_iota(jnp.int32, (BLOCK, BLOCK), 0)
    lane = _lane_iota((BLOCK, BLOCK))
    causal = lane <= row
    lo = lane < HEAD_DIM
    top = row < HEAD_DIM
    n_pairs = N_SSM_HEADS // 2
    for sub in range(SSD_SUB):
        rs = slice(sub * BLOCK, (sub + 1) * BLOCK)
        xa, dt = xa_all[rs], dt_all[rs]
        xs = xa[:, 0:D_INNER]
        acum = jnp.dot(tri_ref[...], a_all[rs], precision=HIGHEST, preferred_element_type=F32)
        bms = [xa[:, D_INNER + g * D_STATE:D_INNER + (g + 1) * D_STATE].astype(BF16) for g in range(2)]
        cms = [xa[:, D_INNER + (2 + g) * D_STATE:D_INNER + (3 + g) * D_STATE].astype(BF16) for g in range(2)]
        cbs = [_dot_t(cms[g], bms[g]) for g in range(2)]
        hps = [hst[2 * j:2 * j + 2].reshape(BLOCK, D_STATE) for j in range(n_pairs)]
        y_off = [_dot_t(cms[j // 2], hps[j].astype(BF16)) for j in range(n_pairs)]
        last = acum[BLOCK - 1:BLOCK, :]
        e_last = jnp.exp(last)
        w_all = _expand_heads(jnp.exp(last - acum) * dt, ex_ref[...])
        e_all = _expand_heads(jnp.exp(acum), ex_ref[...])
        xs_pairs = [xs[:, j * LANES:(j + 1) * LANES] for j in range(n_pairs)]
        for j in range(n_pairs):
            h0_, h1_ = 2 * j, 2 * j + 1
            xw = xs_pairs[j] * w_all[:, j * LANES:(j + 1) * LANES]
            s_new = _dot_t0(xw.astype(BF16), bms[j // 2])
            cd_pair = jnp.where(top, e_last[:, h0_:h0_ + 1], e_last[:, h1_:h1_ + 1])
            hst[h0_:h0_ + 2] = (hps[j] * cd_pair + s_new).reshape(2, HEAD_DIM, D_STATE)
        acum_t = acum.T
        dt_t = dt.T
        y_parts = []
        for j in range(n_pairs):
            xs_b = xs_pairs[j].astype(BF16)
            yd = []
            for hh in range(2):
                h = 2 * j + hh
                seg = acum[:, h:h + 1] - acum_t[h:h + 1, :]
                dec = jnp.exp(jnp.where(causal, seg, NEG)) * dt_t[h:h + 1, :]
                yd.append(_dot((cbs[j // 2] * dec).astype(BF16), xs_b))
            y_parts.append(jnp.where(lo, yd[0], yd[1]) + y_off[j] * e_all[:, j * LANES:(j + 1) * LANES])
        y = jnp.concatenate(y_parts, axis=1) + dsk_ref[...] * xs
        ob_ref[rs, :] = _gated_group_norm(y, z_ref[rs, :], ng_ref[...])

    @pl.when(c == pl.num_programs(1) - 1)
    def _():
        hl_ref[...] = hst[...]


def _ssd_prompt(xbc, z, dt, dtb, alog, dsk, ng, tri, ex, l, depth, n, nc, bufs):
    t = xbc.shape[0]
    nc = nc // SSD_SUB
    rows = SSD_SUB * BLOCK
    row = lambda i, c: (i * nc + c, 0)
    layer = lambda i, c: (l, 0, 0)
    n_in = 9
    body, xspecs, xargs, k_alias = _stacked(_ssd_prompt_kernel, n_in, bufs)
    st = (N_SSM_HEADS, HEAD_DIM, D_STATE)
    return pl.pallas_call(
        body,
        grid=(n, nc),
        out_shape=[jax.ShapeDtypeStruct((t, D_INNER), F32),
                   jax.ShapeDtypeStruct((depth, n) + st, F32)],
        in_specs=[pl.BlockSpec((rows, CONV_DIM), row),
                  pl.BlockSpec((rows, D_INNER), row),
                  pl.BlockSpec((rows, LANES), row),
                  pl.BlockSpec((None, 1, LANES), layer),
                  pl.BlockSpec((None, 1, LANES), layer),
                  pl.BlockSpec((None, 1, D_INNER), layer),
                  pl.BlockSpec((None, 1, D_INNER), layer),
                  pl.BlockSpec((BLOCK, BLOCK), lambda i, c: (0, 0)),
                  pl.BlockSpec((LANES, D_INNER), lambda i, c: (0, 0))] + xspecs,
        out_specs=[pl.BlockSpec((rows, D_INNER), row),
                   pl.BlockSpec((None, None) + st, lambda i, c: (l, i, 0, 0, 0))],
        scratch_shapes=[pltpu.VMEM(st, F32)],
        input_output_aliases=_alias_map(n_in, k_alias, 2) if k_alias else {},
        compiler_params=_cparams(("parallel", "arbitrary")),
        name="ssd_prompt",
    )(xbc, z, dt, dtb, alog, dsk, ng, tri, ex, *xargs)


def _attn_a_sample_kernel(sink_ref, q_ref, kn_ref, vn_ref, kt_ref, vt_ref, bc_ref, bn_ref, o_ref, *, nbs, l):
    rb = nbs * SROWS
    lo = _lane_iota((rb, LANES)) < HEAD_DIM
    s0, s1 = q_ref[:, 0:LANES], q_ref[:, LANES:2 * LANES]
    per_seq = lambda v: v.reshape(nbs, SROWS, LANES)
    qm = jnp.concatenate([
        per_seq(jnp.where(lo, s0, 0.0)),
        per_seq(jnp.where(lo, pltpu.roll(s0, HEAD_DIM, 1), 0.0)),
        per_seq(jnp.where(lo, 0.0, pltpu.roll(s1, HEAD_DIM, 1))),
        per_seq(jnp.where(lo, 0.0, s1))], axis=1).astype(BF16)
    kt = kt_ref[...].reshape(nbs, LANES, WIN).astype(BF16)
    vt = vt_ref[...].reshape(nbs, LANES, WIN).astype(BF16)
    kn = kn_ref[...].reshape(nbs, SROWS, LANES).astype(BF16)
    vn = vn_ref[...].reshape(nbs, SROWS, LANES).astype(BF16)
    s_c = jnp.einsum('nqd,ndk->nqk', qm, kt, preferred_element_type=F32) + bc_ref[...]
    s_n = jnp.einsum('nqd,nkd->nqk', qm, kn, preferred_element_type=F32) + bn_ref[...]
    hrow = lax.broadcasted_iota(jnp.int32, (nbs, 4 * SROWS, 1), 1) // SROWS
    sink = jnp.where(hrow == 0, sink_ref[l, 0], jnp.where(hrow == 1, sink_ref[l, 1],
                     jnp.where(hrow == 2, sink_ref[l, 2], sink_ref[l, 3])))
    m = jnp.maximum(jnp.max(s_c, axis=-1, keepdims=True), jnp.max(s_n, axis=-1, keepdims=True))
    m = jnp.maximum(m, sink)
    p_c = jnp.exp2(s_c - m)
    p_n = jnp.exp2(s_n - m)
    den = jnp.sum(p_c, axis=-1, keepdims=True) + jnp.sum(p_n, axis=-1, keepdims=True) + jnp.exp2(sink - m)
    o = (jnp.einsum('nqk,ndk->nqd', p_c.astype(BF16), vt, preferred_element_type=F32)
         + jnp.einsum('nqk,nkd->nqd', p_n.astype(BF16), vn, preferred_element_type=F32)) / den
    o0, o1, o2, o3 = (o[:, i * SROWS:(i + 1) * SROWS, :].reshape(rb, LANES) for i in range(4))
    o_ref[:, 0:LANES] = jnp.where(lo, o0, pltpu.roll(o1, HEAD_DIM, 1))
    o_ref[:, LANES:2 * LANES] = jnp.where(lo, pltpu.roll(o2, HEAD_DIM, 1), o3)


def _attn_a_sample(q8, k8, v8, cache_kt, cache_vt, bias_c, bias_n, sinks, l, nbs):
    ns = cache_kt.shape[1]
    tok = lambda i: (i, 0)
    cache = pl.BlockSpec((None, nbs, 2, HEAD_DIM, WIN), lambda i: (l, i, 0, 0, 0))
    c3 = lambda i: (0, 0, 0)
    return pl.pallas_call(
        functools.partial(_attn_a_sample_kernel, nbs=nbs, l=l),
        grid=(ns // nbs,),
        out_shape=jax.ShapeDtypeStruct((ns * SROWS, 256), F32),
        in_specs=[pl.BlockSpec(memory_space=pltpu.SMEM),
                  pl.BlockSpec((nbs * SROWS, 256), tok),
                  pl.BlockSpec((nbs * SROWS, LANES), tok),
                  pl.BlockSpec((nbs * SROWS, LANES), tok),
                  cache, cache,
                  pl.BlockSpec((1, 4 * SROWS, WIN), c3),
                  pl.BlockSpec((1, 4 * SROWS, SROWS), c3)],
        out_specs=pl.BlockSpec((nbs * SROWS, 256), tok),
        compiler_params=_cparams(("parallel",)),
        name="attn_a_sample",
    )(sinks, q8, k8, v8, cache_kt, cache_vt, bias_c, bias_n)


def _softmax_parts(parts):
    m = None
    for s in parts:
        mm = jnp.max(s, axis=-1, keepdims=True)
        m = mm if m is None else jnp.maximum(m, mm)
    ps = [jnp.exp2(s - m) for s in parts]
    den = None
    for p in ps:
        dd = jnp.sum(p, axis=-1, keepdims=True)
        den = dd if den is None else den + dd
    return ps, den, m + jnp.log2(den)


def _attn_c_sample_kernel(q_ref, kn_ref, vn_ref, kt_ref, vt_ref, b12_ref, b3_ref, bn_ref, o_ref, *, nbs, lc):
    lo = _lane_iota((SROWS, LANES)) < HEAD_DIM
    pr = 2 * SROWS
    units = [(n, j, slice(n * SROWS, (n + 1) * SROWS), slice(j * LANES, (j + 1) * LANES))
             for n in range(nbs) for j in range(2)]
    scores = []
    for n, j, rows, lanes in units:
        qp = q_ref[rows, lanes]
        lhs = jnp.concatenate([jnp.where(lo, qp, 0.0), jnp.where(lo, 0.0, qp)], axis=0).astype(BF16)
        ktp = kt_ref[n, 2 * j:2 * j + 2].reshape(LANES, lc).astype(BF16)
        scores.append((_dot(lhs, ktp), _dot_t(lhs, kn_ref[rows, lanes].astype(BF16))))
    probs = []
    for (n, j, rows, lanes), (s, sn) in zip(units, scores):
        s_near = s[:, lc - NEAR:]
        (p1, p1n), d1, l1 = _softmax_parts([s_near + b12_ref[0, j], sn + bn_ref[0, j]])
        (p2, p2n), d2, l2 = _softmax_parts([s_near + b12_ref[1, j], sn + bn_ref[1, j]])
        (p3, p3n), d3, l3 = _softmax_parts([s + b3_ref[j], sn + bn_ref[2, j]])
        m = jnp.maximum(jnp.maximum(l1, l2), l3)
        w1, w2, w3 = jnp.exp2(l1 - m), jnp.exp2(l2 - m), jnp.exp2(l3 - m)
        wsum = w1 + w2 + w3
        probs.append((jnp.concatenate([p1, p2], axis=0).astype(BF16), p3.astype(BF16),
                      (p1n.astype(BF16), p2n.astype(BF16), p3n.astype(BF16)),
                      (w1 / (d1 * wsum), w2 / (d2 * wsum), w3 / (d3 * wsum))))
    for (n, j, rows, lanes), (p12, p3, pn, wts) in zip(units, probs):
        vtp = vt_ref[n, 2 * j:2 * j + 2].reshape(LANES, lc).astype(BF16)
        vnp = vn_ref[rows, lanes].astype(BF16)
        o12 = _dot_t(p12, vtp[:, lc - NEAR:])
        o = ((o12[0:pr] + _dot(pn[0], vnp)) * wts[0] + (o12[pr:2 * pr] + _dot(pn[1], vnp)) * wts[1]
             + (_dot_t(p3, vtp) + _dot(pn[2], vnp)) * wts[2])
        o_ref[rows, lanes] = jnp.where(lo, o[0:SROWS], o[SROWS:pr])


def _attn_c_sample(q8, k8, v8, cache_kt, cache_vt, b12, b3, bn, l, nbs):
    ns, lc = cache_kt.shape[1], cache_kt.shape[4]
    tok = lambda i: (i, 0)
    cache = pl.BlockSpec((None, nbs, 4, HEAD_DIM, lc), lambda i: (l, i, 0, 0, 0))
    c3 = lambda i: (0, 0, 0)
    c4 = lambda i: (0, 0, 0, 0)
    return pl.pallas_call(
        functools.partial(_attn_c_sample_kernel, nbs=nbs, lc=lc),
        grid=(ns // nbs,),
        out_shape=jax.ShapeDtypeStruct((ns * SROWS, 256), F32),
        in_specs=[pl.BlockSpec((nbs * SROWS, 256), tok),
                  pl.BlockSpec((nbs * SROWS, 256), tok),
                  pl.BlockSpec((nbs * SROWS, 256), tok),
                  cache, cache,
                  pl.BlockSpec((2, 2, 2 * SROWS, NEAR), c4),
                  pl.BlockSpec((2, 2 * SROWS, lc), c3),
                  pl.BlockSpec((3, 2, 2 * SROWS, SROWS), c4)],
        out_specs=pl.BlockSpec((nbs * SROWS, 256), tok),
        compiler_params=_cparams(("parallel",)),
        name="attn_c_sample",
    )(q8, k8, v8, cache_kt, cache_vt, b12, b3, bn)


def _ssd_sample_pre_kernel(xbc_ref, prev_ref, dt_ref, cw_ref, cb_ref, dtb_ref, alog_ref, dsk_ref, ex_ref,
                           yd_ref, eac_ref, xw_ref, bm_ref, cm_ref, cd_ref):
    r = xbc_ref.shape[0]
    rid = lax.broadcasted_iota(jnp.int32, (r, 1), 0) % SROWS
    x = jnp.where(jnp.logical_and(rid >= 1, rid < SVALID), prev_ref[...], xbc_ref[...])
    y = cb_ref[...] + cw_ref[3:4, :] * x
    for d in range(1, 4):
        y = y + cw_ref[3 - d:4 - d, :] * pltpu.roll(x, d, 0)
    xa = _silu(y)
    xs = xa[:, 0:D_INNER]
    bm = xa[:, D_INNER:D_INNER + 2 * D_STATE]
    cm = xa[:, D_INNER + 2 * D_STATE:CONV_DIM]
    valid = rid >= SVALID
    dt = jnp.where(valid, _softplus(dt_ref[...] + dtb_ref[...]), 0.0)
    a = dt * (-jnp.exp(alog_ref[...]))
    acum = a
    rem = jnp.zeros_like(a)
    for d in range(1, 4):
        acum = acum + jnp.where(rid - d >= SVALID, pltpu.roll(a, d, 0), 0.0)
        rem = rem + jnp.where(rid + d < SROWS, pltpu.roll(a, r - d, 0), 0.0)
    ex = ex_ref[...]

    def expand(v):
        return _expand_heads(v, ex)

    lane = _lane_iota((r, LANES))
    y_acc = dsk_ref[...] * xs
    for d in range(4):
        ok = rid - d >= SVALID
        bsh = bm if d == 0 else pltpu.roll(bm, d, 0)
        cb0 = jnp.sum(cm[:, 0:D_STATE] * bsh[:, 0:D_STATE], axis=-1, keepdims=True)
        cb1 = jnp.sum(cm[:, D_STATE:] * bsh[:, D_STATE:], axis=-1, keepdims=True)
        cbh = jnp.where(lane < N_SSM_HEADS // 2, cb0, cb1)
        if d == 0:
            coef = cbh * dt
            xsh = xs
        else:
            dec = jnp.exp(jnp.where(ok, acum - pltpu.roll(acum, d, 0), NEG))
            coef = cbh * dec * pltpu.roll(dt, d, 0)
            xsh = pltpu.roll(xs, d, 0)
        y_acc = y_acc + expand(jnp.where(ok, coef, 0.0)) * xsh
    yd_ref[...] = y_acc
    eac_ref[...] = expand(jnp.exp(acum))
    xw_ref[...] = xs * expand(dt * jnp.exp(rem))
    bm_ref[...] = bm
    cm_ref[...] = cm
    cd_ref[...] = jnp.exp(acum + rem)


def _ssd_sample_pre(xbc8, prev8, dt8, cw8, cb, dtb, alog, dsk, ex, l, rb):
    r = xbc8.shape[0]
    row = lambda i: (i, 0)
    layer = lambda i: (l, 0, 0)
    widths = (D_INNER, D_INNER, D_INNER, 2 * D_STATE, 2 * D_STATE, LANES)
    return pl.pallas_call(
        _ssd_sample_pre_kernel,
        grid=(r // rb,),
        out_shape=[jax.ShapeDtypeStruct((r, w_), F32) for w_ in widths],
        in_specs=[pl.BlockSpec((rb, CONV_DIM), row),
                  pl.BlockSpec((None, rb, CONV_DIM), lambda i: (l, i, 0)),
                  pl.BlockSpec((rb, LANES), row),
                  pl.BlockSpec((None, SUBLANES, CONV_DIM), layer),
                  pl.BlockSpec((None, 1, CONV_DIM), layer),
                  pl.BlockSpec((None, 1, LANES), layer),
                  pl.BlockSpec((None, 1, LANES), layer),
                  pl.BlockSpec((None, 1, D_INNER), layer),
                  pl.BlockSpec((LANES, D_INNER), lambda i: (0, 0))],
        out_specs=[pl.BlockSpec((rb, w_), row) for w_ in widths],
        compiler_params=_cparams(("parallel",)),
        name="ssd_sample_pre",
    )(xbc8, prev8, dt8, cw8, cb, dtb, alog, dsk, ex)


def _ssd_sample_state_kernel(cd_ref, yd_ref, eac_ref, xw_ref, bm_ref, cm_ref, z_ref, h0_ref, ng_ref,
                             ob_ref, hn_ref, *, nbs):
    base = pl.program_id(0) * nbs
    yo = []
    for n in range(nbs):
        rows = slice(n * SROWS, (n + 1) * SROWS)
        parts = []
        for g in range(2):
            hp = h0_ref[n, 4 * g:4 * g + 4].reshape(4 * HEAD_DIM, D_STATE)
            cmg = cm_ref[rows, g * D_STATE:(g + 1) * D_STATE].astype(BF16)
            bmg = bm_ref[rows, g * D_STATE:(g + 1) * D_STATE].astype(BF16)
            parts.append(_dot_t(cmg, hp.astype(BF16)))
            s_new = _dot_t0(xw_ref[rows, g * 256:(g + 1) * 256].astype(BF16), bmg)
            for hh in range(4):
                h = 4 * g + hh
                hn_ref[n, h] = (hp[hh * HEAD_DIM:(hh + 1) * HEAD_DIM, :] * cd_ref[base + n, h]
                                + s_new[hh * HEAD_DIM:(hh + 1) * HEAD_DIM, :])
        yo.append(jnp.concatenate(parts, axis=1))
    y = yd_ref[...] + eac_ref[...] * jnp.concatenate(yo, axis=0)
    ob_ref[...] = _gated_group_norm(y, z_ref[...], ng_ref[...])


def _ssd_sample_state(cd, yd, eac, xw, bm, cm, z8, state, ng, l, depth, nbs, bufs):
    ns = state.shape[1]
    row = lambda i: (i, 0)
    st_spec = pl.BlockSpec((None, nbs, N_SSM_HEADS, HEAD_DIM, D_STATE), lambda i: (l, i, 0, 0, 0))
    rb = nbs * SROWS
    n_in = 9
    body, xspecs, xargs, k_alias = _stacked(functools.partial(_ssd_sample_state_kernel, nbs=nbs), n_in, bufs)
    return pl.pallas_call(
        body,
        grid=(ns // nbs,),
        out_shape=[jax.ShapeDtypeStruct((ns * SROWS, D_INNER), F32),
                   jax.ShapeDtypeStruct(state.shape, F32)],
        in_specs=[pl.BlockSpec(memory_space=pltpu.SMEM),
                  pl.BlockSpec((rb, D_INNER), row),
                  pl.BlockSpec((rb, D_INNER), row),
                  pl.BlockSpec((rb, D_INNER), row),
                  pl.BlockSpec((rb, 2 * D_STATE), row),
                  pl.BlockSpec((rb, 2 * D_STATE), row),
                  pl.BlockSpec((rb, D_INNER), row),
                  st_spec,
                  pl.BlockSpec((None, 1, D_INNER), lambda i: (l, 0, 0))] + xspecs,
        out_specs=[pl.BlockSpec((rb, D_INNER), row), st_spec],
        input_output_aliases=_alias_map(n_in, k_alias, 2) if k_alias else {},
        compiler_params=_cparams(("parallel",)),
        name="ssd_sample_state",
    )(cd, yd, eac, xw, bm, cm, z8, state, ng, *xargs)


def _new_kv_kernel(ka_ref, va_ref, kc_ref, vc_ref, oka_ref, ova_ref, okc_ref, ovc_ref, *, ns):
    def emit(src, dst):
        for t in range(SROWS - SVALID):
            dst[t] = src[pl.ds(SVALID + t, ns, stride=SROWS), :].T

    emit(kc_ref, okc_ref)
    emit(vc_ref, ovc_ref)

    @pl.when(pl.program_id(1) == 0)
    def _():
        emit(ka_ref, oka_ref)
        emit(va_ref, ova_ref)


def _new_kv_states(ka, va, kc, vc, ns):
    depth = ka.shape[0]
    nt = SROWS - SVALID
    narrow_in = pl.BlockSpec((None, ns * SROWS, LANES), lambda l, j: (l, 0, 0))
    wide_in = pl.BlockSpec((None, ns * SROWS, LANES), lambda l, j: (l, 0, j))
    narrow_out = pl.BlockSpec((None, nt, LANES, ns), lambda l, j: (l, 0, 0, 0))
    wide_out = pl.BlockSpec((None, nt, LANES, ns), lambda l, j: (l, 0, j, 0))
    shape = lambda w: jax.ShapeDtypeStruct((depth, nt, w, ns), F32)
    return pl.pallas_call(
        functools.partial(_new_kv_kernel, ns=ns),
        grid=(depth, 2),
        out_shape=[shape(LANES), shape(LANES), shape(2 * LANES), shape(2 * LANES)],
        in_specs=[narrow_in, narrow_in, wide_in, wide_in],
        out_specs=[narrow_out, narrow_out, wide_out, wide_out],
        compiler_params=_cparams(("parallel", "arbitrary")),
        name="new_kv_states",
    )(ka, va, kc, vc)


def _prompt_bias_item(dil, col0):
    qi = np.arange(BLOCK)[:, None]
    kj = np.arange(2 * BLOCK)[None, :]
    dist = BLOCK + qi - kj
    ok = (dist >= 0) & (dist <= WIN)
    variants = [_masked_bucket(dist * dil, ok), _masked_bucket(dist * dil, ok & (kj >= BLOCK))]
    segs = [((2 * h + v) * BLOCK, (2 * h + v + 1) * BLOCK, col0 + h) for h in range(4) for v in range(2)]
    return np.concatenate(variants * 4, axis=0), segs


def _sample_bias_item(dist_of, ok_of, cols, col0):
    t = np.arange(SROWS)[:, None] - SVALID
    bkt = np.where(t >= 0, _masked_bucket(dist_of(t, cols), ok_of(t, cols)), 0).astype(np.int32)
    segs = [(h * SROWS, (h + 1) * SROWS, col0 + h) for h in range(4)]
    return np.concatenate([bkt] * 4, axis=0), segs


def _cache_bias_item(length, window, dil, span, col0):
    dist_of = lambda t, pos: length + t - pos
    ok_of = lambda t, pos: (dist_of(t, pos) >= 0) & (dist_of(t, pos) <= window) & (dist_of(t, pos) % dil == 0)
    return _sample_bias_item(dist_of, ok_of, np.arange(length - span, length)[None, :], col0)


def _new_rows_bias_item(same_token_only, col0):
    dist_of = lambda t, r: t - (r - SVALID)
    if same_token_only:
        ok_of = lambda t, r: (r >= SVALID) & (dist_of(t, r) == 0)
    else:
        ok_of = lambda t, r: (r >= SVALID) & (dist_of(t, r) >= 0)
    return _sample_bias_item(dist_of, ok_of, np.arange(SROWS)[None, :], col0)


def kernel(x_prompt, x_sample, cache_a_k, cache_a_v, cache_c_k, cache_c_v, state_ssm, state_conv,
           norm_mix_g, w_in, a_q_norm_g, a_k_norm_g, a_sinks, c_q_norm_g, c_k_norm_g, rel_bias,
           conv_w, conv_b, dt_bias, a_log, d_skip, ssm_norm_g, w_out, norm_mlp_g, w_up, w_down):
    depth = w_in.shape[0]
    n, seq, _ = x_prompt.shape
    ns, ts, _ = x_sample.shape
    lc = cache_c_k.shape[2]
    assert ts == SROWS - SVALID and seq % (BLOCK * 16) == 0
    assert cache_a_k.shape[2] == WIN and all(w <= lc for w, _ in C_BRANCHES)
    nb = seq // BLOCK

    w_in_b = jnp.pad(w_in, ((0, 0), (0, 0), (0, COL_END - IN_COLS))).astype(BF16)
    w_out_b, w_up_b, w_down_b = w_out.astype(BF16), w_up.astype(BF16), w_down.astype(BF16)
    ones = jnp.ones((depth, 128), F32)
    q_scale = ATTN_SCALE * LOG2E
    gain = jnp.concatenate([jnp.tile(a_q_norm_g, (1, 4)) * q_scale, jnp.tile(a_k_norm_g, (1, 2)), ones,
                            jnp.tile(c_q_norm_g, (1, 4)) * q_scale, jnp.tile(c_k_norm_g, (1, 4))], axis=1)
    rel_bias = rel_bias * LOG2E
    a_sinks = a_sinks * LOG2E
    lane_grp = jnp.arange(2 * LANES) // HEAD_DIM
    bd = (lane_grp[:, None] == lane_grp[None, :]).astype(BF16)
    tri = (jnp.arange(BLOCK)[None, :] <= jnp.arange(BLOCK)[:, None]).astype(F32)
    ex = (jnp.arange(LANES)[:, None] == (jnp.arange(D_INNER) // HEAD_DIM)[None, :]).astype(BF16)
    cw8 = jnp.pad(conv_w, ((0, 0), (0, SUBLANES - conv_w.shape[1]), (0, 0)))
    pad_h = lambda v: jnp.pad(v, ((0, 0), (0, LANES - N_SSM_HEADS)))[:, None, :]
    vec = lambda v: v[:, None, :]
    dtb, alog = pad_h(dt_bias), pad_h(a_log)
    dsk = vec(jnp.repeat(d_skip, HEAD_DIM, axis=1))
    g_mix, g_mlp, gain, cb, ng = vec(norm_mix_g), vec(norm_mlp_g), vec(gain), vec(conv_b), vec(ssm_norm_g)

    (w1, d1), (w2, d2), (w3, d3) = C_BRANCHES
    items = ([_prompt_bias_item(1, 0)] + [_prompt_bias_item(d, 4) for d in C_DILS]
             + [_cache_bias_item(WIN, WIN, 1, WIN, 0), _new_rows_bias_item(False, 0)]
             + [_cache_bias_item(lc, w1, d1, NEAR, 4), _cache_bias_item(lc, w2, d2, NEAR, 4),
                _cache_bias_item(lc, w3, d3, lc, 4), _new_rows_bias_item(False, 4), _new_rows_bias_item(True, 4)])
    pa, pc1, pc4, pc16, sa_c, sa_n, sc1, sc2, sc3, sn_all, sn_same = _expand_biases(rel_bias, items)
    by_variant = lambda b: b.reshape(4, 2, BLOCK, 2 * BLOCK)
    by_pair = lambda b: b.reshape(2, 2 * SROWS, b.shape[-1])
    bias_a = by_variant(pa)
    bias_c = jnp.stack([by_variant(pc1), by_variant(pc4), by_variant(pc16)])
    sbias_a_c, sbias_a_n = sa_c[None], sa_n[None]
    sb12 = jnp.stack([by_pair(sc1), by_pair(sc2)])
    sb3 = by_pair(sc3)
    sbn = jnp.stack([by_pair(sn_all), by_pair(sn_same), by_pair(sn_same)])

    cak_t, cav_t = cache_a_k.transpose(0, 1, 3, 4, 2), cache_a_v.transpose(0, 1, 3, 4, 2)
    cck_t, ccv_t = cache_c_k.transpose(0, 1, 3, 4, 2), cache_c_v.transpose(0, 1, 3, 4, 2)

    hp = x_prompt.reshape(n * seq, D_MODEL)
    hs = jnp.pad(x_sample, ((0, 0), (SVALID, 0), (0, 0))).reshape(ns * SROWS, D_MODEL)
    tm_p = 512
    tm_s = min(512, ns * SROWS)
    nbs_a = min(16, ns)
    nbs_c = min(2, ns)
    nbs_b = min(16, ns)
    prev8 = jnp.pad(state_conv, ((0, 0), (0, 0), (1, SROWS - 4), (0, 0))).reshape(depth, ns * SROWS, CONV_DIM)

    p_ak = p_ck = p_ssm = s_ssm = None
    p_conv, s_small = [], [[] for _ in range(5)]
    for l in range(depth):
        qa, ka, va, qc, kc, vc, z, xbc, dt, tail = _in_proj(hp, g_mix, w_in_b, gain, bd, cw8, cb, l, tm_p, seq // tm_p)
        oa, *p_ak = _attn_a_prompt(qa, ka, va, bias_a, a_sinks, l, depth, n, seq, p_ak)
        oc, *p_ck = _attn_c_prompt(qc, kc, vc, bias_c, l, depth, n, seq, p_ck)
        ob, *p_ssm = _ssd_prompt(xbc, z, dt, dtb, alog, dsk, ng, tri, ex, l, depth, n, nb, p_ssm)
        hp = _out_mlp(hp, oa, ob, oc, w_out_b, g_mlp, w_up_b, w_down_b, l, tm_p)
        p_conv.append(tail)

        qa, ka, va, qc, kc, vc, z, xbc, dt, _ = _in_proj(hs, g_mix, w_in_b, gain, bd, cw8, cb, l, tm_s, 0)
        oa = _attn_a_sample(qa, ka, va, cak_t, cav_t, sbias_a_c, sbias_a_n, a_sinks, l, nbs_a)
        oc = _attn_c_sample(qc, kc, vc, cck_t, ccv_t, sb12, sb3, sbn, l, nbs_c)
        yd, eac, xw, bm, cm, cd = _ssd_sample_pre(xbc, prev8, dt, cw8, cb, dtb, alog, dsk, ex, l,
                                                  min(256, ns * SROWS))
        cd_s = cd.reshape(ns, SROWS, LANES)[:, SROWS - 1, :N_SSM_HEADS]
        ob, *s_ssm = _ssd_sample_state(cd_s, yd, eac, xw, bm, cm, z, state_ssm, ng, l, depth, nbs_b, s_ssm)
        hs = _out_mlp(hs, oa, ob, oc, w_out_b, g_mlp, w_up_b, w_down_b, l, tm_s)
        for dst, val in zip(s_small, (ka, va, kc, vc, xbc)):
            dst.append(val)

    y_prompt = hp.reshape(n, seq, D_MODEL)
    y_sample = hs.reshape(ns, SROWS, D_MODEL)[:, SVALID:]
    unminor = lambda buf, heads: buf.reshape(depth, n, heads, HEAD_DIM, -1).transpose(0, 1, 4, 2, 3)
    p_state = (unminor(p_ak[0], 2), unminor(p_ak[1], 2), unminor(p_ck[0], 4), unminor(p_ck[1], 4),
               p_ssm[0], jnp.stack(p_conv)[:, :, SUBLANES - 3:])
    new_kv = _new_kv_states(*(jnp.stack(v) for v in s_small[:4]), ns)
    s_kv = tuple(b.reshape(depth, SROWS - SVALID, hh, HEAD_DIM, ns).transpose(0, 4, 1, 2, 3)
                 for b, hh in zip(new_kv, (2, 2, 4, 4)))
    s_conv = jnp.stack([x.reshape(ns, SROWS, CONV_DIM)[:, SROWS - 3:] for x in s_small[4]])
    s_state = s_kv + (s_ssm[0], s_conv)
    return (y_prompt, y_sample) + p_state + s_state
```

```python
import collections
import functools
import math

import jax
import jax.numpy as jnp
import numpy as np
from jax import lax
from jax.experimental import pallas as pl
from jax.experimental.pallas import tpu as pltpu

F32 = jnp.float32
BF16 = jnp.bfloat16
HIGHEST = lax.Precision.HIGHEST

D_MODEL = 1024
HEAD_DIM = 64
LANES = 128
SUBLANES = 8
BLOCK = 128
WIN = 128
D_INNER = 512
D_STATE = 128
N_SSM_HEADS = 8
CONV_DIM = 1024
D_FF = 4096
NUM_BUCKETS = 32
REL_MAX_DIST = 2048
EPS = 1e-6
ATTN_SCALE = HEAD_DIM ** -0.5
LOG2E = math.log2(math.e)
NEG = -1e30
C_BRANCHES = ((128, 1), (512, 4), (2048, 16))
C_DILS = tuple(d for _, d in C_BRANCHES)
SROWS = 8
SVALID = 4
NEAR = 512
C_UNROLL = 8
A_UNROLL = 4
SSD_SUB = 4

COL_QA, COL_KA, COL_VA, COL_QC, COL_KC, COL_VC, COL_Z, COL_XBC, COL_DT, COL_END = (
    0, 256, 384, 512, 768, 1024, 1280, 1792, 2816, 2944)
IN_COLS = 2824
V7X_VMEM_BYTES = 64 * 1024 * 1024
VMEM_LIMIT = V7X_VMEM_BYTES * 7 // 8


def _cparams(sem):
    return pltpu.CompilerParams(dimension_semantics=sem, vmem_limit_bytes=VMEM_LIMIT)


def _dot(a, b):
    return jnp.dot(a, b, preferred_element_type=F32)


def _dot_t(a, b):
    return lax.dot_general(a, b, (((1,), (1,)), ((), ())), preferred_element_type=F32)


def _dot_t0(a, b):
    return lax.dot_general(a, b, (((0,), (0,)), ((), ())), preferred_element_type=F32)


def _lane_iota(shape):
    return lax.broadcasted_iota(jnp.int32, shape, len(shape) - 1)


def _silu(x):
    hx = 0.5 * x
    return hx + hx * jnp.tanh(hx)


def _softplus(x):
    return jnp.maximum(x, 0.0) + jnp.log(1.0 + jnp.exp(-jnp.abs(x)))


def _stacked(body, n_in, bufs):
    if bufs is None:
        return body, [], [], 0
    k = len(bufs)

    def wrapped(*refs):
        return body(*refs[:n_in], *refs[n_in + k:])

    return wrapped, [pl.BlockSpec(memory_space=pl.ANY)] * k, list(bufs), k


def _alias_map(n_in, k, n_out):
    return {n_in + i: n_out - k + i for i in range(k)}


def _bias_kernel(tab_ref, *refs, segments):
    n = len(segments)
    for bkt_ref, o_ref, segs in zip(refs[:n], refs[n:], segments):
        for r0, r1, col in segs:
            bkt = bkt_ref[r0:r1, :]
            acc = jnp.full(bkt.shape, NEG, F32)
            for b in range(NUM_BUCKETS):
                acc = jnp.where(bkt == b, tab_ref[b, col], acc)
            o_ref[r0:r1, :] = acc


def _expand_biases(table, items):
    bkts = [jnp.asarray(b) for b, _ in items]
    full = lambda b: pl.BlockSpec(b.shape, lambda: (0, 0))
    return pl.pallas_call(
        functools.partial(_bias_kernel, segments=[s for _, s in items]),
        out_shape=[jax.ShapeDtypeStruct(b.shape, F32) for b in bkts],
        in_specs=[pl.BlockSpec(memory_space=pltpu.SMEM)] + [full(b) for b in bkts],
        out_specs=[full(b) for b in bkts],
        name="bias_expand",
    )(table, *bkts)


def _t5_bucket(dist):
    max_exact = NUM_BUCKETS // 2
    d = np.maximum(dist, 0)
    df = np.maximum(d, 1).astype(np.float32)
    ratio = np.log(df / np.float32(max_exact)) / np.float32(math.log(REL_MAX_DIST / max_exact))
    large = max_exact + (ratio * np.float32(NUM_BUCKETS - max_exact)).astype(np.int32)
    return np.where(d < max_exact, d, np.minimum(large, NUM_BUCKETS - 1))


def _masked_bucket(dist, valid):
    return np.where(valid, _t5_bucket(dist), -1).astype(np.int32)


def _head_norm(p, gain, bd):
    sq = p * p
    hi = sq.astype(BF16)
    lo = (sq - hi.astype(F32)).astype(BF16)
    ss = _dot(hi, bd) + _dot(lo, bd)
    return p * lax.rsqrt(ss * (1.0 / HEAD_DIM) + EPS) * gain


def _in_proj_kernel(h_ref, g_ref, w_ref, gain_ref, bd_ref, cw_ref, cb_ref,
                    qa_ref, ka_ref, va_ref, qc_ref, kc_ref, vc_ref, z_ref, xbc_ref, dt_ref, tail_ref, xbuf,
                    *, conv_tiles):
    if conv_tiles:
        @pl.when(pl.program_id(0) % conv_tiles == 0)
        def _():
            xbuf[0:SUBLANES, :] = jnp.zeros((SUBLANES, CONV_DIM), F32)

    bd = bd_ref[...]
    tm = h_ref.shape[0]
    x = h_ref[...]
    ms = jnp.mean(x * x, axis=-1, keepdims=True)
    u = (x * lax.rsqrt(ms + EPS) * g_ref[...]).astype(BF16)

    def proj(c0, c1):
        return _dot(u, w_ref[:, c0:c1])

    xbc = proj(COL_XBC, COL_DT)
    tail = xbc[tm - SUBLANES:tm, :]
    tail_ref[...] = tail
    if conv_tiles:
        xbuf[SUBLANES:SUBLANES + tm, :] = xbc
        y = cb_ref[...] + cw_ref[3:4, :] * xbc
        for kk in range(3):
            y = y + cw_ref[kk:kk + 1, :] * xbuf[SUBLANES - 3 + kk:SUBLANES - 3 + kk + tm, :]
        xbuf[0:SUBLANES, :] = tail
        xbc_ref[...] = y
    else:
        xbc_ref[...] = xbc
    groups = [COL_QA, COL_KA, COL_QC, COL_KC]
    raw = [proj(c0, c0 + 256) for c0 in groups]
    va_ref[...] = raw[1][:, COL_VA - COL_KA:]
    vc_ref[...] = proj(COL_VC, COL_Z)
    z_ref[...] = proj(COL_Z, COL_XBC)
    normed = [_head_norm(p, gain_ref[:, c0:c0 + 256], bd) for c0, p in zip(groups, raw)]
    qa_ref[...] = normed[0]
    ka_ref[...] = normed[1][:, 0:COL_VA - COL_KA]
    qc_ref[...] = normed[2]
    kc_ref[...] = normed[3]
    dt_ref[...] = proj(COL_DT, COL_END)


def _in_proj(h, g, w, gain, bd, cw8, cb, l, tm, conv_tiles):
    t = h.shape[0]
    widths = (256, 128, 128, 256, 256, 256, 512, 1024, 128)
    layer = lambda i: (l, 0, 0)
    n_tail = t // tm // conv_tiles if conv_tiles else t // tm
    per_seq = conv_tiles if conv_tiles else 1
    return pl.pallas_call(
        functools.partial(_in_proj_kernel, conv_tiles=conv_tiles),
        grid=(t // tm,),
        out_shape=[jax.ShapeDtypeStruct((t, w_), F32) for w_ in widths]
        + [jax.ShapeDtypeStruct((n_tail, SUBLANES, CONV_DIM), F32)],
        in_specs=[pl.BlockSpec((tm, D_MODEL), lambda i: (i, 0)),
                  pl.BlockSpec((None, 1, D_MODEL), layer),
                  pl.BlockSpec((None, D_MODEL, COL_END), layer),
                  pl.BlockSpec((None, 1, D_MODEL), layer),
                  pl.BlockSpec((256, 256), lambda i: (0, 0)),
                  pl.BlockSpec((None, SUBLANES, CONV_DIM), layer),
                  pl.BlockSpec((None, 1, CONV_DIM), layer)],
        out_specs=[pl.BlockSpec((tm, w_), lambda i: (i, 0)) for w_ in widths]
        + [pl.BlockSpec((None, SUBLANES, CONV_DIM), lambda i: (i // per_seq, 0, 0))],
        scratch_shapes=[pltpu.VMEM((SUBLANES + tm, CONV_DIM), F32)],
        compiler_params=_cparams(("arbitrary",)),
        name="in_proj",
    )(h, g, w, gain, bd, cw8, cb)


def _out_mlp_kernel(h_ref, oa_ref, ob_ref, oc_ref, wo_ref, g_ref, wu_ref, wd_ref, o_ref):
    acc = _dot(oa_ref[...].astype(BF16), wo_ref[0:256, :])
    acc += _dot(ob_ref[...].astype(BF16), wo_ref[256:768, :])
    acc += _dot(oc_ref[...].astype(BF16), wo_ref[768:1024, :])
    h2 = h_ref[...] + acc
    ms = jnp.mean(h2 * h2, axis=-1, keepdims=True)
    u = (h2 * lax.rsqrt(ms + EPS) * g_ref[...]).astype(BF16)
    ff_chunk = 1024
    mlp = jnp.zeros_like(h2)
    for c in range(D_FF // ff_chunk):
        a = _dot(u, wu_ref[:, c * ff_chunk:(c + 1) * ff_chunk])
        a = jnp.square(jnp.maximum(a, 0.0)).astype(BF16)
        mlp += _dot(a, wd_ref[c * ff_chunk:(c + 1) * ff_chunk, :])
    o_ref[...] = h2 + mlp


def _out_mlp(h, oa, ob, oc, wo, g, wu, wd, l, tm):
    t = h.shape[0]
    layer = lambda i: (l, 0, 0)
    row = lambda i: (i, 0)
    return pl.pallas_call(
        _out_mlp_kernel,
        grid=(t // tm,),
        out_shape=jax.ShapeDtypeStruct((t, D_MODEL), F32),
        in_specs=[pl.BlockSpec((tm, D_MODEL), row),
                  pl.BlockSpec((tm, 256), row),
                  pl.BlockSpec((tm, 512), row),
                  pl.BlockSpec((tm, 256), row),
                  pl.BlockSpec((None, D_MODEL, D_MODEL), layer),
                  pl.BlockSpec((None, 1, D_MODEL), layer),
                  pl.BlockSpec((None, D_MODEL, D_FF), layer),
                  pl.BlockSpec((None, D_FF, D_MODEL), layer)],
        out_specs=pl.BlockSpec((tm, D_MODEL), row),
        compiler_params=_cparams(("parallel",)),
        name="out_mlp",
    )(h, oa, ob, oc, wo, g, wu, wd)


def _attn_a_prompt_kernel(sink_ref, q_ref, k_ref, v_ref, bias_ref, o_ref, kt_ref, vt_ref,
                          k_st, k_sw, v_st, v_sw, *, l, seq):
    lo = _lane_iota((BLOCK, LANES)) < HEAD_DIM
    k = k_ref[...]
    v = v_ref[...]
    for st in (k_st, k_sw, v_st, v_sw):
        st[0:BLOCK, :] = jnp.zeros((BLOCK, LANES), BF16)
    k_st[BLOCK:, :] = k.astype(BF16)
    v_st[BLOCK:, :] = v.astype(BF16)
    k_sw[BLOCK:, :] = pltpu.roll(k, HEAD_DIM, 1).astype(BF16)
    v_sw[BLOCK:, :] = pltpu.roll(v, HEAD_DIM, 1).astype(BF16)
    kt_ref[...] = k_ref[seq - WIN:seq, :].T
    vt_ref[...] = v_ref[seq - WIN:seq, :].T

    def body(g, carry):
        scores, where = [], []
        for u in range(A_UNROLL):
            b = g * A_UNROLL + u
            cur = pl.ds(pl.multiple_of(b * BLOCK, BLOCK), BLOCK)
            both = pl.ds(pl.multiple_of(b * BLOCK, BLOCK), 2 * BLOCK)
            variant = jnp.where(b == 0, 1, 0)
            where.append((cur, both))
            for h in range(4):
                slab = q_ref[cur, (h // 2) * LANES:(h // 2 + 1) * LANES]
                qm = jnp.where(lo if h % 2 == 0 else ~lo, slab, 0.0).astype(BF16)
                kk = k_st if h in (0, 3) else k_sw
                scores.append(_dot_t(qm, kk[both, :]) + bias_ref[h, variant])
        probs = []
        for i, s in enumerate(scores):
            sink = sink_ref[l, i % 4]
            m = jnp.maximum(jnp.max(jnp.maximum(s[:, 0:BLOCK], s[:, BLOCK:2 * BLOCK]), axis=-1, keepdims=True), sink)
            p = jnp.exp2(s - m)
            den = jnp.sum(p[:, 0:BLOCK] + p[:, BLOCK:2 * BLOCK], axis=-1, keepdims=True) + jnp.exp2(sink - m)
            probs.append((p.astype(BF16), den))
        for u, (cur, both) in enumerate(where):
            outs = []
            for h in range(4):
                p, den = probs[4 * u + h]
                vv = v_st if h in (0, 3) else v_sw
                outs.append(_dot(p, vv[both, :]) / den)
            o_ref[cur, 0:LANES] = jnp.where(lo, outs[0], outs[1])
            o_ref[cur, LANES:2 * LANES] = jnp.where(lo, outs[2], outs[3])
        return carry

    lax.fori_loop(0, seq // BLOCK // A_UNROLL, body, 0)


def _attn_a_prompt(q, k, v, bias, sinks, l, depth, n, seq, bufs):
    t = q.shape[0]
    n_in = 5
    body, xspecs, xargs, k_alias = _stacked(functools.partial(_attn_a_prompt_kernel, l=l, seq=seq), n_in, bufs)
    st_shape = jax.ShapeDtypeStruct((depth, n, LANES, WIN), F32)
    st_spec = pl.BlockSpec((None, None, LANES, WIN), lambda i: (l, i, 0, 0))
    tok = lambda w: pl.BlockSpec((seq, w), lambda i: (i, 0))
    return pl.pallas_call(
        body,
        grid=(n,),
        out_shape=[jax.ShapeDtypeStruct((t, 256), F32), st_shape, st_shape],
        in_specs=[pl.BlockSpec(memory_space=pltpu.SMEM), tok(256), tok(LANES), tok(LANES),
                  pl.BlockSpec((4, 2, BLOCK, 2 * BLOCK), lambda i: (0, 0, 0, 0))] + xspecs,
        out_specs=[tok(256), st_spec, st_spec],
        scratch_shapes=[pltpu.VMEM((BLOCK + seq, LANES), BF16)] * 4,
        input_output_aliases=_alias_map(n_in, k_alias, 3) if k_alias else {},
        compiler_params=_cparams(("parallel",)),
        name="attn_a_prompt",
    )(sinks, q, k, v, bias, *xargs)


def _attn_c_prompt_kernel(q_ref, k_ref, v_ref, bias_ref, o_ref, kt_ref, vt_ref, ob_scr, lse_scr, *, seq):
    lane = _lane_iota((BLOCK, LANES))
    lo = lane < HEAD_DIM
    kt_ref[...] = k_ref[...].T
    vt_ref[...] = v_ref[...].T

    def blocks(br, dil, where, has_prev):
        rows = lambda ref, s: ref[pl.ds(s, BLOCK, stride=dil), :]
        scores, vals = [], []
        for qs, ps, first in where:
            q = rows(q_ref, qs)
            if has_prev:
                k2 = jnp.concatenate([rows(k_ref, ps), rows(k_ref, qs)], axis=0).astype(BF16)
                vals.append(jnp.concatenate([rows(v_ref, ps), rows(v_ref, qs)], axis=0).astype(BF16))
                variant = jnp.where(first, 1, 0)
            else:
                k2 = rows(k_ref, qs).astype(BF16)
                vals.append(rows(v_ref, qs).astype(BF16))
            for hh in range(2):
                qm = jnp.where(lo if hh == 0 else ~lo, q, 0.0).astype(BF16)
                bias = bias_ref[br, hh, variant] if has_prev else bias_ref[br, hh, 0, :, BLOCK:2 * BLOCK]
                scores.append(_dot_t(qm, k2) + bias)
        probs = []
        for s in scores:
            sm = jnp.maximum(s[:, 0:BLOCK], s[:, BLOCK:2 * BLOCK]) if has_prev else s
            m = jnp.max(sm, axis=-1, keepdims=True)
            p = jnp.exp2(s - m)
            pm = p[:, 0:BLOCK] + p[:, BLOCK:2 * BLOCK] if has_prev else p
            probs.append((p.astype(BF16), m, jnp.sum(pm, axis=-1, keepdims=True)))
        for u, (qs, _, _) in enumerate(where):
            outs, lses = [], []
            for hh in range(2):
                p, m, den = probs[2 * u + hh]
                outs.append(_dot(p, vals[u]) / den)
                lses.append(jnp.broadcast_to(m + jnp.log2(den), (BLOCK, LANES)))
            ob_scr[br, pl.ds(qs, BLOCK, stride=dil), :] = jnp.where(lo, outs[0], outs[1])
            lse_scr[br, pl.ds(qs, BLOCK, stride=dil), :] = jnp.where(lo, lses[0], lses[1])

    for br, dil in enumerate(C_DILS):
        nb = seq // dil // BLOCK
        span = BLOCK * dil

        def body(g, carry, br=br, dil=dil, nb=nb, span=span):
            where = []
            for u in range(C_UNROLL):
                i = g * C_UNROLL + u
                res = i % dil
                b = i // dil
                where.append((res + b * span, res + jnp.maximum(b - 1, 0) * span, b == 0))
            blocks(br, dil, where, nb > 1)
            return carry

        lax.fori_loop(0, dil * nb // C_UNROLL, body, 0)

    def merge(i, carry):
        rows = pl.ds(pl.multiple_of(i * BLOCK, BLOCK), BLOCK)
        l0, l1, l2 = lse_scr[0, rows, :], lse_scr[1, rows, :], lse_scr[2, rows, :]
        m = jnp.maximum(jnp.maximum(l0, l1), l2)
        w0, w1, w2 = jnp.exp2(l0 - m), jnp.exp2(l1 - m), jnp.exp2(l2 - m)
        num = w0 * ob_scr[0, rows, :] + w1 * ob_scr[1, rows, :] + w2 * ob_scr[2, rows, :]
        o_ref[rows, :] = num / (w0 + w1 + w2)
        return carry

    lax.fori_loop(0, seq // BLOCK, merge, 0)


def _attn_c_prompt(q, k, v, bias, l, depth, n, seq, bufs):
    t = q.shape[0]
    blk = pl.BlockSpec((seq, LANES), lambda i, hp: (i, hp))
    n_in = 4
    body, xspecs, xargs, k_alias = _stacked(functools.partial(_attn_c_prompt_kernel, seq=seq), n_in, bufs)
    st_shape = jax.ShapeDtypeStruct((depth, n, 256, seq), F32)
    st_spec = pl.BlockSpec((None, None, LANES, seq), lambda i, hp: (l, i, hp, 0))
    return pl.pallas_call(
        body,
        grid=(n, 2),
        out_shape=[jax.ShapeDtypeStruct((t, 256), F32), st_shape, st_shape],
        in_specs=[blk, blk, blk,
                  pl.BlockSpec((3, 2, 2, BLOCK, 2 * BLOCK), lambda i, hp: (0, hp, 0, 0, 0))] + xspecs,
        out_specs=[blk, st_spec, st_spec],
        scratch_shapes=[pltpu.VMEM((3, seq, LANES), F32), pltpu.VMEM((3, seq, LANES), F32)],
        input_output_aliases=_alias_map(n_in, k_alias, 3) if k_alias else {},
        compiler_params=_cparams(("parallel", "parallel")),
        name="attn_c_prompt",
    )(q, k, v, bias, *xargs)


def _gated_group_norm(y, z, ng):
    gated = y * _silu(z)
    parts = []
    for grp in range(2):
        gsl = gated[:, grp * 256:(grp + 1) * 256]
        ms = jnp.mean(gsl * gsl, axis=-1, keepdims=True)
        parts.append(gsl * lax.rsqrt(ms + EPS))
    return jnp.concatenate(parts, axis=1) * ng


def _expand_heads(v, ex):
    hi = v.astype(BF16)
    r1 = v - hi.astype(F32)
    mid = r1.astype(BF16)
    lo = (r1 - mid.astype(F32)).astype(BF16)
    return _dot(hi, ex) + _dot(mid, ex) + _dot(lo, ex)


def _ssd_prompt_kernel(xbc_ref, z_ref, dt_ref, dtb_ref, alog_ref, dsk_ref, ng_ref, tri_ref, ex_ref,
                       ob_ref, hl_ref, hst):
    c = pl.program_id(1)

    @pl.when(c == 0)
    def _():
        hst[...] = jnp.zeros_like(hst)

    xa_all = _silu(xbc_ref[...])
    dt_all = _softplus(dt_ref[...] + dtb_ref[...])
    a_all = dt_all * (-jnp.exp(alog_ref[...]))
    row = lax.broadcasted_iota(jnp.int32, (BLOCK, BLOCK), 0)
    lane = _lane_iota((BLOCK, BLOCK))
    causal = lane <= row
    lo = lane < HEAD_DIM
    top = row < HEAD_DIM
    n_pairs = N_SSM_HEADS // 2
    for sub in range(SSD_SUB):
        rs = slice(sub * BLOCK, (sub + 1) * BLOCK)
        xa, dt = xa_all[rs], dt_all[rs]
        xs = xa[:, 0:D_INNER]
        acum = jnp.dot(tri_ref[...], a_all[rs], precision=HIGHEST, preferred_element_type=F32)
        bms = [xa[:, D_INNER + g * D_STATE:D_INNER + (g + 1) * D_STATE].astype(BF16) for g in range(2)]
        cms = [xa[:, D_INNER + (2 + g) * D_STATE:D_INNER + (3 + g) * D_STATE].astype(BF16) for g in range(2)]
        cbs = [_dot_t(cms[g], bms[g]) for g in range(2)]
        hps = [hst[2 * j:2 * j + 2].reshape(BLOCK, D_STATE) for j in range(n_pairs)]
        y_off = [_dot_t(cms[j // 2], hps[j].astype(BF16)) for j in range(n_pairs)]
        last = acum[BLOCK - 1:BLOCK, :]
        e_last = jnp.exp(last)
        w_all = _expand_heads(jnp.exp(last - acum) * dt, ex_ref[...])
        e_all = _expand_heads(jnp.exp(acum), ex_ref[...])
        xs_pairs = [xs[:, j * LANES:(j + 1) * LANES] for j in range(n_pairs)]
        for j in range(n_pairs):
            h0_, h1_ = 2 * j, 2 * j + 1
            xw = xs_pairs[j] * w_all[:, j * LANES:(j + 1) * LANES]
            s_new = _dot_t0(xw.astype(BF16), bms[j // 2])
            cd_pair = jnp.where(top, e_last[:, h0_:h0_ + 1], e_last[:, h1_:h1_ + 1])
            hst[h0_:h0_ + 2] = (hps[j] * cd_pair + s_new).reshape(2, HEAD_DIM, D_STATE)
        acum_t = acum.T
        dt_t = dt.T
        y_parts = []
        for j in range(n_pairs):
            xs_b = xs_pairs[j].astype(BF16)
            yd = []
            for hh in range(2):
                h = 2 * j + hh
                seg = acum[:, h:h + 1] - acum_t[h:h + 1, :]
                dec = jnp.exp(jnp.where(causal, seg, NEG)) * dt_t[h:h + 1, :]
                yd.append(_dot((cbs[j // 2] * dec).astype(BF16), xs_b))
            y_parts.append(jnp.where(lo, yd[0], yd[1]) + y_off[j] * e_all[:, j * LANES:(j + 1) * LANES])
        y = jnp.concatenate(y_parts, axis=1) + dsk_ref[...] * xs
        ob_ref[rs, :] = _gated_group_norm(y, z_ref[rs, :], ng_ref[...])

    @pl.when(c == pl.num_programs(1) - 1)
    def _():
        hl_ref[...] = hst[...]


def _ssd_prompt(xbc, z, dt, dtb, alog, dsk, ng, tri, ex, l, depth, n, nc, bufs):
    t = xbc.shape[0]
    nc = nc // SSD_SUB
    rows = SSD_SUB * BLOCK
    row = lambda i, c: (i * nc + c, 0)
    layer = lambda i, c: (l, 0, 0)
    n_in = 9
    body, xspecs, xargs, k_alias = _stacked(_ssd_prompt_kernel, n_in, bufs)
    st = (N_SSM_HEADS, HEAD_DIM, D_STATE)
    return pl.pallas_call(
        body,
        grid=(n, nc),
        out_shape=[jax.ShapeDtypeStruct((t, D_INNER), F32),
                   jax.ShapeDtypeStruct((depth, n) + st, F32)],
        in_specs=[pl.BlockSpec((rows, CONV_DIM), row),
                  pl.BlockSpec((rows, D_INNER), row),
                  pl.BlockSpec((rows, LANES), row),
                  pl.BlockSpec((None, 1, LANES), layer),
                  pl.BlockSpec((None, 1, LANES), layer),
                  pl.BlockSpec((None, 1, D_INNER), layer),
                  pl.BlockSpec((None, 1, D_INNER), layer),
                  pl.BlockSpec((BLOCK, BLOCK), lambda i, c: (0, 0)),
                  pl.BlockSpec((LANES, D_INNER), lambda i, c: (0, 0))] + xspecs,
        out_specs=[pl.BlockSpec((rows, D_INNER), row),
                   pl.BlockSpec((None, None) + st, lambda i, c: (l, i, 0, 0, 0))],
        scratch_shapes=[pltpu.VMEM(st, F32)],
        input_output_aliases=_alias_map(n_in, k_alias, 2) if k_alias else {},
        compiler_params=_cparams(("parallel", "arbitrary")),
        name="ssd_prompt",
    )(xbc, z, dt, dtb, alog, dsk, ng, tri, ex, *xargs)


def _attn_a_sample_kernel(sink_ref, q_ref, kn_ref, vn_ref, kt_ref, vt_ref, bc_ref, bn_ref, o_ref, *, nbs, l):
    rb = nbs * SROWS
    lo = _lane_iota((rb, LANES)) < HEAD_DIM
    s0, s1 = q_ref[:, 0:LANES], q_ref[:, LANES:2 * LANES]
    per_seq = lambda v: v.reshape(nbs, SROWS, LANES)
    qm = jnp.concatenate([
        per_seq(jnp.where(lo, s0, 0.0)),
        per_seq(jnp.where(lo, pltpu.roll(s0, HEAD_DIM, 1), 0.0)),
        per_seq(jnp.where(lo, 0.0, pltpu.roll(s1, HEAD_DIM, 1))),
        per_seq(jnp.where(lo, 0.0, s1))], axis=1).astype(BF16)
    kt = kt_ref[...].reshape(nbs, LANES, WIN).astype(BF16)
    vt = vt_ref[...].reshape(nbs, LANES, WIN).astype(BF16)
    kn = kn_ref[...].reshape(nbs, SROWS, LANES).astype(BF16)
    vn = vn_ref[...].reshape(nbs, SROWS, LANES).astype(BF16)
    s_c = jnp.einsum('nqd,ndk->nqk', qm, kt, preferred_element_type=F32) + bc_ref[...]
    s_n = jnp.einsum('nqd,nkd->nqk', qm, kn, preferred_element_type=F32) + bn_ref[...]
    hrow = lax.broadcasted_iota(jnp.int32, (nbs, 4 * SROWS, 1), 1) // SROWS
    sink = jnp.where(hrow == 0, sink_ref[l, 0], jnp.where(hrow == 1, sink_ref[l, 1],
                     jnp.where(hrow == 2, sink_ref[l, 2], sink_ref[l, 3])))
    m = jnp.maximum(jnp.max(s_c, axis=-1, keepdims=True), jnp.max(s_n, axis=-1, keepdims=True))
    m = jnp.maximum(m, sink)
    p_c = jnp.exp2(s_c - m)
    p_n = jnp.exp2(s_n - m)
    den = jnp.sum(p_c, axis=-1, keepdims=True) + jnp.sum(p_n, axis=-1, keepdims=True) + jnp.exp2(sink - m)
    o = (jnp.einsum('nqk,ndk->nqd', p_c.astype(BF16), vt, preferred_element_type=F32)
         + jnp.einsum('nqk,nkd->nqd', p_n.astype(BF16), vn, preferred_element_type=F32)) / den
    o0, o1, o2, o3 = (o[:, i * SROWS:(i + 1) * SROWS, :].reshape(rb, LANES) for i in range(4))
    o_ref[:, 0:LANES] = jnp.where(lo, o0, pltpu.roll(o1, HEAD_DIM, 1))
    o_ref[:, LANES:2 * LANES] = jnp.where(lo, pltpu.roll(o2, HEAD_DIM, 1), o3)


def _attn_a_sample(q8, k8, v8, cache_kt, cache_vt, bias_c, bias_n, sinks, l, nbs):
    ns = cache_kt.shape[1]
    tok = lambda i: (i, 0)
    cache = pl.BlockSpec((None, nbs, 2, HEAD_DIM, WIN), lambda i: (l, i, 0, 0, 0))
    c3 = lambda i: (0, 0, 0)
    return pl.pallas_call(
        functools.partial(_attn_a_sample_kernel, nbs=nbs, l=l),
        grid=(ns // nbs,),
        out_shape=jax.ShapeDtypeStruct((ns * SROWS, 256), F32),
        in_specs=[pl.BlockSpec(memory_space=pltpu.SMEM),
                  pl.BlockSpec((nbs * SROWS, 256), tok),
                  pl.BlockSpec((nbs * SROWS, LANES), tok),
                  pl.BlockSpec((nbs * SROWS, LANES), tok),
                  cache, cache,
                  pl.BlockSpec((1, 4 * SROWS, WIN), c3),
                  pl.BlockSpec((1, 4 * SROWS, SROWS), c3)],
        out_specs=pl.BlockSpec((nbs * SROWS, 256), tok),
        compiler_params=_cparams(("parallel",)),
        name="attn_a_sample",
    )(sinks, q8, k8, v8, cache_kt, cache_vt, bias_c, bias_n)


def _softmax_parts(parts):
    m = None
    for s in parts:
        mm = jnp.max(s, axis=-1, keepdims=True)
        m = mm if m is None else jnp.maximum(m, mm)
    ps = [jnp.exp2(s - m) for s in parts]
    den = None
    for p in ps:
        dd = jnp.sum(p, axis=-1, keepdims=True)
        den = dd if den is None else den + dd
    return ps, den, m + jnp.log2(den)


def _attn_c_sample_kernel(q_ref, kn_ref, vn_ref, kt_ref, vt_ref, b12_ref, b3_ref, bn_ref, o_ref, *, nbs, lc):
    lo = _lane_iota((SROWS, LANES)) < HEAD_DIM
    pr = 2 * SROWS
    units = [(n, j, slice(n * SROWS, (n + 1) * SROWS), slice(j * LANES, (j + 1) * LANES))
             for n in range(nbs) for j in range(2)]
    scores = []
    for n, j, rows, lanes in units:
        qp = q_ref[rows, lanes]
        lhs = jnp.concatenate([jnp.where(lo, qp, 0.0), jnp.where(lo, 0.0, qp)], axis=0).astype(BF16)
        ktp = kt_ref[n, 2 * j:2 * j + 2].reshape(LANES, lc).astype(BF16)
        scores.append((_dot(lhs, ktp), _dot_t(lhs, kn_ref[rows, lanes].astype(BF16))))
    probs = []
    for (n, j, rows, lanes), (s, sn) in zip(units, scores):
        s_near = s[:, lc - NEAR:]
        (p1, p1n), d1, l1 = _softmax_parts([s_near + b12_ref[0, j], sn + bn_ref[0, j]])
        (p2, p2n), d2, l2 = _softmax_parts([s_near + b12_ref[1, j], sn + bn_ref[1, j]])
        (p3, p3n), d3, l3 = _softmax_parts([s + b3_ref[j], sn + bn_ref[2, j]])
        m = jnp.maximum(jnp.maximum(l1, l2), l3)
        w1, w2, w3 = jnp.exp2(l1 - m), jnp.exp2(l2 - m), jnp.exp2(l3 - m)
        wsum = w1 + w2 + w3
        probs.append((jnp.concatenate([p1, p2], axis=0).astype(BF16), p3.astype(BF16),
                      (p1n.astype(BF16), p2n.astype(BF16), p3n.astype(BF16)),
                      (w1 / (d1 * wsum), w2 / (d2 * wsum), w3 / (d3 * wsum))))
    for (n, j, rows, lanes), (p12, p3, pn, wts) in zip(units, probs):
        vtp = vt_ref[n, 2 * j:2 * j + 2].reshape(LANES, lc).astype(BF16)
        vnp = vn_ref[rows, lanes].astype(BF16)
        o12 = _dot_t(p12, vtp[:, lc - NEAR:])
        o = ((o12[0:pr] + _dot(pn[0], vnp)) * wts[0] + (o12[pr:2 * pr] + _dot(pn[1], vnp)) * wts[1]
             + (_dot_t(p3, vtp) + _dot(pn[2], vnp)) * wts[2])
        o_ref[rows, lanes] = jnp.where(lo, o[0:SROWS], o[SROWS:pr])


def _attn_c_sample(q8, k8, v8, cache_kt, cache_vt, b12, b3, bn, l, nbs):
    ns, lc = cache_kt.shape[1], cache_kt.shape[4]
    tok = lambda i: (i, 0)
    cache = pl.BlockSpec((None, nbs, 4, HEAD_DIM, lc), lambda i: (l, i, 0, 0, 0))
    c3 = lambda i: (0, 0, 0)
    c4 = lambda i: (0, 0, 0, 0)
    return pl.pallas_call(
        functools.partial(_attn_c_sample_kernel, nbs=nbs, lc=lc),
        grid=(ns // nbs,),
        out_shape=jax.ShapeDtypeStruct((ns * SROWS, 256), F32),
        in_specs=[pl.BlockSpec((nbs * SROWS, 256), tok),
                  pl.BlockSpec((nbs * SROWS, 256), tok),
                  pl.BlockSpec((nbs * SROWS, 256), tok),
                  cache, cache,
                  pl.BlockSpec((2, 2, 2 * SROWS, NEAR), c4),
                  pl.BlockSpec((2, 2 * SROWS, lc), c3),
                  pl.BlockSpec((3, 2, 2 * SROWS, SROWS), c4)],
        out_specs=pl.BlockSpec((nbs * SROWS, 256), tok),
        compiler_params=_cparams(("parallel",)),
        name="attn_c_sample",
    )(q8, k8, v8, cache_kt, cache_vt, b12, b3, bn)


def _ssd_sample_pre_kernel(xbc_ref, prev_ref, dt_ref, cw_ref, cb_ref, dtb_ref, alog_ref, dsk_ref, ex_ref,
                           yd_ref, eac_ref, xw_ref, bm_ref, cm_ref, cd_ref):
    r = xbc_ref.shape[0]
    rid = lax.broadcasted_iota(jnp.int32, (r, 1), 0) % SROWS
    x = jnp.where(jnp.logical_and(rid >= 1, rid < SVALID), prev_ref[...], xbc_ref[...])
    y = cb_ref[...] + cw_ref[3:4, :] * x
    for d in range(1, 4):
        y = y + cw_ref[3 - d:4 - d, :] * pltpu.roll(x, d, 0)
    xa = _silu(y)
    xs = xa[:, 0:D_INNER]
    bm = xa[:, D_INNER:D_INNER + 2 * D_STATE]
    cm = xa[:, D_INNER + 2 * D_STATE:CONV_DIM]
    valid = rid >= SVALID
    dt = jnp.where(valid, _softplus(dt_ref[...] + dtb_ref[...]), 0.0)
    a = dt * (-jnp.exp(alog_ref[...]))
    acum = a
    rem = jnp.zeros_like(a)
    for d in range(1, 4):
        acum = acum + jnp.where(rid - d >= SVALID, pltpu.roll(a, d, 0), 0.0)
        rem = rem + jnp.where(rid + d < SROWS, pltpu.roll(a, r - d, 0), 0.0)
    ex = ex_ref[...]

    def expand(v):
        return _expand_heads(v, ex)

    lane = _lane_iota((r, LANES))
    y_acc = dsk_ref[...] * xs
    for d in range(4):
        ok = rid - d >= SVALID
        bsh = bm if d == 0 else pltpu.roll(bm, d, 0)
        cb0 = jnp.sum(cm[:, 0:D_STATE] * bsh[:, 0:D_STATE], axis=-1, keepdims=True)
        cb1 = jnp.sum(cm[:, D_STATE:] * bsh[:, D_STATE:], axis=-1, keepdims=True)
        cbh = jnp.where(lane < N_SSM_HEADS // 2, cb0, cb1)
        if d == 0:
            coef = cbh * dt
            xsh = xs
        else:
            dec = jnp.exp(jnp.where(ok, acum - pltpu.roll(acum, d, 0), NEG))
            coef = cbh * dec * pltpu.roll(dt, d, 0)
            xsh = pltpu.roll(xs, d, 0)
        y_acc = y_acc + expand(jnp.where(ok, coef, 0.0)) * xsh
    yd_ref[...] = y_acc
    eac_ref[...] = expand(jnp.exp(acum))
    xw_ref[...] = xs * expand(dt * jnp.exp(rem))
    bm_ref[...] = bm
    cm_ref[...] = cm
    cd_ref[...] = jnp.exp(acum + rem)


def _ssd_sample_pre(xbc8, prev8, dt8, cw8, cb, dtb, alog, dsk, ex, l, rb):
    r = xbc8.shape[0]
    row = lambda i: (i, 0)
    layer = lambda i: (l, 0, 0)
    widths = (D_INNER, D_INNER, D_INNER, 2 * D_STATE, 2 * D_STATE, LANES)
    return pl.pallas_call(
        _ssd_sample_pre_kernel,
        grid=(r // rb,),
        out_shape=[jax.ShapeDtypeStruct((r, w_), F32) for w_ in widths],
        in_specs=[pl.BlockSpec((rb, CONV_DIM), row),
                  pl.BlockSpec((None, rb, CONV_DIM), lambda i: (l, i, 0)),
                  pl.BlockSpec((rb, LANES), row),
                  pl.BlockSpec((None, SUBLANES, CONV_DIM), layer),
                  pl.BlockSpec((None, 1, CONV_DIM), layer),
                  pl.BlockSpec((None, 1, LANES), layer),
                  pl.BlockSpec((None, 1, LANES), layer),
                  pl.BlockSpec((None, 1, D_INNER), layer),
                  pl.BlockSpec((LANES, D_INNER), lambda i: (0, 0))],
        out_specs=[pl.BlockSpec((rb, w_), row) for w_ in widths],
        compiler_params=_cparams(("parallel",)),
        name="ssd_sample_pre",
    )(xbc8, prev8, dt8, cw8, cb, dtb, alog, dsk, ex)


def _ssd_sample_state_kernel(cd_ref, yd_ref, eac_ref, xw_ref, bm_ref, cm_ref, z_ref, h0_ref, ng_ref,
                             ob_ref, hn_ref, *, nbs):
    base = pl.program_id(0) * nbs
    yo = []
    for n in range(nbs):
        rows = slice(n * SROWS, (n + 1) * SROWS)
        parts = []
        for g in range(2):
            hp = h0_ref[n, 4 * g:4 * g + 4].reshape(4 * HEAD_DIM, D_STATE)
            cmg = cm_ref[rows, g * D_STATE:(g + 1) * D_STATE].astype(BF16)
            bmg = bm_ref[rows, g * D_STATE:(g + 1) * D_STATE].astype(BF16)
            parts.append(_dot_t(cmg, hp.astype(BF16)))
            s_new = _dot_t0(xw_ref[rows, g * 256:(g + 1) * 256].astype(BF16), bmg)
            for hh in range(4):
                h = 4 * g + hh
                hn_ref[n, h] = (hp[hh * HEAD_DIM:(hh + 1) * HEAD_DIM, :] * cd_ref[base + n, h]
                                + s_new[hh * HEAD_DIM:(hh + 1) * HEAD_DIM, :])
        yo.append(jnp.concatenate(parts, axis=1))
    y = yd_ref[...] + eac_ref[...] * jnp.concatenate(yo, axis=0)
    ob_ref[...] = _gated_group_norm(y, z_ref[...], ng_ref[...])


def _ssd_sample_state(cd, yd, eac, xw, bm, cm, z8, state, ng, l, depth, nbs, bufs):
    ns = state.shape[1]
    row = lambda i: (i, 0)
    st_spec = pl.BlockSpec((None, nbs, N_SSM_HEADS, HEAD_DIM, D_STATE), lambda i: (l, i, 0, 0, 0))
    rb = nbs * SROWS
    n_in = 9
    body, xspecs, xargs, k_alias = _stacked(functools.partial(_ssd_sample_state_kernel, nbs=nbs), n_in, bufs)
    return pl.pallas_call(
        body,
        grid=(ns // nbs,),
        out_shape=[jax.ShapeDtypeStruct((ns * SROWS, D_INNER), F32),
                   jax.ShapeDtypeStruct(state.shape, F32)],
        in_specs=[pl.BlockSpec(memory_space=pltpu.SMEM),
                  pl.BlockSpec((rb, D_INNER), row),
                  pl.BlockSpec((rb, D_INNER), row),
                  pl.BlockSpec((rb, D_INNER), row),
                  pl.BlockSpec((rb, 2 * D_STATE), row),
                  pl.BlockSpec((rb, 2 * D_STATE), row),
                  pl.BlockSpec((rb, D_INNER), row),
                  st_spec,
                  pl.BlockSpec((None, 1, D_INNER), lambda i: (l, 0, 0))] + xspecs,
        out_specs=[pl.BlockSpec((rb, D_INNER), row), st_spec],
        input_output_aliases=_alias_map(n_in, k_alias, 2) if k_alias else {},
        compiler_params=_cparams(("parallel",)),
        name="ssd_sample_state",
    )(cd, yd, eac, xw, bm, cm, z8, state, ng, *xargs)


def _new_kv_kernel(ka_ref, va_ref, kc_ref, vc_ref, oka_ref, ova_ref, okc_ref, ovc_ref, *, ns):
    def emit(src, dst):
        for t in range(SROWS - SVALID):
            dst[t] = src[pl.ds(SVALID + t, ns, stride=SROWS), :].T

    emit(kc_ref, okc_ref)
    emit(vc_ref, ovc_ref)

    @pl.when(pl.program_id(1) == 0)
    def _():
        emit(ka_ref, oka_ref)
        emit(va_ref, ova_ref)


def _new_kv_states(ka, va, kc, vc, ns):
    depth = ka.shape[0]
    nt = SROWS - SVALID
    narrow_in = pl.BlockSpec((None, ns * SROWS, LANES), lambda l, j: (l, 0, 0))
    wide_in = pl.BlockSpec((None, ns * SROWS, LANES), lambda l, j: (l, 0, j))
    narrow_out = pl.BlockSpec((None, nt, LANES, ns), lambda l, j: (l, 0, 0, 0))
    wide_out = pl.BlockSpec((None, nt, LANES, ns), lambda l, j: (l, 0, j, 0))
    shape = lambda w: jax.ShapeDtypeStruct((depth, nt, w, ns), F32)
    return pl.pallas_call(
        functools.partial(_new_kv_kernel, ns=ns),
        grid=(depth, 2),
        out_shape=[shape(LANES), shape(LANES), shape(2 * LANES), shape(2 * LANES)],
        in_specs=[narrow_in, narrow_in, wide_in, wide_in],
        out_specs=[narrow_out, narrow_out, wide_out, wide_out],
        compiler_params=_cparams(("parallel", "arbitrary")),
        name="new_kv_states",
    )(ka, va, kc, vc)


def _prompt_bias_item(dil, col0):
    qi = np.arange(BLOCK)[:, None]
    kj = np.arange(2 * BLOCK)[None, :]
    dist = BLOCK + qi - kj
    ok = (dist >= 0) & (dist <= WIN)
    variants = [_masked_bucket(dist * dil, ok), _masked_bucket(dist * dil, ok & (kj >= BLOCK))]
    segs = [((2 * h + v) * BLOCK, (2 * h + v + 1) * BLOCK, col0 + h) for h in range(4) for v in range(2)]
    return np.concatenate(variants * 4, axis=0), segs


def _sample_bias_item(dist_of, ok_of, cols, col0):
    t = np.arange(SROWS)[:, None] - SVALID
    bkt = np.where(t >= 0, _masked_bucket(dist_of(t, cols), ok_of(t, cols)), 0).astype(np.int32)
    segs = [(h * SROWS, (h + 1) * SROWS, col0 + h) for h in range(4)]
    return np.concatenate([bkt] * 4, axis=0), segs


def _cache_bias_item(length, window, dil, span, col0):
    dist_of = lambda t, pos: length + t - pos
    ok_of = lambda t, pos: (dist_of(t, pos) >= 0) & (dist_of(t, pos) <= window) & (dist_of(t, pos) % dil == 0)
    return _sample_bias_item(dist_of, ok_of, np.arange(length - span, length)[None, :], col0)


def _new_rows_bias_item(same_token_only, col0):
    dist_of = lambda t, r: t - (r - SVALID)
    if same_token_only:
        ok_of = lambda t, r: (r >= SVALID) & (dist_of(t, r) == 0)
    else:
        ok_of = lambda t, r: (r >= SVALID) & (dist_of(t, r) >= 0)
    return _sample_bias_item(dist_of, ok_of, np.arange(SROWS)[None, :], col0)


Tiling = collections.namedtuple("Tiling", "tm_prompt tm_sample seqs_a seqs_c seqs_b rows_b")


def _tiling(ns):
    rows = ns * SROWS
    return Tiling(tm_prompt=512, tm_sample=min(512, rows), seqs_a=min(16, ns), seqs_c=min(2, ns),
                  seqs_b=min(16, ns), rows_b=min(512, rows))
def kernel(x_prompt, x_sample, cache_a_k, cache_a_v, cache_c_k, cache_c_v, state_ssm, state_conv,
           norm_mix_g, w_in, a_q_norm_g, a_k_norm_g, a_sinks, c_q_norm_g, c_k_norm_g, rel_bias,
           conv_w, conv_b, dt_bias, a_log, d_skip, ssm_norm_g, w_out, norm_mlp_g, w_up, w_down):
    depth = w_in.shape[0]
    n, seq, _ = x_prompt.shape
    ns, ts, _ = x_sample.shape
    lc = cache_c_k.shape[2]
    assert ts == SROWS - SVALID and seq % (BLOCK * 16) == 0
    assert cache_a_k.shape[2] == WIN and all(w <= lc for w, _ in C_BRANCHES)
    nb = seq // BLOCK

    w_in_b = jnp.pad(w_in, ((0, 0), (0, 0), (0, COL_END - IN_COLS))).astype(BF16)
    w_out_b, w_up_b, w_down_b = w_out.astype(BF16), w_up.astype(BF16), w_down.astype(BF16)
    ones = jnp.ones((depth, 128), F32)
    q_scale = ATTN_SCALE * LOG2E
    gain = jnp.concatenate([jnp.tile(a_q_norm_g, (1, 4)) * q_scale, jnp.tile(a_k_norm_g, (1, 2)), ones,
                            jnp.tile(c_q_norm_g, (1, 4)) * q_scale, jnp.tile(c_k_norm_g, (1, 4))], axis=1)
    rel_bias = rel_bias * LOG2E
    a_sinks = a_sinks * LOG2E
    lane_grp = jnp.arange(2 * LANES) // HEAD_DIM
    bd = (lane_grp[:, None] == lane_grp[None, :]).astype(BF16)
    tri = (jnp.arange(BLOCK)[None, :] <= jnp.arange(BLOCK)[:, None]).astype(F32)
    ex = (jnp.arange(LANES)[:, None] == (jnp.arange(D_INNER) // HEAD_DIM)[None, :]).astype(BF16)
    cw8 = jnp.pad(conv_w, ((0, 0), (0, SUBLANES - conv_w.shape[1]), (0, 0)))
    pad_h = lambda v: jnp.pad(v, ((0, 0), (0, LANES - N_SSM_HEADS)))[:, None, :]
    vec = lambda v: v[:, None, :]
    dtb, alog = pad_h(dt_bias), pad_h(a_log)
    dsk = vec(jnp.repeat(d_skip, HEAD_DIM, axis=1))
    g_mix, g_mlp, gain, cb, ng = vec(norm_mix_g), vec(norm_mlp_g), vec(gain), vec(conv_b), vec(ssm_norm_g)

    (w1, d1), (w2, d2), (w3, d3) = C_BRANCHES
    items = ([_prompt_bias_item(1, 0)] + [_prompt_bias_item(d, 4) for d in C_DILS]
             + [_cache_bias_item(WIN, WIN, 1, WIN, 0), _new_rows_bias_item(False, 0)]
             + [_cache_bias_item(lc, w1, d1, NEAR, 4), _cache_bias_item(lc, w2, d2, NEAR, 4),
                _cache_bias_item(lc, w3, d3, lc, 4), _new_rows_bias_item(False, 4), _new_rows_bias_item(True, 4)])
    pa, pc1, pc4, pc16, sa_c, sa_n, sc1, sc2, sc3, sn_all, sn_same = _expand_biases(rel_bias, items)
    by_variant = lambda b: b.reshape(4, 2, BLOCK, 2 * BLOCK)
    by_pair = lambda b: b.reshape(2, 2 * SROWS, b.shape[-1])
    bias_a = by_variant(pa)
    bias_c = jnp.stack([by_variant(pc1), by_variant(pc4), by_variant(pc16)])
    sbias_a_c, sbias_a_n = sa_c[None], sa_n[None]
    sb12 = jnp.stack([by_pair(sc1), by_pair(sc2)])
    sb3 = by_pair(sc3)
    sbn = jnp.stack([by_pair(sn_all), by_pair(sn_same), by_pair(sn_same)])

    cak_t, cav_t = cache_a_k.transpose(0, 1, 3, 4, 2), cache_a_v.transpose(0, 1, 3, 4, 2)
    cck_t, ccv_t = cache_c_k.transpose(0, 1, 3, 4, 2), cache_c_v.transpose(0, 1, 3, 4, 2)

    hp = x_prompt.reshape(n * seq, D_MODEL)
    hs = jnp.pad(x_sample, ((0, 0), (SVALID, 0), (0, 0))).reshape(ns * SROWS, D_MODEL)
    tl = _tiling(ns)
    prev8 = jnp.pad(state_conv, ((0, 0), (0, 0), (1, SROWS - 4), (0, 0))).reshape(depth, ns * SROWS, CONV_DIM)

    p_ak = p_ck = p_ssm = s_ssm = None
    p_conv, s_small = [], [[] for _ in range(5)]
    for l in range(depth):
        qa, ka, va, qc, kc, vc, z, xbc, dt, tail = _in_proj(hp, g_mix, w_in_b, gain, bd, cw8, cb, l, tl.tm_prompt,
                                                            seq // tl.tm_prompt)
        oa, *p_ak = _attn_a_prompt(qa, ka, va, bias_a, a_sinks, l, depth, n, seq, p_ak)
        oc, *p_ck = _attn_c_prompt(qc, kc, vc, bias_c, l, depth, n, seq, p_ck)
        ob, *p_ssm = _ssd_prompt(xbc, z, dt, dtb, alog, dsk, ng, tri, ex, l, depth, n, nb, p_ssm)
        hp = _out_mlp(hp, oa, ob, oc, w_out_b, g_mlp, w_up_b, w_down_b, l, tl.tm_prompt)
        p_conv.append(tail)

        qa, ka, va, qc, kc, vc, z, xbc, dt, _ = _in_proj(hs, g_mix, w_in_b, gain, bd, cw8, cb, l, tl.tm_sample, 0)
        oa = _attn_a_sample(qa, ka, va, cak_t, cav_t, sbias_a_c, sbias_a_n, a_sinks, l, tl.seqs_a)
        oc = _attn_c_sample(qc, kc, vc, cck_t, ccv_t, sb12, sb3, sbn, l, tl.seqs_c)
        yd, eac, xw, bm, cm, cd = _ssd_sample_pre(xbc, prev8, dt, cw8, cb, dtb, alog, dsk, ex, l, tl.rows_b)
        cd_s = cd.reshape(ns, SROWS, LANES)[:, SROWS - 1, :N_SSM_HEADS]
        ob, *s_ssm = _ssd_sample_state(cd_s, yd, eac, xw, bm, cm, z, state_ssm, ng, l, depth, tl.seqs_b, s_ssm)
        hs = _out_mlp(hs, oa, ob, oc, w_out_b, g_mlp, w_up_b, w_down_b, l, tl.tm_sample)
        for dst, val in zip(s_small, (ka, va, kc, vc, xbc)):
            dst.append(val)

    y_prompt = hp.reshape(n, seq, D_MODEL)
    y_sample = hs.reshape(ns, SROWS, D_MODEL)[:, SVALID:]
    unminor = lambda buf, heads: buf.reshape(depth, n, heads, HEAD_DIM, -1).transpose(0, 1, 4, 2, 3)
    p_state = (unminor(p_ak[0], 2), unminor(p_ak[1], 2), unminor(p_ck[0], 4), unminor(p_ck[1], 4),
               p_ssm[0], jnp.stack(p_conv)[:, :, SUBLANES - 3:])
    new_kv = _new_kv_states(*(jnp.stack(v) for v in s_small[:4]), ns)
    s_kv = tuple(b.reshape(depth, SROWS - SVALID, hh, HEAD_DIM, ns).transpose(0, 4, 1, 2, 3)
                 for b, hh in zip(new_kv, (2, 2, 4, 4)))
    s_conv = jnp.stack([x.reshape(ns, SROWS, CONV_DIM)[:, SROWS - 3:] for x in s_small[4]])
    s_state = s_kv + (s_ssm[0], s_conv)
    return (y_prompt, y_sample) + p_state + s_state
```

```python
import collections
import functools
import math

import jax
import jax.numpy as jnp
import numpy as np
from jax import lax
from jax.experimental import pallas as pl
from jax.experimental.pallas import tpu as pltpu

F32 = jnp.float32
BF16 = jnp.bfloat16
HIGHEST = lax.Precision.HIGHEST

D_MODEL = 1024
HEAD_DIM = 64
LANES = 128
SUBLANES = 8
BLOCK = 128
WIN = 128
D_INNER = 512
D_STATE = 128
N_SSM_HEADS = 8
CONV_DIM = 1024
D_FF = 4096
NUM_BUCKETS = 32
REL_MAX_DIST = 2048
EPS = 1e-6
ATTN_SCALE = HEAD_DIM ** -0.5
LOG2E = math.log2(math.e)
NEG = -1e30
C_BRANCHES = ((128, 1), (512, 4), (2048, 16))
C_DILS = tuple(d for _, d in C_BRANCHES)
SROWS = 8
SVALID = 4
NEAR = 512
C_UNROLL = 8
A_UNROLL = 4
SSD_SUB = 4

COL_QA, COL_KA, COL_VA, COL_QC, COL_KC, COL_VC, COL_Z, COL_XBC, COL_DT, COL_END = (
    0, 256, 384, 512, 768, 1024, 1280, 1792, 2816, 2944)
IN_COLS = 2824
V7X_VMEM_BYTES = 64 * 1024 * 1024
VMEM_LIMIT = V7X_VMEM_BYTES * 7 // 8


def _cparams(sem):
    return pltpu.CompilerParams(dimension_semantics=sem, vmem_limit_bytes=VMEM_LIMIT)


def _dot(a, b):
    return jnp.dot(a, b, preferred_element_type=F32)


def _dot_t(a, b):
    return lax.dot_general(a, b, (((1,), (1,)), ((), ())), preferred_element_type=F32)


def _dot_t0(a, b):
    return lax.dot_general(a, b, (((0,), (0,)), ((), ())), preferred_element_type=F32)


def _lane_iota(shape):
    return lax.broadcasted_iota(jnp.int32, shape, len(shape) - 1)


def _silu(x):
    hx = 0.5 * x
    return hx + hx * jnp.tanh(hx)


def _softplus(x):
    return jnp.maximum(x, 0.0) + jnp.log(1.0 + jnp.exp(-jnp.abs(x)))


def _stacked(body, n_in, bufs):
    if bufs is None:
        return body, [], [], 0
    k = len(bufs)

    def wrapped(*refs):
        return body(*refs[:n_in], *refs[n_in + k:])

    return wrapped, [pl.BlockSpec(memory_space=pl.ANY)] * k, list(bufs), k


def _alias_map(n_in, k, n_out):
    return {n_in + i: n_out - k + i for i in range(k)}


def _bias_kernel(tab_ref, *refs, segments):
    n = len(segments)
    for bkt_ref, o_ref, segs in zip(refs[:n], refs[n:], segments):
        for r0, r1, col in segs:
            bkt = bkt_ref[r0:r1, :]
            acc = jnp.full(bkt.shape, NEG, F32)
            for b in range(NUM_BUCKETS):
                acc = jnp.where(bkt == b, tab_ref[b, col], acc)
            o_ref[r0:r1, :] = acc


def _expand_biases(table, items):
    bkts = [jnp.asarray(b) for b, _ in items]
    full = lambda b: pl.BlockSpec(b.shape, lambda: (0, 0))
    return pl.pallas_call(
        functools.partial(_bias_kernel, segments=[s for _, s in items]),
        out_shape=[jax.ShapeDtypeStruct(b.shape, F32) for b in bkts],
        in_specs=[pl.BlockSpec(memory_space=pltpu.SMEM)] + [full(b) for b in bkts],
        out_specs=[full(b) for b in bkts],
        name="bias_expand",
    )(table, *bkts)


def _t5_bucket(dist):
    max_exact = NUM_BUCKETS // 2
    d = np.maximum(dist, 0)
    df = np.maximum(d, 1).astype(np.float32)
    ratio = np.log(df / np.float32(max_exact)) / np.float32(math.log(REL_MAX_DIST / max_exact))
    large = max_exact + (ratio * np.float32(NUM_BUCKETS - max_exact)).astype(np.int32)
    return np.where(d < max_exact, d, np.minimum(large, NUM_BUCKETS - 1))


def _masked_bucket(dist, valid):
    return np.where(valid, _t5_bucket(dist), -1).astype(np.int32)


def _head_norm(p, gain, slabs):
    lo = _lane_iota((p.shape[0], LANES)) < HEAD_DIM
    out = []
    for s in range(slabs):
        x = p[:, s * LANES:(s + 1) * LANES]
        sq = x * x
        s_lo = jnp.sum(jnp.where(lo, sq, 0.0), axis=-1, keepdims=True)
        s_hi = jnp.sum(jnp.where(lo, 0.0, sq), axis=-1, keepdims=True)
        ss = jnp.where(lo, s_lo, s_hi)
        out.append(x * lax.rsqrt(ss * (1.0 / HEAD_DIM) + EPS) * gain[:, s * LANES:(s + 1) * LANES])
    return out[0] if slabs == 1 else jnp.concatenate(out, axis=1)


def _in_proj_kernel(h_ref, g_ref, w_ref, gain_ref, cw_ref, cb_ref,
                    qa_ref, ka_ref, va_ref, qc_ref, kc_ref, vc_ref, z_ref, xbc_ref, dt_ref, tail_ref, xbuf,
                    *, conv_tiles):
    if conv_tiles:
        @pl.when(pl.program_id(0) % conv_tiles == 0)
        def _():
            xbuf[0:SUBLANES, :] = jnp.zeros((SUBLANES, CONV_DIM), F32)

    tm = h_ref.shape[0]
    x = h_ref[...]
    ms = jnp.mean(x * x, axis=-1, keepdims=True)
    u = (x * lax.rsqrt(ms + EPS) * g_ref[...]).astype(BF16)

    def proj(c0, c1):
        return _dot(u, w_ref[:, c0:c1])

    xbc = proj(COL_XBC, COL_DT)
    tail = xbc[tm - SUBLANES:tm, :]
    tail_ref[...] = tail
    if conv_tiles:
        xbuf[SUBLANES:SUBLANES + tm, :] = xbc
        y = cb_ref[...] + cw_ref[3:4, :] * xbc
        for kk in range(3):
            y = y + cw_ref[kk:kk + 1, :] * xbuf[SUBLANES - 3 + kk:SUBLANES - 3 + kk + tm, :]
        xbuf[0:SUBLANES, :] = tail
        xbc_ref[...] = y
    else:
        xbc_ref[...] = xbc
    groups = [COL_QA, COL_KA, COL_QC, COL_KC]
    raw = [proj(c0, c0 + 256) for c0 in groups]
    va_ref[...] = raw[1][:, COL_VA - COL_KA:]
    vc_ref[...] = proj(COL_VC, COL_Z)
    z_ref[...] = proj(COL_Z, COL_XBC)
    qa_ref[...] = _head_norm(raw[0], gain_ref[:, COL_QA:COL_QA + 256], 2)
    ka_ref[...] = _head_norm(raw[1], gain_ref[:, COL_KA:COL_KA + 256], 1)
    qc_ref[...] = _head_norm(raw[2], gain_ref[:, COL_QC:COL_QC + 256], 2)
    kc_ref[...] = _head_norm(raw[3], gain_ref[:, COL_KC:COL_KC + 256], 2)
    dt_ref[...] = proj(COL_DT, COL_END)


def _in_proj(h, g, w, gain, cw8, cb, l, tm, conv_tiles):
    t = h.shape[0]
    widths = (256, 128, 128, 256, 256, 256, 512, 1024, 128)
    layer = lambda i: (l, 0, 0)
    n_tail = t // tm // conv_tiles if conv_tiles else t // tm
    per_seq = conv_tiles if conv_tiles else 1
    return pl.pallas_call(
        functools.partial(_in_proj_kernel, conv_tiles=conv_tiles),
        grid=(t // tm,),
        out_shape=[jax.ShapeDtypeStruct((t, w_), F32) for w_ in widths]
        + [jax.ShapeDtypeStruct((n_tail, SUBLANES, CONV_DIM), F32)],
        in_specs=[pl.BlockSpec((tm, D_MODEL), lambda i: (i, 0)),
                  pl.BlockSpec((None, 1, D_MODEL), layer),
                  pl.BlockSpec((None, D_MODEL, COL_END), layer),
                  pl.BlockSpec((None, 1, D_MODEL), layer),
                  pl.BlockSpec((None, SUBLANES, CONV_DIM), layer),
                  pl.BlockSpec((None, 1, CONV_DIM), layer)],
        out_specs=[pl.BlockSpec((tm, w_), lambda i: (i, 0)) for w_ in widths]
        + [pl.BlockSpec((None, SUBLANES, CONV_DIM), lambda i: (i // per_seq, 0, 0))],
        scratch_shapes=[pltpu.VMEM((SUBLANES + tm, CONV_DIM), F32)],
        compiler_params=_cparams(("arbitrary",)),
        name="in_proj",
    )(h, g, w, gain, cw8, cb)


def _out_mlp_kernel(h_ref, oa_ref, ob_ref, oc_ref, wo_ref, g_ref, wu_ref, wd_ref, o_ref):
    acc = _dot(oa_ref[...].astype(BF16), wo_ref[0:256, :])
    acc += _dot(ob_ref[...].astype(BF16), wo_ref[256:768, :])
    acc += _dot(oc_ref[...].astype(BF16), wo_ref[768:1024, :])
    h2 = h_ref[...] + acc
    ms = jnp.mean(h2 * h2, axis=-1, keepdims=True)
    u = (h2 * lax.rsqrt(ms + EPS) * g_ref[...]).astype(BF16)
    ff_chunk = 1024
    mlp = jnp.zeros_like(h2)
    for c in range(D_FF // ff_chunk):
        a = _dot(u, wu_ref[:, c * ff_chunk:(c + 1) * ff_chunk])
        a = jnp.square(jnp.maximum(a, 0.0)).astype(BF16)
        mlp += _dot(a, wd_ref[c * ff_chunk:(c + 1) * ff_chunk, :])
    o_ref[...] = h2 + mlp


def _out_mlp(h, oa, ob, oc, wo, g, wu, wd, l, tm):
    t = h.shape[0]
    layer = lambda i: (l, 0, 0)
    row = lambda i: (i, 0)
    return pl.pallas_call(
        _out_mlp_kernel,
        grid=(t // tm,),
        out_shape=jax.ShapeDtypeStruct((t, D_MODEL), F32),
        in_specs=[pl.BlockSpec((tm, D_MODEL), row),
                  pl.BlockSpec((tm, 256), row),
                  pl.BlockSpec((tm, 512), row),
                  pl.BlockSpec((tm, 256), row),
                  pl.BlockSpec((None, D_MODEL, D_MODEL), layer),
                  pl.BlockSpec((None, 1, D_MODEL), layer),
                  pl.BlockSpec((None, D_MODEL, D_FF), layer),
                  pl.BlockSpec((None, D_FF, D_MODEL), layer)],
        out_specs=pl.BlockSpec((tm, D_MODEL), row),
        compiler_params=_cparams(("parallel",)),
        name="out_mlp",
    )(h, oa, ob, oc, wo, g, wu, wd)


def _attn_a_prompt_kernel(sink_ref, q_ref, k_ref, v_ref, bias_ref, o_ref, kt_ref, vt_ref,
                          k_st, k_sw, v_st, v_sw, *, l, seq):
    lo = _lane_iota((BLOCK, LANES)) < HEAD_DIM
    k = k_ref[...]
    v = v_ref[...]
    for st in (k_st, k_sw, v_st, v_sw):
        st[0:BLOCK, :] = jnp.zeros((BLOCK, LANES), BF16)
    k_st[BLOCK:, :] = k.astype(BF16)
    v_st[BLOCK:, :] = v.astype(BF16)
    k_sw[BLOCK:, :] = pltpu.roll(k, HEAD_DIM, 1).astype(BF16)
    v_sw[BLOCK:, :] = pltpu.roll(v, HEAD_DIM, 1).astype(BF16)
    kt_ref[...] = k_ref[seq - WIN:seq, :].T
    vt_ref[...] = v_ref[seq - WIN:seq, :].T

    def body(g, carry):
        scores, where = [], []
        for u in range(A_UNROLL):
            b = g * A_UNROLL + u
            cur = pl.ds(pl.multiple_of(b * BLOCK, BLOCK), BLOCK)
            both = pl.ds(pl.multiple_of(b * BLOCK, BLOCK), 2 * BLOCK)
            variant = jnp.where(b == 0, 1, 0)
            where.append((cur, both))
            for h in range(4):
                slab = q_ref[cur, (h // 2) * LANES:(h // 2 + 1) * LANES]
                qm = jnp.where(lo if h % 2 == 0 else ~lo, slab, 0.0).astype(BF16)
                kk = k_st if h in (0, 3) else k_sw
                scores.append(_dot_t(qm, kk[both, :]) + bias_ref[h, variant])
        probs = []
        for i, s in enumerate(scores):
            sink = sink_ref[l, i % 4]
            m = jnp.maximum(jnp.max(jnp.maximum(s[:, 0:BLOCK], s[:, BLOCK:2 * BLOCK]), axis=-1, keepdims=True), sink)
            p = jnp.exp2(s - m)
            den = jnp.sum(p[:, 0:BLOCK] + p[:, BLOCK:2 * BLOCK], axis=-1, keepdims=True) + jnp.exp2(sink - m)
            probs.append((p.astype(BF16), den))
        for u, (cur, both) in enumerate(where):
            outs = []
            for h in range(4):
                p, den = probs[4 * u + h]
                vv = v_st if h in (0, 3) else v_sw
                outs.append(_dot(p, vv[both, :]) / den)
            o_ref[cur, 0:LANES] = jnp.where(lo, outs[0], outs[1])
            o_ref[cur, LANES:2 * LANES] = jnp.where(lo, outs[2], outs[3])
        return carry

    lax.fori_loop(0, seq // BLOCK // A_UNROLL, body, 0)


def _attn_a_prompt(q, k, v, bias, sinks, l, depth, n, seq, bufs):
    t = q.shape[0]
    n_in = 5
    body, xspecs, xargs, k_alias = _stacked(functools.partial(_attn_a_prompt_kernel, l=l, seq=seq), n_in, bufs)
    st_shape = jax.ShapeDtypeStruct((depth, n, LANES, WIN), F32)
    st_spec = pl.BlockSpec((None, None, LANES, WIN), lambda i: (l, i, 0, 0))
    tok = lambda w: pl.BlockSpec((seq, w), lambda i: (i, 0))
    return pl.pallas_call(
        body,
        grid=(n,),
        out_shape=[jax.ShapeDtypeStruct((t, 256), F32), st_shape, st_shape],
        in_specs=[pl.BlockSpec(memory_space=pltpu.SMEM), tok(256), tok(LANES), tok(LANES),
                  pl.BlockSpec((4, 2, BLOCK, 2 * BLOCK), lambda i: (0, 0, 0, 0))] + xspecs,
        out_specs=[tok(256), st_spec, st_spec],
        scratch_shapes=[pltpu.VMEM((BLOCK + seq, LANES), BF16)] * 4,
        input_output_aliases=_alias_map(n_in, k_alias, 3) if k_alias else {},
        compiler_params=_cparams(("parallel",)),
        name="attn_a_prompt",
    )(sinks, q, k, v, bias, *xargs)


def _attn_c_prompt_kernel(q_ref, k_ref, v_ref, bias_ref, o_ref, kt_ref, vt_ref, ob_scr, lse_scr, *, seq):
    lane = _lane_iota((BLOCK, LANES))
    lo = lane < HEAD_DIM
    kt_ref[...] = k_ref[...].T
    vt_ref[...] = v_ref[...].T

    def blocks(br, dil, where, has_prev):
        rows = lambda ref, s: ref[pl.ds(s, BLOCK, stride=dil), :]
        scores, vals = [], []
        for qs, ps, first in where:
            q = rows(q_ref, qs)
            if has_prev:
                k2 = jnp.concatenate([rows(k_ref, ps), rows(k_ref, qs)], axis=0).astype(BF16)
                vals.append(jnp.concatenate([rows(v_ref, ps), rows(v_ref, qs)], axis=0).astype(BF16))
                variant = jnp.where(first, 1, 0)
            else:
                k2 = rows(k_ref, qs).astype(BF16)
                vals.append(rows(v_ref, qs).astype(BF16))
            for hh in range(2):
                qm = jnp.where(lo if hh == 0 else ~lo, q, 0.0).astype(BF16)
                bias = bias_ref[br, hh, variant] if has_prev else bias_ref[br, hh, 0, :, BLOCK:2 * BLOCK]
                scores.append(_dot_t(qm, k2) + bias)
        probs = []
        for s in scores:
            sm = jnp.maximum(s[:, 0:BLOCK], s[:, BLOCK:2 * BLOCK]) if has_prev else s
            m = jnp.max(sm, axis=-1, keepdims=True)
            p = jnp.exp2(s - m)
            pm = p[:, 0:BLOCK] + p[:, BLOCK:2 * BLOCK] if has_prev else p
            probs.append((p.astype(BF16), m, jnp.sum(pm, axis=-1, keepdims=True)))
        for u, (qs, _, _) in enumerate(where):
            outs, lses = [], []
            for hh in range(2):
                p, m, den = probs[2 * u + hh]
                outs.append(_dot(p, vals[u]) / den)
                lses.append(jnp.broadcast_to(m + jnp.log2(den), (BLOCK, LANES)))
            ob_scr[br, pl.ds(qs, BLOCK, stride=dil), :] = jnp.where(lo, outs[0], outs[1])
            lse_scr[br, pl.ds(qs, BLOCK, stride=dil), :] = jnp.where(lo, lses[0], lses[1])

    for br, dil in enumerate(C_DILS):
        nb = seq // dil // BLOCK
        span = BLOCK * dil

        def body(g, carry, br=br, dil=dil, nb=nb, span=span):
            where = []
            for u in range(C_UNROLL):
                i = g * C_UNROLL + u
                res = i % dil
                b = i // dil
                where.append((res + b * span, res + jnp.maximum(b - 1, 0) * span, b == 0))
            blocks(br, dil, where, nb > 1)
            return carry

        lax.fori_loop(0, dil * nb // C_UNROLL, body, 0)

    def merge(i, carry):
        rows = pl.ds(pl.multiple_of(i * BLOCK, BLOCK), BLOCK)
        l0, l1, l2 = lse_scr[0, rows, :], lse_scr[1, rows, :], lse_scr[2, rows, :]
        m = jnp.maximum(jnp.maximum(l0, l1), l2)
        w0, w1, w2 = jnp.exp2(l0 - m), jnp.exp2(l1 - m), jnp.exp2(l2 - m)
        num = w0 * ob_scr[0, rows, :] + w1 * ob_scr[1, rows, :] + w2 * ob_scr[2, rows, :]
        o_ref[rows, :] = num / (w0 + w1 + w2)
        return carry

    lax.fori_loop(0, seq // BLOCK, merge, 0)


def _attn_c_prompt(q, k, v, bias, l, depth, n, seq, bufs):
    t = q.shape[0]
    blk = pl.BlockSpec((seq, LANES), lambda i, hp: (i, hp))
    n_in = 4
    body, xspecs, xargs, k_alias = _stacked(functools.partial(_attn_c_prompt_kernel, seq=seq), n_in, bufs)
    st_shape = jax.ShapeDtypeStruct((depth, n, 256, seq), F32)
    st_spec = pl.BlockSpec((None, None, LANES, seq), lambda i, hp: (l, i, hp, 0))
    return pl.pallas_call(
        body,
        grid=(n, 2),
        out_shape=[jax.ShapeDtypeStruct((t, 256), F32), st_shape, st_shape],
        in_specs=[blk, blk, blk,
                  pl.BlockSpec((3, 2, 2, BLOCK, 2 * BLOCK), lambda i, hp: (0, hp, 0, 0, 0))] + xspecs,
        out_specs=[blk, st_spec, st_spec],
        scratch_shapes=[pltpu.VMEM((3, seq, LANES), F32), pltpu.VMEM((3, seq, LANES), F32)],
        input_output_aliases=_alias_map(n_in, k_alias, 3) if k_alias else {},
        compiler_params=_cparams(("parallel", "parallel")),
        name="attn_c_prompt",
    )(q, k, v, bias, *xargs)


def _gated_group_norm(y, z, ng):
    gated = y * _silu(z)
    parts = []
    for grp in range(2):
        gsl = gated[:, grp * 256:(grp + 1) * 256]
        ms = jnp.mean(gsl * gsl, axis=-1, keepdims=True)
        parts.append(gsl * lax.rsqrt(ms + EPS))
    return jnp.concatenate(parts, axis=1) * ng


def _expand_heads(v, ex):
    hi = v.astype(BF16)
    r1 = v - hi.astype(F32)
    mid = r1.astype(BF16)
    lo = (r1 - mid.astype(F32)).astype(BF16)
    return _dot(hi, ex) + _dot(mid, ex) + _dot(lo, ex)


def _ssd_prompt_kernel(xbc_ref, z_ref, dt_ref, dtb_ref, alog_ref, dsk_ref, ng_ref, tri_ref, ex_ref,
                       ob_ref, hl_ref, hst):
    c = pl.program_id(1)

    @pl.when(c == 0)
    def _():
        hst[...] = jnp.zeros_like(hst)

    xa_all = _silu(xbc_ref[...])
    dt_all = _softplus(dt_ref[...] + dtb_ref[...])
    a_all = dt_all * (-jnp.exp(alog_ref[...]))
    row = lax.broadcasted_iota(jnp.int32, (BLOCK, BLOCK), 0)
    lane = _lane_iota((BLOCK, BLOCK))
    causal = lane <= row
    lo = lane < HEAD_DIM
    top = row < HEAD_DIM
    n_pairs = N_SSM_HEADS // 2
    for sub in range(SSD_SUB):
        rs = slice(sub * BLOCK, (sub + 1) * BLOCK)
        xa, dt = xa_all[rs], dt_all[rs]
        xs = xa[:, 0:D_INNER]
        acum = jnp.dot(tri_ref[...], a_all[rs], precision=HIGHEST, preferred_element_type=F32)
        bms = [xa[:, D_INNER + g * D_STATE:D_INNER + (g + 1) * D_STATE].astype(BF16) for g in range(2)]
        cms = [xa[:, D_INNER + (2 + g) * D_STATE:D_INNER + (3 + g) * D_STATE].astype(BF16) for g in range(2)]
        cbs = [_dot_t(cms[g], bms[g]) for g in range(2)]
        hps = [hst[2 * j:2 * j + 2].reshape(BLOCK, D_STATE) for j in range(n_pairs)]
        y_off = [_dot_t(cms[j // 2], hps[j].astype(BF16)) for j in range(n_pairs)]
        last = acum[BLOCK - 1:BLOCK, :]
        e_last = jnp.exp(last)
        w_all = _expand_heads(jnp.exp(last - acum) * dt, ex_ref[...])
        e_all = _expand_heads(jnp.exp(acum), ex_ref[...])
        xs_pairs = [xs[:, j * LANES:(j + 1) * LANES] for j in range(n_pairs)]
        for j in range(n_pairs):
            h0_, h1_ = 2 * j, 2 * j + 1
            xw = xs_pairs[j] * w_all[:, j * LANES:(j + 1) * LANES]
            s_new = _dot_t0(xw.astype(BF16), bms[j // 2])
            cd_pair = jnp.where(top, e_last[:, h0_:h0_ + 1], e_last[:, h1_:h1_ + 1])
            hst[h0_:h0_ + 2] = (hps[j] * cd_pair + s_new).reshape(2, HEAD_DIM, D_STATE)
        acum_t = acum.T
        dt_t = dt.T
        y_parts = []
        for j in range(n_pairs):
            xs_b = xs_pairs[j].astype(BF16)
            yd = []
            for hh in range(2):
                h = 2 * j + hh
                seg = acum[:, h:h + 1] - acum_t[h:h + 1, :]
                dec = jnp.exp(jnp.where(causal, seg, NEG)) * dt_t[h:h + 1, :]
                yd.append(_dot((cbs[j // 2] * dec).astype(BF16), xs_b))
            y_parts.append(jnp.where(lo, yd[0], yd[1]) + y_off[j] * e_all[:, j * LANES:(j + 1) * LANES])
        y = jnp.concatenate(y_parts, axis=1) + dsk_ref[...] * xs
        ob_ref[rs, :] = _gated_group_norm(y, z_ref[rs, :], ng_ref[...])

    @pl.when(c == pl.num_programs(1) - 1)
    def _():
        hl_ref[...] = hst[...]


def _ssd_prompt(xbc, z, dt, dtb, alog, dsk, ng, tri, ex, l, depth, n, nc, bufs):
    t = xbc.shape[0]
    nc = nc // SSD_SUB
    rows = SSD_SUB * BLOCK
    row = lambda i, c: (i * nc + c, 0)
    layer = lambda i, c: (l, 0, 0)
    n_in = 9
    body, xspecs, xargs, k_alias = _stacked(_ssd_prompt_kernel, n_in, bufs)
    st = (N_SSM_HEADS, HEAD_DIM, D_STATE)
    return pl.pallas_call(
        body,
        grid=(n, nc),
        out_shape=[jax.ShapeDtypeStruct((t, D_INNER), F32),
                   jax.ShapeDtypeStruct((depth, n) + st, F32)],
        in_specs=[pl.BlockSpec((rows, CONV_DIM), row),
                  pl.BlockSpec((rows, D_INNER), row),
                  pl.BlockSpec((rows, LANES), row),
                  pl.BlockSpec((None, 1, LANES), layer),
                  pl.BlockSpec((None, 1, LANES), layer),
                  pl.BlockSpec((None, 1, D_INNER), layer),
                  pl.BlockSpec((None, 1, D_INNER), layer),
                  pl.BlockSpec((BLOCK, BLOCK), lambda i, c: (0, 0)),
                  pl.BlockSpec((LANES, D_INNER), lambda i, c: (0, 0))] + xspecs,
        out_specs=[pl.BlockSpec((rows, D_INNER), row),
                   pl.BlockSpec((None, None) + st, lambda i, c: (l, i, 0, 0, 0))],
        scratch_shapes=[pltpu.VMEM(st, F32)],
        input_output_aliases=_alias_map(n_in, k_alias, 2) if k_alias else {},
        compiler_params=_cparams(("parallel", "arbitrary")),
        name="ssd_prompt",
    )(xbc, z, dt, dtb, alog, dsk, ng, tri, ex, *xargs)


def _attn_a_sample_kernel(sink_ref, q_ref, kn_ref, vn_ref, kt_ref, vt_ref, bc_ref, bn_ref, o_ref, *, nbs, l):
    rb = nbs * SROWS
    lo = _lane_iota((rb, LANES)) < HEAD_DIM
    s0, s1 = q_ref[:, 0:LANES], q_ref[:, LANES:2 * LANES]
    per_seq = lambda v: v.reshape(nbs, SROWS, LANES)
    qm = jnp.concatenate([
        per_seq(jnp.where(lo, s0, 0.0)),
        per_seq(jnp.where(lo, pltpu.roll(s0, HEAD_DIM, 1), 0.0)),
        per_seq(jnp.where(lo, 0.0, pltpu.roll(s1, HEAD_DIM, 1))),
        per_seq(jnp.where(lo, 0.0, s1))], axis=1).astype(BF16)
    kt = kt_ref[...].reshape(nbs, LANES, WIN).astype(BF16)
    vt = vt_ref[...].reshape(nbs, LANES, WIN).astype(BF16)
    kn = kn_ref[...].reshape(nbs, SROWS, LANES).astype(BF16)
    vn = vn_ref[...].reshape(nbs, SROWS, LANES).astype(BF16)
    s_c = jnp.einsum('nqd,ndk->nqk', qm, kt, preferred_element_type=F32) + bc_ref[...]
    s_n = jnp.einsum('nqd,nkd->nqk', qm, kn, preferred_element_type=F32) + bn_ref[...]
    hrow = lax.broadcasted_iota(jnp.int32, (nbs, 4 * SROWS, 1), 1) // SROWS
    sink = jnp.where(hrow == 0, sink_ref[l, 0], jnp.where(hrow == 1, sink_ref[l, 1],
                     jnp.where(hrow == 2, sink_ref[l, 2], sink_ref[l, 3])))
    m = jnp.maximum(jnp.max(s_c, axis=-1, keepdims=True), jnp.max(s_n, axis=-1, keepdims=True))
    m = jnp.maximum(m, sink)
    p_c = jnp.exp2(s_c - m)
    p_n = jnp.exp2(s_n - m)
    den = jnp.sum(p_c, axis=-1, keepdims=True) + jnp.sum(p_n, axis=-1, keepdims=True) + jnp.exp2(sink - m)
    o = (jnp.einsum('nqk,ndk->nqd', p_c.astype(BF16), vt, preferred_element_type=F32)
         + jnp.einsum('nqk,nkd->nqd', p_n.astype(BF16), vn, preferred_element_type=F32)) / den
    o0, o1, o2, o3 = (o[:, i * SROWS:(i + 1) * SROWS, :].reshape(rb, LANES) for i in range(4))
    o_ref[:, 0:LANES] = jnp.where(lo, o0, pltpu.roll(o1, HEAD_DIM, 1))
    o_ref[:, LANES:2 * LANES] = jnp.where(lo, pltpu.roll(o2, HEAD_DIM, 1), o3)


def _attn_a_sample(q8, k8, v8, cache_kt, cache_vt, bias_c, bias_n, sinks, l, nbs):
    ns = cache_kt.shape[1]
    tok = lambda i: (i, 0)
    cache = pl.BlockSpec((None, nbs, 2, HEAD_DIM, WIN), lambda i: (l, i, 0, 0, 0))
    c3 = lambda i: (0, 0, 0)
    return pl.pallas_call(
        functools.partial(_attn_a_sample_kernel, nbs=nbs, l=l),
        grid=(ns // nbs,),
        out_shape=jax.ShapeDtypeStruct((ns * SROWS, 256), F32),
        in_specs=[pl.BlockSpec(memory_space=pltpu.SMEM),
                  pl.BlockSpec((nbs * SROWS, 256), tok),
                  pl.BlockSpec((nbs * SROWS, LANES), tok),
                  pl.BlockSpec((nbs * SROWS, LANES), tok),
                  cache, cache,
                  pl.BlockSpec((1, 4 * SROWS, WIN), c3),
                  pl.BlockSpec((1, 4 * SROWS, SROWS), c3)],
        out_specs=pl.BlockSpec((nbs * SROWS, 256), tok),
        compiler_params=_cparams(("parallel",)),
        name="attn_a_sample",
    )(sinks, q8, k8, v8, cache_kt, cache_vt, bias_c, bias_n)


def _softmax_parts(parts):
    m = None
    for s in parts:
        mm = jnp.max(s, axis=-1, keepdims=True)
        m = mm if m is None else jnp.maximum(m, mm)
    ps = [jnp.exp2(s - m) for s in parts]
    den = None
    for p in ps:
        dd = jnp.sum(p, axis=-1, keepdims=True)
        den = dd if den is None else den + dd
    return ps, den, m + jnp.log2(den)


def _attn_c_sample_kernel(q_ref, kn_ref, vn_ref, kt_ref, vt_ref, b12_ref, b3_ref, bn_ref, o_ref, *, nbs, lc):
    lo = _lane_iota((SROWS, LANES)) < HEAD_DIM
    pr = 2 * SROWS
    units = [(n, j, slice(n * SROWS, (n + 1) * SROWS), slice(j * LANES, (j + 1) * LANES))
             for n in range(nbs) for j in range(2)]
    scores = []
    for n, j, rows, lanes in units:
        qp = q_ref[rows, lanes]
        lhs = jnp.concatenate([jnp.where(lo, qp, 0.0), jnp.where(lo, 0.0, qp)], axis=0).astype(BF16)
        ktp = kt_ref[n, 2 * j:2 * j + 2].reshape(LANES, lc).astype(BF16)
        scores.append((_dot(lhs, ktp), _dot_t(lhs, kn_ref[rows, lanes].astype(BF16))))
    probs = []
    for (n, j, rows, lanes), (s, sn) in zip(units, scores):
        s_near = s[:, lc - NEAR:]
        (p1, p1n), d1, l1 = _softmax_parts([s_near + b12_ref[0, j], sn + bn_ref[0, j]])
        (p2, p2n), d2, l2 = _softmax_parts([s_near + b12_ref[1, j], sn + bn_ref[1, j]])
        (p3, p3n), d3, l3 = _softmax_parts([s + b3_ref[j], sn + bn_ref[2, j]])
        m = jnp.maximum(jnp.maximum(l1, l2), l3)
        w1, w2, w3 = jnp.exp2(l1 - m), jnp.exp2(l2 - m), jnp.exp2(l3 - m)
        wsum = w1 + w2 + w3
        probs.append((jnp.concatenate([p1, p2], axis=0).astype(BF16), p3.astype(BF16),
                      (p1n.astype(BF16), p2n.astype(BF16), p3n.astype(BF16)),
                      (w1 / (d1 * wsum), w2 / (d2 * wsum), w3 / (d3 * wsum))))
    for (n, j, rows, lanes), (p12, p3, pn, wts) in zip(units, probs):
        vtp = vt_ref[n, 2 * j:2 * j + 2].reshape(LANES, lc).astype(BF16)
        vnp = vn_ref[rows, lanes].astype(BF16)
        o12 = _dot_t(p12, vtp[:, lc - NEAR:])
        o = ((o12[0:pr] + _dot(pn[0], vnp)) * wts[0] + (o12[pr:2 * pr] + _dot(pn[1], vnp)) * wts[1]
             + (_dot_t(p3, vtp) + _dot(pn[2], vnp)) * wts[2])
        o_ref[rows, lanes] = jnp.where(lo, o[0:SROWS], o[SROWS:pr])


def _attn_c_sample(q8, k8, v8, cache_kt, cache_vt, b12, b3, bn, l, nbs):
    ns, lc = cache_kt.shape[1], cache_kt.shape[4]
    tok = lambda i: (i, 0)
    cache = pl.BlockSpec((None, nbs, 4, HEAD_DIM, lc), lambda i: (l, i, 0, 0, 0))
    c3 = lambda i: (0, 0, 0)
    c4 = lambda i: (0, 0, 0, 0)
    return pl.pallas_call(
        functools.partial(_attn_c_sample_kernel, nbs=nbs, lc=lc),
        grid=(ns // nbs,),
        out_shape=jax.ShapeDtypeStruct((ns * SROWS, 256), F32),
        in_specs=[pl.BlockSpec((nbs * SROWS, 256), tok),
                  pl.BlockSpec((nbs * SROWS, 256), tok),
                  pl.BlockSpec((nbs * SROWS, 256), tok),
                  cache, cache,
                  pl.BlockSpec((2, 2, 2 * SROWS, NEAR), c4),
                  pl.BlockSpec((2, 2 * SROWS, lc), c3),
                  pl.BlockSpec((3, 2, 2 * SROWS, SROWS), c4)],
        out_specs=pl.BlockSpec((nbs * SROWS, 256), tok),
        compiler_params=_cparams(("parallel",)),
        name="attn_c_sample",
    )(q8, k8, v8, cache_kt, cache_vt, b12, b3, bn)


def _ssd_sample_pre_kernel(xbc_ref, prev_ref, dt_ref, cw_ref, cb_ref, dtb_ref, alog_ref, dsk_ref, ex_ref,
                           yd_ref, eac_ref, xw_ref, bm_ref, cm_ref, cd_ref):
    r = xbc_ref.shape[0]
    rid = lax.broadcasted_iota(jnp.int32, (r, 1), 0) % SROWS
    x = jnp.where(jnp.logical_and(rid >= 1, rid < SVALID), prev_ref[...], xbc_ref[...])
    y = cb_ref[...] + cw_ref[3:4, :] * x
    for d in range(1, 4):
        y = y + cw_ref[3 - d:4 - d, :] * pltpu.roll(x, d, 0)
    xa = _silu(y)
    xs = xa[:, 0:D_INNER]
    bm = xa[:, D_INNER:D_INNER + 2 * D_STATE]
    cm = xa[:, D_INNER + 2 * D_STATE:CONV_DIM]
    valid = rid >= SVALID
    dt = jnp.where(valid, _softplus(dt_ref[...] + dtb_ref[...]), 0.0)
    a = dt * (-jnp.exp(alog_ref[...]))
    acum = a
    rem = jnp.zeros_like(a)
    for d in range(1, 4):
        acum = acum + jnp.where(rid - d >= SVALID, pltpu.roll(a, d, 0), 0.0)
        rem = rem + jnp.where(rid + d < SROWS, pltpu.roll(a, r - d, 0), 0.0)
    ex = ex_ref[...]

    def expand(v):
        return _expand_heads(v, ex)

    lane = _lane_iota((r, LANES))
    y_acc = dsk_ref[...] * xs
    for d in range(4):
        ok = rid - d >= SVALID
        bsh = bm if d == 0 else pltpu.roll(bm, d, 0)
        cb0 = jnp.sum(cm[:, 0:D_STATE] * bsh[:, 0:D_STATE], axis=-1, keepdims=True)
        cb1 = jnp.sum(cm[:, D_STATE:] * bsh[:, D_STATE:], axis=-1, keepdims=True)
        cbh = jnp.where(lane < N_SSM_HEADS // 2, cb0, cb1)
        if d == 0:
            coef = cbh * dt
            xsh = xs
        else:
            dec = jnp.exp(jnp.where(ok, acum - pltpu.roll(acum, d, 0), NEG))
            coef = cbh * dec * pltpu.roll(dt, d, 0)
            xsh = pltpu.roll(xs, d, 0)
        y_acc = y_acc + expand(jnp.where(ok, coef, 0.0)) * xsh
    yd_ref[...] = y_acc
    eac_ref[...] = expand(jnp.exp(acum))
    xw_ref[...] = xs * expand(dt * jnp.exp(rem))
    bm_ref[...] = bm
    cm_ref[...] = cm
    cd_ref[...] = jnp.exp(acum + rem)


def _ssd_sample_pre(xbc8, prev8, dt8, cw8, cb, dtb, alog, dsk, ex, l, rb):
    r = xbc8.shape[0]
    row = lambda i: (i, 0)
    layer = lambda i: (l, 0, 0)
    widths = (D_INNER, D_INNER, D_INNER, 2 * D_STATE, 2 * D_STATE, LANES)
    return pl.pallas_call(
        _ssd_sample_pre_kernel,
        grid=(r // rb,),
        out_shape=[jax.ShapeDtypeStruct((r, w_), F32) for w_ in widths],
        in_specs=[pl.BlockSpec((rb, CONV_DIM), row),
                  pl.BlockSpec((None, rb, CONV_DIM), lambda i: (l, i, 0)),
                  pl.BlockSpec((rb, LANES), row),
                  pl.BlockSpec((None, SUBLANES, CONV_DIM), layer),
                  pl.BlockSpec((None, 1, CONV_DIM), layer),
                  pl.BlockSpec((None, 1, LANES), layer),
                  pl.BlockSpec((None, 1, LANES), layer),
                  pl.BlockSpec((None, 1, D_INNER), layer),
                  pl.BlockSpec((LANES, D_INNER), lambda i: (0, 0))],
        out_specs=[pl.BlockSpec((rb, w_), row) for w_ in widths],
        compiler_params=_cparams(("parallel",)),
        name="ssd_sample_pre",
    )(xbc8, prev8, dt8, cw8, cb, dtb, alog, dsk, ex)


def _ssd_sample_state_kernel(cd_ref, yd_ref, eac_ref, xw_ref, bm_ref, cm_ref, z_ref, h0_ref, ng_ref,
                             ob_ref, hn_ref, *, nbs):
    base = pl.program_id(0) * nbs
    yo = []
    for n in range(nbs):
        rows = slice(n * SROWS, (n + 1) * SROWS)
        parts = []
        for g in range(2):
            hp = h0_ref[n, 4 * g:4 * g + 4].reshape(4 * HEAD_DIM, D_STATE)
            cmg = cm_ref[rows, g * D_STATE:(g + 1) * D_STATE].astype(BF16)
            bmg = bm_ref[rows, g * D_STATE:(g + 1) * D_STATE].astype(BF16)
            parts.append(_dot_t(cmg, hp.astype(BF16)))
            s_new = _dot_t0(xw_ref[rows, g * 256:(g + 1) * 256].astype(BF16), bmg)
            for hh in range(4):
                h = 4 * g + hh
                hn_ref[n, h] = (hp[hh * HEAD_DIM:(hh + 1) * HEAD_DIM, :] * cd_ref[base + n, h]
                                + s_new[hh * HEAD_DIM:(hh + 1) * HEAD_DIM, :])
        yo.append(jnp.concatenate(parts, axis=1))
    y = yd_ref[...] + eac_ref[...] * jnp.concatenate(yo, axis=0)
    ob_ref[...] = _gated_group_norm(y, z_ref[...], ng_ref[...])


def _ssd_sample_state(cd, yd, eac, xw, bm, cm, z8, state, ng, l, depth, nbs, bufs):
    ns = state.shape[1]
    row = lambda i: (i, 0)
    st_spec = pl.BlockSpec((None, nbs, N_SSM_HEADS, HEAD_DIM, D_STATE), lambda i: (l, i, 0, 0, 0))
    rb = nbs * SROWS
    n_in = 9
    body, xspecs, xargs, k_alias = _stacked(functools.partial(_ssd_sample_state_kernel, nbs=nbs), n_in, bufs)
    return pl.pallas_call(
        body,
        grid=(ns // nbs,),
        out_shape=[jax.ShapeDtypeStruct((ns * SROWS, D_INNER), F32),
                   jax.ShapeDtypeStruct(state.shape, F32)],
        in_specs=[pl.BlockSpec(memory_space=pltpu.SMEM),
                  pl.BlockSpec((rb, D_INNER), row),
                  pl.BlockSpec((rb, D_INNER), row),
                  pl.BlockSpec((rb, D_INNER), row),
                  pl.BlockSpec((rb, 2 * D_STATE), row),
                  pl.BlockSpec((rb, 2 * D_STATE), row),
                  pl.BlockSpec((rb, D_INNER), row),
                  st_spec,
                  pl.BlockSpec((None, 1, D_INNER), lambda i: (l, 0, 0))] + xspecs,
        out_specs=[pl.BlockSpec((rb, D_INNER), row), st_spec],
        input_output_aliases=_alias_map(n_in, k_alias, 2) if k_alias else {},
        compiler_params=_cparams(("parallel",)),
        name="ssd_sample_state",
    )(cd, yd, eac, xw, bm, cm, z8, state, ng, *xargs)


def _new_kv_kernel(ka_ref, va_ref, kc_ref, vc_ref, oka_ref, ova_ref, okc_ref, ovc_ref, *, ns):
    def emit(src, dst):
        for t in range(SROWS - SVALID):
            dst[t] = src[pl.ds(SVALID + t, ns, stride=SROWS), :].T

    emit(kc_ref, okc_ref)
    emit(vc_ref, ovc_ref)

    @pl.when(pl.program_id(1) == 0)
    def _():
        emit(ka_ref, oka_ref)
        emit(va_ref, ova_ref)


def _new_kv_states(ka, va, kc, vc, ns):
    depth = ka.shape[0]
    nt = SROWS - SVALID
    narrow_in = pl.BlockSpec((None, ns * SROWS, LANES), lambda l, j: (l, 0, 0))
    wide_in = pl.BlockSpec((None, ns * SROWS, LANES), lambda l, j: (l, 0, j))
    narrow_out = pl.BlockSpec((None, nt, LANES, ns), lambda l, j: (l, 0, 0, 0))
    wide_out = pl.BlockSpec((None, nt, LANES, ns), lambda l, j: (l, 0, j, 0))
    shape = lambda w: jax.ShapeDtypeStruct((depth, nt, w, ns), F32)
    return pl.pallas_call(
        functools.partial(_new_kv_kernel, ns=ns),
        grid=(depth, 2),
        out_shape=[shape(LANES), shape(LANES), shape(2 * LANES), shape(2 * LANES)],
        in_specs=[narrow_in, narrow_in, wide_in, wide_in],
        out_specs=[narrow_out, narrow_out, wide_out, wide_out],
        compiler_params=_cparams(("parallel", "arbitrary")),
        name="new_kv_states",
    )(ka, va, kc, vc)


def _prompt_bias_item(dil, col0):
    qi = np.arange(BLOCK)[:, None]
    kj = np.arange(2 * BLOCK)[None, :]
    dist = BLOCK + qi - kj
    ok = (dist >= 0) & (dist <= WIN)
    variants = [_masked_bucket(dist * dil, ok), _masked_bucket(dist * dil, ok & (kj >= BLOCK))]
    segs = [((2 * h + v) * BLOCK, (2 * h + v + 1) * BLOCK, col0 + h) for h in range(4) for v in range(2)]
    return np.concatenate(variants * 4, axis=0), segs


def _sample_bias_item(dist_of, ok_of, cols, col0):
    t = np.arange(SROWS)[:, None] - SVALID
    bkt = np.where(t >= 0, _masked_bucket(dist_of(t, cols), ok_of(t, cols)), 0).astype(np.int32)
    segs = [(h * SROWS, (h + 1) * SROWS, col0 + h) for h in range(4)]
    return np.concatenate([bkt] * 4, axis=0), segs


def _cache_bias_item(length, window, dil, span, col0):
    dist_of = lambda t, pos: length + t - pos
    ok_of = lambda t, pos: (dist_of(t, pos) >= 0) & (dist_of(t, pos) <= window) & (dist_of(t, pos) % dil == 0)
    return _sample_bias_item(dist_of, ok_of, np.arange(length - span, length)[None, :], col0)


def _new_rows_bias_item(same_token_only, col0):
    dist_of = lambda t, r: t - (r - SVALID)
    if same_token_only:
        ok_of = lambda t, r: (r >= SVALID) & (dist_of(t, r) == 0)
    else:
        ok_of = lambda t, r: (r >= SVALID) & (dist_of(t, r) >= 0)
    return _sample_bias_item(dist_of, ok_of, np.arange(SROWS)[None, :], col0)


Tiling = collections.namedtuple("Tiling", "tm_prompt tm_sample seqs_a seqs_c seqs_b rows_b")


def _tiling(ns):
    rows = ns * SROWS
    return Tiling(tm_prompt=512, tm_sample=min(512, rows), seqs_a=min(16, ns), seqs_c=min(2, ns),
                  seqs_b=min(16, ns), rows_b=min(256, rows))
def kernel(x_prompt, x_sample, cache_a_k, cache_a_v, cache_c_k, cache_c_v, state_ssm, state_conv,
           norm_mix_g, w_in, a_q_norm_g, a_k_norm_g, a_sinks, c_q_norm_g, c_k_norm_g, rel_bias,
           conv_w, conv_b, dt_bias, a_log, d_skip, ssm_norm_g, w_out, norm_mlp_g, w_up, w_down):
    depth = w_in.shape[0]
    n, seq, _ = x_prompt.shape
    ns, ts, _ = x_sample.shape
    lc = cache_c_k.shape[2]
    assert ts == SROWS - SVALID and seq % (BLOCK * 16) == 0
    assert cache_a_k.shape[2] == WIN and all(w <= lc for w, _ in C_BRANCHES)
    nb = seq // BLOCK

    w_in_b = jnp.pad(w_in, ((0, 0), (0, 0), (0, COL_END - IN_COLS))).astype(BF16)
    w_out_b, w_up_b, w_down_b = w_out.astype(BF16), w_up.astype(BF16), w_down.astype(BF16)
    ones = jnp.ones((depth, 128), F32)
    q_scale = ATTN_SCALE * LOG2E
    gain = jnp.concatenate([jnp.tile(a_q_norm_g, (1, 4)) * q_scale, jnp.tile(a_k_norm_g, (1, 2)), ones,
                            jnp.tile(c_q_norm_g, (1, 4)) * q_scale, jnp.tile(c_k_norm_g, (1, 4))], axis=1)
    rel_bias = rel_bias * LOG2E
    a_sinks = a_sinks * LOG2E
    tri = (jnp.arange(BLOCK)[None, :] <= jnp.arange(BLOCK)[:, None]).astype(F32)
    ex = (jnp.arange(LANES)[:, None] == (jnp.arange(D_INNER) // HEAD_DIM)[None, :]).astype(BF16)
    cw8 = jnp.pad(conv_w, ((0, 0), (0, SUBLANES - conv_w.shape[1]), (0, 0)))
    pad_h = lambda v: jnp.pad(v, ((0, 0), (0, LANES - N_SSM_HEADS)))[:, None, :]
    vec = lambda v: v[:, None, :]
    dtb, alog = pad_h(dt_bias), pad_h(a_log)
    dsk = vec(jnp.repeat(d_skip, HEAD_DIM, axis=1))
    g_mix, g_mlp, gain, cb, ng = vec(norm_mix_g), vec(norm_mlp_g), vec(gain), vec(conv_b), vec(ssm_norm_g)

    (w1, d1), (w2, d2), (w3, d3) = C_BRANCHES
    items = ([_prompt_bias_item(1, 0)] + [_prompt_bias_item(d, 4) for d in C_DILS]
             + [_cache_bias_item(WIN, WIN, 1, WIN, 0), _new_rows_bias_item(False, 0)]
             + [_cache_bias_item(lc, w1, d1, NEAR, 4), _cache_bias_item(lc, w2, d2, NEAR, 4),
                _cache_bias_item(lc, w3, d3, lc, 4), _new_rows_bias_item(False, 4), _new_rows_bias_item(True, 4)])
    pa, pc1, pc4, pc16, sa_c, sa_n, sc1, sc2, sc3, sn_all, sn_same = _expand_biases(rel_bias, items)
    by_variant = lambda b: b.reshape(4, 2, BLOCK, 2 * BLOCK)
    by_pair = lambda b: b.reshape(2, 2 * SROWS, b.shape[-1])
    bias_a = by_variant(pa)
    bias_c = jnp.stack([by_variant(pc1), by_variant(pc4), by_variant(pc16)])
    sbias_a_c, sbias_a_n = sa_c[None], sa_n[None]
    sb12 = jnp.stack([by_pair(sc1), by_pair(sc2)])
    sb3 = by_pair(sc3)
    sbn = jnp.stack([by_pair(sn_all), by_pair(sn_same), by_pair(sn_same)])

    cak_t, cav_t = cache_a_k.transpose(0, 1, 3, 4, 2), cache_a_v.transpose(0, 1, 3, 4, 2)
    cck_t, ccv_t = cache_c_k.transpose(0, 1, 3, 4, 2), cache_c_v.transpose(0, 1, 3, 4, 2)

    hp = x_prompt.reshape(n * seq, D_MODEL)
    hs = jnp.pad(x_sample, ((0, 0), (SVALID, 0), (0, 0))).reshape(ns * SROWS, D_MODEL)
    tl = _tiling(ns)
    prev8 = jnp.pad(state_conv, ((0, 0), (0, 0), (1, SROWS - 4), (0, 0))).reshape(depth, ns * SROWS, CONV_DIM)

    p_ak = p_ck = p_ssm = s_ssm = None
    p_conv, s_small = [], [[] for _ in range(5)]
    for l in range(depth):
        qa, ka, va, qc, kc, vc, z, xbc, dt, tail = _in_proj(hp, g_mix, w_in_b, gain, cw8, cb, l, tl.tm_prompt,
                                                            seq // tl.tm_prompt)
        oa, *p_ak = _attn_a_prompt(qa, ka, va, bias_a, a_sinks, l, depth, n, seq, p_ak)
        oc, *p_ck = _attn_c_prompt(qc, kc, vc, bias_c, l, depth, n, seq, p_ck)
        ob, *p_ssm = _ssd_prompt(xbc, z, dt, dtb, alog, dsk, ng, tri, ex, l, depth, n, nb, p_ssm)
        hp = _out_mlp(hp, oa, ob, oc, w_out_b, g_mlp, w_up_b, w_down_b, l, tl.tm_prompt)
        p_conv.append(tail)

        qa, ka, va, qc, kc, vc, z, xbc, dt, _ = _in_proj(hs, g_mix, w_in_b, gain, cw8, cb, l, tl.tm_sample, 0)
        oa = _attn_a_sample(qa, ka, va, cak_t, cav_t, sbias_a_c, sbias_a_n, a_sinks, l, tl.seqs_a)
        oc = _attn_c_sample(qc, kc, vc, cck_t, ccv_t, sb12, sb3, sbn, l, tl.seqs_c)
        yd, eac, xw, bm, cm, cd = _ssd_sample_pre(xbc, prev8, dt, cw8, cb, dtb, alog, dsk, ex, l, tl.rows_b)
        cd_s = cd.reshape(ns, SROWS, LANES)[:, SROWS - 1, :N_SSM_HEADS]
        ob, *s_ssm = _ssd_sample_state(cd_s, yd, eac, xw, bm, cm, z, state_ssm, ng, l, depth, tl.seqs_b, s_ssm)
        hs = _out_mlp(hs, oa, ob, oc, w_out_b, g_mlp, w_up_b, w_down_b, l, tl.tm_sample)
        for dst, val in zip(s_small, (ka, va, kc, vc, xbc)):
            dst.append(val)

    y_prompt = hp.reshape(n, seq, D_MODEL)
    y_sample = hs.reshape(ns, SROWS, D_MODEL)[:, SVALID:]
    unminor = lambda buf, heads: buf.reshape(depth, n, heads, HEAD_DIM, -1).transpose(0, 1, 4, 2, 3)
    p_state = (unminor(p_ak[0], 2), unminor(p_ak[1], 2), unminor(p_ck[0], 4), unminor(p_ck[1], 4),
               p_ssm[0], jnp.stack(p_conv)[:, :, SUBLANES - 3:])
    new_kv = _new_kv_states(*(jnp.stack(v) for v in s_small[:4]), ns)
    s_kv = tuple(b.reshape(depth, SROWS - SVALID, hh, HEAD_DIM, ns).transpose(0, 4, 1, 2, 3)
                 for b, hh in zip(new_kv, (2, 2, 4, 4)))
    s_conv = jnp.stack([x.reshape(ns, SROWS, CONV_DIM)[:, SROWS - 3:] for x in s_small[4]])
    s_state = s_kv + (s_ssm[0], s_conv)
    return (y_prompt, y_sample) + p_state + s_state
```

```python
import collections
import functools
import math

import jax
import jax.numpy as jnp
import numpy as np
from jax import lax
from jax.experimental import pallas as pl
from jax.experimental.pallas import tpu as pltpu

F32 = jnp.float32
BF16 = jnp.bfloat16
HIGHEST = lax.Precision.HIGHEST

D_MODEL = 1024
HEAD_DIM = 64
LANES = 128
SUBLANES = 8
BLOCK = 128
WIN = 128
D_INNER = 512
D_STATE = 128
N_SSM_HEADS = 8
CONV_DIM = 1024
D_FF = 4096
NUM_BUCKETS = 32
REL_MAX_DIST = 2048
EPS = 1e-6
ATTN_SCALE = HEAD_DIM ** -0.5
LOG2E = math.log2(math.e)
NEG = -1e30
C_BRANCHES = ((128, 1), (512, 4), (2048, 16))
C_DILS = tuple(d for _, d in C_BRANCHES)
SROWS = 8
SVALID = 4
NEAR = 512
C_UNROLL = 8
A_UNROLL = 4
SSD_SUB = 4

COL_QA, COL_KA, COL_VA, COL_QC, COL_KC, COL_VC, COL_Z, COL_XBC, COL_DT, COL_END = (
    0, 256, 384, 512, 768, 1024, 1280, 1792, 2816, 2944)
IN_COLS = 2824
V7X_VMEM_BYTES = 64 * 1024 * 1024
VMEM_LIMIT = V7X_VMEM_BYTES * 7 // 8


def _cparams(sem):
    return pltpu.CompilerParams(dimension_semantics=sem, vmem_limit_bytes=VMEM_LIMIT)


def _dot(a, b):
    return jnp.dot(a, b, preferred_element_type=F32)


def _dot_t(a, b):
    return lax.dot_general(a, b, (((1,), (1,)), ((), ())), preferred_element_type=F32)


def _dot_t0(a, b):
    return lax.dot_general(a, b, (((0,), (0,)), ((), ())), preferred_element_type=F32)


def _lane_iota(shape):
    return lax.broadcasted_iota(jnp.int32, shape, len(shape) - 1)


def _silu(x):
    hx = 0.5 * x
    return hx + hx * jnp.tanh(hx)


def _softplus(x):
    return jnp.maximum(x, 0.0) + jnp.log(1.0 + jnp.exp(-jnp.abs(x)))


def _stacked(body, n_in, bufs):
    if bufs is None:
        return body, [], [], 0
    k = len(bufs)

    def wrapped(*refs):
        return body(*refs[:n_in], *refs[n_in + k:])

    return wrapped, [pl.BlockSpec(memory_space=pl.ANY)] * k, list(bufs), k


def _alias_map(n_in, k, n_out):
    return {n_in + i: n_out - k + i for i in range(k)}


def _bias_kernel(tab_ref, *refs, segments):
    n = len(segments)
    for bkt_ref, o_ref, segs in zip(refs[:n], refs[n:], segments):
        for r0, r1, col in segs:
            bkt = bkt_ref[r0:r1, :]
            acc = jnp.full(bkt.shape, NEG, F32)
            for b in range(NUM_BUCKETS):
                acc = jnp.where(bkt == b, tab_ref[b, col], acc)
            o_ref[r0:r1, :] = acc


def _expand_biases(table, items):
    bkts = [jnp.asarray(b) for b, _ in items]
    full = lambda b: pl.BlockSpec(b.shape, lambda: (0, 0))
    return pl.pallas_call(
        functools.partial(_bias_kernel, segments=[s for _, s in items]),
        out_shape=[jax.ShapeDtypeStruct(b.shape, F32) for b in bkts],
        in_specs=[pl.BlockSpec(memory_space=pltpu.SMEM)] + [full(b) for b in bkts],
        out_specs=[full(b) for b in bkts],
        name="bias_expand",
    )(table, *bkts)


def _t5_bucket(dist):
    max_exact = NUM_BUCKETS // 2
    d = np.maximum(dist, 0)
    df = np.maximum(d, 1).astype(np.float32)
    ratio = np.log(df / np.float32(max_exact)) / np.float32(math.log(REL_MAX_DIST / max_exact))
    large = max_exact + (ratio * np.float32(NUM_BUCKETS - max_exact)).astype(np.int32)
    return np.where(d < max_exact, d, np.minimum(large, NUM_BUCKETS - 1))


def _masked_bucket(dist, valid):
    return np.where(valid, _t5_bucket(dist), -1).astype(np.int32)


def _head_norm(p, gain, slabs):
    lo = _lane_iota((p.shape[0], LANES)) < HEAD_DIM
    out = []
    for s in range(slabs):
        x = p[:, s * LANES:(s + 1) * LANES]
        sq = x * x
        s_lo = jnp.sum(jnp.where(lo, sq, 0.0), axis=-1, keepdims=True)
        s_hi = jnp.sum(jnp.where(lo, 0.0, sq), axis=-1, keepdims=True)
        ss = jnp.where(lo, s_lo, s_hi)
        out.append(x * lax.rsqrt(ss * (1.0 / HEAD_DIM) + EPS) * gain[:, s * LANES:(s + 1) * LANES])
    return out[0] if slabs == 1 else jnp.concatenate(out, axis=1)


def _in_proj_kernel(h_ref, g_ref, w_ref, gain_ref, cw_ref, cb_ref,
                    qa_ref, ka_ref, va_ref, qc_ref, kc_ref, vc_ref, z_ref, xbc_ref, dt_ref, tail_ref, xbuf,
                    *, conv_tiles):
    if conv_tiles:
        @pl.when(pl.program_id(0) % conv_tiles == 0)
        def _():
            xbuf[0:SUBLANES, :] = jnp.zeros((SUBLANES, CONV_DIM), F32)

    tm = h_ref.shape[0]
    x = h_ref[...]
    ms = jnp.mean(x * x, axis=-1, keepdims=True)
    u = (x * lax.rsqrt(ms + EPS) * g_ref[...]).astype(BF16)

    def proj(c0, c1):
        return _dot(u, w_ref[:, c0:c1])

    xbc = proj(COL_XBC, COL_DT)
    tail = xbc[tm - SUBLANES:tm, :]
    tail_ref[...] = tail
    if conv_tiles:
        xbuf[SUBLANES:SUBLANES + tm, :] = xbc
        y = cb_ref[...] + cw_ref[3:4, :] * xbc
        for kk in range(3):
            y = y + cw_ref[kk:kk + 1, :] * xbuf[SUBLANES - 3 + kk:SUBLANES - 3 + kk + tm, :]
        xbuf[0:SUBLANES, :] = tail
        xbc_ref[...] = y
    else:
        xbc_ref[...] = xbc
    groups = [COL_QA, COL_KA, COL_QC, COL_KC]
    raw = [proj(c0, c0 + 256) for c0 in groups]
    va_ref[...] = raw[1][:, COL_VA - COL_KA:]
    vc_ref[...] = proj(COL_VC, COL_Z)
    z_ref[...] = proj(COL_Z, COL_XBC)
    qa_ref[...] = _head_norm(raw[0], gain_ref[:, COL_QA:COL_QA + 256], 2)
    ka_ref[...] = _head_norm(raw[1], gain_ref[:, COL_KA:COL_KA + 256], 1)
    qc_ref[...] = _head_norm(raw[2], gain_ref[:, COL_QC:COL_QC + 256], 2)
    kc_ref[...] = _head_norm(raw[3], gain_ref[:, COL_KC:COL_KC + 256], 2)
    dt_ref[...] = jnp.zeros(dt_ref.shape, F32)
    dt_ref[:, 0:N_SSM_HEADS] = proj(COL_DT, IN_COLS)


def _in_proj(h, g, w, gain, cw8, cb, l, tm, conv_tiles):
    t = h.shape[0]
    widths = (256, 128, 128, 256, 256, 256, 512, 1024, 128)
    layer = lambda i: (l, 0, 0)
    n_tail = t // tm // conv_tiles if conv_tiles else t // tm
    per_seq = conv_tiles if conv_tiles else 1
    return pl.pallas_call(
        functools.partial(_in_proj_kernel, conv_tiles=conv_tiles),
        grid=(t // tm,),
        out_shape=[jax.ShapeDtypeStruct((t, w_), F32) for w_ in widths]
        + [jax.ShapeDtypeStruct((n_tail, SUBLANES, CONV_DIM), F32)],
        in_specs=[pl.BlockSpec((tm, D_MODEL), lambda i: (i, 0)),
                  pl.BlockSpec((None, 1, D_MODEL), layer),
                  pl.BlockSpec((None, D_MODEL, IN_COLS), layer),
                  pl.BlockSpec((None, 1, D_MODEL), layer),
                  pl.BlockSpec((None, SUBLANES, CONV_DIM), layer),
                  pl.BlockSpec((None, 1, CONV_DIM), layer)],
        out_specs=[pl.BlockSpec((tm, w_), lambda i: (i, 0)) for w_ in widths]
        + [pl.BlockSpec((None, SUBLANES, CONV_DIM), lambda i: (i // per_seq, 0, 0))],
        scratch_shapes=[pltpu.VMEM((SUBLANES + tm, CONV_DIM), F32)],
        compiler_params=_cparams(("arbitrary",)),
        name="in_proj",
    )(h, g, w, gain, cw8, cb)


def _out_mlp_kernel(hp_ref, oap_ref, obp_ref, ocp_ref, hs_ref, oas_ref, obs_ref, ocs_ref,
                    wo_ref, g_ref, wu_ref, wd_ref, op_ref, os_ref, *, prompt_tiles):
    def tile(h_ref, oa_ref, ob_ref, oc_ref, o_ref):
        acc = _dot(oa_ref[...].astype(BF16), wo_ref[0:256, :])
        acc += _dot(ob_ref[...].astype(BF16), wo_ref[256:768, :])
        acc += _dot(oc_ref[...].astype(BF16), wo_ref[768:1024, :])
        h2 = h_ref[...] + acc
        ms = jnp.mean(h2 * h2, axis=-1, keepdims=True)
        u = (h2 * lax.rsqrt(ms + EPS) * g_ref[...]).astype(BF16)
        ff_chunk = 1024
        mlp = jnp.zeros_like(h2)
        for c in range(D_FF // ff_chunk):
            a = _dot(u, wu_ref[:, c * ff_chunk:(c + 1) * ff_chunk])
            a = jnp.square(jnp.maximum(a, 0.0)).astype(BF16)
            mlp += _dot(a, wd_ref[c * ff_chunk:(c + 1) * ff_chunk, :])
        o_ref[...] = h2 + mlp

    is_prompt = pl.program_id(0) < prompt_tiles

    @pl.when(is_prompt)
    def _():
        tile(hp_ref, oap_ref, obp_ref, ocp_ref, op_ref)

    @pl.when(jnp.logical_not(is_prompt))
    def _():
        tile(hs_ref, oas_ref, obs_ref, ocs_ref, os_ref)


def _out_mlp(hp, oap, obp, ocp, hs, oas, obs, ocs, wo, g, wu, wd, l, tm):
    tp, ts = hp.shape[0], hs.shape[0]
    prompt_tiles = tp // tm
    layer = lambda i: (l, 0, 0)
    prow = lambda i: (jnp.minimum(i, prompt_tiles - 1), 0)
    srow = lambda i: (jnp.maximum(i - prompt_tiles, 0), 0)
    tok = lambda idx: [pl.BlockSpec((tm, w_), idx) for w_ in (D_MODEL, 256, 512, 256)]
    weights = lambda shape: pl.BlockSpec((None,) + shape, layer, pipeline_mode=pl.Buffered(1))
    return pl.pallas_call(
        functools.partial(_out_mlp_kernel, prompt_tiles=prompt_tiles),
        grid=(prompt_tiles + ts // tm,),
        out_shape=[jax.ShapeDtypeStruct((tp, D_MODEL), F32), jax.ShapeDtypeStruct((ts, D_MODEL), F32)],
        in_specs=tok(prow) + tok(srow) + [weights((D_MODEL, D_MODEL)),
                                          pl.BlockSpec((None, 1, D_MODEL), layer),
                                          weights((D_MODEL, D_FF)),
                                          weights((D_FF, D_MODEL))],
        out_specs=[pl.BlockSpec((tm, D_MODEL), prow), pl.BlockSpec((tm, D_MODEL), srow)],
        compiler_params=_cparams(("arbitrary",)),
        name="out_mlp",
    )(hp, oap, obp, ocp, hs, oas, obs, ocs, wo, g, wu, wd)


def _attn_a_prompt_kernel(sink_ref, q_ref, k_ref, v_ref, bias_ref, o_ref, kt_ref, vt_ref,
                          k_st, k_sw, v_st, v_sw, *, l, seq):
    lo = _lane_iota((BLOCK, LANES)) < HEAD_DIM
    k = k_ref[...]
    v = v_ref[...]
    for st in (k_st, k_sw, v_st, v_sw):
        st[0:BLOCK, :] = jnp.zeros((BLOCK, LANES), BF16)
    k_st[BLOCK:, :] = k.astype(BF16)
    v_st[BLOCK:, :] = v.astype(BF16)
    k_sw[BLOCK:, :] = pltpu.roll(k, HEAD_DIM, 1).astype(BF16)
    v_sw[BLOCK:, :] = pltpu.roll(v, HEAD_DIM, 1).astype(BF16)
    kt_ref[...] = k_ref[seq - WIN:seq, :].T
    vt_ref[...] = v_ref[seq - WIN:seq, :].T

    def body(g, carry):
        scores, where = [], []
        for u in range(A_UNROLL):
            b = g * A_UNROLL + u
            cur = pl.ds(pl.multiple_of(b * BLOCK, BLOCK), BLOCK)
            both = pl.ds(pl.multiple_of(b * BLOCK, BLOCK), 2 * BLOCK)
            variant = jnp.where(b == 0, 1, 0)
            where.append((cur, both))
            for h in range(4):
                slab = q_ref[cur, (h // 2) * LANES:(h // 2 + 1) * LANES]
                qm = jnp.where(lo if h % 2 == 0 else ~lo, slab, 0.0).astype(BF16)
                kk = k_st if h in (0, 3) else k_sw
                scores.append(_dot_t(qm, kk[both, :]) + bias_ref[h, variant])
        probs = []
        for i, s in enumerate(scores):
            sink = sink_ref[l, i % 4]
            m = jnp.maximum(jnp.max(jnp.maximum(s[:, 0:BLOCK], s[:, BLOCK:2 * BLOCK]), axis=-1, keepdims=True), sink)
            p = jnp.exp2(s - m)
            den = jnp.sum(p[:, 0:BLOCK] + p[:, BLOCK:2 * BLOCK], axis=-1, keepdims=True) + jnp.exp2(sink - m)
            probs.append((p.astype(BF16), den))
        for u, (cur, both) in enumerate(where):
            outs = []
            for h in range(4):
                p, den = probs[4 * u + h]
                vv = v_st if h in (0, 3) else v_sw
                outs.append(_dot(p, vv[both, :]) / den)
            o_ref[cur, 0:LANES] = jnp.where(lo, outs[0], outs[1])
            o_ref[cur, LANES:2 * LANES] = jnp.where(lo, outs[2], outs[3])
        return carry

    lax.fori_loop(0, seq // BLOCK // A_UNROLL, body, 0)


def _attn_a_prompt(q, k, v, bias, sinks, l, depth, n, seq, bufs):
    t = q.shape[0]
    n_in = 5
    body, xspecs, xargs, k_alias = _stacked(functools.partial(_attn_a_prompt_kernel, l=l, seq=seq), n_in, bufs)
    st_shape = jax.ShapeDtypeStruct((depth, n, LANES, WIN), F32)
    st_spec = pl.BlockSpec((None, None, LANES, WIN), lambda i: (l, i, 0, 0))
    tok = lambda w: pl.BlockSpec((seq, w), lambda i: (i, 0))
    return pl.pallas_call(
        body,
        grid=(n,),
        out_shape=[jax.ShapeDtypeStruct((t, 256), F32), st_shape, st_shape],
        in_specs=[pl.BlockSpec(memory_space=pltpu.SMEM), tok(256), tok(LANES), tok(LANES),
                  pl.BlockSpec((4, 2, BLOCK, 2 * BLOCK), lambda i: (0, 0, 0, 0))] + xspecs,
        out_specs=[tok(256), st_spec, st_spec],
        scratch_shapes=[pltpu.VMEM((BLOCK + seq, LANES), BF16)] * 4,
        input_output_aliases=_alias_map(n_in, k_alias, 3) if k_alias else {},
        compiler_params=_cparams(("parallel",)),
        name="attn_a_prompt",
    )(sinks, q, k, v, bias, *xargs)


def _attn_c_prompt_kernel(q_ref, k_ref, v_ref, bias_ref, o_ref, kt_ref, vt_ref, ob_scr, lse_scr, *, seq):
    lane = _lane_iota((BLOCK, LANES))
    lo = lane < HEAD_DIM
    kt_ref[...] = k_ref[...].T
    vt_ref[...] = v_ref[...].T

    def blocks(br, dil, where, has_prev):
        rows = lambda ref, s: ref[pl.ds(s, BLOCK, stride=dil), :]
        scores, vals = [], []
        for qs, ps, first in where:
            q = rows(q_ref, qs)
            if has_prev:
                k2 = jnp.concatenate([rows(k_ref, ps), rows(k_ref, qs)], axis=0).astype(BF16)
                vals.append(jnp.concatenate([rows(v_ref, ps), rows(v_ref, qs)], axis=0).astype(BF16))
                variant = jnp.where(first, 1, 0)
            else:
                k2 = rows(k_ref, qs).astype(BF16)
                vals.append(rows(v_ref, qs).astype(BF16))
            for hh in range(2):
                qm = jnp.where(lo if hh == 0 else ~lo, q, 0.0).astype(BF16)
                bias = bias_ref[br, hh, variant] if has_prev else bias_ref[br, hh, 0, :, BLOCK:2 * BLOCK]
                scores.append(_dot_t(qm, k2) + bias)
        probs = []
        for s in scores:
            sm = jnp.maximum(s[:, 0:BLOCK], s[:, BLOCK:2 * BLOCK]) if has_prev else s
            m = jnp.max(sm, axis=-1, keepdims=True)
            p = jnp.exp2(s - m)
            pm = p[:, 0:BLOCK] + p[:, BLOCK:2 * BLOCK] if has_prev else p
            probs.append((p.astype(BF16), m, jnp.sum(pm, axis=-1, keepdims=True)))
        for u, (qs, _, _) in enumerate(where):
            outs, lses = [], []
            for hh in range(2):
                p, m, den = probs[2 * u + hh]
                outs.append(_dot(p, vals[u]) / den)
                lses.append(jnp.broadcast_to(m + jnp.log2(den), (BLOCK, LANES)))
            ob_scr[br, pl.ds(qs, BLOCK, stride=dil), :] = jnp.where(lo, outs[0], outs[1])
            lse_scr[br, pl.ds(qs, BLOCK, stride=dil), :] = jnp.where(lo, lses[0], lses[1])

    for br, dil in enumerate(C_DILS):
        nb = seq // dil // BLOCK
        span = BLOCK * dil

        def body(g, carry, br=br, dil=dil, nb=nb, span=span):
            where = []
            for u in range(C_UNROLL):
                i = g * C_UNROLL + u
                res = i % dil
                b = i // dil
                where.append((res + b * span, res + jnp.maximum(b - 1, 0) * span, b == 0))
            blocks(br, dil, where, nb > 1)
            return carry

        lax.fori_loop(0, dil * nb // C_UNROLL, body, 0)

    def merge(i, carry):
        rows = pl.ds(pl.multiple_of(i * BLOCK, BLOCK), BLOCK)
        l0, l1, l2 = lse_scr[0, rows, :], lse_scr[1, rows, :], lse_scr[2, rows, :]
        m = jnp.maximum(jnp.maximum(l0, l1), l2)
        w0, w1, w2 = jnp.exp2(l0 - m), jnp.exp2(l1 - m), jnp.exp2(l2 - m)
        num = w0 * ob_scr[0, rows, :] + w1 * ob_scr[1, rows, :] + w2 * ob_scr[2, rows, :]
        o_ref[rows, :] = num / (w0 + w1 + w2)
        return carry

    lax.fori_loop(0, seq // BLOCK, merge, 0)


def _attn_c_prompt(q, k, v, bias, l, depth, n, seq, bufs):
    t = q.shape[0]
    blk = pl.BlockSpec((seq, LANES), lambda i, hp: (i, hp))
    n_in = 4
    body, xspecs, xargs, k_alias = _stacked(functools.partial(_attn_c_prompt_kernel, seq=seq), n_in, bufs)
    st_shape = jax.ShapeDtypeStruct((depth, n, 256, seq), F32)
    st_spec = pl.BlockSpec((None, None, LANES, seq), lambda i, hp: (l, i, hp, 0))
    return pl.pallas_call(
        body,
        grid=(n, 2),
        out_shape=[jax.ShapeDtypeStruct((t, 256), F32), st_shape, st_shape],
        in_specs=[blk, blk, blk,
                  pl.BlockSpec((3, 2, 2, BLOCK, 2 * BLOCK), lambda i, hp: (0, hp, 0, 0, 0))] + xspecs,
        out_specs=[blk, st_spec, st_spec],
        scratch_shapes=[pltpu.VMEM((3, seq, LANES), F32), pltpu.VMEM((3, seq, LANES), F32)],
        input_output_aliases=_alias_map(n_in, k_alias, 3) if k_alias else {},
        compiler_params=_cparams(("parallel", "parallel")),
        name="attn_c_prompt",
    )(q, k, v, bias, *xargs)


def _gated_group_norm(y, z, ng):
    gated = y * _silu(z)
    parts = []
    for grp in range(2):
        gsl = gated[:, grp * 256:(grp + 1) * 256]
        ms = jnp.mean(gsl * gsl, axis=-1, keepdims=True)
        parts.append(gsl * lax.rsqrt(ms + EPS))
    return jnp.concatenate(parts, axis=1) * ng


def _expand_heads(v, ex):
    hi = v.astype(BF16)
    r1 = v - hi.astype(F32)
    mid = r1.astype(BF16)
    lo = (r1 - mid.astype(F32)).astype(BF16)
    return _dot(hi, ex) + _dot(mid, ex) + _dot(lo, ex)


def _ssd_prompt_kernel(xbc_ref, z_ref, dt_ref, dtb_ref, alog_ref, dsk_ref, ng_ref, tri_ref, ex_ref,
                       ob_ref, hl_ref, hst):
    c = pl.program_id(1)

    @pl.when(c == 0)
    def _():
        hst[...] = jnp.zeros_like(hst)

    xa_all = _silu(xbc_ref[...])
    dt_all = _softplus(dt_ref[...] + dtb_ref[...])
    a_all = dt_all * (-jnp.exp(alog_ref[...]))
    row = lax.broadcasted_iota(jnp.int32, (BLOCK, BLOCK), 0)
    lane = _lane_iota((BLOCK, BLOCK))
    causal = lane <= row
    lo = lane < HEAD_DIM
    top = row < HEAD_DIM
    n_pairs = N_SSM_HEADS // 2
    for sub in range(SSD_SUB):
        rs = slice(sub * BLOCK, (sub + 1) * BLOCK)
        xa, dt = xa_all[rs], dt_all[rs]
        xs = xa[:, 0:D_INNER]
        acum = jnp.dot(tri_ref[...], a_all[rs], precision=HIGHEST, preferred_element_type=F32)
        bms = [xa[:, D_INNER + g * D_STATE:D_INNER + (g + 1) * D_STATE].astype(BF16) for g in range(2)]
        cms = [xa[:, D_INNER + (2 + g) * D_STATE:D_INNER + (3 + g) * D_STATE].astype(BF16) for g in range(2)]
        cbs = [_dot_t(cms[g], bms[g]) for g in range(2)]
        hps = [hst[2 * j:2 * j + 2].reshape(BLOCK, D_STATE) for j in range(n_pairs)]
        y_off = [_dot_t(cms[j // 2], hps[j].astype(BF16)) for j in range(n_pairs)]
        last = acum[BLOCK - 1:BLOCK, :]
        e_last = jnp.exp(last)
        w_all = _expand_heads(jnp.exp(last - acum) * dt, ex_ref[...])
        e_all = _expand_heads(jnp.exp(acum), ex_ref[...])
        xs_pairs = [xs[:, j * LANES:(j + 1) * LANES] for j in range(n_pairs)]
        for j in range(n_pairs):
            h0_, h1_ = 2 * j, 2 * j + 1
            xw = xs_pairs[j] * w_all[:, j * LANES:(j + 1) * LANES]
            s_new = _dot_t0(xw.astype(BF16), bms[j // 2])
            cd_pair = jnp.where(top, e_last[:, h0_:h0_ + 1], e_last[:, h1_:h1_ + 1])
            hst[h0_:h0_ + 2] = (hps[j] * cd_pair + s_new).reshape(2, HEAD_DIM, D_STATE)
        acum_t = acum.T
        dt_t = dt.T
        y_parts = []
        for j in range(n_pairs):
            xs_b = xs_pairs[j].astype(BF16)
            yd = []
            for hh in range(2):
                h = 2 * j + hh
                seg = acum[:, h:h + 1] - acum_t[h:h + 1, :]
                dec = jnp.exp(jnp.where(causal, seg, NEG)) * dt_t[h:h + 1, :]
                yd.append(_dot((cbs[j // 2] * dec).astype(BF16), xs_b))
            y_parts.append(jnp.where(lo, yd[0], yd[1]) + y_off[j] * e_all[:, j * LANES:(j + 1) * LANES])
        y = jnp.concatenate(y_parts, axis=1) + dsk_ref[...] * xs
        ob_ref[rs, :] = _gated_group_norm(y, z_ref[rs, :], ng_ref[...])

    @pl.when(c == pl.num_programs(1) - 1)
    def _():
        hl_ref[...] = hst[...]


def _ssd_prompt(xbc, z, dt, dtb, alog, dsk, ng, tri, ex, l, depth, n, nc, bufs):
    t = xbc.shape[0]
    nc = nc // SSD_SUB
    rows = SSD_SUB * BLOCK
    row = lambda i, c: (i * nc + c, 0)
    layer = lambda i, c: (l, 0, 0)
    n_in = 9
    body, xspecs, xargs, k_alias = _stacked(_ssd_prompt_kernel, n_in, bufs)
    st = (N_SSM_HEADS, HEAD_DIM, D_STATE)
    return pl.pallas_call(
        body,
        grid=(n, nc),
        out_shape=[jax.ShapeDtypeStruct((t, D_INNER), F32),
                   jax.ShapeDtypeStruct((depth, n) + st, F32)],
        in_specs=[pl.BlockSpec((rows, CONV_DIM), row),
                  pl.BlockSpec((rows, D_INNER), row),
                  pl.BlockSpec((rows, LANES), row),
                  pl.BlockSpec((None, 1, LANES), layer),
                  pl.BlockSpec((None, 1, LANES), layer),
                  pl.BlockSpec((None, 1, D_INNER), layer),
                  pl.BlockSpec((None, 1, D_INNER), layer),
                  pl.BlockSpec((BLOCK, BLOCK), lambda i, c: (0, 0)),
                  pl.BlockSpec((LANES, D_INNER), lambda i, c: (0, 0))] + xspecs,
        out_specs=[pl.BlockSpec((rows, D_INNER), row),
                   pl.BlockSpec((None, None) + st, lambda i, c: (l, i, 0, 0, 0))],
        scratch_shapes=[pltpu.VMEM(st, F32)],
        input_output_aliases=_alias_map(n_in, k_alias, 2) if k_alias else {},
        compiler_params=_cparams(("parallel", "arbitrary")),
        name="ssd_prompt",
    )(xbc, z, dt, dtb, alog, dsk, ng, tri, ex, *xargs)


def _attn_a_sample_kernel(sink_ref, q_ref, kn_ref, vn_ref, kt_ref, vt_ref, bc_ref, bn_ref, o_ref, *, nbs, l):
    rb = nbs * SROWS
    lo = _lane_iota((rb, LANES)) < HEAD_DIM
    s0, s1 = q_ref[:, 0:LANES], q_ref[:, LANES:2 * LANES]
    per_seq = lambda v: v.reshape(nbs, SROWS, LANES)
    qm = jnp.concatenate([
        per_seq(jnp.where(lo, s0, 0.0)),
        per_seq(jnp.where(lo, pltpu.roll(s0, HEAD_DIM, 1), 0.0)),
        per_seq(jnp.where(lo, 0.0, pltpu.roll(s1, HEAD_DIM, 1))),
        per_seq(jnp.where(lo, 0.0, s1))], axis=1).astype(BF16)
    kt = kt_ref[...].reshape(nbs, LANES, WIN).astype(BF16)
    vt = vt_ref[...].reshape(nbs, LANES, WIN).astype(BF16)
    kn = kn_ref[...].reshape(nbs, SROWS, LANES).astype(BF16)
    vn = vn_ref[...].reshape(nbs, SROWS, LANES).astype(BF16)
    s_c = jnp.einsum('nqd,ndk->nqk', qm, kt, preferred_element_type=F32) + bc_ref[...]
    s_n = jnp.einsum('nqd,nkd->nqk', qm, kn, preferred_element_type=F32) + bn_ref[...]
    hrow = lax.broadcasted_iota(jnp.int32, (nbs, 4 * SROWS, 1), 1) // SROWS
    sink = jnp.where(hrow == 0, sink_ref[l, 0], jnp.where(hrow == 1, sink_ref[l, 1],
                     jnp.where(hrow == 2, sink_ref[l, 2], sink_ref[l, 3])))
    m = jnp.maximum(jnp.max(s_c, axis=-1, keepdims=True), jnp.max(s_n, axis=-1, keepdims=True))
    m = jnp.maximum(m, sink)
    p_c = jnp.exp2(s_c - m)
    p_n = jnp.exp2(s_n - m)
    den = jnp.sum(p_c, axis=-1, keepdims=True) + jnp.sum(p_n, axis=-1, keepdims=True) + jnp.exp2(sink - m)
    o = (jnp.einsum('nqk,ndk->nqd', p_c.astype(BF16), vt, preferred_element_type=F32)
         + jnp.einsum('nqk,nkd->nqd', p_n.astype(BF16), vn, preferred_element_type=F32)) / den
    o0, o1, o2, o3 = (o[:, i * SROWS:(i + 1) * SROWS, :].reshape(rb, LANES) for i in range(4))
    o_ref[:, 0:LANES] = jnp.where(lo, o0, pltpu.roll(o1, HEAD_DIM, 1))
    o_ref[:, LANES:2 * LANES] = jnp.where(lo, pltpu.roll(o2, HEAD_DIM, 1), o3)


def _attn_a_sample(q8, k8, v8, cache_kt, cache_vt, bias_c, bias_n, sinks, l, nbs):
    ns = cache_kt.shape[1]
    tok = lambda i: (i, 0)
    cache = pl.BlockSpec((None, nbs, 2, HEAD_DIM, WIN), lambda i: (l, i, 0, 0, 0))
    c3 = lambda i: (0, 0, 0)
    return pl.pallas_call(
        functools.partial(_attn_a_sample_kernel, nbs=nbs, l=l),
        grid=(ns // nbs,),
        out_shape=jax.ShapeDtypeStruct((ns * SROWS, 256), F32),
        in_specs=[pl.BlockSpec(memory_space=pltpu.SMEM),
                  pl.BlockSpec((nbs * SROWS, 256), tok),
                  pl.BlockSpec((nbs * SROWS, LANES), tok),
                  pl.BlockSpec((nbs * SROWS, LANES), tok),
                  cache, cache,
                  pl.BlockSpec((1, 4 * SROWS, WIN), c3),
                  pl.BlockSpec((1, 4 * SROWS, SROWS), c3)],
        out_specs=pl.BlockSpec((nbs * SROWS, 256), tok),
        compiler_params=_cparams(("parallel",)),
        name="attn_a_sample",
    )(sinks, q8, k8, v8, cache_kt, cache_vt, bias_c, bias_n)


def _softmax_parts(parts):
    m = None
    for s in parts:
        mm = jnp.max(s, axis=-1, keepdims=True)
        m = mm if m is None else jnp.maximum(m, mm)
    ps = [jnp.exp2(s - m) for s in parts]
    den = None
    for p in ps:
        dd = jnp.sum(p, axis=-1, keepdims=True)
        den = dd if den is None else den + dd
    return ps, den, m + jnp.log2(den)


def _attn_c_sample_kernel(q_ref, kn_ref, vn_ref, kt_ref, vt_ref, b12_ref, b3_ref, bn_ref, o_ref, *, nbs, lc):
    lo = _lane_iota((SROWS, LANES)) < HEAD_DIM
    pr = 2 * SROWS
    units = [(n, j, slice(n * SROWS, (n + 1) * SROWS), slice(j * LANES, (j + 1) * LANES))
             for n in range(nbs) for j in range(2)]
    scores = []
    for n, j, rows, lanes in units:
        qp = q_ref[rows, lanes]
        lhs = jnp.concatenate([jnp.where(lo, qp, 0.0), jnp.where(lo, 0.0, qp)], axis=0).astype(BF16)
        ktp = kt_ref[n, 2 * j:2 * j + 2].reshape(LANES, lc).astype(BF16)
        scores.append((_dot(lhs, ktp), _dot_t(lhs, kn_ref[rows, lanes].astype(BF16))))
    probs = []
    for (n, j, rows, lanes), (s, sn) in zip(units, scores):
        s_near = s[:, lc - NEAR:]
        (p1, p1n), d1, l1 = _softmax_parts([s_near + b12_ref[0, j], sn + bn_ref[0, j]])
        (p2, p2n), d2, l2 = _softmax_parts([s_near + b12_ref[1, j], sn + bn_ref[1, j]])
        (p3, p3n), d3, l3 = _softmax_parts([s + b3_ref[j], sn + bn_ref[2, j]])
        m = jnp.maximum(jnp.maximum(l1, l2), l3)
        w1, w2, w3 = jnp.exp2(l1 - m), jnp.exp2(l2 - m), jnp.exp2(l3 - m)
        wsum = w1 + w2 + w3
        probs.append((jnp.concatenate([p1, p2], axis=0).astype(BF16), p3.astype(BF16),
                      (p1n.astype(BF16), p2n.astype(BF16), p3n.astype(BF16)),
                      (w1 / (d1 * wsum), w2 / (d2 * wsum), w3 / (d3 * wsum))))
    for (n, j, rows, lanes), (p12, p3, pn, wts) in zip(units, probs):
        vtp = vt_ref[n, 2 * j:2 * j + 2].reshape(LANES, lc).astype(BF16)
        vnp = vn_ref[rows, lanes].astype(BF16)
        o12 = _dot_t(p12, vtp[:, lc - NEAR:])
        o = ((o12[0:pr] + _dot(pn[0], vnp)) * wts[0] + (o12[pr:2 * pr] + _dot(pn[1], vnp)) * wts[1]
             + (_dot_t(p3, vtp) + _dot(pn[2], vnp)) * wts[2])
        o_ref[rows, lanes] = jnp.where(lo, o[0:SROWS], o[SROWS:pr])


def _attn_c_sample(q8, k8, v8, cache_kt, cache_vt, b12, b3, bn, l, nbs):
    ns, lc = cache_kt.shape[1], cache_kt.shape[4]
    tok = lambda i: (i, 0)
    cache = pl.BlockSpec((None, nbs, 4, HEAD_DIM, lc), lambda i: (l, i, 0, 0, 0))
    c3 = lambda i: (0, 0, 0)
    c4 = lambda i: (0, 0, 0, 0)
    return pl.pallas_call(
        functools.partial(_attn_c_sample_kernel, nbs=nbs, lc=lc),
        grid=(ns // nbs,),
        out_shape=jax.ShapeDtypeStruct((ns * SROWS, 256), F32),
        in_specs=[pl.BlockSpec((nbs * SROWS, 256), tok),
                  pl.BlockSpec((nbs * SROWS, 256), tok),
                  pl.BlockSpec((nbs * SROWS, 256), tok),
                  cache, cache,
                  pl.BlockSpec((2, 2, 2 * SROWS, NEAR), c4),
                  pl.BlockSpec((2, 2 * SROWS, lc), c3),
                  pl.BlockSpec((3, 2, 2 * SROWS, SROWS), c4)],
        out_specs=pl.BlockSpec((nbs * SROWS, 256), tok),
        compiler_params=_cparams(("parallel",)),
        name="attn_c_sample",
    )(q8, k8, v8, cache_kt, cache_vt, b12, b3, bn)


def _ssd_sample_pre_kernel(xbc_ref, prev_ref, dt_ref, cw_ref, cb_ref, dtb_ref, alog_ref, dsk_ref, ex_ref,
                           yd_ref, eac_ref, xw_ref, bm_ref, cm_ref, cd_ref):
    r = xbc_ref.shape[0]
    rid = lax.broadcasted_iota(jnp.int32, (r, 1), 0) % SROWS
    x = jnp.where(jnp.logical_and(rid >= 1, rid < SVALID), prev_ref[...], xbc_ref[...])
    y = cb_ref[...] + cw_ref[3:4, :] * x
    for d in range(1, 4):
        y = y + cw_ref[3 - d:4 - d, :] * pltpu.roll(x, d, 0)
    xa = _silu(y)
    xs = xa[:, 0:D_INNER]
    bm = xa[:, D_INNER:D_INNER + 2 * D_STATE]
    cm = xa[:, D_INNER + 2 * D_STATE:CONV_DIM]
    valid = rid >= SVALID
    dt = jnp.where(valid, _softplus(dt_ref[...] + dtb_ref[...]), 0.0)
    a = dt * (-jnp.exp(alog_ref[...]))
    acum = a
    rem = jnp.zeros_like(a)
    for d in range(1, 4):
        acum = acum + jnp.where(rid - d >= SVALID, pltpu.roll(a, d, 0), 0.0)
        rem = rem + jnp.where(rid + d < SROWS, pltpu.roll(a, r - d, 0), 0.0)
    ex = ex_ref[...]

    def expand(v):
        return _expand_heads(v, ex)

    lane = _lane_iota((r, LANES))
    y_acc = dsk_ref[...] * xs
    for d in range(4):
        ok = rid - d >= SVALID
        bsh = bm if d == 0 else pltpu.roll(bm, d, 0)
        cb0 = jnp.sum(cm[:, 0:D_STATE] * bsh[:, 0:D_STATE], axis=-1, keepdims=True)
        cb1 = jnp.sum(cm[:, D_STATE:] * bsh[:, D_STATE:], axis=-1, keepdims=True)
        cbh = jnp.where(lane < N_SSM_HEADS // 2, cb0, cb1)
        if d == 0:
            coef = cbh * dt
            xsh = xs
        else:
            dec = jnp.exp(jnp.where(ok, acum - pltpu.roll(acum, d, 0), NEG))
            coef = cbh * dec * pltpu.roll(dt, d, 0)
            xsh = pltpu.roll(xs, d, 0)
        y_acc = y_acc + expand(jnp.where(ok, coef, 0.0)) * xsh
    yd_ref[...] = y_acc
    eac_ref[...] = expand(jnp.exp(acum))
    xw_ref[...] = xs * expand(dt * jnp.exp(rem))
    bm_ref[...] = bm
    cm_ref[...] = cm
    cd_ref[...] = jnp.exp(acum + rem)


def _ssd_sample_pre(xbc8, prev8, dt8, cw8, cb, dtb, alog, dsk, ex, l, rb):
    r = xbc8.shape[0]
    row = lambda i: (i, 0)
    layer = lambda i: (l, 0, 0)
    widths = (D_INNER, D_INNER, D_INNER, 2 * D_STATE, 2 * D_STATE, LANES)
    return pl.pallas_call(
        _ssd_sample_pre_kernel,
        grid=(r // rb,),
        out_shape=[jax.ShapeDtypeStruct((r, w_), F32) for w_ in widths],
        in_specs=[pl.BlockSpec((rb, CONV_DIM), row),
                  pl.BlockSpec((None, rb, CONV_DIM), lambda i: (l, i, 0)),
                  pl.BlockSpec((rb, LANES), row),
                  pl.BlockSpec((None, SUBLANES, CONV_DIM), layer),
                  pl.BlockSpec((None, 1, CONV_DIM), layer),
                  pl.BlockSpec((None, 1, LANES), layer),
                  pl.BlockSpec((None, 1, LANES), layer),
                  pl.BlockSpec((None, 1, D_INNER), layer),
                  pl.BlockSpec((LANES, D_INNER), lambda i: (0, 0))],
        out_specs=[pl.BlockSpec((rb, w_), row) for w_ in widths],
        compiler_params=_cparams(("parallel",)),
        name="ssd_sample_pre",
    )(xbc8, prev8, dt8, cw8, cb, dtb, alog, dsk, ex)


def _ssd_sample_state_kernel(cd_ref, yd_ref, eac_ref, xw_ref, bm_ref, cm_ref, z_ref, h0_ref, ng_ref,
                             ob_ref, hn_ref, *, nbs):
    base = pl.program_id(0) * nbs
    yo = []
    for n in range(nbs):
        rows = slice(n * SROWS, (n + 1) * SROWS)
        parts = []
        for g in range(2):
            hp = h0_ref[n, 4 * g:4 * g + 4].reshape(4 * HEAD_DIM, D_STATE)
            cmg = cm_ref[rows, g * D_STATE:(g + 1) * D_STATE].astype(BF16)
            bmg = bm_ref[rows, g * D_STATE:(g + 1) * D_STATE].astype(BF16)
            parts.append(_dot_t(cmg, hp.astype(BF16)))
            s_new = _dot_t0(xw_ref[rows, g * 256:(g + 1) * 256].astype(BF16), bmg)
            for hh in range(4):
                h = 4 * g + hh
                hn_ref[n, h] = (hp[hh * HEAD_DIM:(hh + 1) * HEAD_DIM, :] * cd_ref[base + n, h]
                                + s_new[hh * HEAD_DIM:(hh + 1) * HEAD_DIM, :])
        yo.append(jnp.concatenate(parts, axis=1))
    y = yd_ref[...] + eac_ref[...] * jnp.concatenate(yo, axis=0)
    ob_ref[...] = _gated_group_norm(y, z_ref[...], ng_ref[...])


def _ssd_sample_state(cd, yd, eac, xw, bm, cm, z8, state, ng, l, depth, nbs, bufs):
    ns = state.shape[1]
    row = lambda i: (i, 0)
    st_spec = pl.BlockSpec((None, nbs, N_SSM_HEADS, HEAD_DIM, D_STATE), lambda i: (l, i, 0, 0, 0))
    rb = nbs * SROWS
    n_in = 9
    body, xspecs, xargs, k_alias = _stacked(functools.partial(_ssd_sample_state_kernel, nbs=nbs), n_in, bufs)
    return pl.pallas_call(
        body,
        grid=(ns // nbs,),
        out_shape=[jax.ShapeDtypeStruct((ns * SROWS, D_INNER), F32),
                   jax.ShapeDtypeStruct(state.shape, F32)],
        in_specs=[pl.BlockSpec(memory_space=pltpu.SMEM),
                  pl.BlockSpec((rb, D_INNER), row),
                  pl.BlockSpec((rb, D_INNER), row),
                  pl.BlockSpec((rb, D_INNER), row),
                  pl.BlockSpec((rb, 2 * D_STATE), row),
                  pl.BlockSpec((rb, 2 * D_STATE), row),
                  pl.BlockSpec((rb, D_INNER), row),
                  st_spec,
                  pl.BlockSpec((None, 1, D_INNER), lambda i: (l, 0, 0))] + xspecs,
        out_specs=[pl.BlockSpec((rb, D_INNER), row), st_spec],
        input_output_aliases=_alias_map(n_in, k_alias, 2) if k_alias else {},
        compiler_params=_cparams(("parallel",)),
        name="ssd_sample_state",
    )(cd, yd, eac, xw, bm, cm, z8, state, ng, *xargs)


def _new_kv_kernel(ka_ref, va_ref, kc_ref, vc_ref, oka_ref, ova_ref, okc_ref, ovc_ref, *, ns):
    def emit(src, dst):
        for t in range(SROWS - SVALID):
            dst[t] = src[pl.ds(SVALID + t, ns, stride=SROWS), :].T

    emit(kc_ref, okc_ref)
    emit(vc_ref, ovc_ref)

    @pl.when(pl.program_id(1) == 0)
    def _():
        emit(ka_ref, oka_ref)
        emit(va_ref, ova_ref)


def _new_kv_states(ka, va, kc, vc, ns):
    depth = ka.shape[0]
    nt = SROWS - SVALID
    narrow_in = pl.BlockSpec((None, ns * SROWS, LANES), lambda l, j: (l, 0, 0))
    wide_in = pl.BlockSpec((None, ns * SROWS, LANES), lambda l, j: (l, 0, j))
    narrow_out = pl.BlockSpec((None, nt, LANES, ns), lambda l, j: (l, 0, 0, 0))
    wide_out = pl.BlockSpec((None, nt, LANES, ns), lambda l, j: (l, 0, j, 0))
    shape = lambda w: jax.ShapeDtypeStruct((depth, nt, w, ns), F32)
    return pl.pallas_call(
        functools.partial(_new_kv_kernel, ns=ns),
        grid=(depth, 2),
        out_shape=[shape(LANES), shape(LANES), shape(2 * LANES), shape(2 * LANES)],
        in_specs=[narrow_in, narrow_in, wide_in, wide_in],
        out_specs=[narrow_out, narrow_out, wide_out, wide_out],
        compiler_params=_cparams(("parallel", "arbitrary")),
        name="new_kv_states",
    )(ka, va, kc, vc)


def _prompt_bias_item(dil, col0):
    qi = np.arange(BLOCK)[:, None]
    kj = np.arange(2 * BLOCK)[None, :]
    dist = BLOCK + qi - kj
    ok = (dist >= 0) & (dist <= WIN)
    variants = [_masked_bucket(dist * dil, ok), _masked_bucket(dist * dil, ok & (kj >= BLOCK))]
    segs = [((2 * h + v) * BLOCK, (2 * h + v + 1) * BLOCK, col0 + h) for h in range(4) for v in range(2)]
    return np.concatenate(variants * 4, axis=0), segs


def _sample_bias_item(dist_of, ok_of, cols, col0):
    t = np.arange(SROWS)[:, None] - SVALID
    bkt = np.where(t >= 0, _masked_bucket(dist_of(t, cols), ok_of(t, cols)), 0).astype(np.int32)
    segs = [(h * SROWS, (h + 1) * SROWS, col0 + h) for h in range(4)]
    return np.concatenate([bkt] * 4, axis=0), segs


def _cache_bias_item(length, window, dil, span, col0):
    dist_of = lambda t, pos: length + t - pos
    ok_of = lambda t, pos: (dist_of(t, pos) >= 0) & (dist_of(t, pos) <= window) & (dist_of(t, pos) % dil == 0)
    return _sample_bias_item(dist_of, ok_of, np.arange(length - span, length)[None, :], col0)


def _new_rows_bias_item(same_token_only, col0):
    dist_of = lambda t, r: t - (r - SVALID)
    if same_token_only:
        ok_of = lambda t, r: (r >= SVALID) & (dist_of(t, r) == 0)
    else:
        ok_of = lambda t, r: (r >= SVALID) & (dist_of(t, r) >= 0)
    return _sample_bias_item(dist_of, ok_of, np.arange(SROWS)[None, :], col0)


Tiling = collections.namedtuple("Tiling", "tm_prompt tm_sample seqs_a seqs_c seqs_b rows_b")


def _tiling(ns):
    rows = ns * SROWS
    return Tiling(tm_prompt=512, tm_sample=min(512, rows), seqs_a=min(16, ns), seqs_c=min(2, ns),
                  seqs_b=min(16, ns), rows_b=min(256, rows))
def kernel(x_prompt, x_sample, cache_a_k, cache_a_v, cache_c_k, cache_c_v, state_ssm, state_conv,
           norm_mix_g, w_in, a_q_norm_g, a_k_norm_g, a_sinks, c_q_norm_g, c_k_norm_g, rel_bias,
           conv_w, conv_b, dt_bias, a_log, d_skip, ssm_norm_g, w_out, norm_mlp_g, w_up, w_down):
    depth = w_in.shape[0]
    n, seq, _ = x_prompt.shape
    ns, ts, _ = x_sample.shape
    lc = cache_c_k.shape[2]
    assert ts == SROWS - SVALID and seq % (BLOCK * 16) == 0
    assert cache_a_k.shape[2] == WIN and all(w <= lc for w, _ in C_BRANCHES)
    nb = seq // BLOCK

    w_in_b = w_in.astype(BF16)
    w_out_b, w_up_b, w_down_b = w_out.astype(BF16), w_up.astype(BF16), w_down.astype(BF16)
    ones = jnp.ones((depth, 128), F32)
    q_scale = ATTN_SCALE * LOG2E
    gain = jnp.concatenate([jnp.tile(a_q_norm_g, (1, 4)) * q_scale, jnp.tile(a_k_norm_g, (1, 2)), ones,
                            jnp.tile(c_q_norm_g, (1, 4)) * q_scale, jnp.tile(c_k_norm_g, (1, 4))], axis=1)
    rel_bias = rel_bias * LOG2E
    a_sinks = a_sinks * LOG2E
    tri = (jnp.arange(BLOCK)[None, :] <= jnp.arange(BLOCK)[:, None]).astype(F32)
    ex = (jnp.arange(LANES)[:, None] == (jnp.arange(D_INNER) // HEAD_DIM)[None, :]).astype(BF16)
    cw8 = jnp.pad(conv_w, ((0, 0), (0, SUBLANES - conv_w.shape[1]), (0, 0)))
    pad_h = lambda v: jnp.pad(v, ((0, 0), (0, LANES - N_SSM_HEADS)))[:, None, :]
    vec = lambda v: v[:, None, :]
    dtb, alog = pad_h(dt_bias), pad_h(a_log)
    dsk = vec(jnp.repeat(d_skip, HEAD_DIM, axis=1))
    g_mix, g_mlp, gain, cb, ng = vec(norm_mix_g), vec(norm_mlp_g), vec(gain), vec(conv_b), vec(ssm_norm_g)

    (w1, d1), (w2, d2), (w3, d3) = C_BRANCHES
    items = ([_prompt_bias_item(1, 0)] + [_prompt_bias_item(d, 4) for d in C_DILS]
             + [_cache_bias_item(WIN, WIN, 1, WIN, 0), _new_rows_bias_item(False, 0)]
             + [_cache_bias_item(lc, w1, d1, NEAR, 4), _cache_bias_item(lc, w2, d2, NEAR, 4),
                _cache_bias_item(lc, w3, d3, lc, 4), _new_rows_bias_item(False, 4), _new_rows_bias_item(True, 4)])
    pa, pc1, pc4, pc16, sa_c, sa_n, sc1, sc2, sc3, sn_all, sn_same = _expand_biases(rel_bias, items)
    by_variant = lambda b: b.reshape(4, 2, BLOCK, 2 * BLOCK)
    by_pair = lambda b: b.reshape(2, 2 * SROWS, b.shape[-1])
    bias_a = by_variant(pa)
    bias_c = jnp.stack([by_variant(pc1), by_variant(pc4), by_variant(pc16)])
    sbias_a_c, sbias_a_n = sa_c[None], sa_n[None]
    sb12 = jnp.stack([by_pair(sc1), by_pair(sc2)])
    sb3 = by_pair(sc3)
    sbn = jnp.stack([by_pair(sn_all), by_pair(sn_same), by_pair(sn_same)])

    cak_t, cav_t = cache_a_k.transpose(0, 1, 3, 4, 2), cache_a_v.transpose(0, 1, 3, 4, 2)
    cck_t, ccv_t = cache_c_k.transpose(0, 1, 3, 4, 2), cache_c_v.transpose(0, 1, 3, 4, 2)

    hp = x_prompt.reshape(n * seq, D_MODEL)
    hs = jnp.pad(x_sample, ((0, 0), (SVALID, 0), (0, 0))).reshape(ns * SROWS, D_MODEL)
    tl = _tiling(ns)
    prev8 = jnp.pad(state_conv, ((0, 0), (0, 0), (1, SROWS - 4), (0, 0))).reshape(depth, ns * SROWS, CONV_DIM)

    p_ak = p_ck = p_ssm = s_ssm = None
    p_conv, s_small = [], [[] for _ in range(5)]
    for l in range(depth):
        qa, ka, va, qc, kc, vc, z, xbc, dt, tail = _in_proj(hp, g_mix, w_in_b, gain, cw8, cb, l, tl.tm_prompt,
                                                            seq // tl.tm_prompt)
        oa_p, *p_ak = _attn_a_prompt(qa, ka, va, bias_a, a_sinks, l, depth, n, seq, p_ak)
        oc_p, *p_ck = _attn_c_prompt(qc, kc, vc, bias_c, l, depth, n, seq, p_ck)
        ob_p, *p_ssm = _ssd_prompt(xbc, z, dt, dtb, alog, dsk, ng, tri, ex, l, depth, n, nb, p_ssm)
        p_conv.append(tail)

        qa, ka, va, qc, kc, vc, z, xbc, dt, _ = _in_proj(hs, g_mix, w_in_b, gain, cw8, cb, l, tl.tm_sample, 0)
        oa = _attn_a_sample(qa, ka, va, cak_t, cav_t, sbias_a_c, sbias_a_n, a_sinks, l, tl.seqs_a)
        oc = _attn_c_sample(qc, kc, vc, cck_t, ccv_t, sb12, sb3, sbn, l, tl.seqs_c)
        yd, eac, xw, bm, cm, cd = _ssd_sample_pre(xbc, prev8, dt, cw8, cb, dtb, alog, dsk, ex, l, tl.rows_b)
        cd_s = cd.reshape(ns, SROWS, LANES)[:, SROWS - 1, :N_SSM_HEADS]
        ob, *s_ssm = _ssd_sample_state(cd_s, yd, eac, xw, bm, cm, z, state_ssm, ng, l, depth, tl.seqs_b, s_ssm)
        for dst, val in zip(s_small, (ka, va, kc, vc, xbc)):
            dst.append(val)

        hp, hs = _out_mlp(hp, oa_p, ob_p, oc_p, hs, oa, ob, oc, w_out_b, g_mlp, w_up_b, w_down_b, l,
                          math.gcd(tl.tm_prompt, ns * SROWS))

    y_prompt = hp.reshape(n, seq, D_MODEL)
    y_sample = hs.reshape(ns, SROWS, D_MODEL)[:, SVALID:]
    unminor = lambda buf, heads: buf.reshape(depth, n, heads, HEAD_DIM, -1).transpose(0, 1, 4, 2, 3)
    p_state = (unminor(p_ak[0], 2), unminor(p_ak[1], 2), unminor(p_ck[0], 4), unminor(p_ck[1], 4),
               p_ssm[0], jnp.stack(p_conv)[:, :, SUBLANES - 3:])
    new_kv = _new_kv_states(*(jnp.stack(v) for v in s_small[:4]), ns)
    s_kv = tuple(b.reshape(depth, SROWS - SVALID, hh, HEAD_DIM, ns).transpose(0, 4, 1, 2, 3)
                 for b, hh in zip(new_kv, (2, 2, 4, 4)))
    s_conv = jnp.stack([x.reshape(ns, SROWS, CONV_DIM)[:, SROWS - 3:] for x in s_small[4]])
    s_state = s_kv + (s_ssm[0], s_conv)
    return (y_prompt, y_sample) + p_state + s_state
```

```python
import collections
import functools
import math

import jax
import jax.numpy as jnp
import numpy as np
from jax import lax
from jax.experimental import pallas as pl
from jax.experimental.pallas import tpu as pltpu

F32 = jnp.float32
BF16 = jnp.bfloat16
HIGHEST = lax.Precision.HIGHEST

D_MODEL = 1024
HEAD_DIM = 64
LANES = 128
SUBLANES = 8
BLOCK = 128
WIN = 128
D_INNER = 512
D_STATE = 128
N_SSM_HEADS = 8
CONV_DIM = 1024
D_FF = 4096
NUM_BUCKETS = 32
REL_MAX_DIST = 2048
EPS = 1e-6
ATTN_SCALE = HEAD_DIM ** -0.5
LOG2E = math.log2(math.e)
NEG = -1e30
C_BRANCHES = ((128, 1), (512, 4), (2048, 16))
C_DILS = tuple(d for _, d in C_BRANCHES)
SROWS = 8
SVALID = 4
NEAR = 512
C_STAGE = 4
C_UNROLL = 8
A_UNROLL = 4
SSD_SUB = 4

COL_QA, COL_KA, COL_VA, COL_QC, COL_KC, COL_VC, COL_Z, COL_XBC, COL_DT, COL_END = (
    0, 256, 384, 512, 768, 1024, 1280, 1792, 2816, 2944)
IN_COLS = 2824
V7X_VMEM_BYTES = 64 * 1024 * 1024
VMEM_LIMIT = V7X_VMEM_BYTES * 7 // 8


def _cparams(sem):
    return pltpu.CompilerParams(dimension_semantics=sem, vmem_limit_bytes=VMEM_LIMIT)


def _dot(a, b):
    return jnp.dot(a, b, preferred_element_type=F32)


def _dot_t(a, b):
    return lax.dot_general(a, b, (((1,), (1,)), ((), ())), preferred_element_type=F32)


def _dot_t0(a, b):
    return lax.dot_general(a, b, (((0,), (0,)), ((), ())), preferred_element_type=F32)


def _lane_iota(shape):
    return lax.broadcasted_iota(jnp.int32, shape, len(shape) - 1)


def _silu(x):
    hx = 0.5 * x
    return hx + hx * jnp.tanh(hx)


def _softplus(x):
    return jnp.maximum(x, 0.0) + jnp.log(1.0 + jnp.exp(-jnp.abs(x)))


def _stacked(body, n_in, bufs):
    if bufs is None:
        return body, [], [], 0
    k = len(bufs)

    def wrapped(*refs):
        return body(*refs[:n_in], *refs[n_in + k:])

    return wrapped, [pl.BlockSpec(memory_space=pl.ANY)] * k, list(bufs), k


def _alias_map(n_in, k, n_out):
    return {n_in + i: n_out - k + i for i in range(k)}


def _bias_kernel(tab_ref, *refs, segments):
    n = len(segments)
    for bkt_ref, o_ref, segs in zip(refs[:n], refs[n:], segments):
        for r0, r1, col in segs:
            bkt = bkt_ref[r0:r1, :]
            acc = jnp.full(bkt.shape, NEG, F32)
            for b in range(NUM_BUCKETS):
                acc = jnp.where(bkt == b, tab_ref[b, col], acc)
            o_ref[r0:r1, :] = acc


def _expand_biases(table, items):
    bkts = [jnp.asarray(b) for b, _ in items]
    full = lambda b: pl.BlockSpec(b.shape, lambda: (0, 0))
    return pl.pallas_call(
        functools.partial(_bias_kernel, segments=[s for _, s in items]),
        out_shape=[jax.ShapeDtypeStruct(b.shape, F32) for b in bkts],
        in_specs=[pl.BlockSpec(memory_space=pltpu.SMEM)] + [full(b) for b in bkts],
        out_specs=[full(b) for b in bkts],
        name="bias_expand",
    )(table, *bkts)


def _t5_bucket(dist):
    max_exact = NUM_BUCKETS // 2
    d = np.maximum(dist, 0)
    df = np.maximum(d, 1).astype(np.float32)
    ratio = np.log(df / np.float32(max_exact)) / np.float32(math.log(REL_MAX_DIST / max_exact))
    large = max_exact + (ratio * np.float32(NUM_BUCKETS - max_exact)).astype(np.int32)
    return np.where(d < max_exact, d, np.minimum(large, NUM_BUCKETS - 1))


def _masked_bucket(dist, valid):
    return np.where(valid, _t5_bucket(dist), -1).astype(np.int32)


def _head_norm(p, gain, slabs):
    lo = _lane_iota((p.shape[0], LANES)) < HEAD_DIM
    out = []
    for s in range(slabs):
        x = p[:, s * LANES:(s + 1) * LANES]
        sq = x * x
        s_lo = jnp.sum(jnp.where(lo, sq, 0.0), axis=-1, keepdims=True)
        s_hi = jnp.sum(jnp.where(lo, 0.0, sq), axis=-1, keepdims=True)
        ss = jnp.where(lo, s_lo, s_hi)
        out.append(x * lax.rsqrt(ss * (1.0 / HEAD_DIM) + EPS) * gain[:, s * LANES:(s + 1) * LANES])
    return out[0] if slabs == 1 else jnp.concatenate(out, axis=1)


def _in_proj_kernel(h_ref, g_ref, w_ref, gain_ref, cw_ref, cb_ref,
                    qa_ref, ka_ref, va_ref, qc_ref, kc_ref, vc_ref, z_ref, xbc_ref, dt_ref, tail_ref, xbuf,
                    *, conv_tiles):
    if conv_tiles:
        @pl.when(pl.program_id(0) % conv_tiles == 0)
        def _():
            xbuf[0:SUBLANES, :] = jnp.zeros((SUBLANES, CONV_DIM), F32)

    tm = h_ref.shape[0]
    x = h_ref[...]
    ms = jnp.mean(x * x, axis=-1, keepdims=True)
    u = (x * lax.rsqrt(ms + EPS) * g_ref[...]).astype(BF16)

    def proj(c0, c1):
        return _dot(u, w_ref[:, c0:c1])

    xbc = proj(COL_XBC, COL_DT)
    tail = xbc[tm - SUBLANES:tm, :]
    tail_ref[...] = tail
    if conv_tiles:
        xbuf[SUBLANES:SUBLANES + tm, :] = xbc
        y = cb_ref[...] + cw_ref[3:4, :] * xbc
        for kk in range(3):
            y = y + cw_ref[kk:kk + 1, :] * xbuf[SUBLANES - 3 + kk:SUBLANES - 3 + kk + tm, :]
        xbuf[0:SUBLANES, :] = tail
        xbc_ref[...] = y
    else:
        xbc_ref[...] = xbc
    groups = [COL_QA, COL_KA, COL_QC, COL_KC]
    raw = [proj(c0, c0 + 256) for c0 in groups]
    va_ref[...] = raw[1][:, COL_VA - COL_KA:]
    vc_ref[...] = proj(COL_VC, COL_Z)
    z_ref[...] = proj(COL_Z, COL_XBC)
    qa_ref[...] = _head_norm(raw[0], gain_ref[:, COL_QA:COL_QA + 256], 2)
    ka_ref[...] = _head_norm(raw[1], gain_ref[:, COL_KA:COL_KA + 256], 1)
    qc_ref[...] = _head_norm(raw[2], gain_ref[:, COL_QC:COL_QC + 256], 2)
    kc_ref[...] = _head_norm(raw[3], gain_ref[:, COL_KC:COL_KC + 256], 2)
    dt_ref[...] = jnp.zeros(dt_ref.shape, F32)
    dt_ref[:, 0:N_SSM_HEADS] = proj(COL_DT, IN_COLS)


def _in_proj(h, g, w, gain, cw8, cb, l, tm, conv_tiles):
    t = h.shape[0]
    widths = (256, 128, 128, 256, 256, 256, 512, 1024, 128)
    layer = lambda i: (l, 0, 0)
    n_tail = t // tm // conv_tiles if conv_tiles else t // tm
    per_seq = conv_tiles if conv_tiles else 1
    return pl.pallas_call(
        functools.partial(_in_proj_kernel, conv_tiles=conv_tiles),
        grid=(t // tm,),
        out_shape=[jax.ShapeDtypeStruct((t, w_), F32) for w_ in widths]
        + [jax.ShapeDtypeStruct((n_tail, SUBLANES, CONV_DIM), F32)],
        in_specs=[pl.BlockSpec((tm, D_MODEL), lambda i: (i, 0)),
                  pl.BlockSpec((None, 1, D_MODEL), layer),
                  pl.BlockSpec((None, D_MODEL, IN_COLS), layer),
                  pl.BlockSpec((None, 1, D_MODEL), layer),
                  pl.BlockSpec((None, SUBLANES, CONV_DIM), layer),
                  pl.BlockSpec((None, 1, CONV_DIM), layer)],
        out_specs=[pl.BlockSpec((tm, w_), lambda i: (i, 0)) for w_ in widths]
        + [pl.BlockSpec((None, SUBLANES, CONV_DIM), lambda i: (i // per_seq, 0, 0))],
        scratch_shapes=[pltpu.VMEM((SUBLANES + tm, CONV_DIM), F32)],
        compiler_params=_cparams(("arbitrary",)),
        name="in_proj",
    )(h, g, w, gain, cw8, cb)


def _out_mlp_kernel(hp_ref, oap_ref, obp_ref, ocp_ref, hs_ref, oas_ref, obs_ref, ocs_ref,
                    wo_ref, g_ref, wu_ref, wd_ref, op_ref, os_ref, *, prompt_tiles):
    def tile(h_ref, oa_ref, ob_ref, oc_ref, o_ref):
        acc = _dot(oa_ref[...].astype(BF16), wo_ref[0:256, :])
        acc += _dot(ob_ref[...].astype(BF16), wo_ref[256:768, :])
        acc += _dot(oc_ref[...].astype(BF16), wo_ref[768:1024, :])
        h2 = h_ref[...] + acc
        ms = jnp.mean(h2 * h2, axis=-1, keepdims=True)
        u = (h2 * lax.rsqrt(ms + EPS) * g_ref[...]).astype(BF16)
        ff_chunk = 1024
        mlp = jnp.zeros_like(h2)
        for c in range(D_FF // ff_chunk):
            a = _dot(u, wu_ref[:, c * ff_chunk:(c + 1) * ff_chunk])
            a = jnp.square(jnp.maximum(a, 0.0)).astype(BF16)
            mlp += _dot(a, wd_ref[c * ff_chunk:(c + 1) * ff_chunk, :])
        o_ref[...] = h2 + mlp

    is_prompt = pl.program_id(0) < prompt_tiles

    @pl.when(is_prompt)
    def _():
        tile(hp_ref, oap_ref, obp_ref, ocp_ref, op_ref)

    @pl.when(jnp.logical_not(is_prompt))
    def _():
        tile(hs_ref, oas_ref, obs_ref, ocs_ref, os_ref)


def _out_mlp(hp, oap, obp, ocp, hs, oas, obs, ocs, wo, g, wu, wd, l, tm):
    tp, ts = hp.shape[0], hs.shape[0]
    prompt_tiles = tp // tm
    layer = lambda i: (l, 0, 0)
    prow = lambda i: (jnp.minimum(i, prompt_tiles - 1), 0)
    srow = lambda i: (jnp.maximum(i - prompt_tiles, 0), 0)
    tok = lambda idx: [pl.BlockSpec((tm, w_), idx) for w_ in (D_MODEL, 256, 512, 256)]
    weights = lambda shape: pl.BlockSpec((None,) + shape, layer, pipeline_mode=pl.Buffered(1))
    return pl.pallas_call(
        functools.partial(_out_mlp_kernel, prompt_tiles=prompt_tiles),
        grid=(prompt_tiles + ts // tm,),
        out_shape=[jax.ShapeDtypeStruct((tp, D_MODEL), F32), jax.ShapeDtypeStruct((ts, D_MODEL), F32)],
        in_specs=tok(prow) + tok(srow) + [weights((D_MODEL, D_MODEL)),
                                          pl.BlockSpec((None, 1, D_MODEL), layer),
                                          weights((D_MODEL, D_FF)),
                                          weights((D_FF, D_MODEL))],
        out_specs=[pl.BlockSpec((tm, D_MODEL), prow), pl.BlockSpec((tm, D_MODEL), srow)],
        compiler_params=_cparams(("arbitrary",)),
        name="out_mlp",
    )(hp, oap, obp, ocp, hs, oas, obs, ocs, wo, g, wu, wd)


def _attn_a_prompt_kernel(sink_ref, q_ref, k_ref, v_ref, bias_ref, o_ref, kt_ref, vt_ref,
                          k_st, k_sw, v_st, v_sw, *, l, seq):
    lo = _lane_iota((BLOCK, LANES)) < HEAD_DIM
    k = k_ref[...]
    v = v_ref[...]
    for st in (k_st, k_sw, v_st, v_sw):
        st[0:BLOCK, :] = jnp.zeros((BLOCK, LANES), BF16)
    k_st[BLOCK:, :] = k.astype(BF16)
    v_st[BLOCK:, :] = v.astype(BF16)
    k_sw[BLOCK:, :] = pltpu.roll(k, HEAD_DIM, 1).astype(BF16)
    v_sw[BLOCK:, :] = pltpu.roll(v, HEAD_DIM, 1).astype(BF16)
    kt_ref[...] = k_ref[seq - WIN:seq, :].T
    vt_ref[...] = v_ref[seq - WIN:seq, :].T

    def body(g, carry):
        scores, where = [], []
        for u in range(A_UNROLL):
            b = g * A_UNROLL + u
            cur = pl.ds(pl.multiple_of(b * BLOCK, BLOCK), BLOCK)
            both = pl.ds(pl.multiple_of(b * BLOCK, BLOCK), 2 * BLOCK)
            variant = jnp.where(b == 0, 1, 0)
            where.append((cur, both))
            for h in range(4):
                slab = q_ref[cur, (h // 2) * LANES:(h // 2 + 1) * LANES]
                qm = jnp.where(lo if h % 2 == 0 else ~lo, slab, 0.0).astype(BF16)
                kk = k_st if h in (0, 3) else k_sw
                scores.append(_dot_t(qm, kk[both, :]) + bias_ref[h, variant])
        probs = []
        for i, s in enumerate(scores):
            sink = sink_ref[l, i % 4]
            m = jnp.maximum(jnp.max(jnp.maximum(s[:, 0:BLOCK], s[:, BLOCK:2 * BLOCK]), axis=-1, keepdims=True), sink)
            p = jnp.exp2(s - m)
            den = jnp.sum(p[:, 0:BLOCK] + p[:, BLOCK:2 * BLOCK], axis=-1, keepdims=True) + jnp.exp2(sink - m)
            probs.append((p.astype(BF16), den))
        for u, (cur, both) in enumerate(where):
            outs = []
            for h in range(4):
                p, den = probs[4 * u + h]
                vv = v_st if h in (0, 3) else v_sw
                outs.append(_dot(p, vv[both, :]) / den)
            o_ref[cur, 0:LANES] = jnp.where(lo, outs[0], outs[1])
            o_ref[cur, LANES:2 * LANES] = jnp.where(lo, outs[2], outs[3])
        return carry

    lax.fori_loop(0, seq // BLOCK // A_UNROLL, body, 0)


def _attn_a_prompt(q, k, v, bias, sinks, l, depth, n, seq, bufs):
    t = q.shape[0]
    n_in = 5
    body, xspecs, xargs, k_alias = _stacked(functools.partial(_attn_a_prompt_kernel, l=l, seq=seq), n_in, bufs)
    st_shape = jax.ShapeDtypeStruct((depth, n, LANES, WIN), F32)
    st_spec = pl.BlockSpec((None, None, LANES, WIN), lambda i: (l, i, 0, 0))
    tok = lambda w: pl.BlockSpec((seq, w), lambda i: (i, 0))
    return pl.pallas_call(
        body,
        grid=(n,),
        out_shape=[jax.ShapeDtypeStruct((t, 256), F32), st_shape, st_shape],
        in_specs=[pl.BlockSpec(memory_space=pltpu.SMEM), tok(256), tok(LANES), tok(LANES),
                  pl.BlockSpec((4, 2, BLOCK, 2 * BLOCK), lambda i: (0, 0, 0, 0))] + xspecs,
        out_specs=[tok(256), st_spec, st_spec],
        scratch_shapes=[pltpu.VMEM((BLOCK + seq, LANES), BF16)] * 4,
        input_output_aliases=_alias_map(n_in, k_alias, 3) if k_alias else {},
        compiler_params=_cparams(("parallel",)),
        name="attn_a_prompt",
    )(sinks, q, k, v, bias, *xargs)


def _attn_c_prompt_kernel(q_ref, k_ref, v_ref, bias_ref, o_ref, kt_ref, vt_ref, ob_scr, lse_scr,
                          q_st, k_st, v_st, *, seq):
    lane = _lane_iota((BLOCK, LANES))
    lo = lane < HEAD_DIM
    kt_ref[...] = k_ref[...].T
    vt_ref[...] = v_ref[...].T
    part = seq // C_STAGE
    for src, dst in ((q_ref, q_st), (k_ref, k_st), (v_ref, v_st)):
        for rho in range(C_STAGE):
            dst[rho * part:(rho + 1) * part, :] = src[pl.ds(rho, part, stride=C_STAGE), :]

    def blocks(br, staged, stride, where, has_prev):
        rows = lambda ref, s: ref[pl.ds(s, BLOCK, stride=stride), :]
        q_src, k_src, v_src = (q_st, k_st, v_st) if staged else (q_ref, k_ref, v_ref)
        scores, vals = [], []
        for qs, ps, first in where:
            q = rows(q_src, qs)
            if has_prev:
                k2 = jnp.concatenate([rows(k_src, ps), rows(k_src, qs)], axis=0).astype(BF16)
                vals.append(jnp.concatenate([rows(v_src, ps), rows(v_src, qs)], axis=0).astype(BF16))
                variant = jnp.where(first, 1, 0)
            else:
                k2 = rows(k_src, qs).astype(BF16)
                vals.append(rows(v_src, qs).astype(BF16))
            for hh in range(2):
                qm = jnp.where(lo if hh == 0 else ~lo, q, 0.0).astype(BF16)
                bias = bias_ref[br, hh, variant] if has_prev else bias_ref[br, hh, 0, :, BLOCK:2 * BLOCK]
                scores.append(_dot_t(qm, k2) + bias)
        probs = []
        for s in scores:
            sm = jnp.maximum(s[:, 0:BLOCK], s[:, BLOCK:2 * BLOCK]) if has_prev else s
            m = jnp.max(sm, axis=-1, keepdims=True)
            p = jnp.exp2(s - m)
            pm = p[:, 0:BLOCK] + p[:, BLOCK:2 * BLOCK] if has_prev else p
            probs.append((p.astype(BF16), m, jnp.sum(pm, axis=-1, keepdims=True)))
        for u, (qs, _, _) in enumerate(where):
            outs, lses = [], []
            for hh in range(2):
                p, m, den = probs[2 * u + hh]
                outs.append(_dot(p, vals[u]) / den)
                lses.append(jnp.broadcast_to(m + jnp.log2(den), (BLOCK, LANES)))
            ob_scr[br, pl.ds(qs, BLOCK, stride=stride), :] = jnp.where(lo, outs[0], outs[1])
            lse_scr[br, pl.ds(qs, BLOCK, stride=stride), :] = jnp.where(lo, lses[0], lses[1])

    for br, dil in enumerate(C_DILS):
        nb = seq // dil // BLOCK
        staged = dil % C_STAGE == 0
        stride = dil // C_STAGE if staged else dil

        def body(g, carry, br=br, dil=dil, nb=nb, staged=staged, stride=stride):
            where = []
            for u in range(C_UNROLL):
                i = g * C_UNROLL + u
                res = i % dil
                b = i // dil
                base = (res % C_STAGE) * part + res // C_STAGE if staged else res
                step = BLOCK * stride
                where.append((base + b * step, base + jnp.maximum(b - 1, 0) * step, b == 0))
            blocks(br, staged, stride, where, nb > 1)
            return carry

        lax.fori_loop(0, dil * nb // C_UNROLL, body, 0)

    def merge(i, carry):
        rho = i // (part // BLOCK)
        jb = i % (part // BLOCK)
        st = pl.ds(pl.multiple_of(i * BLOCK, BLOCK), BLOCK)
        nat = pl.ds(rho + C_STAGE * BLOCK * jb, BLOCK, stride=C_STAGE)
        ls, os = [], []
        for br, dil in enumerate(C_DILS):
            rows = st if dil % C_STAGE == 0 else nat
            ls.append(lse_scr[br, rows, :])
            os.append(ob_scr[br, rows, :])
        m = jnp.maximum(jnp.maximum(ls[0], ls[1]), ls[2])
        ws = [jnp.exp2(l_ - m) for l_ in ls]
        num = ws[0] * os[0] + ws[1] * os[1] + ws[2] * os[2]
        o_ref[nat, :] = num / (ws[0] + ws[1] + ws[2])
        return carry

    lax.fori_loop(0, seq // BLOCK, merge, 0)


def _attn_c_prompt(q, k, v, bias, l, depth, n, seq, bufs):
    t = q.shape[0]
    blk = pl.BlockSpec((seq, LANES), lambda i, hp: (i, hp))
    n_in = 4
    body, xspecs, xargs, k_alias = _stacked(functools.partial(_attn_c_prompt_kernel, seq=seq), n_in, bufs)
    st_shape = jax.ShapeDtypeStruct((depth, n, 256, seq), F32)
    st_spec = pl.BlockSpec((None, None, LANES, seq), lambda i, hp: (l, i, hp, 0))
    return pl.pallas_call(
        body,
        grid=(n, 2),
        out_shape=[jax.ShapeDtypeStruct((t, 256), F32), st_shape, st_shape],
        in_specs=[blk, blk, blk,
                  pl.BlockSpec((3, 2, 2, BLOCK, 2 * BLOCK), lambda i, hp: (0, hp, 0, 0, 0))] + xspecs,
        out_specs=[blk, st_spec, st_spec],
        scratch_shapes=[pltpu.VMEM((3, seq, LANES), F32), pltpu.VMEM((3, seq, LANES), F32)]
        + [pltpu.VMEM((seq, LANES), F32)] * 3,
        input_output_aliases=_alias_map(n_in, k_alias, 3) if k_alias else {},
        compiler_params=_cparams(("parallel", "parallel")),
        name="attn_c_prompt",
    )(q, k, v, bias, *xargs)


def _gated_group_norm(y, z, ng):
    gated = y * _silu(z)
    parts = []
    for grp in range(2):
        gsl = gated[:, grp * 256:(grp + 1) * 256]
        ms = jnp.mean(gsl * gsl, axis=-1, keepdims=True)
        parts.append(gsl * lax.rsqrt(ms + EPS))
    return jnp.concatenate(parts, axis=1) * ng


def _expand_heads(v, ex):
    hi = v.astype(BF16)
    r1 = v - hi.astype(F32)
    mid = r1.astype(BF16)
    lo = (r1 - mid.astype(F32)).astype(BF16)
    return _dot(hi, ex) + _dot(mid, ex) + _dot(lo, ex)


def _ssd_prompt_kernel(xbc_ref, z_ref, dt_ref, dtb_ref, alog_ref, dsk_ref, ng_ref, tri_ref, ex_ref,
                       ob_ref, hl_ref, hst):
    c = pl.program_id(1)

    @pl.when(c == 0)
    def _():
        hst[...] = jnp.zeros_like(hst)

    xa_all = _silu(xbc_ref[...])
    dt_all = _softplus(dt_ref[...] + dtb_ref[...])
    a_all = dt_all * (-jnp.exp(alog_ref[...]))
    row = lax.broadcasted_iota(jnp.int32, (BLOCK, BLOCK), 0)
    lane = _lane_iota((BLOCK, BLOCK))
    causal = lane <= row
    lo = lane < HEAD_DIM
    top = row < HEAD_DIM
    n_pairs = N_SSM_HEADS // 2
    for sub in range(SSD_SUB):
        rs = slice(sub * BLOCK, (sub + 1) * BLOCK)
        xa, dt = xa_all[rs], dt_all[rs]
        xs = xa[:, 0:D_INNER]
        acum = jnp.dot(tri_ref[...], a_all[rs], precision=HIGHEST, preferred_element_type=F32)
        bms = [xa[:, D_INNER + g * D_STATE:D_INNER + (g + 1) * D_STATE].astype(BF16) for g in range(2)]
        cms = [xa[:, D_INNER + (2 + g) * D_STATE:D_INNER + (3 + g) * D_STATE].astype(BF16) for g in range(2)]
        cbs = [_dot_t(cms[g], bms[g]) for g in range(2)]
        hps = [hst[2 * j:2 * j + 2].reshape(BLOCK, D_STATE) for j in range(n_pairs)]
        y_off = [_dot_t(cms[j // 2], hps[j].astype(BF16)) for j in range(n_pairs)]
        last = acum[BLOCK - 1:BLOCK, :]
        e_last = jnp.exp(last)
        w_all = _expand_heads(jnp.exp(last - acum) * dt, ex_ref[...])
        e_all = _expand_heads(jnp.exp(acum), ex_ref[...])
        xs_pairs = [xs[:, j * LANES:(j + 1) * LANES] for j in range(n_pairs)]
        for j in range(n_pairs):
            h0_, h1_ = 2 * j, 2 * j + 1
            xw = xs_pairs[j] * w_all[:, j * LANES:(j + 1) * LANES]
            s_new = _dot_t0(xw.astype(BF16), bms[j // 2])
            cd_pair = jnp.where(top, e_last[:, h0_:h0_ + 1], e_last[:, h1_:h1_ + 1])
            hst[h0_:h0_ + 2] = (hps[j] * cd_pair + s_new).reshape(2, HEAD_DIM, D_STATE)
        acum_t = acum.T
        dt_t = dt.T
        y_parts = []
        for j in range(n_pairs):
            xs_b = xs_pairs[j].astype(BF16)
            yd = []
            for hh in range(2):
                h = 2 * j + hh
                seg = acum[:, h:h + 1] - acum_t[h:h + 1, :]
                dec = jnp.exp(jnp.where(causal, seg, NEG)) * dt_t[h:h + 1, :]
                yd.append(_dot((cbs[j // 2] * dec).astype(BF16), xs_b))
            y_parts.append(jnp.where(lo, yd[0], yd[1]) + y_off[j] * e_all[:, j * LANES:(j + 1) * LANES])
        y = jnp.concatenate(y_parts, axis=1) + dsk_ref[...] * xs
        ob_ref[rs, :] = _gated_group_norm(y, z_ref[rs, :], ng_ref[...])

    @pl.when(c == pl.num_programs(1) - 1)
    def _():
        hl_ref[...] = hst[...]


def _ssd_prompt(xbc, z, dt, dtb, alog, dsk, ng, tri, ex, l, depth, n, nc, bufs):
    t = xbc.shape[0]
    nc = nc // SSD_SUB
    rows = SSD_SUB * BLOCK
    row = lambda i, c: (i * nc + c, 0)
    layer = lambda i, c: (l, 0, 0)
    n_in = 9
    body, xspecs, xargs, k_alias = _stacked(_ssd_prompt_kernel, n_in, bufs)
    st = (N_SSM_HEADS, HEAD_DIM, D_STATE)
    return pl.pallas_call(
        body,
        grid=(n, nc),
        out_shape=[jax.ShapeDtypeStruct((t, D_INNER), F32),
                   jax.ShapeDtypeStruct((depth, n) + st, F32)],
        in_specs=[pl.BlockSpec((rows, CONV_DIM), row),
                  pl.BlockSpec((rows, D_INNER), row),
                  pl.BlockSpec((rows, LANES), row),
                  pl.BlockSpec((None, 1, LANES), layer),
                  pl.BlockSpec((None, 1, LANES), layer),
                  pl.BlockSpec((None, 1, D_INNER), layer),
                  pl.BlockSpec((None, 1, D_INNER), layer),
                  pl.BlockSpec((BLOCK, BLOCK), lambda i, c: (0, 0)),
                  pl.BlockSpec((LANES, D_INNER), lambda i, c: (0, 0))] + xspecs,
        out_specs=[pl.BlockSpec((rows, D_INNER), row),
                   pl.BlockSpec((None, None) + st, lambda i, c: (l, i, 0, 0, 0))],
        scratch_shapes=[pltpu.VMEM(st, F32)],
        input_output_aliases=_alias_map(n_in, k_alias, 2) if k_alias else {},
        compiler_params=_cparams(("parallel", "arbitrary")),
        name="ssd_prompt",
    )(xbc, z, dt, dtb, alog, dsk, ng, tri, ex, *xargs)


def _attn_a_sample_kernel(sink_ref, q_ref, kn_ref, vn_ref, kt_ref, vt_ref, bc_ref, bn_ref, o_ref, *, nbs, l):
    rb = nbs * SROWS
    lo = _lane_iota((rb, LANES)) < HEAD_DIM
    s0, s1 = q_ref[:, 0:LANES], q_ref[:, LANES:2 * LANES]
    per_seq = lambda v: v.reshape(nbs, SROWS, LANES)
    qm = jnp.concatenate([
        per_seq(jnp.where(lo, s0, 0.0)),
        per_seq(jnp.where(lo, pltpu.roll(s0, HEAD_DIM, 1), 0.0)),
        per_seq(jnp.where(lo, 0.0, pltpu.roll(s1, HEAD_DIM, 1))),
        per_seq(jnp.where(lo, 0.0, s1))], axis=1).astype(BF16)
    kt = kt_ref[...].reshape(nbs, LANES, WIN).astype(BF16)
    vt = vt_ref[...].reshape(nbs, LANES, WIN).astype(BF16)
    kn = kn_ref[...].reshape(nbs, SROWS, LANES).astype(BF16)
    vn = vn_ref[...].reshape(nbs, SROWS, LANES).astype(BF16)
    s_c = jnp.einsum('nqd,ndk->nqk', qm, kt, preferred_element_type=F32) + bc_ref[...]
    s_n = jnp.einsum('nqd,nkd->nqk', qm, kn, preferred_element_type=F32) + bn_ref[...]
    hrow = lax.broadcasted_iota(jnp.int32, (nbs, 4 * SROWS, 1), 1) // SROWS
    sink = jnp.where(hrow == 0, sink_ref[l, 0], jnp.where(hrow == 1, sink_ref[l, 1],
                     jnp.where(hrow == 2, sink_ref[l, 2], sink_ref[l, 3])))
    m = jnp.maximum(jnp.max(s_c, axis=-1, keepdims=True), jnp.max(s_n, axis=-1, keepdims=True))
    m = jnp.maximum(m, sink)
    p_c = jnp.exp2(s_c - m)
    p_n = jnp.exp2(s_n - m)
    den = jnp.sum(p_c, axis=-1, keepdims=True) + jnp.sum(p_n, axis=-1, keepdims=True) + jnp.exp2(sink - m)
    o = (jnp.einsum('nqk,ndk->nqd', p_c.astype(BF16), vt, preferred_element_type=F32)
         + jnp.einsum('nqk,nkd->nqd', p_n.astype(BF16), vn, preferred_element_type=F32)) / den
    o0, o1, o2, o3 = (o[:, i * SROWS:(i + 1) * SROWS, :].reshape(rb, LANES) for i in range(4))
    o_ref[:, 0:LANES] = jnp.where(lo, o0, pltpu.roll(o1, HEAD_DIM, 1))
    o_ref[:, LANES:2 * LANES] = jnp.where(lo, pltpu.roll(o2, HEAD_DIM, 1), o3)


def _attn_a_sample(q8, k8, v8, cache_kt, cache_vt, bias_c, bias_n, sinks, l, nbs):
    ns = cache_kt.shape[1]
    tok = lambda i: (i, 0)
    cache = pl.BlockSpec((None, nbs, 2, HEAD_DIM, WIN), lambda i: (l, i, 0, 0, 0))
    c3 = lambda i: (0, 0, 0)
    return pl.pallas_call(
        functools.partial(_attn_a_sample_kernel, nbs=nbs, l=l),
        grid=(ns // nbs,),
        out_shape=jax.ShapeDtypeStruct((ns * SROWS, 256), F32),
        in_specs=[pl.BlockSpec(memory_space=pltpu.SMEM),
                  pl.BlockSpec((nbs * SROWS, 256), tok),
                  pl.BlockSpec((nbs * SROWS, LANES), tok),
                  pl.BlockSpec((nbs * SROWS, LANES), tok),
                  cache, cache,
                  pl.BlockSpec((1, 4 * SROWS, WIN), c3),
                  pl.BlockSpec((1, 4 * SROWS, SROWS), c3)],
        out_specs=pl.BlockSpec((nbs * SROWS, 256), tok),
        compiler_params=_cparams(("parallel",)),
        name="attn_a_sample",
    )(sinks, q8, k8, v8, cache_kt, cache_vt, bias_c, bias_n)


def _softmax_parts(parts):
    m = None
    for s in parts:
        mm = jnp.max(s, axis=-1, keepdims=True)
        m = mm if m is None else jnp.maximum(m, mm)
    ps = [jnp.exp2(s - m) for s in parts]
    den = None
    for p in ps:
        dd = jnp.sum(p, axis=-1, keepdims=True)
        den = dd if den is None else den + dd
    return ps, den, m + jnp.log2(den)


def _attn_c_sample_kernel(q_ref, kn_ref, vn_ref, kt_ref, vt_ref, b12_ref, b3_ref, bn_ref, o_ref, *, nbs, lc):
    lo = _lane_iota((SROWS, LANES)) < HEAD_DIM
    pr = 2 * SROWS
    units = [(n, j, slice(n * SROWS, (n + 1) * SROWS), slice(j * LANES, (j + 1) * LANES))
             for n in range(nbs) for j in range(2)]
    scores = []
    for n, j, rows, lanes in units:
        qp = q_ref[rows, lanes]
        lhs = jnp.concatenate([jnp.where(lo, qp, 0.0), jnp.where(lo, 0.0, qp)], axis=0).astype(BF16)
        ktp = kt_ref[n, 2 * j:2 * j + 2].reshape(LANES, lc).astype(BF16)
        scores.append((_dot(lhs, ktp), _dot_t(lhs, kn_ref[rows, lanes].astype(BF16))))
    probs = []
    for (n, j, rows, lanes), (s, sn) in zip(units, scores):
        s_near = s[:, lc - NEAR:]
        (p1, p1n), d1, l1 = _softmax_parts([s_near + b12_ref[0, j], sn + bn_ref[0, j]])
        (p2, p2n), d2, l2 = _softmax_parts([s_near + b12_ref[1, j], sn + bn_ref[1, j]])
        (p3, p3n), d3, l3 = _softmax_parts([s + b3_ref[j], sn + bn_ref[2, j]])
        m = jnp.maximum(jnp.maximum(l1, l2), l3)
        w1, w2, w3 = jnp.exp2(l1 - m), jnp.exp2(l2 - m), jnp.exp2(l3 - m)
        wsum = w1 + w2 + w3
        probs.append((jnp.concatenate([p1, p2], axis=0).astype(BF16), p3.astype(BF16),
                      (p1n.astype(BF16), p2n.astype(BF16), p3n.astype(BF16)),
                      (w1 / (d1 * wsum), w2 / (d2 * wsum), w3 / (d3 * wsum))))
    for (n, j, rows, lanes), (p12, p3, pn, wts) in zip(units, probs):
        vtp = vt_ref[n, 2 * j:2 * j + 2].reshape(LANES, lc).astype(BF16)
        vnp = vn_ref[rows, lanes].astype(BF16)
        o12 = _dot_t(p12, vtp[:, lc - NEAR:])
        o = ((o12[0:pr] + _dot(pn[0], vnp)) * wts[0] + (o12[pr:2 * pr] + _dot(pn[1], vnp)) * wts[1]
             + (_dot_t(p3, vtp) + _dot(pn[2], vnp)) * wts[2])
        o_ref[rows, lanes] = jnp.where(lo, o[0:SROWS], o[SROWS:pr])


def _attn_c_sample(q8, k8, v8, cache_kt, cache_vt, b12, b3, bn, l, nbs):
    ns, lc = cache_kt.shape[1], cache_kt.shape[4]
    tok = lambda i: (i, 0)
    cache = pl.BlockSpec((None, nbs, 4, HEAD_DIM, lc), lambda i: (l, i, 0, 0, 0))
    c3 = lambda i: (0, 0, 0)
    c4 = lambda i: (0, 0, 0, 0)
    return pl.pallas_call(
        functools.partial(_attn_c_sample_kernel, nbs=nbs, lc=lc),
        grid=(ns // nbs,),
        out_shape=jax.ShapeDtypeStruct((ns * SROWS, 256), F32),
        in_specs=[pl.BlockSpec((nbs * SROWS, 256), tok),
                  pl.BlockSpec((nbs * SROWS, 256), tok),
                  pl.BlockSpec((nbs * SROWS, 256), tok),
                  cache, cache,
                  pl.BlockSpec((2, 2, 2 * SROWS, NEAR), c4),
                  pl.BlockSpec((2, 2 * SROWS, lc), c3),
                  pl.BlockSpec((3, 2, 2 * SROWS, SROWS), c4)],
        out_specs=pl.BlockSpec((nbs * SROWS, 256), tok),
        compiler_params=_cparams(("parallel",)),
        name="attn_c_sample",
    )(q8, k8, v8, cache_kt, cache_vt, b12, b3, bn)


def _ssd_sample_pre_kernel(xbc_ref, prev_ref, dt_ref, cw_ref, cb_ref, dtb_ref, alog_ref, dsk_ref, ex_ref,
                           yd_ref, eac_ref, xw_ref, bm_ref, cm_ref, cd_ref):
    r = xbc_ref.shape[0]
    rid = lax.broadcasted_iota(jnp.int32, (r, 1), 0) % SROWS
    x = jnp.where(jnp.logical_and(rid >= 1, rid < SVALID), prev_ref[...], xbc_ref[...])
    y = cb_ref[...] + cw_ref[3:4, :] * x
    for d in range(1, 4):
        y = y + cw_ref[3 - d:4 - d, :] * pltpu.roll(x, d, 0)
    xa = _silu(y)
    xs = xa[:, 0:D_INNER]
    bm = xa[:, D_INNER:D_INNER + 2 * D_STATE]
    cm = xa[:, D_INNER + 2 * D_STATE:CONV_DIM]
    valid = rid >= SVALID
    dt = jnp.where(valid, _softplus(dt_ref[...] + dtb_ref[...]), 0.0)
    a = dt * (-jnp.exp(alog_ref[...]))
    acum = a
    rem = jnp.zeros_like(a)
    for d in range(1, 4):
        acum = acum + jnp.where(rid - d >= SVALID, pltpu.roll(a, d, 0), 0.0)
        rem = rem + jnp.where(rid + d < SROWS, pltpu.roll(a, r - d, 0), 0.0)
    ex = ex_ref[...]

    def expand(v):
        return _expand_heads(v, ex)

    lane = _lane_iota((r, LANES))
    y_acc = dsk_ref[...] * xs
    for d in range(4):
        ok = rid - d >= SVALID
        bsh = bm if d == 0 else pltpu.roll(bm, d, 0)
        cb0 = jnp.sum(cm[:, 0:D_STATE] * bsh[:, 0:D_STATE], axis=-1, keepdims=True)
        cb1 = jnp.sum(cm[:, D_STATE:] * bsh[:, D_STATE:], axis=-1, keepdims=True)
        cbh = jnp.where(lane < N_SSM_HEADS // 2, cb0, cb1)
        if d == 0:
            coef = cbh * dt
            xsh = xs
        else:
            dec = jnp.exp(jnp.where(ok, acum - pltpu.roll(acum, d, 0), NEG))
            coef = cbh * dec * pltpu.roll(dt, d, 0)
            xsh = pltpu.roll(xs, d, 0)
        y_acc = y_acc + expand(jnp.where(ok, coef, 0.0)) * xsh
    yd_ref[...] = y_acc
    eac_ref[...] = expand(jnp.exp(acum))
    xw_ref[...] = xs * expand(dt * jnp.exp(rem))
    bm_ref[...] = bm
    cm_ref[...] = cm
    cd_ref[...] = jnp.exp(acum + rem)


def _ssd_sample_pre(xbc8, prev8, dt8, cw8, cb, dtb, alog, dsk, ex, l, rb):
    r = xbc8.shape[0]
    row = lambda i: (i, 0)
    layer = lambda i: (l, 0, 0)
    widths = (D_INNER, D_INNER, D_INNER, 2 * D_STATE, 2 * D_STATE, LANES)
    return pl.pallas_call(
        _ssd_sample_pre_kernel,
        grid=(r // rb,),
        out_shape=[jax.ShapeDtypeStruct((r, w_), F32) for w_ in widths],
        in_specs=[pl.BlockSpec((rb, CONV_DIM), row),
                  pl.BlockSpec((None, rb, CONV_DIM), lambda i: (l, i, 0)),
                  pl.BlockSpec((rb, LANES), row),
                  pl.BlockSpec((None, SUBLANES, CONV_DIM), layer),
                  pl.BlockSpec((None, 1, CONV_DIM), layer),
                  pl.BlockSpec((None, 1, LANES), layer),
                  pl.BlockSpec((None, 1, LANES), layer),
                  pl.BlockSpec((None, 1, D_INNER), layer),
                  pl.BlockSpec((LANES, D_INNER), lambda i: (0, 0))],
        out_specs=[pl.BlockSpec((rb, w_), row) for w_ in widths],
        compiler_params=_cparams(("parallel",)),
        name="ssd_sample_pre",
    )(xbc8, prev8, dt8, cw8, cb, dtb, alog, dsk, ex)


def _ssd_sample_state_kernel(cd_ref, yd_ref, eac_ref, xw_ref, bm_ref, cm_ref, z_ref, h0_ref, ng_ref,
                             ob_ref, hn_ref, *, nbs):
    base = pl.program_id(0) * nbs
    yo = []
    for n in range(nbs):
        rows = slice(n * SROWS, (n + 1) * SROWS)
        parts = []
        for g in range(2):
            hp = h0_ref[n, 4 * g:4 * g + 4].reshape(4 * HEAD_DIM, D_STATE)
            cmg = cm_ref[rows, g * D_STATE:(g + 1) * D_STATE].astype(BF16)
            bmg = bm_ref[rows, g * D_STATE:(g + 1) * D_STATE].astype(BF16)
            parts.append(_dot_t(cmg, hp.astype(BF16)))
            s_new = _dot_t0(xw_ref[rows, g * 256:(g + 1) * 256].astype(BF16), bmg)
            for hh in range(4):
                h = 4 * g + hh
                hn_ref[n, h] = (hp[hh * HEAD_DIM:(hh + 1) * HEAD_DIM, :] * cd_ref[base + n, h]
                                + s_new[hh * HEAD_DIM:(hh + 1) * HEAD_DIM, :])
        yo.append(jnp.concatenate(parts, axis=1))
    y = yd_ref[...] + eac_ref[...] * jnp.concatenate(yo, axis=0)
    ob_ref[...] = _gated_group_norm(y, z_ref[...], ng_ref[...])


def _ssd_sample_state(cd, yd, eac, xw, bm, cm, z8, state, ng, l, depth, nbs, bufs):
    ns = state.shape[1]
    row = lambda i: (i, 0)
    st_spec = pl.BlockSpec((None, nbs, N_SSM_HEADS, HEAD_DIM, D_STATE), lambda i: (l, i, 0, 0, 0))
    rb = nbs * SROWS
    n_in = 9
    body, xspecs, xargs, k_alias = _stacked(functools.partial(_ssd_sample_state_kernel, nbs=nbs), n_in, bufs)
    return pl.pallas_call(
        body,
        grid=(ns // nbs,),
        out_shape=[jax.ShapeDtypeStruct((ns * SROWS, D_INNER), F32),
                   jax.ShapeDtypeStruct(state.shape, F32)],
        in_specs=[pl.BlockSpec(memory_space=pltpu.SMEM),
                  pl.BlockSpec((rb, D_INNER), row),
                  pl.BlockSpec((rb, D_INNER), row),
                  pl.BlockSpec((rb, D_INNER), row),
                  pl.BlockSpec((rb, 2 * D_STATE), row),
                  pl.BlockSpec((rb, 2 * D_STATE), row),
                  pl.BlockSpec((rb, D_INNER), row),
                  st_spec,
                  pl.BlockSpec((None, 1, D_INNER), lambda i: (l, 0, 0))] + xspecs,
        out_specs=[pl.BlockSpec((rb, D_INNER), row), st_spec],
        input_output_aliases=_alias_map(n_in, k_alias, 2) if k_alias else {},
        compiler_params=_cparams(("parallel",)),
        name="ssd_sample_state",
    )(cd, yd, eac, xw, bm, cm, z8, state, ng, *xargs)


def _new_kv_kernel(ka_ref, va_ref, kc_ref, vc_ref, oka_ref, ova_ref, okc_ref, ovc_ref, *, ns):
    def emit(src, dst):
        for t in range(SROWS - SVALID):
            dst[t] = src[pl.ds(SVALID + t, ns, stride=SROWS), :].T

    emit(kc_ref, okc_ref)
    emit(vc_ref, ovc_ref)

    @pl.when(pl.program_id(1) == 0)
    def _():
        emit(ka_ref, oka_ref)
        emit(va_ref, ova_ref)


def _new_kv_states(ka, va, kc, vc, ns):
    depth = ka.shape[0]
    nt = SROWS - SVALID
    narrow_in = pl.BlockSpec((None, ns * SROWS, LANES), lambda l, j: (l, 0, 0))
    wide_in = pl.BlockSpec((None, ns * SROWS, LANES), lambda l, j: (l, 0, j))
    narrow_out = pl.BlockSpec((None, nt, LANES, ns), lambda l, j: (l, 0, 0, 0))
    wide_out = pl.BlockSpec((None, nt, LANES, ns), lambda l, j: (l, 0, j, 0))
    shape = lambda w: jax.ShapeDtypeStruct((depth, nt, w, ns), F32)
    return pl.pallas_call(
        functools.partial(_new_kv_kernel, ns=ns),
        grid=(depth, 2),
        out_shape=[shape(LANES), shape(LANES), shape(2 * LANES), shape(2 * LANES)],
        in_specs=[narrow_in, narrow_in, wide_in, wide_in],
        out_specs=[narrow_out, narrow_out, wide_out, wide_out],
        compiler_params=_cparams(("parallel", "arbitrary")),
        name="new_kv_states",
    )(ka, va, kc, vc)


def _prompt_bias_item(dil, col0):
    qi = np.arange(BLOCK)[:, None]
    kj = np.arange(2 * BLOCK)[None, :]
    dist = BLOCK + qi - kj
    ok = (dist >= 0) & (dist <= WIN)
    variants = [_masked_bucket(dist * dil, ok), _masked_bucket(dist * dil, ok & (kj >= BLOCK))]
    segs = [((2 * h + v) * BLOCK, (2 * h + v + 1) * BLOCK, col0 + h) for h in range(4) for v in range(2)]
    return np.concatenate(variants * 4, axis=0), segs


def _sample_bias_item(dist_of, ok_of, cols, col0):
    t = np.arange(SROWS)[:, None] - SVALID
    bkt = np.where(t >= 0, _masked_bucket(dist_of(t, cols), ok_of(t, cols)), 0).astype(np.int32)
    segs = [(h * SROWS, (h + 1) * SROWS, col0 + h) for h in range(4)]
    return np.concatenate([bkt] * 4, axis=0), segs


def _cache_bias_item(length, window, dil, span, col0):
    dist_of = lambda t, pos: length + t - pos
    ok_of = lambda t, pos: (dist_of(t, pos) >= 0) & (dist_of(t, pos) <= window) & (dist_of(t, pos) % dil == 0)
    return _sample_bias_item(dist_of, ok_of, np.arange(length - span, length)[None, :], col0)


def _new_rows_bias_item(same_token_only, col0):
    dist_of = lambda t, r: t - (r - SVALID)
    if same_token_only:
        ok_of = lambda t, r: (r >= SVALID) & (dist_of(t, r) == 0)
    else:
        ok_of = lambda t, r: (r >= SVALID) & (dist_of(t, r) >= 0)
    return _sample_bias_item(dist_of, ok_of, np.arange(SROWS)[None, :], col0)


Tiling = collections.namedtuple("Tiling", "tm_prompt tm_sample seqs_a seqs_c seqs_b rows_b")


def _tiling(ns):
    rows = ns * SROWS
    return Tiling(tm_prompt=512, tm_sample=min(512, rows), seqs_a=min(16, ns), seqs_c=min(2, ns),
                  seqs_b=min(16, ns), rows_b=min(256, rows))
def kernel(x_prompt, x_sample, cache_a_k, cache_a_v, cache_c_k, cache_c_v, state_ssm, state_conv,
           norm_mix_g, w_in, a_q_norm_g, a_k_norm_g, a_sinks, c_q_norm_g, c_k_norm_g, rel_bias,
           conv_w, conv_b, dt_bias, a_log, d_skip, ssm_norm_g, w_out, norm_mlp_g, w_up, w_down):
    depth = w_in.shape[0]
    n, seq, _ = x_prompt.shape
    ns, ts, _ = x_sample.shape
    lc = cache_c_k.shape[2]
    assert ts == SROWS - SVALID and seq % (BLOCK * 16) == 0
    assert cache_a_k.shape[2] == WIN and all(w <= lc for w, _ in C_BRANCHES)
    nb = seq // BLOCK

    w_in_b = w_in.astype(BF16)
    w_out_b, w_up_b, w_down_b = w_out.astype(BF16), w_up.astype(BF16), w_down.astype(BF16)
    ones = jnp.ones((depth, 128), F32)
    q_scale = ATTN_SCALE * LOG2E
    gain = jnp.concatenate([jnp.tile(a_q_norm_g, (1, 4)) * q_scale, jnp.tile(a_k_norm_g, (1, 2)), ones,
                            jnp.tile(c_q_norm_g, (1, 4)) * q_scale, jnp.tile(c_k_norm_g, (1, 4))], axis=1)
    rel_bias = rel_bias * LOG2E
    a_sinks = a_sinks * LOG2E
    tri = (jnp.arange(BLOCK)[None, :] <= jnp.arange(BLOCK)[:, None]).astype(F32)
    ex = (jnp.arange(LANES)[:, None] == (jnp.arange(D_INNER) // HEAD_DIM)[None, :]).astype(BF16)
    cw8 = jnp.pad(conv_w, ((0, 0), (0, SUBLANES - conv_w.shape[1]), (0, 0)))
    pad_h = lambda v: jnp.pad(v, ((0, 0), (0, LANES - N_SSM_HEADS)))[:, None, :]
    vec = lambda v: v[:, None, :]
    dtb, alog = pad_h(dt_bias), pad_h(a_log)
    dsk = vec(jnp.repeat(d_skip, HEAD_DIM, axis=1))
    g_mix, g_mlp, gain, cb, ng = vec(norm_mix_g), vec(norm_mlp_g), vec(gain), vec(conv_b), vec(ssm_norm_g)

    (w1, d1), (w2, d2), (w3, d3) = C_BRANCHES
    items = ([_prompt_bias_item(1, 0)] + [_prompt_bias_item(d, 4) for d in C_DILS]
             + [_cache_bias_item(WIN, WIN, 1, WIN, 0), _new_rows_bias_item(False, 0)]
             + [_cache_bias_item(lc, w1, d1, NEAR, 4), _cache_bias_item(lc, w2, d2, NEAR, 4),
                _cache_bias_item(lc, w3, d3, lc, 4), _new_rows_bias_item(False, 4), _new_rows_bias_item(True, 4)])
    pa, pc1, pc4, pc16, sa_c, sa_n, sc1, sc2, sc3, sn_all, sn_same = _expand_biases(rel_bias, items)
    by_variant = lambda b: b.reshape(4, 2, BLOCK, 2 * BLOCK)
    by_pair = lambda b: b.reshape(2, 2 * SROWS, b.shape[-1])
    bias_a = by_variant(pa)
    bias_c = jnp.stack([by_variant(pc1), by_variant(pc4), by_variant(pc16)])
    sbias_a_c, sbias_a_n = sa_c[None], sa_n[None]
    sb12 = jnp.stack([by_pair(sc1), by_pair(sc2)])
    sb3 = by_pair(sc3)
    sbn = jnp.stack([by_pair(sn_all), by_pair(sn_same), by_pair(sn_same)])

    cak_t, cav_t = cache_a_k.transpose(0, 1, 3, 4, 2), cache_a_v.transpose(0, 1, 3, 4, 2)
    cck_t, ccv_t = cache_c_k.transpose(0, 1, 3, 4, 2), cache_c_v.transpose(0, 1, 3, 4, 2)

    hp = x_prompt.reshape(n * seq, D_MODEL)
    hs = jnp.pad(x_sample, ((0, 0), (SVALID, 0), (0, 0))).reshape(ns * SROWS, D_MODEL)
    tl = _tiling(ns)
    prev8 = jnp.pad(state_conv, ((0, 0), (0, 0), (1, SROWS - 4), (0, 0))).reshape(depth, ns * SROWS, CONV_DIM)

    p_ak = p_ck = p_ssm = s_ssm = None
    p_conv, s_small = [], [[] for _ in range(5)]
    for l in range(depth):
        qa, ka, va, qc, kc, vc, z, xbc, dt, tail = _in_proj(hp, g_mix, w_in_b, gain, cw8, cb, l, tl.tm_prompt,
                                                            seq // tl.tm_prompt)
        oa_p, *p_ak = _attn_a_prompt(qa, ka, va, bias_a, a_sinks, l, depth, n, seq, p_ak)
        oc_p, *p_ck = _attn_c_prompt(qc, kc, vc, bias_c, l, depth, n, seq, p_ck)
        ob_p, *p_ssm = _ssd_prompt(xbc, z, dt, dtb, alog, dsk, ng, tri, ex, l, depth, n, nb, p_ssm)
        p_conv.append(tail)

        qa, ka, va, qc, kc, vc, z, xbc, dt, _ = _in_proj(hs, g_mix, w_in_b, gain, cw8, cb, l, tl.tm_sample, 0)
        oa = _attn_a_sample(qa, ka, va, cak_t, cav_t, sbias_a_c, sbias_a_n, a_sinks, l, tl.seqs_a)
        oc = _attn_c_sample(qc, kc, vc, cck_t, ccv_t, sb12, sb3, sbn, l, tl.seqs_c)
        yd, eac, xw, bm, cm, cd = _ssd_sample_pre(xbc, prev8, dt, cw8, cb, dtb, alog, dsk, ex, l, tl.rows_b)
        cd_s = cd.reshape(ns, SROWS, LANES)[:, SROWS - 1, :N_SSM_HEADS]
        ob, *s_ssm = _ssd_sample_state(cd_s, yd, eac, xw, bm, cm, z, state_ssm, ng, l, depth, tl.seqs_b, s_ssm)
        for dst, val in zip(s_small, (ka, va, kc, vc, xbc)):
            dst.append(val)

        hp, hs = _out_mlp(hp, oa_p, ob_p, oc_p, hs, oa, ob, oc, w_out_b, g_mlp, w_up_b, w_down_b, l,
                          math.gcd(tl.tm_prompt, ns * SROWS))

    y_prompt = hp.reshape(n, seq, D_MODEL)
    y_sample = hs.reshape(ns, SROWS, D_MODEL)[:, SVALID:]
    unminor = lambda buf, heads: buf.reshape(depth, n, heads, HEAD_DIM, -1).transpose(0, 1, 4, 2, 3)
    p_state = (unminor(p_ak[0], 2), unminor(p_ak[1], 2), unminor(p_ck[0], 4), unminor(p_ck[1], 4),
               p_ssm[0], jnp.stack(p_conv)[:, :, SUBLANES - 3:])
    new_kv = _new_kv_states(*(jnp.stack(v) for v in s_small[:4]), ns)
    s_kv = tuple(b.reshape(depth, SROWS - SVALID, hh, HEAD_DIM, ns).transpose(0, 4, 1, 2, 3)
                 for b, hh in zip(new_kv, (2, 2, 4, 4)))
    s_conv = jnp.stack([x.reshape(ns, SROWS, CONV_DIM)[:, SROWS - 3:] for x in s_small[4]])
    s_state = s_kv + (s_ssm[0], s_conv)
    return (y_prompt, y_sample) + p_state + s_state
```

```python
import collections
import functools
import math

import jax
import jax.numpy as jnp
import numpy as np
from jax import lax
from jax.experimental import pallas as pl
from jax.experimental.pallas import tpu as pltpu

F32 = jnp.float32
BF16 = jnp.bfloat16
HIGHEST = lax.Precision.HIGHEST

D_MODEL = 1024
HEAD_DIM = 64
LANES = 128
SUBLANES = 8
BLOCK = 128
WIN = 128
D_INNER = 512
D_STATE = 128
N_SSM_HEADS = 8
CONV_DIM = 1024
D_FF = 4096
NUM_BUCKETS = 32
REL_MAX_DIST = 2048
EPS = 1e-6
ATTN_SCALE = HEAD_DIM ** -0.5
LOG2E = math.log2(math.e)
NEG = -1e30
C_BRANCHES = ((128, 1), (512, 4), (2048, 16))
C_DILS = tuple(d for _, d in C_BRANCHES)
SROWS = 8
SVALID = 4
NEAR = 512
C_STAGE = 4
C_UNROLL = 8
A_UNROLL = 4
SSD_SUB = 4

COL_QA, COL_KA, COL_VA, COL_QC, COL_KC, COL_VC, COL_Z, COL_XBC, COL_DT, COL_END = (
    0, 256, 384, 512, 768, 1024, 1280, 1792, 2816, 2944)
IN_COLS = 2824
V7X_VMEM_BYTES = 64 * 1024 * 1024
VMEM_LIMIT = V7X_VMEM_BYTES * 7 // 8


def _cparams(sem):
    return pltpu.CompilerParams(dimension_semantics=sem, vmem_limit_bytes=VMEM_LIMIT)


def _dot(a, b):
    return jnp.dot(a, b, preferred_element_type=F32)


def _dot_t(a, b):
    return lax.dot_general(a, b, (((1,), (1,)), ((), ())), preferred_element_type=F32)


def _dot_t0(a, b):
    return lax.dot_general(a, b, (((0,), (0,)), ((), ())), preferred_element_type=F32)


def _lane_iota(shape):
    return lax.broadcasted_iota(jnp.int32, shape, len(shape) - 1)


def _silu(x):
    hx = 0.5 * x
    return hx + hx * jnp.tanh(hx)


def _softplus(x):
    return jnp.maximum(x, 0.0) + jnp.log(1.0 + jnp.exp(-jnp.abs(x)))


def _stacked(body, n_in, bufs):
    if bufs is None:
        return body, [], [], 0
    k = len(bufs)

    def wrapped(*refs):
        return body(*refs[:n_in], *refs[n_in + k:])

    return wrapped, [pl.BlockSpec(memory_space=pl.ANY)] * k, list(bufs), k


def _alias_map(n_in, k, n_out):
    return {n_in + i: n_out - k + i for i in range(k)}


def _bias_kernel(tab_ref, *refs, segments):
    n = len(segments)
    for bkt_ref, o_ref, segs in zip(refs[:n], refs[n:], segments):
        for r0, r1, col in segs:
            bkt = bkt_ref[r0:r1, :]
            acc = jnp.full(bkt.shape, NEG, F32)
            for b in range(NUM_BUCKETS):
                acc = jnp.where(bkt == b, tab_ref[b, col], acc)
            o_ref[r0:r1, :] = acc


def _expand_biases(table, items):
    bkts = [jnp.asarray(b) for b, _ in items]
    full = lambda b: pl.BlockSpec(b.shape, lambda: (0, 0))
    return pl.pallas_call(
        functools.partial(_bias_kernel, segments=[s for _, s in items]),
        out_shape=[jax.ShapeDtypeStruct(b.shape, F32) for b in bkts],
        in_specs=[pl.BlockSpec(memory_space=pltpu.SMEM)] + [full(b) for b in bkts],
        out_specs=[full(b) for b in bkts],
        name="bias_expand",
    )(table, *bkts)


def _t5_bucket(dist):
    max_exact = NUM_BUCKETS // 2
    d = np.maximum(dist, 0)
    df = np.maximum(d, 1).astype(np.float32)
    ratio = np.log(df / np.float32(max_exact)) / np.float32(math.log(REL_MAX_DIST / max_exact))
    large = max_exact + (ratio * np.float32(NUM_BUCKETS - max_exact)).astype(np.int32)
    return np.where(d < max_exact, d, np.minimum(large, NUM_BUCKETS - 1))


def _masked_bucket(dist, valid):
    return np.where(valid, _t5_bucket(dist), -1).astype(np.int32)


def _head_norm(p, gain, slabs):
    lo = _lane_iota((p.shape[0], LANES)) < HEAD_DIM
    out = []
    for s in range(slabs):
        x = p[:, s * LANES:(s + 1) * LANES]
        sq = x * x
        s_lo = jnp.sum(jnp.where(lo, sq, 0.0), axis=-1, keepdims=True)
        s_hi = jnp.sum(jnp.where(lo, 0.0, sq), axis=-1, keepdims=True)
        ss = jnp.where(lo, s_lo, s_hi)
        out.append(x * lax.rsqrt(ss * (1.0 / HEAD_DIM) + EPS) * gain[:, s * LANES:(s + 1) * LANES])
    return out[0] if slabs == 1 else jnp.concatenate(out, axis=1)


def _in_proj_kernel(h_ref, g_ref, w_ref, wdt_ref, gain_ref, cw_ref, cb_ref,
                    qa_ref, ka_ref, va_ref, qc_ref, kc_ref, vc_ref, z_ref, xbc_ref, dt_ref, tail_ref, xbuf,
                    *, conv_tiles):
    if conv_tiles:
        @pl.when(pl.program_id(0) % conv_tiles == 0)
        def _():
            xbuf[0:SUBLANES, :] = jnp.zeros((SUBLANES, CONV_DIM), F32)

    tm = h_ref.shape[0]
    x = h_ref[...]
    ms = jnp.mean(x * x, axis=-1, keepdims=True)
    u = (x * lax.rsqrt(ms + EPS) * g_ref[...]).astype(BF16)

    def proj(c0, c1):
        return _dot(u, w_ref[:, c0:c1])

    xbc = proj(COL_XBC, COL_DT)
    tail = xbc[tm - SUBLANES:tm, :]
    tail_ref[...] = tail
    if conv_tiles:
        xbuf[SUBLANES:SUBLANES + tm, :] = xbc
        y = cb_ref[...] + cw_ref[3:4, :] * xbc
        for kk in range(3):
            y = y + cw_ref[kk:kk + 1, :] * xbuf[SUBLANES - 3 + kk:SUBLANES - 3 + kk + tm, :]
        xbuf[0:SUBLANES, :] = tail
        xbc_ref[...] = y
    else:
        xbc_ref[...] = xbc
    groups = [COL_QA, COL_KA, COL_QC, COL_KC]
    raw = [proj(c0, c0 + 256) for c0 in groups]
    va_ref[...] = raw[1][:, COL_VA - COL_KA:]
    vc_ref[...] = proj(COL_VC, COL_Z)
    z_ref[...] = proj(COL_Z, COL_XBC)
    qa_ref[...] = _head_norm(raw[0], gain_ref[:, COL_QA:COL_QA + 256], 2)
    ka_ref[...] = _head_norm(raw[1], gain_ref[:, COL_KA:COL_KA + 256], 1)
    qc_ref[...] = _head_norm(raw[2], gain_ref[:, COL_QC:COL_QC + 256], 2)
    kc_ref[...] = _head_norm(raw[3], gain_ref[:, COL_KC:COL_KC + 256], 2)
    dt_ref[...] = _dot(u, wdt_ref[...])


def _in_proj(h, g, w, w_dt, gain, cw8, cb, l, tm, conv_tiles):
    t = h.shape[0]
    widths = (256, 128, 128, 256, 256, 256, 512, 1024, 128)
    layer = lambda i: (l, 0, 0)
    n_tail = t // tm // conv_tiles if conv_tiles else t // tm
    per_seq = conv_tiles if conv_tiles else 1
    return pl.pallas_call(
        functools.partial(_in_proj_kernel, conv_tiles=conv_tiles),
        grid=(t // tm,),
        out_shape=[jax.ShapeDtypeStruct((t, w_), F32) for w_ in widths]
        + [jax.ShapeDtypeStruct((n_tail, SUBLANES, CONV_DIM), F32)],
        in_specs=[pl.BlockSpec((tm, D_MODEL), lambda i: (i, 0)),
                  pl.BlockSpec((None, 1, D_MODEL), layer),
                  pl.BlockSpec((None, D_MODEL, COL_DT), layer),
                  pl.BlockSpec((None, D_MODEL, COL_END - COL_DT), layer),
                  pl.BlockSpec((None, 1, D_MODEL), layer),
                  pl.BlockSpec((None, SUBLANES, CONV_DIM), layer),
                  pl.BlockSpec((None, 1, CONV_DIM), layer)],
        out_specs=[pl.BlockSpec((tm, w_), lambda i: (i, 0)) for w_ in widths]
        + [pl.BlockSpec((None, SUBLANES, CONV_DIM), lambda i: (i // per_seq, 0, 0))],
        scratch_shapes=[pltpu.VMEM((SUBLANES + tm, CONV_DIM), F32)],
        compiler_params=_cparams(("arbitrary",)),
        name="in_proj",
    )(h, g, w, w_dt, gain, cw8, cb)


def _out_mlp_kernel(hp_ref, oap_ref, obp_ref, ocp_ref, hs_ref, oas_ref, obs_ref, ocs_ref,
                    wo_ref, g_ref, wu_ref, wd_ref, op_ref, os_ref, *, prompt_tiles):
    def tile(h_ref, oa_ref, ob_ref, oc_ref, o_ref):
        acc = _dot(oa_ref[...].astype(BF16), wo_ref[0:256, :])
        acc += _dot(ob_ref[...].astype(BF16), wo_ref[256:768, :])
        acc += _dot(oc_ref[...].astype(BF16), wo_ref[768:1024, :])
        h2 = h_ref[...] + acc
        ms = jnp.mean(h2 * h2, axis=-1, keepdims=True)
        u = (h2 * lax.rsqrt(ms + EPS) * g_ref[...]).astype(BF16)
        ff_chunk = 1024
        mlp = jnp.zeros_like(h2)
        for c in range(D_FF // ff_chunk):
            a = _dot(u, wu_ref[:, c * ff_chunk:(c + 1) * ff_chunk])
            a = jnp.square(jnp.maximum(a, 0.0)).astype(BF16)
            mlp += _dot(a, wd_ref[c * ff_chunk:(c + 1) * ff_chunk, :])
        o_ref[...] = h2 + mlp

    is_prompt = pl.program_id(0) < prompt_tiles

    @pl.when(is_prompt)
    def _():
        tile(hp_ref, oap_ref, obp_ref, ocp_ref, op_ref)

    @pl.when(jnp.logical_not(is_prompt))
    def _():
        tile(hs_ref, oas_ref, obs_ref, ocs_ref, os_ref)


def _out_mlp(hp, oap, obp, ocp, hs, oas, obs, ocs, wo, g, wu, wd, l, tm):
    tp, ts = hp.shape[0], hs.shape[0]
    prompt_tiles = tp // tm
    layer = lambda i: (l, 0, 0)
    prow = lambda i: (jnp.minimum(i, prompt_tiles - 1), 0)
    srow = lambda i: (jnp.maximum(i - prompt_tiles, 0), 0)
    tok = lambda idx: [pl.BlockSpec((tm, w_), idx) for w_ in (D_MODEL, 256, 512, 256)]
    weights = lambda shape: pl.BlockSpec((None,) + shape, layer, pipeline_mode=pl.Buffered(1))
    return pl.pallas_call(
        functools.partial(_out_mlp_kernel, prompt_tiles=prompt_tiles),
        grid=(prompt_tiles + ts // tm,),
        out_shape=[jax.ShapeDtypeStruct((tp, D_MODEL), F32), jax.ShapeDtypeStruct((ts, D_MODEL), F32)],
        in_specs=tok(prow) + tok(srow) + [weights((D_MODEL, D_MODEL)),
                                          pl.BlockSpec((None, 1, D_MODEL), layer),
                                          weights((D_MODEL, D_FF)),
                                          weights((D_FF, D_MODEL))],
        out_specs=[pl.BlockSpec((tm, D_MODEL), prow), pl.BlockSpec((tm, D_MODEL), srow)],
        compiler_params=_cparams(("arbitrary",)),
        name="out_mlp",
    )(hp, oap, obp, ocp, hs, oas, obs, ocs, wo, g, wu, wd)


def _attn_a_prompt_kernel(sink_ref, q_ref, k_ref, v_ref, bias_ref, o_ref, kt_ref, vt_ref,
                          k_st, k_sw, v_st, v_sw, *, l, seq):
    lo = _lane_iota((BLOCK, LANES)) < HEAD_DIM
    k = k_ref[...]
    v = v_ref[...]
    for st in (k_st, k_sw, v_st, v_sw):
        st[0:BLOCK, :] = jnp.zeros((BLOCK, LANES), BF16)
    k_st[BLOCK:, :] = k.astype(BF16)
    v_st[BLOCK:, :] = v.astype(BF16)
    k_sw[BLOCK:, :] = pltpu.roll(k, HEAD_DIM, 1).astype(BF16)
    v_sw[BLOCK:, :] = pltpu.roll(v, HEAD_DIM, 1).astype(BF16)
    kt_ref[...] = k_ref[seq - WIN:seq, :].T
    vt_ref[...] = v_ref[seq - WIN:seq, :].T

    def body(g, carry):
        scores, where = [], []
        for u in range(A_UNROLL):
            b = g * A_UNROLL + u
            cur = pl.ds(pl.multiple_of(b * BLOCK, BLOCK), BLOCK)
            both = pl.ds(pl.multiple_of(b * BLOCK, BLOCK), 2 * BLOCK)
            variant = jnp.where(b == 0, 1, 0)
            where.append((cur, both))
            for h in range(4):
                slab = q_ref[cur, (h // 2) * LANES:(h // 2 + 1) * LANES]
                qm = jnp.where(lo if h % 2 == 0 else ~lo, slab, 0.0).astype(BF16)
                kk = k_st if h in (0, 3) else k_sw
                scores.append(_dot_t(qm, kk[both, :]) + bias_ref[h, variant])
        probs = []
        for i, s in enumerate(scores):
            sink = sink_ref[l, i % 4]
            m = jnp.maximum(jnp.max(jnp.maximum(s[:, 0:BLOCK], s[:, BLOCK:2 * BLOCK]), axis=-1, keepdims=True), sink)
            p = jnp.exp2(s - m)
            den = jnp.sum(p[:, 0:BLOCK] + p[:, BLOCK:2 * BLOCK], axis=-1, keepdims=True) + jnp.exp2(sink - m)
            probs.append((p.astype(BF16), den))
        for u, (cur, both) in enumerate(where):
            outs = []
            for h in range(4):
                p, den = probs[4 * u + h]
                vv = v_st if h in (0, 3) else v_sw
                outs.append(_dot(p, vv[both, :]) / den)
            o_ref[cur, 0:LANES] = jnp.where(lo, outs[0], outs[1])
            o_ref[cur, LANES:2 * LANES] = jnp.where(lo, outs[2], outs[3])
        return carry

    lax.fori_loop(0, seq // BLOCK // A_UNROLL, body, 0)


def _attn_a_prompt(q, k, v, bias, sinks, l, depth, n, seq, bufs):
    t = q.shape[0]
    n_in = 5
    body, xspecs, xargs, k_alias = _stacked(functools.partial(_attn_a_prompt_kernel, l=l, seq=seq), n_in, bufs)
    st_shape = jax.ShapeDtypeStruct((depth, n, LANES, WIN), F32)
    st_spec = pl.BlockSpec((None, None, LANES, WIN), lambda i: (l, i, 0, 0))
    tok = lambda w: pl.BlockSpec((seq, w), lambda i: (i, 0))
    return pl.pallas_call(
        body,
        grid=(n,),
        out_shape=[jax.ShapeDtypeStruct((t, 256), F32), st_shape, st_shape],
        in_specs=[pl.BlockSpec(memory_space=pltpu.SMEM), tok(256), tok(LANES), tok(LANES),
                  pl.BlockSpec((4, 2, BLOCK, 2 * BLOCK), lambda i: (0, 0, 0, 0))] + xspecs,
        out_specs=[tok(256), st_spec, st_spec],
        scratch_shapes=[pltpu.VMEM((BLOCK + seq, LANES), BF16)] * 4,
        input_output_aliases=_alias_map(n_in, k_alias, 3) if k_alias else {},
        compiler_params=_cparams(("parallel",)),
        name="attn_a_prompt",
    )(sinks, q, k, v, bias, *xargs)


def _attn_c_prompt_kernel(q_ref, k_ref, v_ref, bias_ref, o_ref, kt_ref, vt_ref, ob_scr, lse_scr,
                          q_st, k_st, v_st, *, seq):
    lane = _lane_iota((BLOCK, LANES))
    lo = lane < HEAD_DIM
    kt_ref[...] = k_ref[...].T
    vt_ref[...] = v_ref[...].T
    part = seq // C_STAGE
    for src, dst in ((q_ref, q_st), (k_ref, k_st), (v_ref, v_st)):
        for rho in range(C_STAGE):
            dst[rho * part:(rho + 1) * part, :] = src[pl.ds(rho, part, stride=C_STAGE), :]

    def blocks(br, staged, stride, where, has_prev):
        rows = lambda ref, s: ref[pl.ds(s, BLOCK, stride=stride), :]
        q_src, k_src, v_src = (q_st, k_st, v_st) if staged else (q_ref, k_ref, v_ref)
        scores, vals = [], []
        for qs, ps, first in where:
            q = rows(q_src, qs)
            if has_prev:
                k2 = jnp.concatenate([rows(k_src, ps), rows(k_src, qs)], axis=0).astype(BF16)
                vals.append(jnp.concatenate([rows(v_src, ps), rows(v_src, qs)], axis=0).astype(BF16))
                variant = jnp.where(first, 1, 0)
            else:
                k2 = rows(k_src, qs).astype(BF16)
                vals.append(rows(v_src, qs).astype(BF16))
            for hh in range(2):
                qm = jnp.where(lo if hh == 0 else ~lo, q, 0.0).astype(BF16)
                bias = bias_ref[br, hh, variant] if has_prev else bias_ref[br, hh, 0, :, BLOCK:2 * BLOCK]
                scores.append(_dot_t(qm, k2) + bias)
        probs = []
        for s in scores:
            sm = jnp.maximum(s[:, 0:BLOCK], s[:, BLOCK:2 * BLOCK]) if has_prev else s
            m = jnp.max(sm, axis=-1, keepdims=True)
            p = jnp.exp2(s - m)
            pm = p[:, 0:BLOCK] + p[:, BLOCK:2 * BLOCK] if has_prev else p
            probs.append((p.astype(BF16), m, jnp.sum(pm, axis=-1, keepdims=True)))
        for u, (qs, _, _) in enumerate(where):
            outs, lses = [], []
            for hh in range(2):
                p, m, den = probs[2 * u + hh]
                outs.append(_dot(p, vals[u]) / den)
                lses.append(jnp.broadcast_to(m + jnp.log2(den), (BLOCK, LANES)))
            ob_scr[br, pl.ds(qs, BLOCK, stride=stride), :] = jnp.where(lo, outs[0], outs[1])
            lse_scr[br, pl.ds(qs, BLOCK, stride=stride), :] = jnp.where(lo, lses[0], lses[1])

    for br, dil in enumerate(C_DILS):
        nb = seq // dil // BLOCK
        staged = dil % C_STAGE == 0
        stride = dil // C_STAGE if staged else dil

        def body(g, carry, br=br, dil=dil, nb=nb, staged=staged, stride=stride):
            where = []
            for u in range(C_UNROLL):
                i = g * C_UNROLL + u
                res = i % dil
                b = i // dil
                base = (res % C_STAGE) * part + res // C_STAGE if staged else res
                step = BLOCK * stride
                where.append((base + b * step, base + jnp.maximum(b - 1, 0) * step, b == 0))
            blocks(br, staged, stride, where, nb > 1)
            return carry

        lax.fori_loop(0, dil * nb // C_UNROLL, body, 0)

    def merge(i, carry):
        rho = i // (part // BLOCK)
        jb = i % (part // BLOCK)
        st = pl.ds(pl.multiple_of(i * BLOCK, BLOCK), BLOCK)
        nat = pl.ds(rho + C_STAGE * BLOCK * jb, BLOCK, stride=C_STAGE)
        ls, os = [], []
        for br, dil in enumerate(C_DILS):
            rows = st if dil % C_STAGE == 0 else nat
            ls.append(lse_scr[br, rows, :])
            os.append(ob_scr[br, rows, :])
        m = jnp.maximum(jnp.maximum(ls[0], ls[1]), ls[2])
        ws = [jnp.exp2(l_ - m) for l_ in ls]
        num = ws[0] * os[0] + ws[1] * os[1] + ws[2] * os[2]
        o_ref[nat, :] = num / (ws[0] + ws[1] + ws[2])
        return carry

    lax.fori_loop(0, seq // BLOCK, merge, 0)


def _attn_c_prompt(q, k, v, bias, l, depth, n, seq, bufs):
    t = q.shape[0]
    blk = pl.BlockSpec((seq, LANES), lambda i, hp: (i, hp))
    n_in = 4
    body, xspecs, xargs, k_alias = _stacked(functools.partial(_attn_c_prompt_kernel, seq=seq), n_in, bufs)
    st_shape = jax.ShapeDtypeStruct((depth, n, 256, seq), F32)
    st_spec = pl.BlockSpec((None, None, LANES, seq), lambda i, hp: (l, i, hp, 0))
    return pl.pallas_call(
        body,
        grid=(n, 2),
        out_shape=[jax.ShapeDtypeStruct((t, 256), F32), st_shape, st_shape],
        in_specs=[blk, blk, blk,
                  pl.BlockSpec((3, 2, 2, BLOCK, 2 * BLOCK), lambda i, hp: (0, hp, 0, 0, 0))] + xspecs,
        out_specs=[blk, st_spec, st_spec],
        scratch_shapes=[pltpu.VMEM((3, seq, LANES), F32), pltpu.VMEM((3, seq, LANES), F32)]
        + [pltpu.VMEM((seq, LANES), F32)] * 3,
        input_output_aliases=_alias_map(n_in, k_alias, 3) if k_alias else {},
        compiler_params=_cparams(("parallel", "parallel")),
        name="attn_c_prompt",
    )(q, k, v, bias, *xargs)


def _gated_group_norm(y, z, ng):
    gated = y * _silu(z)
    parts = []
    for grp in range(2):
        gsl = gated[:, grp * 256:(grp + 1) * 256]
        ms = jnp.mean(gsl * gsl, axis=-1, keepdims=True)
        parts.append(gsl * lax.rsqrt(ms + EPS))
    return jnp.concatenate(parts, axis=1) * ng


def _expand_heads(v, ex):
    hi = v.astype(BF16)
    r1 = v - hi.astype(F32)
    mid = r1.astype(BF16)
    lo = (r1 - mid.astype(F32)).astype(BF16)
    return _dot(hi, ex) + _dot(mid, ex) + _dot(lo, ex)


def _ssd_prompt_kernel(xbc_ref, z_ref, dt_ref, dtb_ref, alog_ref, dsk_ref, ng_ref, tri_ref, ex_ref,
                       ob_ref, hl_ref, hst):
    c = pl.program_id(1)

    @pl.when(c == 0)
    def _():
        hst[...] = jnp.zeros_like(hst)

    xa_all = _silu(xbc_ref[...])
    dt_all = _softplus(dt_ref[...] + dtb_ref[...])
    a_all = dt_all * (-jnp.exp(alog_ref[...]))
    row = lax.broadcasted_iota(jnp.int32, (BLOCK, BLOCK), 0)
    lane = _lane_iota((BLOCK, BLOCK))
    causal = lane <= row
    lo = lane < HEAD_DIM
    top = row < HEAD_DIM
    n_pairs = N_SSM_HEADS // 2
    for sub in range(SSD_SUB):
        rs = slice(sub * BLOCK, (sub + 1) * BLOCK)
        xa, dt = xa_all[rs], dt_all[rs]
        xs = xa[:, 0:D_INNER]
        acum = jnp.dot(tri_ref[...], a_all[rs], precision=HIGHEST, preferred_element_type=F32)
        bms = [xa[:, D_INNER + g * D_STATE:D_INNER + (g + 1) * D_STATE].astype(BF16) for g in range(2)]
        cms = [xa[:, D_INNER + (2 + g) * D_STATE:D_INNER + (3 + g) * D_STATE].astype(BF16) for g in range(2)]
        cbs = [_dot_t(cms[g], bms[g]) for g in range(2)]
        hps = [hst[2 * j:2 * j + 2].reshape(BLOCK, D_STATE) for j in range(n_pairs)]
        y_off = [_dot_t(cms[j // 2], hps[j].astype(BF16)) for j in range(n_pairs)]
        last = acum[BLOCK - 1:BLOCK, :]
        e_last = jnp.exp(last)
        w_all = _expand_heads(jnp.exp(last - acum) * dt, ex_ref[...])
        e_all = _expand_heads(jnp.exp(acum), ex_ref[...])
        xs_pairs = [xs[:, j * LANES:(j + 1) * LANES] for j in range(n_pairs)]
        for j in range(n_pairs):
            h0_, h1_ = 2 * j, 2 * j + 1
            xw = xs_pairs[j] * w_all[:, j * LANES:(j + 1) * LANES]
            s_new = _dot_t0(xw.astype(BF16), bms[j // 2])
            cd_pair = jnp.where(top, e_last[:, h0_:h0_ + 1], e_last[:, h1_:h1_ + 1])
            hst[h0_:h0_ + 2] = (hps[j] * cd_pair + s_new).reshape(2, HEAD_DIM, D_STATE)
        acum_t = acum.T
        dt_t = dt.T
        y_parts = []
        for j in range(n_pairs):
            xs_b = xs_pairs[j].astype(BF16)
            yd = []
            for hh in range(2):
                h = 2 * j + hh
                seg = acum[:, h:h + 1] - acum_t[h:h + 1, :]
                dec = jnp.exp(jnp.where(causal, seg, NEG)) * dt_t[h:h + 1, :]
                yd.append(_dot((cbs[j // 2] * dec).astype(BF16), xs_b))
            y_parts.append(jnp.where(lo, yd[0], yd[1]) + y_off[j] * e_all[:, j * LANES:(j + 1) * LANES])
        y = jnp.concatenate(y_parts, axis=1) + dsk_ref[...] * xs
        ob_ref[rs, :] = _gated_group_norm(y, z_ref[rs, :], ng_ref[...])

    @pl.when(c == pl.num_programs(1) - 1)
    def _():
        hl_ref[...] = hst[...]


def _ssd_prompt(xbc, z, dt, dtb, alog, dsk, ng, tri, ex, l, depth, n, nc, bufs):
    t = xbc.shape[0]
    nc = nc // SSD_SUB
    rows = SSD_SUB * BLOCK
    row = lambda i, c: (i * nc + c, 0)
    layer = lambda i, c: (l, 0, 0)
    n_in = 9
    body, xspecs, xargs, k_alias = _stacked(_ssd_prompt_kernel, n_in, bufs)
    st = (N_SSM_HEADS, HEAD_DIM, D_STATE)
    return pl.pallas_call(
        body,
        grid=(n, nc),
        out_shape=[jax.ShapeDtypeStruct((t, D_INNER), F32),
                   jax.ShapeDtypeStruct((depth, n) + st, F32)],
        in_specs=[pl.BlockSpec((rows, CONV_DIM), row),
                  pl.BlockSpec((rows, D_INNER), row),
                  pl.BlockSpec((rows, LANES), row),
                  pl.BlockSpec((None, 1, LANES), layer),
                  pl.BlockSpec((None, 1, LANES), layer),
                  pl.BlockSpec((None, 1, D_INNER), layer),
                  pl.BlockSpec((None, 1, D_INNER), layer),
                  pl.BlockSpec((BLOCK, BLOCK), lambda i, c: (0, 0)),
                  pl.BlockSpec((LANES, D_INNER), lambda i, c: (0, 0))] + xspecs,
        out_specs=[pl.BlockSpec((rows, D_INNER), row),
                   pl.BlockSpec((None, None) + st, lambda i, c: (l, i, 0, 0, 0))],
        scratch_shapes=[pltpu.VMEM(st, F32)],
        input_output_aliases=_alias_map(n_in, k_alias, 2) if k_alias else {},
        compiler_params=_cparams(("parallel", "arbitrary")),
        name="ssd_prompt",
    )(xbc, z, dt, dtb, alog, dsk, ng, tri, ex, *xargs)


def _attn_a_sample_kernel(sink_ref, q_ref, kn_ref, vn_ref, kt_ref, vt_ref, bc_ref, bn_ref, o_ref, *, nbs, l):
    rb = nbs * SROWS
    lo = _lane_iota((rb, LANES)) < HEAD_DIM
    s0, s1 = q_ref[:, 0:LANES], q_ref[:, LANES:2 * LANES]
    per_seq = lambda v: v.reshape(nbs, SROWS, LANES)
    qm = jnp.concatenate([
        per_seq(jnp.where(lo, s0, 0.0)),
        per_seq(jnp.where(lo, pltpu.roll(s0, HEAD_DIM, 1), 0.0)),
        per_seq(jnp.where(lo, 0.0, pltpu.roll(s1, HEAD_DIM, 1))),
        per_seq(jnp.where(lo, 0.0, s1))], axis=1).astype(BF16)
    kt = kt_ref[...].reshape(nbs, LANES, WIN).astype(BF16)
    vt = vt_ref[...].reshape(nbs, LANES, WIN).astype(BF16)
    kn = kn_ref[...].reshape(nbs, SROWS, LANES).astype(BF16)
    vn = vn_ref[...].reshape(nbs, SROWS, LANES).astype(BF16)
    s_c = jnp.einsum('nqd,ndk->nqk', qm, kt, preferred_element_type=F32) + bc_ref[...]
    s_n = jnp.einsum('nqd,nkd->nqk', qm, kn, preferred_element_type=F32) + bn_ref[...]
    hrow = lax.broadcasted_iota(jnp.int32, (nbs, 4 * SROWS, 1), 1) // SROWS
    sink = jnp.where(hrow == 0, sink_ref[l, 0], jnp.where(hrow == 1, sink_ref[l, 1],
                     jnp.where(hrow == 2, sink_ref[l, 2], sink_ref[l, 3])))
    m = jnp.maximum(jnp.max(s_c, axis=-1, keepdims=True), jnp.max(s_n, axis=-1, keepdims=True))
    m = jnp.maximum(m, sink)
    p_c = jnp.exp2(s_c - m)
    p_n = jnp.exp2(s_n - m)
    den = jnp.sum(p_c, axis=-1, keepdims=True) + jnp.sum(p_n, axis=-1, keepdims=True) + jnp.exp2(sink - m)
    o = (jnp.einsum('nqk,ndk->nqd', p_c.astype(BF16), vt, preferred_element_type=F32)
         + jnp.einsum('nqk,nkd->nqd', p_n.astype(BF16), vn, preferred_element_type=F32)) / den
    o0, o1, o2, o3 = (o[:, i * SROWS:(i + 1) * SROWS, :].reshape(rb, LANES) for i in range(4))
    o_ref[:, 0:LANES] = jnp.where(lo, o0, pltpu.roll(o1, HEAD_DIM, 1))
    o_ref[:, LANES:2 * LANES] = jnp.where(lo, pltpu.roll(o2, HEAD_DIM, 1), o3)


def _attn_a_sample(q8, k8, v8, cache_kt, cache_vt, bias_c, bias_n, sinks, l, nbs):
    ns = cache_kt.shape[1]
    tok = lambda i: (i, 0)
    cache = pl.BlockSpec((None, nbs, 2, HEAD_DIM, WIN), lambda i: (l, i, 0, 0, 0))
    c3 = lambda i: (0, 0, 0)
    return pl.pallas_call(
        functools.partial(_attn_a_sample_kernel, nbs=nbs, l=l),
        grid=(ns // nbs,),
        out_shape=jax.ShapeDtypeStruct((ns * SROWS, 256), F32),
        in_specs=[pl.BlockSpec(memory_space=pltpu.SMEM),
                  pl.BlockSpec((nbs * SROWS, 256), tok),
                  pl.BlockSpec((nbs * SROWS, LANES), tok),
                  pl.BlockSpec((nbs * SROWS, LANES), tok),
                  cache, cache,
                  pl.BlockSpec((1, 4 * SROWS, WIN), c3),
                  pl.BlockSpec((1, 4 * SROWS, SROWS), c3)],
        out_specs=pl.BlockSpec((nbs * SROWS, 256), tok),
        compiler_params=_cparams(("parallel",)),
        name="attn_a_sample",
    )(sinks, q8, k8, v8, cache_kt, cache_vt, bias_c, bias_n)


def _softmax_parts(parts):
    m = None
    for s in parts:
        mm = jnp.max(s, axis=-1, keepdims=True)
        m = mm if m is None else jnp.maximum(m, mm)
    ps = [jnp.exp2(s - m) for s in parts]
    den = None
    for p in ps:
        dd = jnp.sum(p, axis=-1, keepdims=True)
        den = dd if den is None else den + dd
    return ps, den, m + jnp.log2(den)


def _attn_c_sample_kernel(q_ref, kn_ref, vn_ref, kt_ref, vt_ref, b12_ref, b3_ref, bn_ref, o_ref, *, nbs, lc):
    lo = _lane_iota((SROWS, LANES)) < HEAD_DIM
    pr = 2 * SROWS
    units = [(n, j, slice(n * SROWS, (n + 1) * SROWS), slice(j * LANES, (j + 1) * LANES))
             for n in range(nbs) for j in range(2)]
    scores = []
    for n, j, rows, lanes in units:
        qp = q_ref[rows, lanes]
        lhs = jnp.concatenate([jnp.where(lo, qp, 0.0), jnp.where(lo, 0.0, qp)], axis=0).astype(BF16)
        ktp = kt_ref[n, 2 * j:2 * j + 2].reshape(LANES, lc).astype(BF16)
        scores.append((_dot(lhs, ktp), _dot_t(lhs, kn_ref[rows, lanes].astype(BF16))))
    probs = []
    for (n, j, rows, lanes), (s, sn) in zip(units, scores):
        s_near = s[:, lc - NEAR:]
        (p1, p1n), d1, l1 = _softmax_parts([s_near + b12_ref[0, j], sn + bn_ref[0, j]])
        (p2, p2n), d2, l2 = _softmax_parts([s_near + b12_ref[1, j], sn + bn_ref[1, j]])
        (p3, p3n), d3, l3 = _softmax_parts([s + b3_ref[j], sn + bn_ref[2, j]])
        m = jnp.maximum(jnp.maximum(l1, l2), l3)
        w1, w2, w3 = jnp.exp2(l1 - m), jnp.exp2(l2 - m), jnp.exp2(l3 - m)
        wsum = w1 + w2 + w3
        probs.append((jnp.concatenate([p1, p2], axis=0).astype(BF16), p3.astype(BF16),
                      (p1n.astype(BF16), p2n.astype(BF16), p3n.astype(BF16)),
                      (w1 / (d1 * wsum), w2 / (d2 * wsum), w3 / (d3 * wsum))))
    for (n, j, rows, lanes), (p12, p3, pn, wts) in zip(units, probs):
        vtp = vt_ref[n, 2 * j:2 * j + 2].reshape(LANES, lc).astype(BF16)
        vnp = vn_ref[rows, lanes].astype(BF16)
        o12 = _dot_t(p12, vtp[:, lc - NEAR:])
        o = ((o12[0:pr] + _dot(pn[0], vnp)) * wts[0] + (o12[pr:2 * pr] + _dot(pn[1], vnp)) * wts[1]
             + (_dot_t(p3, vtp) + _dot(pn[2], vnp)) * wts[2])
        o_ref[rows, lanes] = jnp.where(lo, o[0:SROWS], o[SROWS:pr])


def _attn_c_sample(q8, k8, v8, cache_kt, cache_vt, b12, b3, bn, l, nbs):
    ns, lc = cache_kt.shape[1], cache_kt.shape[4]
    tok = lambda i: (i, 0)
    cache = pl.BlockSpec((None, nbs, 4, HEAD_DIM, lc), lambda i: (l, i, 0, 0, 0))
    c3 = lambda i: (0, 0, 0)
    c4 = lambda i: (0, 0, 0, 0)
    return pl.pallas_call(
        functools.partial(_attn_c_sample_kernel, nbs=nbs, lc=lc),
        grid=(ns // nbs,),
        out_shape=jax.ShapeDtypeStruct((ns * SROWS, 256), F32),
        in_specs=[pl.BlockSpec((nbs * SROWS, 256), tok),
                  pl.BlockSpec((nbs * SROWS, 256), tok),
                  pl.BlockSpec((nbs * SROWS, 256), tok),
                  cache, cache,
                  pl.BlockSpec((2, 2, 2 * SROWS, NEAR), c4),
                  pl.BlockSpec((2, 2 * SROWS, lc), c3),
                  pl.BlockSpec((3, 2, 2 * SROWS, SROWS), c4)],
        out_specs=pl.BlockSpec((nbs * SROWS, 256), tok),
        compiler_params=_cparams(("parallel",)),
        name="attn_c_sample",
    )(q8, k8, v8, cache_kt, cache_vt, b12, b3, bn)


def _ssd_sample_pre_kernel(xbc_ref, prev_ref, dt_ref, cw_ref, cb_ref, dtb_ref, alog_ref, dsk_ref, ex_ref,
                           yd_ref, eac_ref, xw_ref, bm_ref, cm_ref, cd_ref):
    r = xbc_ref.shape[0]
    rid = lax.broadcasted_iota(jnp.int32, (r, 1), 0) % SROWS
    x = jnp.where(jnp.logical_and(rid >= 1, rid < SVALID), prev_ref[...], xbc_ref[...])
    y = cb_ref[...] + cw_ref[3:4, :] * x
    for d in range(1, 4):
        y = y + cw_ref[3 - d:4 - d, :] * pltpu.roll(x, d, 0)
    xa = _silu(y)
    xs = xa[:, 0:D_INNER]
    bm = xa[:, D_INNER:D_INNER + 2 * D_STATE]
    cm = xa[:, D_INNER + 2 * D_STATE:CONV_DIM]
    valid = rid >= SVALID
    dt = jnp.where(valid, _softplus(dt_ref[...] + dtb_ref[...]), 0.0)
    a = dt * (-jnp.exp(alog_ref[...]))
    acum = a
    rem = jnp.zeros_like(a)
    for d in range(1, 4):
        acum = acum + jnp.where(rid - d >= SVALID, pltpu.roll(a, d, 0), 0.0)
        rem = rem + jnp.where(rid + d < SROWS, pltpu.roll(a, r - d, 0), 0.0)
    ex = ex_ref[...]

    def expand(v):
        return _expand_heads(v, ex)

    lane = _lane_iota((r, LANES))
    y_acc = dsk_ref[...] * xs
    for d in range(4):
        ok = rid - d >= SVALID
        bsh = bm if d == 0 else pltpu.roll(bm, d, 0)
        cb0 = jnp.sum(cm[:, 0:D_STATE] * bsh[:, 0:D_STATE], axis=-1, keepdims=True)
        cb1 = jnp.sum(cm[:, D_STATE:] * bsh[:, D_STATE:], axis=-1, keepdims=True)
        cbh = jnp.where(lane < N_SSM_HEADS // 2, cb0, cb1)
        if d == 0:
            coef = cbh * dt
            xsh = xs
        else:
            dec = jnp.exp(jnp.where(ok, acum - pltpu.roll(acum, d, 0), NEG))
            coef = cbh * dec * pltpu.roll(dt, d, 0)
            xsh = pltpu.roll(xs, d, 0)
        y_acc = y_acc + expand(jnp.where(ok, coef, 0.0)) * xsh
    yd_ref[...] = y_acc
    eac_ref[...] = expand(jnp.exp(acum))
    xw_ref[...] = xs * expand(dt * jnp.exp(rem))
    bm_ref[...] = bm
    cm_ref[...] = cm
    cd_ref[...] = jnp.exp(acum + rem)


def _ssd_sample_pre(xbc8, prev8, dt8, cw8, cb, dtb, alog, dsk, ex, l, rb):
    r = xbc8.shape[0]
    row = lambda i: (i, 0)
    layer = lambda i: (l, 0, 0)
    widths = (D_INNER, D_INNER, D_INNER, 2 * D_STATE, 2 * D_STATE, LANES)
    return pl.pallas_call(
        _ssd_sample_pre_kernel,
        grid=(r // rb,),
        out_shape=[jax.ShapeDtypeStruct((r, w_), F32) for w_ in widths],
        in_specs=[pl.BlockSpec((rb, CONV_DIM), row),
                  pl.BlockSpec((None, rb, CONV_DIM), lambda i: (l, i, 0)),
                  pl.BlockSpec((rb, LANES), row),
                  pl.BlockSpec((None, SUBLANES, CONV_DIM), layer),
                  pl.BlockSpec((None, 1, CONV_DIM), layer),
                  pl.BlockSpec((None, 1, LANES), layer),
                  pl.BlockSpec((None, 1, LANES), layer),
                  pl.BlockSpec((None, 1, D_INNER), layer),
                  pl.BlockSpec((LANES, D_INNER), lambda i: (0, 0))],
        out_specs=[pl.BlockSpec((rb, w_), row) for w_ in widths],
        compiler_params=_cparams(("parallel",)),
        name="ssd_sample_pre",
    )(xbc8, prev8, dt8, cw8, cb, dtb, alog, dsk, ex)


def _ssd_sample_state_kernel(cd_ref, yd_ref, eac_ref, xw_ref, bm_ref, cm_ref, z_ref, h0_ref, ng_ref,
                             ob_ref, hn_ref, *, nbs):
    base = pl.program_id(0) * nbs
    yo = []
    for n in range(nbs):
        rows = slice(n * SROWS, (n + 1) * SROWS)
        parts = []
        for g in range(2):
            hp = h0_ref[n, 4 * g:4 * g + 4].reshape(4 * HEAD_DIM, D_STATE)
            cmg = cm_ref[rows, g * D_STATE:(g + 1) * D_STATE].astype(BF16)
            bmg = bm_ref[rows, g * D_STATE:(g + 1) * D_STATE].astype(BF16)
            parts.append(_dot_t(cmg, hp.astype(BF16)))
            s_new = _dot_t0(xw_ref[rows, g * 256:(g + 1) * 256].astype(BF16), bmg)
            for hh in range(4):
                h = 4 * g + hh
                hn_ref[n, h] = (hp[hh * HEAD_DIM:(hh + 1) * HEAD_DIM, :] * cd_ref[base + n, h]
                                + s_new[hh * HEAD_DIM:(hh + 1) * HEAD_DIM, :])
        yo.append(jnp.concatenate(parts, axis=1))
    y = yd_ref[...] + eac_ref[...] * jnp.concatenate(yo, axis=0)
    ob_ref[...] = _gated_group_norm(y, z_ref[...], ng_ref[...])


def _ssd_sample_state(cd, yd, eac, xw, bm, cm, z8, state, ng, l, depth, nbs, bufs):
    ns = state.shape[1]
    row = lambda i: (i, 0)
    st_spec = pl.BlockSpec((None, nbs, N_SSM_HEADS, HEAD_DIM, D_STATE), lambda i: (l, i, 0, 0, 0))
    rb = nbs * SROWS
    n_in = 9
    body, xspecs, xargs, k_alias = _stacked(functools.partial(_ssd_sample_state_kernel, nbs=nbs), n_in, bufs)
    return pl.pallas_call(
        body,
        grid=(ns // nbs,),
        out_shape=[jax.ShapeDtypeStruct((ns * SROWS, D_INNER), F32),
                   jax.ShapeDtypeStruct(state.shape, F32)],
        in_specs=[pl.BlockSpec(memory_space=pltpu.SMEM),
                  pl.BlockSpec((rb, D_INNER), row),
                  pl.BlockSpec((rb, D_INNER), row),
                  pl.BlockSpec((rb, D_INNER), row),
                  pl.BlockSpec((rb, 2 * D_STATE), row),
                  pl.BlockSpec((rb, 2 * D_STATE), row),
                  pl.BlockSpec((rb, D_INNER), row),
                  st_spec,
                  pl.BlockSpec((None, 1, D_INNER), lambda i: (l, 0, 0))] + xspecs,
        out_specs=[pl.BlockSpec((rb, D_INNER), row), st_spec],
        input_output_aliases=_alias_map(n_in, k_alias, 2) if k_alias else {},
        compiler_params=_cparams(("parallel",)),
        name="ssd_sample_state",
    )(cd, yd, eac, xw, bm, cm, z8, state, ng, *xargs)


def _new_kv_kernel(ka_ref, va_ref, kc_ref, vc_ref, oka_ref, ova_ref, okc_ref, ovc_ref, *, ns):
    def emit(src, dst):
        for t in range(SROWS - SVALID):
            dst[t] = src[pl.ds(SVALID + t, ns, stride=SROWS), :].T

    emit(kc_ref, okc_ref)
    emit(vc_ref, ovc_ref)

    @pl.when(pl.program_id(1) == 0)
    def _():
        emit(ka_ref, oka_ref)
        emit(va_ref, ova_ref)


def _new_kv_states(ka, va, kc, vc, ns):
    depth = ka.shape[0]
    nt = SROWS - SVALID
    narrow_in = pl.BlockSpec((None, ns * SROWS, LANES), lambda l, j: (l, 0, 0))
    wide_in = pl.BlockSpec((None, ns * SROWS, LANES), lambda l, j: (l, 0, j))
    narrow_out = pl.BlockSpec((None, nt, LANES, ns), lambda l, j: (l, 0, 0, 0))
    wide_out = pl.BlockSpec((None, nt, LANES, ns), lambda l, j: (l, 0, j, 0))
    shape = lambda w: jax.ShapeDtypeStruct((depth, nt, w, ns), F32)
    return pl.pallas_call(
        functools.partial(_new_kv_kernel, ns=ns),
        grid=(depth, 2),
        out_shape=[shape(LANES), shape(LANES), shape(2 * LANES), shape(2 * LANES)],
        in_specs=[narrow_in, narrow_in, wide_in, wide_in],
        out_specs=[narrow_out, narrow_out, wide_out, wide_out],
        compiler_params=_cparams(("parallel", "arbitrary")),
        name="new_kv_states",
    )(ka, va, kc, vc)


def _prompt_bias_item(dil, col0):
    qi = np.arange(BLOCK)[:, None]
    kj = np.arange(2 * BLOCK)[None, :]
    dist = BLOCK + qi - kj
    ok = (dist >= 0) & (dist <= WIN)
    variants = [_masked_bucket(dist * dil, ok), _masked_bucket(dist * dil, ok & (kj >= BLOCK))]
    segs = [((2 * h + v) * BLOCK, (2 * h + v + 1) * BLOCK, col0 + h) for h in range(4) for v in range(2)]
    return np.concatenate(variants * 4, axis=0), segs


def _sample_bias_item(dist_of, ok_of, cols, col0):
    t = np.arange(SROWS)[:, None] - SVALID
    bkt = np.where(t >= 0, _masked_bucket(dist_of(t, cols), ok_of(t, cols)), 0).astype(np.int32)
    segs = [(h * SROWS, (h + 1) * SROWS, col0 + h) for h in range(4)]
    return np.concatenate([bkt] * 4, axis=0), segs


def _cache_bias_item(length, window, dil, span, col0):
    dist_of = lambda t, pos: length + t - pos
    ok_of = lambda t, pos: (dist_of(t, pos) >= 0) & (dist_of(t, pos) <= window) & (dist_of(t, pos) % dil == 0)
    return _sample_bias_item(dist_of, ok_of, np.arange(length - span, length)[None, :], col0)


def _new_rows_bias_item(same_token_only, col0):
    dist_of = lambda t, r: t - (r - SVALID)
    if same_token_only:
        ok_of = lambda t, r: (r >= SVALID) & (dist_of(t, r) == 0)
    else:
        ok_of = lambda t, r: (r >= SVALID) & (dist_of(t, r) >= 0)
    return _sample_bias_item(dist_of, ok_of, np.arange(SROWS)[None, :], col0)


Tiling = collections.namedtuple("Tiling", "tm_prompt tm_sample seqs_a seqs_c seqs_b rows_b")


def _tiling(ns):
    rows = ns * SROWS
    return Tiling(tm_prompt=512, tm_sample=min(512, rows), seqs_a=min(16, ns), seqs_c=min(4, ns),
                  seqs_b=min(16, ns), rows_b=min(256, rows))
def kernel(x_prompt, x_sample, cache_a_k, cache_a_v, cache_c_k, cache_c_v, state_ssm, state_conv,
           norm_mix_g, w_in, a_q_norm_g, a_k_norm_g, a_sinks, c_q_norm_g, c_k_norm_g, rel_bias,
           conv_w, conv_b, dt_bias, a_log, d_skip, ssm_norm_g, w_out, norm_mlp_g, w_up, w_down):
    depth = w_in.shape[0]
    n, seq, _ = x_prompt.shape
    ns, ts, _ = x_sample.shape
    lc = cache_c_k.shape[2]
    assert ts == SROWS - SVALID and seq % (BLOCK * 16) == 0
    assert cache_a_k.shape[2] == WIN and all(w <= lc for w, _ in C_BRANCHES)
    nb = seq // BLOCK

    w_in_b = w_in[:, :, :COL_DT].astype(BF16)
    w_dt_b = jnp.pad(w_in[:, :, COL_DT:], ((0, 0), (0, 0), (0, COL_END - IN_COLS))).astype(BF16)
    w_out_b, w_up_b, w_down_b = w_out.astype(BF16), w_up.astype(BF16), w_down.astype(BF16)
    ones = jnp.ones((depth, 128), F32)
    q_scale = ATTN_SCALE * LOG2E
    gain = jnp.concatenate([jnp.tile(a_q_norm_g, (1, 4)) * q_scale, jnp.tile(a_k_norm_g, (1, 2)), ones,
                            jnp.tile(c_q_norm_g, (1, 4)) * q_scale, jnp.tile(c_k_norm_g, (1, 4))], axis=1)
    rel_bias = rel_bias * LOG2E
    a_sinks = a_sinks * LOG2E
    tri = (jnp.arange(BLOCK)[None, :] <= jnp.arange(BLOCK)[:, None]).astype(F32)
    ex = (jnp.arange(LANES)[:, None] == (jnp.arange(D_INNER) // HEAD_DIM)[None, :]).astype(BF16)
    cw8 = jnp.pad(conv_w, ((0, 0), (0, SUBLANES - conv_w.shape[1]), (0, 0)))
    pad_h = lambda v: jnp.pad(v, ((0, 0), (0, LANES - N_SSM_HEADS)))[:, None, :]
    vec = lambda v: v[:, None, :]
    dtb, alog = pad_h(dt_bias), pad_h(a_log)
    dsk = vec(jnp.repeat(d_skip, HEAD_DIM, axis=1))
    g_mix, g_mlp, gain, cb, ng = vec(norm_mix_g), vec(norm_mlp_g), vec(gain), vec(conv_b), vec(ssm_norm_g)

    (w1, d1), (w2, d2), (w3, d3) = C_BRANCHES
    items = ([_prompt_bias_item(1, 0)] + [_prompt_bias_item(d, 4) for d in C_DILS]
             + [_cache_bias_item(WIN, WIN, 1, WIN, 0), _new_rows_bias_item(False, 0)]
             + [_cache_bias_item(lc, w1, d1, NEAR, 4), _cache_bias_item(lc, w2, d2, NEAR, 4),
                _cache_bias_item(lc, w3, d3, lc, 4), _new_rows_bias_item(False, 4), _new_rows_bias_item(True, 4)])
    pa, pc1, pc4, pc16, sa_c, sa_n, sc1, sc2, sc3, sn_all, sn_same = _expand_biases(rel_bias, items)
    by_variant = lambda b: b.reshape(4, 2, BLOCK, 2 * BLOCK)
    by_pair = lambda b: b.reshape(2, 2 * SROWS, b.shape[-1])
    bias_a = by_variant(pa)
    bias_c = jnp.stack([by_variant(pc1), by_variant(pc4), by_variant(pc16)])
    sbias_a_c, sbias_a_n = sa_c[None], sa_n[None]
    sb12 = jnp.stack([by_pair(sc1), by_pair(sc2)])
    sb3 = by_pair(sc3)
    sbn = jnp.stack([by_pair(sn_all), by_pair(sn_same), by_pair(sn_same)])

    cak_t, cav_t = cache_a_k.transpose(0, 1, 3, 4, 2), cache_a_v.transpose(0, 1, 3, 4, 2)
    cck_t, ccv_t = cache_c_k.transpose(0, 1, 3, 4, 2), cache_c_v.transpose(0, 1, 3, 4, 2)

    hp = x_prompt.reshape(n * seq, D_MODEL)
    hs = jnp.pad(x_sample, ((0, 0), (SVALID, 0), (0, 0))).reshape(ns * SROWS, D_MODEL)
    tl = _tiling(ns)
    prev8 = jnp.pad(state_conv, ((0, 0), (0, 0), (1, SROWS - 4), (0, 0))).reshape(depth, ns * SROWS, CONV_DIM)

    p_ak = p_ck = p_ssm = s_ssm = None
    p_conv, s_small = [], [[] for _ in range(5)]
    for l in range(depth):
        qa, ka, va, qc, kc, vc, z, xbc, dt, tail = _in_proj(hp, g_mix, w_in_b, w_dt_b, gain, cw8, cb, l, tl.tm_prompt,
                                                            seq // tl.tm_prompt)
        oa_p, *p_ak = _attn_a_prompt(qa, ka, va, bias_a, a_sinks, l, depth, n, seq, p_ak)
        oc_p, *p_ck = _attn_c_prompt(qc, kc, vc, bias_c, l, depth, n, seq, p_ck)
        ob_p, *p_ssm = _ssd_prompt(xbc, z, dt, dtb, alog, dsk, ng, tri, ex, l, depth, n, nb, p_ssm)
        p_conv.append(tail)

        qa, ka, va, qc, kc, vc, z, xbc, dt, _ = _in_proj(hs, g_mix, w_in_b, w_dt_b, gain, cw8, cb, l, tl.tm_sample, 0)
        oa = _attn_a_sample(qa, ka, va, cak_t, cav_t, sbias_a_c, sbias_a_n, a_sinks, l, tl.seqs_a)
        oc = _attn_c_sample(qc, kc, vc, cck_t, ccv_t, sb12, sb3, sbn, l, tl.seqs_c)
        yd, eac, xw, bm, cm, cd = _ssd_sample_pre(xbc, prev8, dt, cw8, cb, dtb, alog, dsk, ex, l, tl.rows_b)
        cd_s = cd.reshape(ns, SROWS, LANES)[:, SROWS - 1, :N_SSM_HEADS]
        ob, *s_ssm = _ssd_sample_state(cd_s, yd, eac, xw, bm, cm, z, state_ssm, ng, l, depth, tl.seqs_b, s_ssm)
        for dst, val in zip(s_small, (ka, va, kc, vc, xbc)):
            dst.append(val)

        hp, hs = _out_mlp(hp, oa_p, ob_p, oc_p, hs, oa, ob, oc, w_out_b, g_mlp, w_up_b, w_down_b, l,
                          math.gcd(tl.tm_prompt, ns * SROWS))

    y_prompt = hp.reshape(n, seq, D_MODEL)
    y_sample = hs.reshape(ns, SROWS, D_MODEL)[:, SVALID:]
    unminor = lambda buf, heads: buf.reshape(depth, n, heads, HEAD_DIM, -1).transpose(0, 1, 4, 2, 3)
    p_state = (unminor(p_ak[0], 2), unminor(p_ak[1], 2), unminor(p_ck[0], 4), unminor(p_ck[1], 4),
               p_ssm[0], jnp.stack(p_conv)[:, :, SUBLANES - 3:])
    new_kv = _new_kv_states(*(jnp.stack(v) for v in s_small[:4]), ns)
    s_kv = tuple(b.reshape(depth, SROWS - SVALID, hh, HEAD_DIM, ns).transpose(0, 4, 1, 2, 3)
                 for b, hh in zip(new_kv, (2, 2, 4, 4)))
    s_conv = jnp.stack([x.reshape(ns, SROWS, CONV_DIM)[:, SROWS - 3:] for x in s_small[4]])
    s_state = s_kv + (s_ssm[0], s_conv)
    return (y_prompt, y_sample) + p_state + s_state
```

```python
import collections
import functools
import math

import jax
import jax.numpy as jnp
import numpy as np
from jax import lax
from jax.experimental import pallas as pl
from jax.experimental.pallas import tpu as pltpu

F32 = jnp.float32
BF16 = jnp.bfloat16
HIGHEST = lax.Precision.HIGHEST

D_MODEL = 1024
HEAD_DIM = 64
LANES = 128
SUBLANES = 8
BLOCK = 128
WIN = 128
D_INNER = 512
D_STATE = 128
N_SSM_HEADS = 8
CONV_DIM = 1024
D_FF = 4096
NUM_BUCKETS = 32
REL_MAX_DIST = 2048
EPS = 1e-6
ATTN_SCALE = HEAD_DIM ** -0.5
LOG2E = math.log2(math.e)
NEG = -1e30
C_BRANCHES = ((128, 1), (512, 4), (2048, 16))
C_DILS = tuple(d for _, d in C_BRANCHES)
SROWS = 8
SVALID = 4
NEAR = 512
C_STAGE = 4
C_UNROLL = 8
A_UNROLL = 4
SSD_SUB = 4

COL_QA, COL_KA, COL_VA, COL_QC, COL_KC, COL_VC, COL_Z, COL_XBC, COL_DT, COL_END = (
    0, 256, 384, 512, 768, 1024, 1280, 1792, 2816, 2944)
IN_COLS = 2824
V7X_VMEM_BYTES = 64 * 1024 * 1024
VMEM_LIMIT = V7X_VMEM_BYTES * 7 // 8


def _cparams(sem):
    return pltpu.CompilerParams(dimension_semantics=sem, vmem_limit_bytes=VMEM_LIMIT)


def _dot(a, b):
    return jnp.dot(a, b, preferred_element_type=F32)


def _dot_t(a, b):
    return lax.dot_general(a, b, (((1,), (1,)), ((), ())), preferred_element_type=F32)


def _dot_t0(a, b):
    return lax.dot_general(a, b, (((0,), (0,)), ((), ())), preferred_element_type=F32)


def _lane_iota(shape):
    return lax.broadcasted_iota(jnp.int32, shape, len(shape) - 1)


def _silu(x):
    hx = 0.5 * x
    return hx + hx * jnp.tanh(hx)


def _softplus(x):
    return jnp.maximum(x, 0.0) + jnp.log(1.0 + jnp.exp(-jnp.abs(x)))


def _stacked(body, n_in, bufs):
    if bufs is None:
        return body, [], [], 0
    k = len(bufs)

    def wrapped(*refs):
        return body(*refs[:n_in], *refs[n_in + k:])

    return wrapped, [pl.BlockSpec(memory_space=pl.ANY)] * k, list(bufs), k


def _alias_map(n_in, k, n_out):
    return {n_in + i: n_out - k + i for i in range(k)}


def _bias_kernel(tab_ref, *refs, segments):
    n = len(segments)
    for bkt_ref, o_ref, segs in zip(refs[:n], refs[n:], segments):
        for r0, r1, col in segs:
            bkt = bkt_ref[r0:r1, :]
            acc = jnp.full(bkt.shape, NEG, F32)
            for b in range(NUM_BUCKETS):
                acc = jnp.where(bkt == b, tab_ref[b, col], acc)
            o_ref[r0:r1, :] = acc


def _expand_biases(table, items):
    bkts = [jnp.asarray(b) for b, _ in items]
    full = lambda b: pl.BlockSpec(b.shape, lambda: (0, 0))
    return pl.pallas_call(
        functools.partial(_bias_kernel, segments=[s for _, s in items]),
        out_shape=[jax.ShapeDtypeStruct(b.shape, F32) for b in bkts],
        in_specs=[pl.BlockSpec(memory_space=pltpu.SMEM)] + [full(b) for b in bkts],
        out_specs=[full(b) for b in bkts],
        name="bias_expand",
    )(table, *bkts)


def _t5_bucket(dist):
    max_exact = NUM_BUCKETS // 2
    d = np.maximum(dist, 0)
    df = np.maximum(d, 1).astype(np.float32)
    ratio = np.log(df / np.float32(max_exact)) / np.float32(math.log(REL_MAX_DIST / max_exact))
    large = max_exact + (ratio * np.float32(NUM_BUCKETS - max_exact)).astype(np.int32)
    return np.where(d < max_exact, d, np.minimum(large, NUM_BUCKETS - 1))


def _masked_bucket(dist, valid):
    return np.where(valid, _t5_bucket(dist), -1).astype(np.int32)


def _head_norm(p, gain, slabs):
    lo = _lane_iota((p.shape[0], LANES)) < HEAD_DIM
    out = []
    for s in range(slabs):
        x = p[:, s * LANES:(s + 1) * LANES]
        sq = x * x
        s_lo = jnp.sum(jnp.where(lo, sq, 0.0), axis=-1, keepdims=True)
        s_hi = jnp.sum(jnp.where(lo, 0.0, sq), axis=-1, keepdims=True)
        ss = jnp.where(lo, s_lo, s_hi)
        out.append(x * lax.rsqrt(ss * (1.0 / HEAD_DIM) + EPS) * gain[:, s * LANES:(s + 1) * LANES])
    return out[0] if slabs == 1 else jnp.concatenate(out, axis=1)


def _in_proj_kernel(h_ref, g_ref, w_ref, gain_ref, cw_ref, cb_ref,
                    qa_ref, ka_ref, va_ref, qc_ref, kc_ref, vc_ref, z_ref, xbc_ref, dt_ref, tail_ref, xbuf,
                    *, conv_tiles):
    if conv_tiles:
        @pl.when(pl.program_id(0) % conv_tiles == 0)
        def _():
            xbuf[0:SUBLANES, :] = jnp.zeros((SUBLANES, CONV_DIM), F32)

    tm = h_ref.shape[0]
    x = h_ref[...]
    ms = jnp.mean(x * x, axis=-1, keepdims=True)
    u = (x * lax.rsqrt(ms + EPS) * g_ref[...]).astype(BF16)

    def proj(c0, c1):
        return _dot(u, w_ref[:, c0:c1])

    xbc = proj(COL_XBC, COL_DT)
    tail = xbc[tm - SUBLANES:tm, :]
    tail_ref[...] = tail
    if conv_tiles:
        xbuf[SUBLANES:SUBLANES + tm, :] = xbc
        y = cb_ref[...] + cw_ref[3:4, :] * xbc
        for kk in range(3):
            y = y + cw_ref[kk:kk + 1, :] * xbuf[SUBLANES - 3 + kk:SUBLANES - 3 + kk + tm, :]
        xbuf[0:SUBLANES, :] = tail
        xbc_ref[...] = y
    else:
        xbc_ref[...] = xbc
    groups = [COL_QA, COL_KA, COL_QC, COL_KC]
    raw = [proj(c0, c0 + 256) for c0 in groups]
    va_ref[...] = raw[1][:, COL_VA - COL_KA:]
    vc_ref[...] = proj(COL_VC, COL_Z)
    z_ref[...] = proj(COL_Z, COL_XBC)
    qa_ref[...] = _head_norm(raw[0], gain_ref[:, COL_QA:COL_QA + 256], 2)
    ka_ref[...] = _head_norm(raw[1], gain_ref[:, COL_KA:COL_KA + 256], 1)
    qc_ref[...] = _head_norm(raw[2], gain_ref[:, COL_QC:COL_QC + 256], 2)
    kc_ref[...] = _head_norm(raw[3], gain_ref[:, COL_KC:COL_KC + 256], 2)
    dt_ref[...] = jnp.zeros(dt_ref.shape, F32)
    dt_ref[:, 0:N_SSM_HEADS] = proj(COL_DT, IN_COLS)


def _in_proj(h, g, w, gain, cw8, cb, l, tm, conv_tiles):
    t = h.shape[0]
    widths = (256, 128, 128, 256, 256, 256, 512, 1024, 128)
    layer = lambda i: (l, 0, 0)
    n_tail = t // tm // conv_tiles if conv_tiles else t // tm
    per_seq = conv_tiles if conv_tiles else 1
    return pl.pallas_call(
        functools.partial(_in_proj_kernel, conv_tiles=conv_tiles),
        grid=(t // tm,),
        out_shape=[jax.ShapeDtypeStruct((t, w_), F32) for w_ in widths]
        + [jax.ShapeDtypeStruct((n_tail, SUBLANES, CONV_DIM), F32)],
        in_specs=[pl.BlockSpec((tm, D_MODEL), lambda i: (i, 0)),
                  pl.BlockSpec((None, 1, D_MODEL), layer),
                  pl.BlockSpec((None, D_MODEL, IN_COLS), layer),
                  pl.BlockSpec((None, 1, D_MODEL), layer),
                  pl.BlockSpec((None, SUBLANES, CONV_DIM), layer),
                  pl.BlockSpec((None, 1, CONV_DIM), layer)],
        out_specs=[pl.BlockSpec((tm, w_), lambda i: (i, 0)) for w_ in widths]
        + [pl.BlockSpec((None, SUBLANES, CONV_DIM), lambda i: (i // per_seq, 0, 0))],
        scratch_shapes=[pltpu.VMEM((SUBLANES + tm, CONV_DIM), F32)],
        compiler_params=_cparams(("arbitrary",)),
        name="in_proj",
    )(h, g, w, gain, cw8, cb)


def _out_mlp_kernel(hp_ref, oap_ref, obp_ref, ocp_ref, hs_ref, oas_ref, obs_ref, ocs_ref,
                    wo_ref, g_ref, wu_ref, wd_ref, op_ref, os_ref, *, prompt_tiles):
    def tile(h_ref, oa_ref, ob_ref, oc_ref, o_ref):
        acc = _dot(oa_ref[...].astype(BF16), wo_ref[0:256, :])
        acc += _dot(ob_ref[...].astype(BF16), wo_ref[256:768, :])
        acc += _dot(oc_ref[...].astype(BF16), wo_ref[768:1024, :])
        h2 = h_ref[...] + acc
        ms = jnp.mean(h2 * h2, axis=-1, keepdims=True)
        u = (h2 * lax.rsqrt(ms + EPS) * g_ref[...]).astype(BF16)
        ff_chunk = 1024
        mlp = jnp.zeros_like(h2)
        for c in range(D_FF // ff_chunk):
            a = _dot(u, wu_ref[:, c * ff_chunk:(c + 1) * ff_chunk])
            a = jnp.square(jnp.maximum(a, 0.0)).astype(BF16)
            mlp += _dot(a, wd_ref[c * ff_chunk:(c + 1) * ff_chunk, :])
        o_ref[...] = h2 + mlp

    is_prompt = pl.program_id(0) < prompt_tiles

    @pl.when(is_prompt)
    def _():
        tile(hp_ref, oap_ref, obp_ref, ocp_ref, op_ref)

    @pl.when(jnp.logical_not(is_prompt))
    def _():
        tile(hs_ref, oas_ref, obs_ref, ocs_ref, os_ref)


def _out_mlp(hp, oap, obp, ocp, hs, oas, obs, ocs, wo, g, wu, wd, l, tm):
    tp, ts = hp.shape[0], hs.shape[0]
    prompt_tiles = tp // tm
    layer = lambda i: (l, 0, 0)
    prow = lambda i: (jnp.minimum(i, prompt_tiles - 1), 0)
    srow = lambda i: (jnp.maximum(i - prompt_tiles, 0), 0)
    tok = lambda idx: [pl.BlockSpec((tm, w_), idx) for w_ in (D_MODEL, 256, 512, 256)]
    weights = lambda shape: pl.BlockSpec((None,) + shape, layer, pipeline_mode=pl.Buffered(1))
    return pl.pallas_call(
        functools.partial(_out_mlp_kernel, prompt_tiles=prompt_tiles),
        grid=(prompt_tiles + ts // tm,),
        out_shape=[jax.ShapeDtypeStruct((tp, D_MODEL), F32), jax.ShapeDtypeStruct((ts, D_MODEL), F32)],
        in_specs=tok(prow) + tok(srow) + [weights((D_MODEL, D_MODEL)),
                                          pl.BlockSpec((None, 1, D_MODEL), layer),
                                          weights((D_MODEL, D_FF)),
                                          weights((D_FF, D_MODEL))],
        out_specs=[pl.BlockSpec((tm, D_MODEL), prow), pl.BlockSpec((tm, D_MODEL), srow)],
        compiler_params=_cparams(("arbitrary",)),
        name="out_mlp",
    )(hp, oap, obp, ocp, hs, oas, obs, ocs, wo, g, wu, wd)


def _attn_a_prompt_kernel(sink_ref, q_ref, k_ref, v_ref, bias_ref, o_ref, kt_ref, vt_ref,
                          k_st, k_sw, v_st, v_sw, *, l, seq):
    lo = _lane_iota((BLOCK, LANES)) < HEAD_DIM
    k = k_ref[...]
    v = v_ref[...]
    for st in (k_st, k_sw, v_st, v_sw):
        st[0:BLOCK, :] = jnp.zeros((BLOCK, LANES), BF16)
    k_st[BLOCK:, :] = k.astype(BF16)
    v_st[BLOCK:, :] = v.astype(BF16)
    k_sw[BLOCK:, :] = pltpu.roll(k, HEAD_DIM, 1).astype(BF16)
    v_sw[BLOCK:, :] = pltpu.roll(v, HEAD_DIM, 1).astype(BF16)
    kt_ref[...] = k_ref[seq - WIN:seq, :].T
    vt_ref[...] = v_ref[seq - WIN:seq, :].T

    def body(g, carry):
        scores, where = [], []
        for u in range(A_UNROLL):
            b = g * A_UNROLL + u
            cur = pl.ds(pl.multiple_of(b * BLOCK, BLOCK), BLOCK)
            both = pl.ds(pl.multiple_of(b * BLOCK, BLOCK), 2 * BLOCK)
            variant = jnp.where(b == 0, 1, 0)
            where.append((cur, both))
            for h in range(4):
                slab = q_ref[cur, (h // 2) * LANES:(h // 2 + 1) * LANES]
                qm = jnp.where(lo if h % 2 == 0 else ~lo, slab, 0.0).astype(BF16)
                kk = k_st if h in (0, 3) else k_sw
                scores.append(_dot_t(qm, kk[both, :]) + bias_ref[h, variant])
        probs = []
        for i, s in enumerate(scores):
            sink = sink_ref[l, i % 4]
            m = jnp.maximum(jnp.max(jnp.maximum(s[:, 0:BLOCK], s[:, BLOCK:2 * BLOCK]), axis=-1, keepdims=True), sink)
            p = jnp.exp2(s - m)
            den = jnp.sum(p[:, 0:BLOCK] + p[:, BLOCK:2 * BLOCK], axis=-1, keepdims=True) + jnp.exp2(sink - m)
            probs.append((p.astype(BF16), den))
        for u, (cur, both) in enumerate(where):
            outs = []
            for h in range(4):
                p, den = probs[4 * u + h]
                vv = v_st if h in (0, 3) else v_sw
                outs.append(_dot(p, vv[both, :]) / den)
            o_ref[cur, 0:LANES] = jnp.where(lo, outs[0], outs[1])
            o_ref[cur, LANES:2 * LANES] = jnp.where(lo, outs[2], outs[3])
        return carry

    lax.fori_loop(0, seq // BLOCK // A_UNROLL, body, 0)


def _attn_a_prompt(q, k, v, bias, sinks, l, depth, n, seq, bufs):
    t = q.shape[0]
    n_in = 5
    body, xspecs, xargs, k_alias = _stacked(functools.partial(_attn_a_prompt_kernel, l=l, seq=seq), n_in, bufs)
    st_shape = jax.ShapeDtypeStruct((depth, n, LANES, WIN), F32)
    st_spec = pl.BlockSpec((None, None, LANES, WIN), lambda i: (l, i, 0, 0))
    tok = lambda w: pl.BlockSpec((seq, w), lambda i: (i, 0))
    return pl.pallas_call(
        body,
        grid=(n,),
        out_shape=[jax.ShapeDtypeStruct((t, 256), F32), st_shape, st_shape],
        in_specs=[pl.BlockSpec(memory_space=pltpu.SMEM), tok(256), tok(LANES), tok(LANES),
                  pl.BlockSpec((4, 2, BLOCK, 2 * BLOCK), lambda i: (0, 0, 0, 0))] + xspecs,
        out_specs=[tok(256), st_spec, st_spec],
        scratch_shapes=[pltpu.VMEM((BLOCK + seq, LANES), BF16)] * 4,
        input_output_aliases=_alias_map(n_in, k_alias, 3) if k_alias else {},
        compiler_params=_cparams(("parallel",)),
        name="attn_a_prompt",
    )(sinks, q, k, v, bias, *xargs)


def _attn_c_prompt_kernel(q_ref, k_ref, v_ref, bias_ref, o_ref, kt_ref, vt_ref, ob_scr, lse_scr,
                          q_st, k_st, v_st, *, seq):
    lane = _lane_iota((BLOCK, LANES))
    lo = lane < HEAD_DIM
    kt_ref[...] = k_ref[...].T
    vt_ref[...] = v_ref[...].T
    part = seq // C_STAGE
    for src, dst in ((q_ref, q_st), (k_ref, k_st), (v_ref, v_st)):
        for rho in range(C_STAGE):
            dst[rho * part:(rho + 1) * part, :] = src[pl.ds(rho, part, stride=C_STAGE), :]

    def blocks(br, staged, stride, where, has_prev):
        rows = lambda ref, s: ref[pl.ds(s, BLOCK, stride=stride), :]
        q_src, k_src, v_src = (q_st, k_st, v_st) if staged else (q_ref, k_ref, v_ref)
        scores, vals = [], []
        for qs, ps, first in where:
            q = rows(q_src, qs)
            if has_prev:
                k2 = jnp.concatenate([rows(k_src, ps), rows(k_src, qs)], axis=0).astype(BF16)
                vals.append(jnp.concatenate([rows(v_src, ps), rows(v_src, qs)], axis=0).astype(BF16))
                variant = jnp.where(first, 1, 0)
            else:
                k2 = rows(k_src, qs).astype(BF16)
                vals.append(rows(v_src, qs).astype(BF16))
            for hh in range(2):
                qm = jnp.where(lo if hh == 0 else ~lo, q, 0.0).astype(BF16)
                bias = bias_ref[br, hh, variant] if has_prev else bias_ref[br, hh, 0, :, BLOCK:2 * BLOCK]
                scores.append(_dot_t(qm, k2) + bias)
        probs = []
        for s in scores:
            sm = jnp.maximum(s[:, 0:BLOCK], s[:, BLOCK:2 * BLOCK]) if has_prev else s
            m = jnp.max(sm, axis=-1, keepdims=True)
            p = jnp.exp2(s - m)
            pm = p[:, 0:BLOCK] + p[:, BLOCK:2 * BLOCK] if has_prev else p
            probs.append((p.astype(BF16), m, jnp.sum(pm, axis=-1, keepdims=True)))
        for u, (qs, _, _) in enumerate(where):
            outs, lses = [], []
            for hh in range(2):
                p, m, den = probs[2 * u + hh]
                outs.append(_dot(p, vals[u]) / den)
                lses.append(jnp.broadcast_to(m + jnp.log2(den), (BLOCK, LANES)))
            ob_scr[br, pl.ds(qs, BLOCK, stride=stride), :] = jnp.where(lo, outs[0], outs[1])
            lse_scr[br, pl.ds(qs, BLOCK, stride=stride), :] = jnp.where(lo, lses[0], lses[1])

    for br, dil in enumerate(C_DILS):
        nb = seq // dil // BLOCK
        staged = dil % C_STAGE == 0
        stride = dil // C_STAGE if staged else dil

        def body(g, carry, br=br, dil=dil, nb=nb, staged=staged, stride=stride):
            where = []
            for u in range(C_UNROLL):
                i = g * C_UNROLL + u
                res = i % dil
                b = i // dil
                base = (res % C_STAGE) * part + res // C_STAGE if staged else res
                step = BLOCK * stride
                where.append((base + b * step, base + jnp.maximum(b - 1, 0) * step, b == 0))
            blocks(br, staged, stride, where, nb > 1)
            return carry

        lax.fori_loop(0, dil * nb // C_UNROLL, body, 0)

    def merge(i, carry):
        rho = i // (part // BLOCK)
        jb = i % (part // BLOCK)
        st = pl.ds(pl.multiple_of(i * BLOCK, BLOCK), BLOCK)
        nat = pl.ds(rho + C_STAGE * BLOCK * jb, BLOCK, stride=C_STAGE)
        ls, os = [], []
        for br, dil in enumerate(C_DILS):
            rows = st if dil % C_STAGE == 0 else nat
            ls.append(lse_scr[br, rows, :])
            os.append(ob_scr[br, rows, :])
        m = jnp.maximum(jnp.maximum(ls[0], ls[1]), ls[2])
        ws = [jnp.exp2(l_ - m) for l_ in ls]
        num = ws[0] * os[0] + ws[1] * os[1] + ws[2] * os[2]
        o_ref[nat, :] = num / (ws[0] + ws[1] + ws[2])
        return carry

    lax.fori_loop(0, seq // BLOCK, merge, 0)


def _attn_c_prompt(q, k, v, bias, l, depth, n, seq, bufs):
    t = q.shape[0]
    blk = pl.BlockSpec((seq, LANES), lambda i, hp: (i, hp))
    n_in = 4
    body, xspecs, xargs, k_alias = _stacked(functools.partial(_attn_c_prompt_kernel, seq=seq), n_in, bufs)
    st_shape = jax.ShapeDtypeStruct((depth, n, 256, seq), F32)
    st_spec = pl.BlockSpec((None, None, LANES, seq), lambda i, hp: (l, i, hp, 0))
    return pl.pallas_call(
        body,
        grid=(n, 2),
        out_shape=[jax.ShapeDtypeStruct((t, 256), F32), st_shape, st_shape],
        in_specs=[blk, blk, blk,
                  pl.BlockSpec((3, 2, 2, BLOCK, 2 * BLOCK), lambda i, hp: (0, hp, 0, 0, 0))] + xspecs,
        out_specs=[blk, st_spec, st_spec],
        scratch_shapes=[pltpu.VMEM((3, seq, LANES), F32), pltpu.VMEM((3, seq, LANES), F32)]
        + [pltpu.VMEM((seq, LANES), F32)] * 3,
        input_output_aliases=_alias_map(n_in, k_alias, 3) if k_alias else {},
        compiler_params=_cparams(("parallel", "parallel")),
        name="attn_c_prompt",
    )(q, k, v, bias, *xargs)


def _gated_group_norm(y, z, ng):
    gated = y * _silu(z)
    parts = []
    for grp in range(2):
        gsl = gated[:, grp * 256:(grp + 1) * 256]
        ms = jnp.mean(gsl * gsl, axis=-1, keepdims=True)
        parts.append(gsl * lax.rsqrt(ms + EPS))
    return jnp.concatenate(parts, axis=1) * ng


def _expand_heads(v, ex):
    hi = v.astype(BF16)
    r1 = v - hi.astype(F32)
    mid = r1.astype(BF16)
    lo = (r1 - mid.astype(F32)).astype(BF16)
    return _dot(hi, ex) + _dot(mid, ex) + _dot(lo, ex)


def _ssd_prompt_kernel(xbc_ref, z_ref, dt_ref, dtb_ref, alog_ref, dsk_ref, ng_ref, tri_ref, ex_ref,
                       ob_ref, hl_ref, hst):
    c = pl.program_id(1)

    @pl.when(c == 0)
    def _():
        hst[...] = jnp.zeros_like(hst)

    xa_all = _silu(xbc_ref[...])
    dt_all = _softplus(dt_ref[...] + dtb_ref[...])
    a_all = dt_all * (-jnp.exp(alog_ref[...]))
    row = lax.broadcasted_iota(jnp.int32, (BLOCK, BLOCK), 0)
    lane = _lane_iota((BLOCK, BLOCK))
    causal = lane <= row
    lo = lane < HEAD_DIM
    top = row < HEAD_DIM
    n_pairs = N_SSM_HEADS // 2
    for sub in range(SSD_SUB):
        rs = slice(sub * BLOCK, (sub + 1) * BLOCK)
        xa, dt = xa_all[rs], dt_all[rs]
        xs = xa[:, 0:D_INNER]
        acum = jnp.dot(tri_ref[...], a_all[rs], precision=HIGHEST, preferred_element_type=F32)
        bms = [xa[:, D_INNER + g * D_STATE:D_INNER + (g + 1) * D_STATE].astype(BF16) for g in range(2)]
        cms = [xa[:, D_INNER + (2 + g) * D_STATE:D_INNER + (3 + g) * D_STATE].astype(BF16) for g in range(2)]
        cbs = [_dot_t(cms[g], bms[g]) for g in range(2)]
        hps = [hst[2 * j:2 * j + 2].reshape(BLOCK, D_STATE) for j in range(n_pairs)]
        y_off = [_dot_t(cms[j // 2], hps[j].astype(BF16)) for j in range(n_pairs)]
        last = acum[BLOCK - 1:BLOCK, :]
        e_last = jnp.exp(last)
        w_all = _expand_heads(jnp.exp(last - acum) * dt, ex_ref[...])
        e_all = _expand_heads(jnp.exp(acum), ex_ref[...])
        xs_pairs = [xs[:, j * LANES:(j + 1) * LANES] for j in range(n_pairs)]
        for j in range(n_pairs):
            h0_, h1_ = 2 * j, 2 * j + 1
            xw = xs_pairs[j] * w_all[:, j * LANES:(j + 1) * LANES]
            s_new = _dot_t0(xw.astype(BF16), bms[j // 2])
            cd_pair = jnp.where(top, e_last[:, h0_:h0_ + 1], e_last[:, h1_:h1_ + 1])
            hst[h0_:h0_ + 2] = (hps[j] * cd_pair + s_new).reshape(2, HEAD_DIM, D_STATE)
        acum_t = acum.T
        dt_t = dt.T
        y_parts = []
        for j in range(n_pairs):
            xs_b = xs_pairs[j].astype(BF16)
            yd = []
            for hh in range(2):
                h = 2 * j + hh
                seg = acum[:, h:h + 1] - acum_t[h:h + 1, :]
                dec = jnp.exp(jnp.where(causal, seg, NEG)) * dt_t[h:h + 1, :]
                yd.append(_dot((cbs[j // 2] * dec).astype(BF16), xs_b))
            y_parts.append(jnp.where(lo, yd[0], yd[1]) + y_off[j] * e_all[:, j * LANES:(j + 1) * LANES])
        y = jnp.concatenate(y_parts, axis=1) + dsk_ref[...] * xs
        ob_ref[rs, :] = _gated_group_norm(y, z_ref[rs, :], ng_ref[...])

    @pl.when(c == pl.num_programs(1) - 1)
    def _():
        hl_ref[...] = hst[...]


def _ssd_prompt(xbc, z, dt, dtb, alog, dsk, ng, tri, ex, l, depth, n, nc, bufs):
    t = xbc.shape[0]
    nc = nc // SSD_SUB
    rows = SSD_SUB * BLOCK
    row = lambda i, c: (i * nc + c, 0)
    layer = lambda i, c: (l, 0, 0)
    n_in = 9
    body, xspecs, xargs, k_alias = _stacked(_ssd_prompt_kernel, n_in, bufs)
    st = (N_SSM_HEADS, HEAD_DIM, D_STATE)
    return pl.pallas_call(
        body,
        grid=(n, nc),
        out_shape=[jax.ShapeDtypeStruct((t, D_INNER), F32),
                   jax.ShapeDtypeStruct((depth, n) + st, F32)],
        in_specs=[pl.BlockSpec((rows, CONV_DIM), row),
                  pl.BlockSpec((rows, D_INNER), row),
                  pl.BlockSpec((rows, LANES), row),
                  pl.BlockSpec((None, 1, LANES), layer),
                  pl.BlockSpec((None, 1, LANES), layer),
                  pl.BlockSpec((None, 1, D_INNER), layer),
                  pl.BlockSpec((None, 1, D_INNER), layer),
                  pl.BlockSpec((BLOCK, BLOCK), lambda i, c: (0, 0)),
                  pl.BlockSpec((LANES, D_INNER), lambda i, c: (0, 0))] + xspecs,
        out_specs=[pl.BlockSpec((rows, D_INNER), row),
                   pl.BlockSpec((None, None) + st, lambda i, c: (l, i, 0, 0, 0))],
        scratch_shapes=[pltpu.VMEM(st, F32)],
        input_output_aliases=_alias_map(n_in, k_alias, 2) if k_alias else {},
        compiler_params=_cparams(("parallel", "arbitrary")),
        name="ssd_prompt",
    )(xbc, z, dt, dtb, alog, dsk, ng, tri, ex, *xargs)


def _attn_a_sample_kernel(sink_ref, q_ref, kn_ref, vn_ref, kt_ref, vt_ref, bc_ref, bn_ref, o_ref, *, nbs, l):
    rb = nbs * SROWS
    lo = _lane_iota((rb, LANES)) < HEAD_DIM
    s0, s1 = q_ref[:, 0:LANES], q_ref[:, LANES:2 * LANES]
    per_seq = lambda v: v.reshape(nbs, SROWS, LANES)
    qm = jnp.concatenate([
        per_seq(jnp.where(lo, s0, 0.0)),
        per_seq(jnp.where(lo, pltpu.roll(s0, HEAD_DIM, 1), 0.0)),
        per_seq(jnp.where(lo, 0.0, pltpu.roll(s1, HEAD_DIM, 1))),
        per_seq(jnp.where(lo, 0.0, s1))], axis=1).astype(BF16)
    kt = kt_ref[...].reshape(nbs, LANES, WIN).astype(BF16)
    vt = vt_ref[...].reshape(nbs, LANES, WIN).astype(BF16)
    kn = kn_ref[...].reshape(nbs, SROWS, LANES).astype(BF16)
    vn = vn_ref[...].reshape(nbs, SROWS, LANES).astype(BF16)
    s_c = jnp.einsum('nqd,ndk->nqk', qm, kt, preferred_element_type=F32) + bc_ref[...]
    s_n = jnp.einsum('nqd,nkd->nqk', qm, kn, preferred_element_type=F32) + bn_ref[...]
    hrow = lax.broadcasted_iota(jnp.int32, (nbs, 4 * SROWS, 1), 1) // SROWS
    sink = jnp.where(hrow == 0, sink_ref[l, 0], jnp.where(hrow == 1, sink_ref[l, 1],
                     jnp.where(hrow == 2, sink_ref[l, 2], sink_ref[l, 3])))
    m = jnp.maximum(jnp.max(s_c, axis=-1, keepdims=True), jnp.max(s_n, axis=-1, keepdims=True))
    m = jnp.maximum(m, sink)
    p_c = jnp.exp2(s_c - m)
    p_n = jnp.exp2(s_n - m)
    den = jnp.sum(p_c, axis=-1, keepdims=True) + jnp.sum(p_n, axis=-1, keepdims=True) + jnp.exp2(sink - m)
    o = (jnp.einsum('nqk,ndk->nqd', p_c.astype(BF16), vt, preferred_element_type=F32)
         + jnp.einsum('nqk,nkd->nqd', p_n.astype(BF16), vn, preferred_element_type=F32)) / den
    o0, o1, o2, o3 = (o[:, i * SROWS:(i + 1) * SROWS, :].reshape(rb, LANES) for i in range(4))
    o_ref[:, 0:LANES] = jnp.where(lo, o0, pltpu.roll(o1, HEAD_DIM, 1))
    o_ref[:, LANES:2 * LANES] = jnp.where(lo, pltpu.roll(o2, HEAD_DIM, 1), o3)


def _attn_a_sample(q8, k8, v8, cache_kt, cache_vt, bias_c, bias_n, sinks, l, nbs):
    ns = cache_kt.shape[1]
    tok = lambda i: (i, 0)
    cache = pl.BlockSpec((None, nbs, 2, HEAD_DIM, WIN), lambda i: (l, i, 0, 0, 0))
    c3 = lambda i: (0, 0, 0)
    return pl.pallas_call(
        functools.partial(_attn_a_sample_kernel, nbs=nbs, l=l),
        grid=(ns // nbs,),
        out_shape=jax.ShapeDtypeStruct((ns * SROWS, 256), F32),
        in_specs=[pl.BlockSpec(memory_space=pltpu.SMEM),
                  pl.BlockSpec((nbs * SROWS, 256), tok),
                  pl.BlockSpec((nbs * SROWS, LANES), tok),
                  pl.BlockSpec((nbs * SROWS, LANES), tok),
                  cache, cache,
                  pl.BlockSpec((1, 4 * SROWS, WIN), c3),
                  pl.BlockSpec((1, 4 * SROWS, SROWS), c3)],
        out_specs=pl.BlockSpec((nbs * SROWS, 256), tok),
        compiler_params=_cparams(("parallel",)),
        name="attn_a_sample",
    )(sinks, q8, k8, v8, cache_kt, cache_vt, bias_c, bias_n)


def _softmax_parts(parts):
    m = None
    for s in parts:
        mm = jnp.max(s, axis=-1, keepdims=True)
        m = mm if m is None else jnp.maximum(m, mm)
    ps = [jnp.exp2(s - m) for s in parts]
    den = None
    for p in ps:
        dd = jnp.sum(p, axis=-1, keepdims=True)
        den = dd if den is None else den + dd
    return ps, den, m + jnp.log2(den)


def _attn_c_sample_kernel(q_ref, kn_ref, vn_ref, kt_ref, vt_ref, b12_ref, b3_ref, bn_ref, o_ref, *, nbs, lc):
    lo = _lane_iota((SROWS, LANES)) < HEAD_DIM
    pr = 2 * SROWS
    units = [(n, j, slice(n * SROWS, (n + 1) * SROWS), slice(j * LANES, (j + 1) * LANES))
             for n in range(nbs) for j in range(2)]
    scores = []
    for n, j, rows, lanes in units:
        qp = q_ref[rows, lanes]
        lhs = jnp.concatenate([jnp.where(lo, qp, 0.0), jnp.where(lo, 0.0, qp)], axis=0).astype(BF16)
        ktp = kt_ref[n, 2 * j:2 * j + 2].reshape(LANES, lc).astype(BF16)
        scores.append((_dot(lhs, ktp), _dot_t(lhs, kn_ref[rows, lanes].astype(BF16))))
    probs = []
    for (n, j, rows, lanes), (s, sn) in zip(units, scores):
        s_near = s[:, lc - NEAR:]
        (p1, p1n), d1, l1 = _softmax_parts([s_near + b12_ref[0, j], sn + bn_ref[0, j]])
        (p2, p2n), d2, l2 = _softmax_parts([s_near + b12_ref[1, j], sn + bn_ref[1, j]])
        (p3, p3n), d3, l3 = _softmax_parts([s + b3_ref[j], sn + bn_ref[2, j]])
        m = jnp.maximum(jnp.maximum(l1, l2), l3)
        w1, w2, w3 = jnp.exp2(l1 - m), jnp.exp2(l2 - m), jnp.exp2(l3 - m)
        wsum = w1 + w2 + w3
        probs.append((jnp.concatenate([p1, p2], axis=0).astype(BF16), p3.astype(BF16),
                      (p1n.astype(BF16), p2n.astype(BF16), p3n.astype(BF16)),
                      (w1 / (d1 * wsum), w2 / (d2 * wsum), w3 / (d3 * wsum))))
    for (n, j, rows, lanes), (p12, p3, pn, wts) in zip(units, probs):
        vtp = vt_ref[n, 2 * j:2 * j + 2].reshape(LANES, lc).astype(BF16)
        vnp = vn_ref[rows, lanes].astype(BF16)
        o12 = _dot_t(p12, vtp[:, lc - NEAR:])
        o = ((o12[0:pr] + _dot(pn[0], vnp)) * wts[0] + (o12[pr:2 * pr] + _dot(pn[1], vnp)) * wts[1]
             + (_dot_t(p3, vtp) + _dot(pn[2], vnp)) * wts[2])
        o_ref[rows, lanes] = jnp.where(lo, o[0:SROWS], o[SROWS:pr])


def _attn_c_sample(q8, k8, v8, cache_kt, cache_vt, b12, b3, bn, l, nbs):
    ns, lc = cache_kt.shape[1], cache_kt.shape[4]
    tok = lambda i: (i, 0)
    cache = pl.BlockSpec((None, nbs, 4, HEAD_DIM, lc), lambda i: (l, i, 0, 0, 0))
    c3 = lambda i: (0, 0, 0)
    c4 = lambda i: (0, 0, 0, 0)
    return pl.pallas_call(
        functools.partial(_attn_c_sample_kernel, nbs=nbs, lc=lc),
        grid=(ns // nbs,),
        out_shape=jax.ShapeDtypeStruct((ns * SROWS, 256), F32),
        in_specs=[pl.BlockSpec((nbs * SROWS, 256), tok),
                  pl.BlockSpec((nbs * SROWS, 256), tok),
                  pl.BlockSpec((nbs * SROWS, 256), tok),
                  cache, cache,
                  pl.BlockSpec((2, 2, 2 * SROWS, NEAR), c4),
                  pl.BlockSpec((2, 2 * SROWS, lc), c3),
                  pl.BlockSpec((3, 2, 2 * SROWS, SROWS), c4)],
        out_specs=pl.BlockSpec((nbs * SROWS, 256), tok),
        compiler_params=_cparams(("parallel",)),
        name="attn_c_sample",
    )(q8, k8, v8, cache_kt, cache_vt, b12, b3, bn)


def _ssd_sample_pre_kernel(xbc_ref, prev_ref, dt_ref, cw_ref, cb_ref, dtb_ref, alog_ref, dsk_ref, ex_ref,
                           yd_ref, eac_ref, xw_ref, bm_ref, cm_ref, cd_ref):
    r = xbc_ref.shape[0]
    rid = lax.broadcasted_iota(jnp.int32, (r, 1), 0) % SROWS
    x = jnp.where(jnp.logical_and(rid >= 1, rid < SVALID), prev_ref[...], xbc_ref[...])
    y = cb_ref[...] + cw_ref[3:4, :] * x
    for d in range(1, 4):
        y = y + cw_ref[3 - d:4 - d, :] * pltpu.roll(x, d, 0)
    xa = _silu(y)
    xs = xa[:, 0:D_INNER]
    bm = xa[:, D_INNER:D_INNER + 2 * D_STATE]
    cm = xa[:, D_INNER + 2 * D_STATE:CONV_DIM]
    valid = rid >= SVALID
    dt = jnp.where(valid, _softplus(dt_ref[...] + dtb_ref[...]), 0.0)
    a = dt * (-jnp.exp(alog_ref[...]))
    acum = a
    rem = jnp.zeros_like(a)
    for d in range(1, 4):
        acum = acum + jnp.where(rid - d >= SVALID, pltpu.roll(a, d, 0), 0.0)
        rem = rem + jnp.where(rid + d < SROWS, pltpu.roll(a, r - d, 0), 0.0)
    ex = ex_ref[...]

    def expand(v):
        return _expand_heads(v, ex)

    lane = _lane_iota((r, LANES))
    y_acc = dsk_ref[...] * xs
    for d in range(4):
        ok = rid - d >= SVALID
        bsh = bm if d == 0 else pltpu.roll(bm, d, 0)
        cb0 = jnp.sum(cm[:, 0:D_STATE] * bsh[:, 0:D_STATE], axis=-1, keepdims=True)
        cb1 = jnp.sum(cm[:, D_STATE:] * bsh[:, D_STATE:], axis=-1, keepdims=True)
        cbh = jnp.where(lane < N_SSM_HEADS // 2, cb0, cb1)
        if d == 0:
            coef = cbh * dt
            xsh = xs
        else:
            dec = jnp.exp(jnp.where(ok, acum - pltpu.roll(acum, d, 0), NEG))
            coef = cbh * dec * pltpu.roll(dt, d, 0)
            xsh = pltpu.roll(xs, d, 0)
        y_acc = y_acc + expand(jnp.where(ok, coef, 0.0)) * xsh
    yd_ref[...] = y_acc
    eac_ref[...] = expand(jnp.exp(acum))
    xw_ref[...] = xs * expand(dt * jnp.exp(rem))
    bm_ref[...] = bm
    cm_ref[...] = cm
    cd_ref[...] = jnp.exp(acum + rem)


def _ssd_sample_pre(xbc8, prev8, dt8, cw8, cb, dtb, alog, dsk, ex, l, rb):
    r = xbc8.shape[0]
    row = lambda i: (i, 0)
    layer = lambda i: (l, 0, 0)
    widths = (D_INNER, D_INNER, D_INNER, 2 * D_STATE, 2 * D_STATE, LANES)
    return pl.pallas_call(
        _ssd_sample_pre_kernel,
        grid=(r // rb,),
        out_shape=[jax.ShapeDtypeStruct((r, w_), F32) for w_ in widths],
        in_specs=[pl.BlockSpec((rb, CONV_DIM), row),
                  pl.BlockSpec((None, rb, CONV_DIM), lambda i: (l, i, 0)),
                  pl.BlockSpec((rb, LANES), row),
                  pl.BlockSpec((None, SUBLANES, CONV_DIM), layer),
                  pl.BlockSpec((None, 1, CONV_DIM), layer),
                  pl.BlockSpec((None, 1, LANES), layer),
                  pl.BlockSpec((None, 1, LANES), layer),
                  pl.BlockSpec((None, 1, D_INNER), layer),
                  pl.BlockSpec((LANES, D_INNER), lambda i: (0, 0))],
        out_specs=[pl.BlockSpec((rb, w_), row) for w_ in widths],
        compiler_params=_cparams(("parallel",)),
        name="ssd_sample_pre",
    )(xbc8, prev8, dt8, cw8, cb, dtb, alog, dsk, ex)


def _ssd_sample_state_kernel(cd_ref, yd_ref, eac_ref, xw_ref, bm_ref, cm_ref, z_ref, h0_ref, ng_ref,
                             ob_ref, hn_ref, *, nbs):
    base = pl.program_id(0) * nbs
    yo = []
    for n in range(nbs):
        rows = slice(n * SROWS, (n + 1) * SROWS)
        parts = []
        for g in range(2):
            hp = h0_ref[n, 4 * g:4 * g + 4].reshape(4 * HEAD_DIM, D_STATE)
            cmg = cm_ref[rows, g * D_STATE:(g + 1) * D_STATE].astype(BF16)
            bmg = bm_ref[rows, g * D_STATE:(g + 1) * D_STATE].astype(BF16)
            parts.append(_dot_t(cmg, hp.astype(BF16)))
            s_new = _dot_t0(xw_ref[rows, g * 256:(g + 1) * 256].astype(BF16), bmg)
            for hh in range(4):
                h = 4 * g + hh
                hn_ref[n, h] = (hp[hh * HEAD_DIM:(hh + 1) * HEAD_DIM, :] * cd_ref[base + n, h]
                                + s_new[hh * HEAD_DIM:(hh + 1) * HEAD_DIM, :])
        yo.append(jnp.concatenate(parts, axis=1))
    y = yd_ref[...] + eac_ref[...] * jnp.concatenate(yo, axis=0)
    ob_ref[...] = _gated_group_norm(y, z_ref[...], ng_ref[...])


def _ssd_sample_state(cd, yd, eac, xw, bm, cm, z8, state, ng, l, depth, nbs, bufs):
    ns = state.shape[1]
    row = lambda i: (i, 0)
    st_spec = pl.BlockSpec((None, nbs, N_SSM_HEADS, HEAD_DIM, D_STATE), lambda i: (l, i, 0, 0, 0))
    rb = nbs * SROWS
    n_in = 9
    body, xspecs, xargs, k_alias = _stacked(functools.partial(_ssd_sample_state_kernel, nbs=nbs), n_in, bufs)
    return pl.pallas_call(
        body,
        grid=(ns // nbs,),
        out_shape=[jax.ShapeDtypeStruct((ns * SROWS, D_INNER), F32),
                   jax.ShapeDtypeStruct(state.shape, F32)],
        in_specs=[pl.BlockSpec(memory_space=pltpu.SMEM),
                  pl.BlockSpec((rb, D_INNER), row),
                  pl.BlockSpec((rb, D_INNER), row),
                  pl.BlockSpec((rb, D_INNER), row),
                  pl.BlockSpec((rb, 2 * D_STATE), row),
                  pl.BlockSpec((rb, 2 * D_STATE), row),
                  pl.BlockSpec((rb, D_INNER), row),
                  st_spec,
                  pl.BlockSpec((None, 1, D_INNER), lambda i: (l, 0, 0))] + xspecs,
        out_specs=[pl.BlockSpec((rb, D_INNER), row), st_spec],
        input_output_aliases=_alias_map(n_in, k_alias, 2) if k_alias else {},
        compiler_params=_cparams(("parallel",)),
        name="ssd_sample_state",
    )(cd, yd, eac, xw, bm, cm, z8, state, ng, *xargs)


def _new_kv_kernel(ka_ref, va_ref, kc_ref, vc_ref, oka_ref, ova_ref, okc_ref, ovc_ref, *, ns):
    def emit(src, dst):
        for t in range(SROWS - SVALID):
            dst[t] = src[pl.ds(SVALID + t, ns, stride=SROWS), :].T

    emit(kc_ref, okc_ref)
    emit(vc_ref, ovc_ref)

    @pl.when(pl.program_id(1) == 0)
    def _():
        emit(ka_ref, oka_ref)
        emit(va_ref, ova_ref)


def _new_kv_states(ka, va, kc, vc, ns):
    depth = ka.shape[0]
    nt = SROWS - SVALID
    narrow_in = pl.BlockSpec((None, ns * SROWS, LANES), lambda l, j: (l, 0, 0))
    wide_in = pl.BlockSpec((None, ns * SROWS, LANES), lambda l, j: (l, 0, j))
    narrow_out = pl.BlockSpec((None, nt, LANES, ns), lambda l, j: (l, 0, 0, 0))
    wide_out = pl.BlockSpec((None, nt, LANES, ns), lambda l, j: (l, 0, j, 0))
    shape = lambda w: jax.ShapeDtypeStruct((depth, nt, w, ns), F32)
    return pl.pallas_call(
        functools.partial(_new_kv_kernel, ns=ns),
        grid=(depth, 2),
        out_shape=[shape(LANES), shape(LANES), shape(2 * LANES), shape(2 * LANES)],
        in_specs=[narrow_in, narrow_in, wide_in, wide_in],
        out_specs=[narrow_out, narrow_out, wide_out, wide_out],
        compiler_params=_cparams(("parallel", "arbitrary")),
        name="new_kv_states",
    )(ka, va, kc, vc)


def _prompt_bias_item(dil, col0):
    qi = np.arange(BLOCK)[:, None]
    kj = np.arange(2 * BLOCK)[None, :]
    dist = BLOCK + qi - kj
    ok = (dist >= 0) & (dist <= WIN)
    variants = [_masked_bucket(dist * dil, ok), _masked_bucket(dist * dil, ok & (kj >= BLOCK))]
    segs = [((2 * h + v) * BLOCK, (2 * h + v + 1) * BLOCK, col0 + h) for h in range(4) for v in range(2)]
    return np.concatenate(variants * 4, axis=0), segs


def _sample_bias_item(dist_of, ok_of, cols, col0):
    t = np.arange(SROWS)[:, None] - SVALID
    bkt = np.where(t >= 0, _masked_bucket(dist_of(t, cols), ok_of(t, cols)), 0).astype(np.int32)
    segs = [(h * SROWS, (h + 1) * SROWS, col0 + h) for h in range(4)]
    return np.concatenate([bkt] * 4, axis=0), segs


def _cache_bias_item(length, window, dil, span, col0):
    dist_of = lambda t, pos: length + t - pos
    ok_of = lambda t, pos: (dist_of(t, pos) >= 0) & (dist_of(t, pos) <= window) & (dist_of(t, pos) % dil == 0)
    return _sample_bias_item(dist_of, ok_of, np.arange(length - span, length)[None, :], col0)


def _new_rows_bias_item(same_token_only, col0):
    dist_of = lambda t, r: t - (r - SVALID)
    if same_token_only:
        ok_of = lambda t, r: (r >= SVALID) & (dist_of(t, r) == 0)
    else:
        ok_of = lambda t, r: (r >= SVALID) & (dist_of(t, r) >= 0)
    return _sample_bias_item(dist_of, ok_of, np.arange(SROWS)[None, :], col0)


Tiling = collections.namedtuple("Tiling", "tm_prompt tm_sample seqs_a seqs_c seqs_b rows_b")


def _tiling(ns):
    rows = ns * SROWS
    return Tiling(tm_prompt=512, tm_sample=min(512, rows), seqs_a=min(16, ns), seqs_c=min(4, ns),
                  seqs_b=min(16, ns), rows_b=min(256, rows))
def kernel(x_prompt, x_sample, cache_a_k, cache_a_v, cache_c_k, cache_c_v, state_ssm, state_conv,
           norm_mix_g, w_in, a_q_norm_g, a_k_norm_g, a_sinks, c_q_norm_g, c_k_norm_g, rel_bias,
           conv_w, conv_b, dt_bias, a_log, d_skip, ssm_norm_g, w_out, norm_mlp_g, w_up, w_down):
    depth = w_in.shape[0]
    n, seq, _ = x_prompt.shape
    ns, ts, _ = x_sample.shape
    lc = cache_c_k.shape[2]
    assert ts == SROWS - SVALID and seq % (BLOCK * 16) == 0
    assert cache_a_k.shape[2] == WIN and all(w <= lc for w, _ in C_BRANCHES)
    nb = seq // BLOCK

    w_in_b = w_in.astype(BF16)
    w_out_b, w_up_b, w_down_b = w_out.astype(BF16), w_up.astype(BF16), w_down.astype(BF16)
    ones = jnp.ones((depth, 128), F32)
    q_scale = ATTN_SCALE * LOG2E
    gain = jnp.concatenate([jnp.tile(a_q_norm_g, (1, 4)) * q_scale, jnp.tile(a_k_norm_g, (1, 2)), ones,
                            jnp.tile(c_q_norm_g, (1, 4)) * q_scale, jnp.tile(c_k_norm_g, (1, 4))], axis=1)
    rel_bias = rel_bias * LOG2E
    a_sinks = a_sinks * LOG2E
    tri = (jnp.arange(BLOCK)[None, :] <= jnp.arange(BLOCK)[:, None]).astype(F32)
    ex = (jnp.arange(LANES)[:, None] == (jnp.arange(D_INNER) // HEAD_DIM)[None, :]).astype(BF16)
    cw8 = jnp.pad(conv_w, ((0, 0), (0, SUBLANES - conv_w.shape[1]), (0, 0)))
    pad_h = lambda v: jnp.pad(v, ((0, 0), (0, LANES - N_SSM_HEADS)))[:, None, :]
    vec = lambda v: v[:, None, :]
    dtb, alog = pad_h(dt_bias), pad_h(a_log)
    dsk = vec(jnp.repeat(d_skip, HEAD_DIM, axis=1))
    g_mix, g_mlp, gain, cb, ng = vec(norm_mix_g), vec(norm_mlp_g), vec(gain), vec(conv_b), vec(ssm_norm_g)

    (w1, d1), (w2, d2), (w3, d3) = C_BRANCHES
    items = ([_prompt_bias_item(1, 0)] + [_prompt_bias_item(d, 4) for d in C_DILS]
             + [_cache_bias_item(WIN, WIN, 1, WIN, 0), _new_rows_bias_item(False, 0)]
             + [_cache_bias_item(lc, w1, d1, NEAR, 4), _cache_bias_item(lc, w2, d2, NEAR, 4),
                _cache_bias_item(lc, w3, d3, lc, 4), _new_rows_bias_item(False, 4), _new_rows_bias_item(True, 4)])
    pa, pc1, pc4, pc16, sa_c, sa_n, sc1, sc2, sc3, sn_all, sn_same = _expand_biases(rel_bias, items)
    by_variant = lambda b: b.reshape(4, 2, BLOCK, 2 * BLOCK)
    by_pair = lambda b: b.reshape(2, 2 * SROWS, b.shape[-1])
    bias_a = by_variant(pa)
    bias_c = jnp.stack([by_variant(pc1), by_variant(pc4), by_variant(pc16)])
    sbias_a_c, sbias_a_n = sa_c[None], sa_n[None]
    sb12 = jnp.stack([by_pair(sc1), by_pair(sc2)])
    sb3 = by_pair(sc3)
    sbn = jnp.stack([by_pair(sn_all), by_pair(sn_same), by_pair(sn_same)])

    cak_t, cav_t = cache_a_k.transpose(0, 1, 3, 4, 2), cache_a_v.transpose(0, 1, 3, 4, 2)
    cck_t, ccv_t = cache_c_k.transpose(0, 1, 3, 4, 2), cache_c_v.transpose(0, 1, 3, 4, 2)

    hp = x_prompt.reshape(n * seq, D_MODEL)
    hs = jnp.pad(x_sample, ((0, 0), (SVALID, 0), (0, 0))).reshape(ns * SROWS, D_MODEL)
    tl = _tiling(ns)
    prev8 = jnp.pad(state_conv, ((0, 0), (0, 0), (1, SROWS - 4), (0, 0))).reshape(depth, ns * SROWS, CONV_DIM)

    p_ak = p_ck = p_ssm = s_ssm = None
    p_conv, s_small = [], [[] for _ in range(5)]
    for l in range(depth):
        qa, ka, va, qc, kc, vc, z, xbc, dt, tail = _in_proj(hp, g_mix, w_in_b, gain, cw8, cb, l, tl.tm_prompt,
                                                            seq // tl.tm_prompt)
        oa_p, *p_ak = _attn_a_prompt(qa, ka, va, bias_a, a_sinks, l, depth, n, seq, p_ak)
        oc_p, *p_ck = _attn_c_prompt(qc, kc, vc, bias_c, l, depth, n, seq, p_ck)
        ob_p, *p_ssm = _ssd_prompt(xbc, z, dt, dtb, alog, dsk, ng, tri, ex, l, depth, n, nb, p_ssm)
        p_conv.append(tail)

        qa, ka, va, qc, kc, vc, z, xbc, dt, _ = _in_proj(hs, g_mix, w_in_b, gain, cw8, cb, l, tl.tm_sample, 0)
        oa = _attn_a_sample(qa, ka, va, cak_t, cav_t, sbias_a_c, sbias_a_n, a_sinks, l, tl.seqs_a)
        oc = _attn_c_sample(qc, kc, vc, cck_t, ccv_t, sb12, sb3, sbn, l, tl.seqs_c)
        yd, eac, xw, bm, cm, cd = _ssd_sample_pre(xbc, prev8, dt, cw8, cb, dtb, alog, dsk, ex, l, tl.rows_b)
        cd_s = cd.reshape(ns, SROWS, LANES)[:, SROWS - 1, :N_SSM_HEADS]
        ob, *s_ssm = _ssd_sample_state(cd_s, yd, eac, xw, bm, cm, z, state_ssm, ng, l, depth, tl.seqs_b, s_ssm)
        for dst, val in zip(s_small, (ka, va, kc, vc, xbc)):
            dst.append(val)

        hp, hs = _out_mlp(hp, oa_p, ob_p, oc_p, hs, oa, ob, oc, w_out_b, g_mlp, w_up_b, w_down_b, l,
                          math.gcd(tl.tm_prompt, ns * SROWS))

    y_prompt = hp.reshape(n, seq, D_MODEL)
    y_sample = hs.reshape(ns, SROWS, D_MODEL)[:, SVALID:]
    unminor = lambda buf, heads: buf.reshape(depth, n, heads, HEAD_DIM, -1).transpose(0, 1, 4, 2, 3)
    p_state = (unminor(p_ak[0], 2), unminor(p_ak[1], 2), unminor(p_ck[0], 4), unminor(p_ck[1], 4),
               p_ssm[0], jnp.stack(p_conv)[:, :, SUBLANES - 3:])
    new_kv = _new_kv_states(*(jnp.stack(v) for v in s_small[:4]), ns)
    s_kv = tuple(b.reshape(depth, SROWS - SVALID, hh, HEAD_DIM, ns).transpose(0, 4, 1, 2, 3)
                 for b, hh in zip(new_kv, (2, 2, 4, 4)))
    s_conv = jnp.stack([x.reshape(ns, SROWS, CONV_DIM)[:, SROWS - 3:] for x in s_small[4]])
    s_state = s_kv + (s_ssm[0], s_conv)
    return (y_prompt, y_sample) + p_state + s_state
```

```python
import collections
import functools
import math

import jax
import jax.numpy as jnp
import numpy as np
from jax import lax
from jax.experimental import pallas as pl
from jax.experimental.pallas import tpu as pltpu

F32 = jnp.float32
BF16 = jnp.bfloat16
HIGHEST = lax.Precision.HIGHEST

D_MODEL = 1024
HEAD_DIM = 64
LANES = 128
SUBLANES = 8
BLOCK = 128
WIN = 128
D_INNER = 512
D_STATE = 128
N_SSM_HEADS = 8
CONV_DIM = 1024
D_FF = 4096
NUM_BUCKETS = 32
REL_MAX_DIST = 2048
EPS = 1e-6
ATTN_SCALE = HEAD_DIM ** -0.5
LOG2E = math.log2(math.e)
NEG = -1e30
C_BRANCHES = ((128, 1), (512, 4), (2048, 16))
C_DILS = tuple(d for _, d in C_BRANCHES)
SROWS = 8
SVALID = 4
NEAR = 512
C_STAGE = 4
C_UNROLL = 8
A_UNROLL = 4
SSD_SUB = 4

COL_QA, COL_KA, COL_VA, COL_QC, COL_KC, COL_VC, COL_Z, COL_XBC, COL_DT, COL_END = (
    0, 256, 384, 512, 768, 1024, 1280, 1792, 2816, 2944)
IN_COLS = 2824
V7X_VMEM_BYTES = 64 * 1024 * 1024
VMEM_LIMIT = V7X_VMEM_BYTES * 7 // 8


def _cparams(sem):
    return pltpu.CompilerParams(dimension_semantics=sem, vmem_limit_bytes=VMEM_LIMIT)


def _dot(a, b):
    return jnp.dot(a, b, preferred_element_type=F32)


def _dot_t(a, b):
    return lax.dot_general(a, b, (((1,), (1,)), ((), ())), preferred_element_type=F32)


def _dot_t0(a, b):
    return lax.dot_general(a, b, (((0,), (0,)), ((), ())), preferred_element_type=F32)


def _lane_iota(shape):
    return lax.broadcasted_iota(jnp.int32, shape, len(shape) - 1)


def _silu(x):
    hx = 0.5 * x
    return hx + hx * jnp.tanh(hx)


def _softplus(x):
    return jnp.maximum(x, 0.0) + jnp.log(1.0 + jnp.exp(-jnp.abs(x)))


def _stacked(body, n_in, bufs):
    if bufs is None:
        return body, [], [], 0
    k = len(bufs)

    def wrapped(*refs):
        return body(*refs[:n_in], *refs[n_in + k:])

    return wrapped, [pl.BlockSpec(memory_space=pl.ANY)] * k, list(bufs), k


def _alias_map(n_in, k, n_out):
    return {n_in + i: n_out - k + i for i in range(k)}


def _bias_kernel(tab_ref, *refs, segments):
    n = len(segments)
    for bkt_ref, o_ref, segs in zip(refs[:n], refs[n:], segments):
        for r0, r1, col in segs:
            bkt = bkt_ref[r0:r1, :]
            acc = jnp.full(bkt.shape, NEG, F32)
            for b in range(NUM_BUCKETS):
                acc = jnp.where(bkt == b, tab_ref[b, col], acc)
            o_ref[r0:r1, :] = acc


def _expand_biases(table, items):
    bkts = [jnp.asarray(b) for b, _ in items]
    full = lambda b: pl.BlockSpec(b.shape, lambda: (0, 0))
    return pl.pallas_call(
        functools.partial(_bias_kernel, segments=[s for _, s in items]),
        out_shape=[jax.ShapeDtypeStruct(b.shape, F32) for b in bkts],
        in_specs=[pl.BlockSpec(memory_space=pltpu.SMEM)] + [full(b) for b in bkts],
        out_specs=[full(b) for b in bkts],
        name="bias_expand",
    )(table, *bkts)


def _t5_bucket(dist):
    max_exact = NUM_BUCKETS // 2
    d = np.maximum(dist, 0)
    df = np.maximum(d, 1).astype(np.float32)
    ratio = np.log(df / np.float32(max_exact)) / np.float32(math.log(REL_MAX_DIST / max_exact))
    large = max_exact + (ratio * np.float32(NUM_BUCKETS - max_exact)).astype(np.int32)
    return np.where(d < max_exact, d, np.minimum(large, NUM_BUCKETS - 1))


def _masked_bucket(dist, valid):
    return np.where(valid, _t5_bucket(dist), -1).astype(np.int32)


def _head_norm(p, gain, slabs):
    lo = _lane_iota((p.shape[0], LANES)) < HEAD_DIM
    out = []
    for s in range(slabs):
        x = p[:, s * LANES:(s + 1) * LANES]
        sq = x * x
        s_lo = jnp.sum(jnp.where(lo, sq, 0.0), axis=-1, keepdims=True)
        s_hi = jnp.sum(jnp.where(lo, 0.0, sq), axis=-1, keepdims=True)
        ss = jnp.where(lo, s_lo, s_hi)
        out.append(x * lax.rsqrt(ss * (1.0 / HEAD_DIM) + EPS) * gain[:, s * LANES:(s + 1) * LANES])
    return out[0] if slabs == 1 else jnp.concatenate(out, axis=1)


def _in_proj_kernel(h_ref, g_ref, w_ref, gain_ref, cw_ref, cb_ref,
                    qa_ref, ka_ref, va_ref, qc_ref, kc_ref, vc_ref, z_ref, xbc_ref, dt_ref, tail_ref, xbuf,
                    *, conv_tiles):
    if conv_tiles:
        @pl.when(pl.program_id(0) % conv_tiles == 0)
        def _():
            xbuf[0:SUBLANES, :] = jnp.zeros((SUBLANES, CONV_DIM), F32)

    tm = h_ref.shape[0]
    x = h_ref[...]
    ms = jnp.mean(x * x, axis=-1, keepdims=True)
    u = (x * lax.rsqrt(ms + EPS) * g_ref[...]).astype(BF16)

    def proj(c0, c1):
        return _dot(u, w_ref[:, c0:c1])

    xbc = proj(COL_XBC, COL_DT)
    tail = xbc[tm - SUBLANES:tm, :]
    tail_ref[...] = tail
    if conv_tiles:
        xbuf[SUBLANES:SUBLANES + tm, :] = xbc
        y = cb_ref[...] + cw_ref[3:4, :] * xbc
        for kk in range(3):
            y = y + cw_ref[kk:kk + 1, :] * xbuf[SUBLANES - 3 + kk:SUBLANES - 3 + kk + tm, :]
        xbuf[0:SUBLANES, :] = tail
        xbc_ref[...] = y
    else:
        xbc_ref[...] = xbc
    groups = [COL_QA, COL_KA, COL_QC, COL_KC]
    raw = [proj(c0, c0 + 256) for c0 in groups]
    va_ref[...] = raw[1][:, COL_VA - COL_KA:]
    vc_ref[...] = proj(COL_VC, COL_Z)
    z_ref[...] = proj(COL_Z, COL_XBC)
    qa_ref[...] = _head_norm(raw[0], gain_ref[:, COL_QA:COL_QA + 256], 2)
    ka_ref[...] = _head_norm(raw[1], gain_ref[:, COL_KA:COL_KA + 256], 1)
    qc_ref[...] = _head_norm(raw[2], gain_ref[:, COL_QC:COL_QC + 256], 2)
    kc_ref[...] = _head_norm(raw[3], gain_ref[:, COL_KC:COL_KC + 256], 2)
    dt_ref[...] = jnp.zeros(dt_ref.shape, F32)
    dt_ref[:, 0:N_SSM_HEADS] = proj(COL_DT, IN_COLS)


def _in_proj(h, g, w, gain, cw8, cb, l, tm, conv_tiles):
    t = h.shape[0]
    widths = (256, 128, 128, 256, 256, 256, 512, 1024, 128)
    layer = lambda i: (l, 0, 0)
    n_tail = t // tm // conv_tiles if conv_tiles else t // tm
    per_seq = conv_tiles if conv_tiles else 1
    return pl.pallas_call(
        functools.partial(_in_proj_kernel, conv_tiles=conv_tiles),
        grid=(t // tm,),
        out_shape=[jax.ShapeDtypeStruct((t, w_), F32) for w_ in widths]
        + [jax.ShapeDtypeStruct((n_tail, SUBLANES, CONV_DIM), F32)],
        in_specs=[pl.BlockSpec((tm, D_MODEL), lambda i: (i, 0)),
                  pl.BlockSpec((None, 1, D_MODEL), layer),
                  pl.BlockSpec((None, D_MODEL, IN_COLS), layer),
                  pl.BlockSpec((None, 1, D_MODEL), layer),
                  pl.BlockSpec((None, SUBLANES, CONV_DIM), layer),
                  pl.BlockSpec((None, 1, CONV_DIM), layer)],
        out_specs=[pl.BlockSpec((tm, w_), lambda i: (i, 0)) for w_ in widths]
        + [pl.BlockSpec((None, SUBLANES, CONV_DIM), lambda i: (i // per_seq, 0, 0))],
        scratch_shapes=[pltpu.VMEM((SUBLANES + tm, CONV_DIM), F32)],
        compiler_params=_cparams(("arbitrary",)),
        name="in_proj",
    )(h, g, w, gain, cw8, cb)


def _out_mlp_kernel(hp_ref, oap_ref, obp_ref, ocp_ref, hs_ref, oas_ref, obs_ref, ocs_ref,
                    wo_ref, g_ref, wu_ref, wd_ref, op_ref, os_ref, *, prompt_tiles):
    def tile(h_ref, oa_ref, ob_ref, oc_ref, o_ref):
        acc = _dot(oa_ref[...].astype(BF16), wo_ref[0:256, :])
        acc += _dot(ob_ref[...].astype(BF16), wo_ref[256:768, :])
        acc += _dot(oc_ref[...].astype(BF16), wo_ref[768:1024, :])
        h2 = h_ref[...] + acc
        ms = jnp.mean(h2 * h2, axis=-1, keepdims=True)
        u = (h2 * lax.rsqrt(ms + EPS) * g_ref[...]).astype(BF16)
        ff_chunk = 1024
        mlp = jnp.zeros_like(h2)
        for c in range(D_FF // ff_chunk):
            a = _dot(u, wu_ref[:, c * ff_chunk:(c + 1) * ff_chunk])
            a = jnp.square(jnp.maximum(a, 0.0)).astype(BF16)
            mlp += _dot(a, wd_ref[c * ff_chunk:(c + 1) * ff_chunk, :])
        o_ref[...] = h2 + mlp

    is_prompt = pl.program_id(0) < prompt_tiles

    @pl.when(is_prompt)
    def _():
        tile(hp_ref, oap_ref, obp_ref, ocp_ref, op_ref)

    @pl.when(jnp.logical_not(is_prompt))
    def _():
        tile(hs_ref, oas_ref, obs_ref, ocs_ref, os_ref)


def _out_mlp(hp, oap, obp, ocp, hs, oas, obs, ocs, wo, g, wu, wd, l, tm):
    tp, ts = hp.shape[0], hs.shape[0]
    prompt_tiles = tp // tm
    layer = lambda i: (l, 0, 0)
    prow = lambda i: (jnp.minimum(i, prompt_tiles - 1), 0)
    srow = lambda i: (jnp.maximum(i - prompt_tiles, 0), 0)
    tok = lambda idx: [pl.BlockSpec((tm, w_), idx) for w_ in (D_MODEL, 256, 512, 256)]
    weights = lambda shape: pl.BlockSpec((None,) + shape, layer, pipeline_mode=pl.Buffered(1))
    return pl.pallas_call(
        functools.partial(_out_mlp_kernel, prompt_tiles=prompt_tiles),
        grid=(prompt_tiles + ts // tm,),
        out_shape=[jax.ShapeDtypeStruct((tp, D_MODEL), F32), jax.ShapeDtypeStruct((ts, D_MODEL), F32)],
        in_specs=tok(prow) + tok(srow) + [weights((D_MODEL, D_MODEL)),
                                          pl.BlockSpec((None, 1, D_MODEL), layer),
                                          weights((D_MODEL, D_FF)),
                                          weights((D_FF, D_MODEL))],
        out_specs=[pl.BlockSpec((tm, D_MODEL), prow), pl.BlockSpec((tm, D_MODEL), srow)],
        compiler_params=_cparams(("arbitrary",)),
        name="out_mlp",
    )(hp, oap, obp, ocp, hs, oas, obs, ocs, wo, g, wu, wd)


def _attn_a_prompt_kernel(sink_ref, q_ref, k_ref, v_ref, bias_ref, o_ref, kt_ref, vt_ref,
                          k_st, k_sw, v_st, v_sw, *, l, seq):
    lo = _lane_iota((BLOCK, LANES)) < HEAD_DIM
    k = k_ref[...]
    v = v_ref[...]
    for st in (k_st, k_sw, v_st, v_sw):
        st[0:BLOCK, :] = jnp.zeros((BLOCK, LANES), BF16)
    k_st[BLOCK:, :] = k.astype(BF16)
    v_st[BLOCK:, :] = v.astype(BF16)
    k_sw[BLOCK:, :] = pltpu.roll(k, HEAD_DIM, 1).astype(BF16)
    v_sw[BLOCK:, :] = pltpu.roll(v, HEAD_DIM, 1).astype(BF16)
    kt_ref[...] = k_ref[seq - WIN:seq, :].T
    vt_ref[...] = v_ref[seq - WIN:seq, :].T

    def body(g, carry):
        scores, where = [], []
        for u in range(A_UNROLL):
            b = g * A_UNROLL + u
            cur = pl.ds(pl.multiple_of(b * BLOCK, BLOCK), BLOCK)
            both = pl.ds(pl.multiple_of(b * BLOCK, BLOCK), 2 * BLOCK)
            variant = jnp.where(b == 0, 1, 0)
            where.append((cur, both))
            for h in range(4):
                slab = q_ref[cur, (h // 2) * LANES:(h // 2 + 1) * LANES]
                qm = jnp.where(lo if h % 2 == 0 else ~lo, slab, 0.0).astype(BF16)
                kk = k_st if h in (0, 3) else k_sw
                scores.append(_dot_t(qm, kk[both, :]) + bias_ref[h, variant])
        probs = []
        for i, s in enumerate(scores):
            sink = sink_ref[l, i % 4]
            m = jnp.maximum(jnp.max(jnp.maximum(s[:, 0:BLOCK], s[:, BLOCK:2 * BLOCK]), axis=-1, keepdims=True), sink)
            p = jnp.exp2(s - m)
            den = jnp.sum(p[:, 0:BLOCK] + p[:, BLOCK:2 * BLOCK], axis=-1, keepdims=True) + jnp.exp2(sink - m)
            probs.append((p.astype(BF16), den))
        for u, (cur, both) in enumerate(where):
            outs = []
            for h in range(4):
                p, den = probs[4 * u + h]
                vv = v_st if h in (0, 3) else v_sw
                outs.append(_dot(p, vv[both, :]) / den)
            o_ref[cur, 0:LANES] = jnp.where(lo, outs[0], outs[1])
            o_ref[cur, LANES:2 * LANES] = jnp.where(lo, outs[2], outs[3])
        return carry

    lax.fori_loop(0, seq // BLOCK // A_UNROLL, body, 0)


def _attn_a_prompt(q, k, v, bias, sinks, l, depth, n, seq, bufs):
    t = q.shape[0]
    n_in = 5
    body, xspecs, xargs, k_alias = _stacked(functools.partial(_attn_a_prompt_kernel, l=l, seq=seq), n_in, bufs)
    st_shape = jax.ShapeDtypeStruct((depth, n, LANES, WIN), F32)
    st_spec = pl.BlockSpec((None, None, LANES, WIN), lambda i: (l, i, 0, 0))
    tok = lambda w: pl.BlockSpec((seq, w), lambda i: (i, 0))
    return pl.pallas_call(
        body,
        grid=(n,),
        out_shape=[jax.ShapeDtypeStruct((t, 256), F32), st_shape, st_shape],
        in_specs=[pl.BlockSpec(memory_space=pltpu.SMEM), tok(256), tok(LANES), tok(LANES),
                  pl.BlockSpec((4, 2, BLOCK, 2 * BLOCK), lambda i: (0, 0, 0, 0))] + xspecs,
        out_specs=[tok(256), st_spec, st_spec],
        scratch_shapes=[pltpu.VMEM((BLOCK + seq, LANES), BF16)] * 4,
        input_output_aliases=_alias_map(n_in, k_alias, 3) if k_alias else {},
        compiler_params=_cparams(("parallel",)),
        name="attn_a_prompt",
    )(sinks, q, k, v, bias, *xargs)


def _attn_c_prompt_kernel(q_ref, k_ref, v_ref, bias_ref, o_ref, kt_ref, vt_ref, ob_scr, lse_scr,
                          q_st, k_st, v_st, *, seq):
    lane = _lane_iota((BLOCK, LANES))
    lo = lane < HEAD_DIM
    kt_ref[...] = k_ref[...].T
    vt_ref[...] = v_ref[...].T
    part = seq // C_STAGE
    for src, dst in ((q_ref, q_st), (k_ref, k_st), (v_ref, v_st)):
        for rho in range(C_STAGE):
            dst[rho * part:(rho + 1) * part, :] = src[pl.ds(rho, part, stride=C_STAGE), :]

    def blocks(br, staged, stride, where, has_prev):
        rows = lambda ref, s: ref[pl.ds(s, BLOCK, stride=stride), :]
        q_src, k_src, v_src = (q_st, k_st, v_st) if staged else (q_ref, k_ref, v_ref)
        scores, vals = [], []
        for qs, ps, first in where:
            q = rows(q_src, qs)
            if has_prev:
                k2 = jnp.concatenate([rows(k_src, ps), rows(k_src, qs)], axis=0).astype(BF16)
                vals.append(jnp.concatenate([rows(v_src, ps), rows(v_src, qs)], axis=0).astype(BF16))
                variant = jnp.where(first, 1, 0)
            else:
                k2 = rows(k_src, qs).astype(BF16)
                vals.append(rows(v_src, qs).astype(BF16))
            for hh in range(2):
                qm = jnp.where(lo if hh == 0 else ~lo, q, 0.0).astype(BF16)
                bias = bias_ref[br, hh, variant] if has_prev else bias_ref[br, hh, 0, :, BLOCK:2 * BLOCK]
                scores.append(_dot_t(qm, k2) + bias)
        probs = []
        for s in scores:
            sm = jnp.maximum(s[:, 0:BLOCK], s[:, BLOCK:2 * BLOCK]) if has_prev else s
            m = jnp.max(sm, axis=-1, keepdims=True)
            p = jnp.exp2(s - m)
            pm = p[:, 0:BLOCK] + p[:, BLOCK:2 * BLOCK] if has_prev else p
            probs.append((p.astype(BF16), m, jnp.sum(pm, axis=-1, keepdims=True)))
        for u, (qs, _, _) in enumerate(where):
            outs, lses = [], []
            for hh in range(2):
                p, m, den = probs[2 * u + hh]
                outs.append(_dot(p, vals[u]) / den)
                lses.append(jnp.broadcast_to(m + jnp.log2(den), (BLOCK, LANES)))
            ob_scr[br, pl.ds(qs, BLOCK, stride=stride), :] = jnp.where(lo, outs[0], outs[1])
            lse_scr[br, pl.ds(qs, BLOCK, stride=stride), :] = jnp.where(lo, lses[0], lses[1])

    for br, dil in enumerate(C_DILS):
        nb = seq // dil // BLOCK
        staged = dil % C_STAGE == 0
        stride = dil // C_STAGE if staged else dil

        def body(g, carry, br=br, dil=dil, nb=nb, staged=staged, stride=stride):
            where = []
            for u in range(C_UNROLL):
                i = g * C_UNROLL + u
                res = i % dil
                b = i // dil
                base = (res % C_STAGE) * part + res // C_STAGE if staged else res
                step = BLOCK * stride
                where.append((base + b * step, base + jnp.maximum(b - 1, 0) * step, b == 0))
            blocks(br, staged, stride, where, nb > 1)
            return carry

        lax.fori_loop(0, dil * nb // C_UNROLL, body, 0)

    def merge(i, carry):
        rho = i // (part // BLOCK)
        jb = i % (part // BLOCK)
        st = pl.ds(pl.multiple_of(i * BLOCK, BLOCK), BLOCK)
        nat = pl.ds(rho + C_STAGE * BLOCK * jb, BLOCK, stride=C_STAGE)
        ls, os = [], []
        for br, dil in enumerate(C_DILS):
            rows = st if dil % C_STAGE == 0 else nat
            ls.append(lse_scr[br, rows, :])
            os.append(ob_scr[br, rows, :])
        m = jnp.maximum(jnp.maximum(ls[0], ls[1]), ls[2])
        ws = [jnp.exp2(l_ - m) for l_ in ls]
        num = ws[0] * os[0] + ws[1] * os[1] + ws[2] * os[2]
        o_ref[nat, :] = num / (ws[0] + ws[1] + ws[2])
        return carry

    lax.fori_loop(0, seq // BLOCK, merge, 0)


def _attn_c_prompt(q, k, v, bias, l, depth, n, seq, bufs):
    t = q.shape[0]
    blk = pl.BlockSpec((seq, LANES), lambda i, hp: (i, hp))
    n_in = 4
    body, xspecs, xargs, k_alias = _stacked(functools.partial(_attn_c_prompt_kernel, seq=seq), n_in, bufs)
    st_shape = jax.ShapeDtypeStruct((depth, n, 256, seq), F32)
    st_spec = pl.BlockSpec((None, None, LANES, seq), lambda i, hp: (l, i, hp, 0))
    return pl.pallas_call(
        body,
        grid=(n, 2),
        out_shape=[jax.ShapeDtypeStruct((t, 256), F32), st_shape, st_shape],
        in_specs=[blk, blk, blk,
                  pl.BlockSpec((3, 2, 2, BLOCK, 2 * BLOCK), lambda i, hp: (0, hp, 0, 0, 0))] + xspecs,
        out_specs=[blk, st_spec, st_spec],
        scratch_shapes=[pltpu.VMEM((3, seq, LANES), F32), pltpu.VMEM((3, seq, LANES), F32)]
        + [pltpu.VMEM((seq, LANES), F32)] * 3,
        input_output_aliases=_alias_map(n_in, k_alias, 3) if k_alias else {},
        compiler_params=_cparams(("parallel", "parallel")),
        name="attn_c_prompt",
    )(q, k, v, bias, *xargs)


def _gated_group_norm(y, z, ng):
    gated = y * _silu(z)
    parts = []
    for grp in range(2):
        gsl = gated[:, grp * 256:(grp + 1) * 256]
        ms = jnp.mean(gsl * gsl, axis=-1, keepdims=True)
        parts.append(gsl * lax.rsqrt(ms + EPS))
    return jnp.concatenate(parts, axis=1) * ng


def _expand_heads(v, ex):
    hi = v.astype(BF16)
    r1 = v - hi.astype(F32)
    mid = r1.astype(BF16)
    lo = (r1 - mid.astype(F32)).astype(BF16)
    return _dot(hi, ex) + _dot(mid, ex) + _dot(lo, ex)


def _ssd_prompt_kernel(xbc_ref, z_ref, dt_ref, dtb_ref, alog_ref, dsk_ref, ng_ref, tri_ref, ex_ref,
                       ob_ref, hl_ref, hst):
    c = pl.program_id(1)

    @pl.when(c == 0)
    def _():
        hst[...] = jnp.zeros_like(hst)

    xa_all = _silu(xbc_ref[...])
    dt_all = _softplus(dt_ref[...] + dtb_ref[...])
    a_all = dt_all * (-jnp.exp(alog_ref[...]))
    row = lax.broadcasted_iota(jnp.int32, (BLOCK, BLOCK), 0)
    lane = _lane_iota((BLOCK, BLOCK))
    causal = lane <= row
    lo = lane < HEAD_DIM
    top = row < HEAD_DIM
    n_pairs = N_SSM_HEADS // 2
    for sub in range(SSD_SUB):
        rs = slice(sub * BLOCK, (sub + 1) * BLOCK)
        xa, dt = xa_all[rs], dt_all[rs]
        xs = xa[:, 0:D_INNER]
        acum = jnp.dot(tri_ref[...], a_all[rs], precision=HIGHEST, preferred_element_type=F32)
        bms = [xa[:, D_INNER + g * D_STATE:D_INNER + (g + 1) * D_STATE].astype(BF16) for g in range(2)]
        cms = [xa[:, D_INNER + (2 + g) * D_STATE:D_INNER + (3 + g) * D_STATE].astype(BF16) for g in range(2)]
        cbs = [_dot_t(cms[g], bms[g]) for g in range(2)]
        hps = [hst[2 * j:2 * j + 2].reshape(BLOCK, D_STATE) for j in range(n_pairs)]
        y_off = [_dot_t(cms[j // 2], hps[j].astype(BF16)) for j in range(n_pairs)]
        last = acum[BLOCK - 1:BLOCK, :]
        e_last = jnp.exp(last)
        w_all = _expand_heads(jnp.exp(last - acum) * dt, ex_ref[...])
        e_all = _expand_heads(jnp.exp(acum), ex_ref[...])
        xs_pairs = [xs[:, j * LANES:(j + 1) * LANES] for j in range(n_pairs)]
        for j in range(n_pairs):
            h0_, h1_ = 2 * j, 2 * j + 1
            xw = xs_pairs[j] * w_all[:, j * LANES:(j + 1) * LANES]
            s_new = _dot_t0(xw.astype(BF16), bms[j // 2])
            cd_pair = jnp.where(top, e_last[:, h0_:h0_ + 1], e_last[:, h1_:h1_ + 1])
            hst[h0_:h0_ + 2] = (hps[j] * cd_pair + s_new).reshape(2, HEAD_DIM, D_STATE)
        acum_t = acum.T
        dt_t = dt.T
        y_parts = []
        for j in range(n_pairs):
            xs_b = xs_pairs[j].astype(BF16)
            yd = []
            for hh in range(2):
                h = 2 * j + hh
                seg = acum[:, h:h + 1] - acum_t[h:h + 1, :]
                dec = jnp.exp(jnp.where(causal, seg, NEG)) * dt_t[h:h + 1, :]
                yd.append(_dot((cbs[j // 2] * dec).astype(BF16), xs_b))
            y_parts.append(jnp.where(lo, yd[0], yd[1]) + y_off[j] * e_all[:, j * LANES:(j + 1) * LANES])
        y = jnp.concatenate(y_parts, axis=1) + dsk_ref[...] * xs
        ob_ref[rs, :] = _gated_group_norm(y, z_ref[rs, :], ng_ref[...])

    @pl.when(c == pl.num_programs(1) - 1)
    def _():
        hl_ref[...] = hst[...]


def _ssd_prompt(xbc, z, dt, dtb, alog, dsk, ng, tri, ex, l, depth, n, nc, bufs):
    t = xbc.shape[0]
    nc = nc // SSD_SUB
    rows = SSD_SUB * BLOCK
    row = lambda i, c: (i * nc + c, 0)
    layer = lambda i, c: (l, 0, 0)
    n_in = 9
    body, xspecs, xargs, k_alias = _stacked(_ssd_prompt_kernel, n_in, bufs)
    st = (N_SSM_HEADS, HEAD_DIM, D_STATE)
    return pl.pallas_call(
        body,
        grid=(n, nc),
        out_shape=[jax.ShapeDtypeStruct((t, D_INNER), F32),
                   jax.ShapeDtypeStruct((depth, n) + st, F32)],
        in_specs=[pl.BlockSpec((rows, CONV_DIM), row),
                  pl.BlockSpec((rows, D_INNER), row),
                  pl.BlockSpec((rows, LANES), row),
                  pl.BlockSpec((None, 1, LANES), layer),
                  pl.BlockSpec((None, 1, LANES), layer),
                  pl.BlockSpec((None, 1, D_INNER), layer),
                  pl.BlockSpec((None, 1, D_INNER), layer),
                  pl.BlockSpec((BLOCK, BLOCK), lambda i, c: (0, 0)),
                  pl.BlockSpec((LANES, D_INNER), lambda i, c: (0, 0))] + xspecs,
        out_specs=[pl.BlockSpec((rows, D_INNER), row),
                   pl.BlockSpec((None, None) + st, lambda i, c: (l, i, 0, 0, 0))],
        scratch_shapes=[pltpu.VMEM(st, F32)],
        input_output_aliases=_alias_map(n_in, k_alias, 2) if k_alias else {},
        compiler_params=_cparams(("parallel", "arbitrary")),
        name="ssd_prompt",
    )(xbc, z, dt, dtb, alog, dsk, ng, tri, ex, *xargs)


def _attn_a_sample_kernel(sink_ref, q_ref, kn_ref, vn_ref, kt_ref, vt_ref, bc_ref, bn_ref, o_ref, *, nbs, l):
    rb = nbs * SROWS
    lo = _lane_iota((rb, LANES)) < HEAD_DIM
    s0, s1 = q_ref[:, 0:LANES], q_ref[:, LANES:2 * LANES]
    per_seq = lambda v: v.reshape(nbs, SROWS, LANES)
    qm = jnp.concatenate([
        per_seq(jnp.where(lo, s0, 0.0)),
        per_seq(jnp.where(lo, pltpu.roll(s0, HEAD_DIM, 1), 0.0)),
        per_seq(jnp.where(lo, 0.0, pltpu.roll(s1, HEAD_DIM, 1))),
        per_seq(jnp.where(lo, 0.0, s1))], axis=1).astype(BF16)
    kt = kt_ref[...].reshape(nbs, LANES, WIN).astype(BF16)
    vt = vt_ref[...].reshape(nbs, LANES, WIN).astype(BF16)
    kn = kn_ref[...].reshape(nbs, SROWS, LANES).astype(BF16)
    vn = vn_ref[...].reshape(nbs, SROWS, LANES).astype(BF16)
    s_c = jnp.einsum('nqd,ndk->nqk', qm, kt, preferred_element_type=F32) + bc_ref[...]
    s_n = jnp.einsum('nqd,nkd->nqk', qm, kn, preferred_element_type=F32) + bn_ref[...]
    hrow = lax.broadcasted_iota(jnp.int32, (nbs, 4 * SROWS, 1), 1) // SROWS
    sink = jnp.where(hrow == 0, sink_ref[l, 0], jnp.where(hrow == 1, sink_ref[l, 1],
                     jnp.where(hrow == 2, sink_ref[l, 2], sink_ref[l, 3])))
    m = jnp.maximum(jnp.max(s_c, axis=-1, keepdims=True), jnp.max(s_n, axis=-1, keepdims=True))
    m = jnp.maximum(m, sink)
    p_c = jnp.exp2(s_c - m)
    p_n = jnp.exp2(s_n - m)
    den = jnp.sum(p_c, axis=-1, keepdims=True) + jnp.sum(p_n, axis=-1, keepdims=True) + jnp.exp2(sink - m)
    o = (jnp.einsum('nqk,ndk->nqd', p_c.astype(BF16), vt, preferred_element_type=F32)
         + jnp.einsum('nqk,nkd->nqd', p_n.astype(BF16), vn, preferred_element_type=F32)) / den
    o0, o1, o2, o3 = (o[:, i * SROWS:(i + 1) * SROWS, :].reshape(rb, LANES) for i in range(4))
    o_ref[:, 0:LANES] = jnp.where(lo, o0, pltpu.roll(o1, HEAD_DIM, 1))
    o_ref[:, LANES:2 * LANES] = jnp.where(lo, pltpu.roll(o2, HEAD_DIM, 1), o3)


def _attn_a_sample(q8, k8, v8, cache_kt, cache_vt, bias_c, bias_n, sinks, l, nbs):
    ns = cache_kt.shape[1]
    tok = lambda i: (i, 0)
    cache = pl.BlockSpec((None, nbs, 2, HEAD_DIM, WIN), lambda i: (l, i, 0, 0, 0))
    c3 = lambda i: (0, 0, 0)
    return pl.pallas_call(
        functools.partial(_attn_a_sample_kernel, nbs=nbs, l=l),
        grid=(ns // nbs,),
        out_shape=jax.ShapeDtypeStruct((ns * SROWS, 256), F32),
        in_specs=[pl.BlockSpec(memory_space=pltpu.SMEM),
                  pl.BlockSpec((nbs * SROWS, 256), tok),
                  pl.BlockSpec((nbs * SROWS, LANES), tok),
                  pl.BlockSpec((nbs * SROWS, LANES), tok),
                  cache, cache,
                  pl.BlockSpec((1, 4 * SROWS, WIN), c3),
                  pl.BlockSpec((1, 4 * SROWS, SROWS), c3)],
        out_specs=pl.BlockSpec((nbs * SROWS, 256), tok),
        compiler_params=_cparams(("parallel",)),
        name="attn_a_sample",
    )(sinks, q8, k8, v8, cache_kt, cache_vt, bias_c, bias_n)


def _softmax_parts(parts):
    m = None
    for s in parts:
        mm = jnp.max(s, axis=-1, keepdims=True)
        m = mm if m is None else jnp.maximum(m, mm)
    ps = [jnp.exp2(s - m) for s in parts]
    den = None
    for p in ps:
        dd = jnp.sum(p, axis=-1, keepdims=True)
        den = dd if den is None else den + dd
    return ps, den, m + jnp.log2(den)


def _attn_c_sample_kernel(q_ref, kn_ref, vn_ref, kt_ref, vt_ref, b12_ref, b3_ref, bn_ref, o_ref, *, nbs, lc):
    lo = _lane_iota((SROWS, LANES)) < HEAD_DIM
    pr = 2 * SROWS
    units = [(n, j, slice(n * SROWS, (n + 1) * SROWS), slice(j * LANES, (j + 1) * LANES))
             for n in range(nbs) for j in range(2)]
    scores = []
    for n, j, rows, lanes in units:
        qp = q_ref[rows, lanes]
        lhs = jnp.concatenate([jnp.where(lo, qp, 0.0), jnp.where(lo, 0.0, qp)], axis=0).astype(BF16)
        ktp = kt_ref[n, 2 * j:2 * j + 2].reshape(LANES, lc).astype(BF16)
        scores.append((_dot(lhs, ktp), _dot_t(lhs, kn_ref[rows, lanes].astype(BF16))))
    probs = []
    for (n, j, rows, lanes), (s, sn) in zip(units, scores):
        s_near = s[:, lc - NEAR:]
        (p1, p1n), d1, l1 = _softmax_parts([s_near + b12_ref[0, j], sn + bn_ref[0, j]])
        (p2, p2n), d2, l2 = _softmax_parts([s_near + b12_ref[1, j], sn + bn_ref[1, j]])
        (p3, p3n), d3, l3 = _softmax_parts([s + b3_ref[j], sn + bn_ref[2, j]])
        m = jnp.maximum(jnp.maximum(l1, l2), l3)
        w1, w2, w3 = jnp.exp2(l1 - m), jnp.exp2(l2 - m), jnp.exp2(l3 - m)
        wsum = w1 + w2 + w3
        probs.append((jnp.concatenate([p1, p2], axis=0).astype(BF16), p3.astype(BF16),
                      (p1n.astype(BF16), p2n.astype(BF16), p3n.astype(BF16)),
                      (w1 / (d1 * wsum), w2 / (d2 * wsum), w3 / (d3 * wsum))))
    for (n, j, rows, lanes), (p12, p3, pn, wts) in zip(units, probs):
        vtp = vt_ref[n, 2 * j:2 * j + 2].reshape(LANES, lc).astype(BF16)
        vnp = vn_ref[rows, lanes].astype(BF16)
        o12 = _dot_t(p12, vtp[:, lc - NEAR:])
        o = ((o12[0:pr] + _dot(pn[0], vnp)) * wts[0] + (o12[pr:2 * pr] + _dot(pn[1], vnp)) * wts[1]
             + (_dot_t(p3, vtp) + _dot(pn[2], vnp)) * wts[2])
        o_ref[rows, lanes] = jnp.where(lo, o[0:SROWS], o[SROWS:pr])


def _attn_c_sample(q8, k8, v8, cache_kt, cache_vt, b12, b3, bn, l, nbs):
    ns, lc = cache_kt.shape[1], cache_kt.shape[4]
    tok = lambda i: (i, 0)
    cache = pl.BlockSpec((None, nbs, 4, HEAD_DIM, lc), lambda i: (l, i, 0, 0, 0))
    c3 = lambda i: (0, 0, 0)
    c4 = lambda i: (0, 0, 0, 0)
    return pl.pallas_call(
        functools.partial(_attn_c_sample_kernel, nbs=nbs, lc=lc),
        grid=(ns // nbs,),
        out_shape=jax.ShapeDtypeStruct((ns * SROWS, 256), F32),
        in_specs=[pl.BlockSpec((nbs * SROWS, 256), tok),
                  pl.BlockSpec((nbs * SROWS, 256), tok),
                  pl.BlockSpec((nbs * SROWS, 256), tok),
                  cache, cache,
                  pl.BlockSpec((2, 2, 2 * SROWS, NEAR), c4),
                  pl.BlockSpec((2, 2 * SROWS, lc), c3),
                  pl.BlockSpec((3, 2, 2 * SROWS, SROWS), c4)],
        out_specs=pl.BlockSpec((nbs * SROWS, 256), tok),
        compiler_params=_cparams(("parallel",)),
        name="attn_c_sample",
    )(q8, k8, v8, cache_kt, cache_vt, b12, b3, bn)


def _ssd_sample_kernel(xbc_ref, prev_ref, dt_ref, z_ref, h0_ref, cw_ref, cb_ref, dtb_ref, alog_ref, dsk_ref,
                       ex_ref, ng_ref, ob_ref, hn_ref, *, nbs):
    r = xbc_ref.shape[0]
    rid = lax.broadcasted_iota(jnp.int32, (r, 1), 0) % SROWS
    x = jnp.where(jnp.logical_and(rid >= 1, rid < SVALID), prev_ref[...], xbc_ref[...])
    y = cb_ref[...] + cw_ref[3:4, :] * x
    for d in range(1, 4):
        y = y + cw_ref[3 - d:4 - d, :] * pltpu.roll(x, d, 0)
    xa = _silu(y)
    xs = xa[:, 0:D_INNER]
    bm = xa[:, D_INNER:D_INNER + 2 * D_STATE]
    cm = xa[:, D_INNER + 2 * D_STATE:CONV_DIM]
    valid = rid >= SVALID
    dt = jnp.where(valid, _softplus(dt_ref[...] + dtb_ref[...]), 0.0)
    a = dt * (-jnp.exp(alog_ref[...]))
    acum = a
    rem = jnp.zeros_like(a)
    for d in range(1, 4):
        acum = acum + jnp.where(rid - d >= SVALID, pltpu.roll(a, d, 0), 0.0)
        rem = rem + jnp.where(rid + d < SROWS, pltpu.roll(a, r - d, 0), 0.0)
    ex = ex_ref[...]

    def expand(v):
        return _expand_heads(v, ex)

    lane = _lane_iota((r, LANES))
    y_acc = dsk_ref[...] * xs
    for d in range(4):
        ok = rid - d >= SVALID
        bsh = bm if d == 0 else pltpu.roll(bm, d, 0)
        cb0 = jnp.sum(cm[:, 0:D_STATE] * bsh[:, 0:D_STATE], axis=-1, keepdims=True)
        cb1 = jnp.sum(cm[:, D_STATE:] * bsh[:, D_STATE:], axis=-1, keepdims=True)
        cbh = jnp.where(lane < N_SSM_HEADS // 2, cb0, cb1)
        if d == 0:
            coef = cbh * dt
            xsh = xs
        else:
            dec = jnp.exp(jnp.where(ok, acum - pltpu.roll(acum, d, 0), NEG))
            coef = cbh * dec * pltpu.roll(dt, d, 0)
            xsh = pltpu.roll(xs, d, 0)
        y_acc = y_acc + expand(jnp.where(ok, coef, 0.0)) * xsh
    eac = expand(jnp.exp(acum))
    xw = (xs * expand(dt * jnp.exp(rem))).astype(BF16)
    cd = jnp.exp(acum + rem)
    bm, cm = bm.astype(BF16), cm.astype(BF16)

    yo = []
    for n in range(nbs):
        rows = slice(n * SROWS, (n + 1) * SROWS)
        last = (n + 1) * SROWS - 1
        parts = []
        for g in range(2):
            hp = h0_ref[n, 4 * g:4 * g + 4].reshape(4 * HEAD_DIM, D_STATE)
            parts.append(_dot_t(cm[rows, g * D_STATE:(g + 1) * D_STATE], hp.astype(BF16)))
            s_new = _dot_t0(xw[rows, g * 256:(g + 1) * 256], bm[rows, g * D_STATE:(g + 1) * D_STATE])
            for hh in range(4):
                h = 4 * g + hh
                hn_ref[n, h] = (hp[hh * HEAD_DIM:(hh + 1) * HEAD_DIM, :] * cd[last:last + 1, h:h + 1]
                                + s_new[hh * HEAD_DIM:(hh + 1) * HEAD_DIM, :])
        yo.append(jnp.concatenate(parts, axis=1))
    y_all = y_acc + eac * jnp.concatenate(yo, axis=0)
    ob_ref[...] = _gated_group_norm(y_all, z_ref[...], ng_ref[...])


def _ssd_sample(xbc8, prev8, dt8, z8, state, cw8, cb, dtb, alog, dsk, ex, ng, l, depth, nbs, bufs):
    ns = state.shape[1]
    rb = nbs * SROWS
    row = lambda i: (i, 0)
    layer = lambda i: (l, 0, 0)
    st_spec = pl.BlockSpec((None, nbs, N_SSM_HEADS, HEAD_DIM, D_STATE), lambda i: (l, i, 0, 0, 0))
    n_in = 12
    body, xspecs, xargs, k_alias = _stacked(functools.partial(_ssd_sample_kernel, nbs=nbs), n_in, bufs)
    return pl.pallas_call(
        body,
        grid=(ns // nbs,),
        out_shape=[jax.ShapeDtypeStruct((ns * SROWS, D_INNER), F32),
                   jax.ShapeDtypeStruct(state.shape, F32)],
        in_specs=[pl.BlockSpec((rb, CONV_DIM), row),
                  pl.BlockSpec((None, rb, CONV_DIM), lambda i: (l, i, 0)),
                  pl.BlockSpec((rb, LANES), row),
                  pl.BlockSpec((rb, D_INNER), row),
                  st_spec,
                  pl.BlockSpec((None, SUBLANES, CONV_DIM), layer),
                  pl.BlockSpec((None, 1, CONV_DIM), layer),
                  pl.BlockSpec((None, 1, LANES), layer),
                  pl.BlockSpec((None, 1, LANES), layer),
                  pl.BlockSpec((None, 1, D_INNER), layer),
                  pl.BlockSpec((LANES, D_INNER), lambda i: (0, 0)),
                  pl.BlockSpec((None, 1, D_INNER), layer)] + xspecs,
        out_specs=[pl.BlockSpec((rb, D_INNER), row), st_spec],
        input_output_aliases=_alias_map(n_in, k_alias, 2) if k_alias else {},
        compiler_params=_cparams(("parallel",)),
        name="ssd_sample",
    )(xbc8, prev8, dt8, z8, state, cw8, cb, dtb, alog, dsk, ex, ng, *xargs)


def _new_kv_kernel(ka_ref, va_ref, kc_ref, vc_ref, oka_ref, ova_ref, okc_ref, ovc_ref, *, ns):
    def emit(src, dst):
        for t in range(SROWS - SVALID):
            dst[t] = src[pl.ds(SVALID + t, ns, stride=SROWS), :].T

    emit(kc_ref, okc_ref)
    emit(vc_ref, ovc_ref)

    @pl.when(pl.program_id(1) == 0)
    def _():
        emit(ka_ref, oka_ref)
        emit(va_ref, ova_ref)


def _new_kv_states(ka, va, kc, vc, ns):
    depth = ka.shape[0]
    nt = SROWS - SVALID
    narrow_in = pl.BlockSpec((None, ns * SROWS, LANES), lambda l, j: (l, 0, 0))
    wide_in = pl.BlockSpec((None, ns * SROWS, LANES), lambda l, j: (l, 0, j))
    narrow_out = pl.BlockSpec((None, nt, LANES, ns), lambda l, j: (l, 0, 0, 0))
    wide_out = pl.BlockSpec((None, nt, LANES, ns), lambda l, j: (l, 0, j, 0))
    shape = lambda w: jax.ShapeDtypeStruct((depth, nt, w, ns), F32)
    return pl.pallas_call(
        functools.partial(_new_kv_kernel, ns=ns),
        grid=(depth, 2),
        out_shape=[shape(LANES), shape(LANES), shape(2 * LANES), shape(2 * LANES)],
        in_specs=[narrow_in, narrow_in, wide_in, wide_in],
        out_specs=[narrow_out, narrow_out, wide_out, wide_out],
        compiler_params=_cparams(("parallel", "arbitrary")),
        name="new_kv_states",
    )(ka, va, kc, vc)


def _prompt_bias_item(dil, col0):
    qi = np.arange(BLOCK)[:, None]
    kj = np.arange(2 * BLOCK)[None, :]
    dist = BLOCK + qi - kj
    ok = (dist >= 0) & (dist <= WIN)
    variants = [_masked_bucket(dist * dil, ok), _masked_bucket(dist * dil, ok & (kj >= BLOCK))]
    segs = [((2 * h + v) * BLOCK, (2 * h + v + 1) * BLOCK, col0 + h) for h in range(4) for v in range(2)]
    return np.concatenate(variants * 4, axis=0), segs


def _sample_bias_item(dist_of, ok_of, cols, col0):
    t = np.arange(SROWS)[:, None] - SVALID
    bkt = np.where(t >= 0, _masked_bucket(dist_of(t, cols), ok_of(t, cols)), 0).astype(np.int32)
    segs = [(h * SROWS, (h + 1) * SROWS, col0 + h) for h in range(4)]
    return np.concatenate([bkt] * 4, axis=0), segs


def _cache_bias_item(length, window, dil, span, col0):
    dist_of = lambda t, pos: length + t - pos
    ok_of = lambda t, pos: (dist_of(t, pos) >= 0) & (dist_of(t, pos) <= window) & (dist_of(t, pos) % dil == 0)
    return _sample_bias_item(dist_of, ok_of, np.arange(length - span, length)[None, :], col0)


def _new_rows_bias_item(same_token_only, col0):
    dist_of = lambda t, r: t - (r - SVALID)
    if same_token_only:
        ok_of = lambda t, r: (r >= SVALID) & (dist_of(t, r) == 0)
    else:
        ok_of = lambda t, r: (r >= SVALID) & (dist_of(t, r) >= 0)
    return _sample_bias_item(dist_of, ok_of, np.arange(SROWS)[None, :], col0)


Tiling = collections.namedtuple("Tiling", "tm_prompt tm_sample seqs_a seqs_c seqs_b")


def _tiling(ns):
    rows = ns * SROWS
    return Tiling(tm_prompt=512, tm_sample=min(512, rows), seqs_a=min(16, ns), seqs_c=min(2, ns),
                  seqs_b=min(16, ns))
def kernel(x_prompt, x_sample, cache_a_k, cache_a_v, cache_c_k, cache_c_v, state_ssm, state_conv,
           norm_mix_g, w_in, a_q_norm_g, a_k_norm_g, a_sinks, c_q_norm_g, c_k_norm_g, rel_bias,
           conv_w, conv_b, dt_bias, a_log, d_skip, ssm_norm_g, w_out, norm_mlp_g, w_up, w_down):
    depth = w_in.shape[0]
    n, seq, _ = x_prompt.shape
    ns, ts, _ = x_sample.shape
    lc = cache_c_k.shape[2]
    assert ts == SROWS - SVALID and seq % (BLOCK * 16) == 0
    assert cache_a_k.shape[2] == WIN and all(w <= lc for w, _ in C_BRANCHES)
    nb = seq // BLOCK

    w_in_b = w_in.astype(BF16)
    w_out_b, w_up_b, w_down_b = w_out.astype(BF16), w_up.astype(BF16), w_down.astype(BF16)
    ones = jnp.ones((depth, 128), F32)
    q_scale = ATTN_SCALE * LOG2E
    gain = jnp.concatenate([jnp.tile(a_q_norm_g, (1, 4)) * q_scale, jnp.tile(a_k_norm_g, (1, 2)), ones,
                            jnp.tile(c_q_norm_g, (1, 4)) * q_scale, jnp.tile(c_k_norm_g, (1, 4))], axis=1)
    rel_bias = rel_bias * LOG2E
    a_sinks = a_sinks * LOG2E
    tri = (jnp.arange(BLOCK)[None, :] <= jnp.arange(BLOCK)[:, None]).astype(F32)
    ex = (jnp.arange(LANES)[:, None] == (jnp.arange(D_INNER) // HEAD_DIM)[None, :]).astype(BF16)
    cw8 = jnp.pad(conv_w, ((0, 0), (0, SUBLANES - conv_w.shape[1]), (0, 0)))
    pad_h = lambda v: jnp.pad(v, ((0, 0), (0, LANES - N_SSM_HEADS)))[:, None, :]
    vec = lambda v: v[:, None, :]
    dtb, alog = pad_h(dt_bias), pad_h(a_log)
    dsk = vec(jnp.repeat(d_skip, HEAD_DIM, axis=1))
    g_mix, g_mlp, gain, cb, ng = vec(norm_mix_g), vec(norm_mlp_g), vec(gain), vec(conv_b), vec(ssm_norm_g)

    (w1, d1), (w2, d2), (w3, d3) = C_BRANCHES
    items = ([_prompt_bias_item(1, 0)] + [_prompt_bias_item(d, 4) for d in C_DILS]
             + [_cache_bias_item(WIN, WIN, 1, WIN, 0), _new_rows_bias_item(False, 0)]
             + [_cache_bias_item(lc, w1, d1, NEAR, 4), _cache_bias_item(lc, w2, d2, NEAR, 4),
                _cache_bias_item(lc, w3, d3, lc, 4), _new_rows_bias_item(False, 4), _new_rows_bias_item(True, 4)])
    pa, pc1, pc4, pc16, sa_c, sa_n, sc1, sc2, sc3, sn_all, sn_same = _expand_biases(rel_bias, items)
    by_variant = lambda b: b.reshape(4, 2, BLOCK, 2 * BLOCK)
    by_pair = lambda b: b.reshape(2, 2 * SROWS, b.shape[-1])
    bias_a = by_variant(pa)
    bias_c = jnp.stack([by_variant(pc1), by_variant(pc4), by_variant(pc16)])
    sbias_a_c, sbias_a_n = sa_c[None], sa_n[None]
    sb12 = jnp.stack([by_pair(sc1), by_pair(sc2)])
    sb3 = by_pair(sc3)
    sbn = jnp.stack([by_pair(sn_all), by_pair(sn_same), by_pair(sn_same)])

    cak_t, cav_t = cache_a_k.transpose(0, 1, 3, 4, 2), cache_a_v.transpose(0, 1, 3, 4, 2)
    cck_t, ccv_t = cache_c_k.transpose(0, 1, 3, 4, 2), cache_c_v.transpose(0, 1, 3, 4, 2)

    hp = x_prompt.reshape(n * seq, D_MODEL)
    hs = jnp.pad(x_sample, ((0, 0), (SVALID, 0), (0, 0))).reshape(ns * SROWS, D_MODEL)
    tl = _tiling(ns)
    prev8 = jnp.pad(state_conv, ((0, 0), (0, 0), (1, SROWS - 4), (0, 0))).reshape(depth, ns * SROWS, CONV_DIM)

    p_ak = p_ck = p_ssm = s_ssm = None
    p_conv, s_small = [], [[] for _ in range(5)]
    for l in range(depth):
        qa, ka, va, qc, kc, vc, z, xbc, dt, tail = _in_proj(hp, g_mix, w_in_b, gain, cw8, cb, l, tl.tm_prompt,
                                                            seq // tl.tm_prompt)
        oa_p, *p_ak = _attn_a_prompt(qa, ka, va, bias_a, a_sinks, l, depth, n, seq, p_ak)
        oc_p, *p_ck = _attn_c_prompt(qc, kc, vc, bias_c, l, depth, n, seq, p_ck)
        ob_p, *p_ssm = _ssd_prompt(xbc, z, dt, dtb, alog, dsk, ng, tri, ex, l, depth, n, nb, p_ssm)
        p_conv.append(tail)

        qa, ka, va, qc, kc, vc, z, xbc, dt, _ = _in_proj(hs, g_mix, w_in_b, gain, cw8, cb, l, tl.tm_sample, 0)
        oa = _attn_a_sample(qa, ka, va, cak_t, cav_t, sbias_a_c, sbias_a_n, a_sinks, l, tl.seqs_a)
        oc = _attn_c_sample(qc, kc, vc, cck_t, ccv_t, sb12, sb3, sbn, l, tl.seqs_c)
        ob, *s_ssm = _ssd_sample(xbc, prev8, dt, z, state_ssm, cw8, cb, dtb, alog, dsk, ex, ng, l, depth,
                                 tl.seqs_b, s_ssm)
        for dst, val in zip(s_small, (ka, va, kc, vc, xbc)):
            dst.append(val)

        hp, hs = _out_mlp(hp, oa_p, ob_p, oc_p, hs, oa, ob, oc, w_out_b, g_mlp, w_up_b, w_down_b, l,
                          math.gcd(tl.tm_prompt, ns * SROWS))

    y_prompt = hp.reshape(n, seq, D_MODEL)
    y_sample = hs.reshape(ns, SROWS, D_MODEL)[:, SVALID:]
    unminor = lambda buf, heads: buf.reshape(depth, n, heads, HEAD_DIM, -1).transpose(0, 1, 4, 2, 3)
    p_state = (unminor(p_ak[0], 2), unminor(p_ak[1], 2), unminor(p_ck[0], 4), unminor(p_ck[1], 4),
               p_ssm[0], jnp.stack(p_conv)[:, :, SUBLANES - 3:])
    new_kv = _new_kv_states(*(jnp.stack(v) for v in s_small[:4]), ns)
    s_kv = tuple(b.reshape(depth, SROWS - SVALID, hh, HEAD_DIM, ns).transpose(0, 4, 1, 2, 3)
                 for b, hh in zip(new_kv, (2, 2, 4, 4)))
    s_conv = jnp.stack([x.reshape(ns, SROWS, CONV_DIM)[:, SROWS - 3:] for x in s_small[4]])
    s_state = s_kv + (s_ssm[0], s_conv)
    return (y_prompt, y_sample) + p_state + s_state
```

```python
import collections
import functools
import math

import jax
import jax.numpy as jnp
import numpy as np
from jax import lax
from jax.experimental import pallas as pl
from jax.experimental.pallas import tpu as pltpu

F32 = jnp.float32
BF16 = jnp.bfloat16
HIGHEST = lax.Precision.HIGHEST

D_MODEL = 1024
HEAD_DIM = 64
LANES = 128
SUBLANES = 8
BLOCK = 128
WIN = 128
D_INNER = 512
D_STATE = 128
N_SSM_HEADS = 8
CONV_DIM = 1024
D_FF = 4096
NUM_BUCKETS = 32
REL_MAX_DIST = 2048
EPS = 1e-6
ATTN_SCALE = HEAD_DIM ** -0.5
LOG2E = math.log2(math.e)
NEG = -1e30
C_BRANCHES = ((128, 1), (512, 4), (2048, 16))
C_DILS = tuple(d for _, d in C_BRANCHES)
SROWS = 8
SVALID = 4
NEAR = 512
C_STAGE = 4
C_UNROLL = 8
A_UNROLL = 4
SSD_SUB = 4

COL_QA, COL_KA, COL_VA, COL_QC, COL_KC, COL_VC, COL_Z, COL_XBC, COL_DT, COL_END = (
    0, 256, 384, 512, 768, 1024, 1280, 1792, 2816, 2944)
IN_COLS = 2824
V7X_VMEM_BYTES = 64 * 1024 * 1024
VMEM_LIMIT = V7X_VMEM_BYTES * 7 // 8


def _cparams(sem):
    return pltpu.CompilerParams(dimension_semantics=sem, vmem_limit_bytes=VMEM_LIMIT)


def _dot(a, b):
    return jnp.dot(a, b, preferred_element_type=F32)


def _dot_t(a, b):
    return lax.dot_general(a, b, (((1,), (1,)), ((), ())), preferred_element_type=F32)


def _dot_t0(a, b):
    return lax.dot_general(a, b, (((0,), (0,)), ((), ())), preferred_element_type=F32)


def _lane_iota(shape):
    return lax.broadcasted_iota(jnp.int32, shape, len(shape) - 1)


def _silu(x):
    hx = 0.5 * x
    return hx + hx * jnp.tanh(hx)


def _softplus(x):
    return jnp.maximum(x, 0.0) + jnp.log(1.0 + jnp.exp(-jnp.abs(x)))


def _stacked(body, n_in, bufs):
    if bufs is None:
        return body, [], [], 0
    k = len(bufs)

    def wrapped(*refs):
        return body(*refs[:n_in], *refs[n_in + k:])

    return wrapped, [pl.BlockSpec(memory_space=pl.ANY)] * k, list(bufs), k


def _alias_map(n_in, k, n_out):
    return {n_in + i: n_out - k + i for i in range(k)}


def _bias_kernel(tab_ref, *refs, segments):
    n = len(segments)
    for bkt_ref, o_ref, segs in zip(refs[:n], refs[n:], segments):
        for r0, r1, col in segs:
            bkt = bkt_ref[r0:r1, :]
            acc = jnp.full(bkt.shape, NEG, F32)
            for b in range(NUM_BUCKETS):
                acc = jnp.where(bkt == b, tab_ref[b, col], acc)
            o_ref[r0:r1, :] = acc


def _expand_biases(table, items):
    bkts = [jnp.asarray(b) for b, _ in items]
    full = lambda b: pl.BlockSpec(b.shape, lambda: (0, 0))
    return pl.pallas_call(
        functools.partial(_bias_kernel, segments=[s for _, s in items]),
        out_shape=[jax.ShapeDtypeStruct(b.shape, F32) for b in bkts],
        in_specs=[pl.BlockSpec(memory_space=pltpu.SMEM)] + [full(b) for b in bkts],
        out_specs=[full(b) for b in bkts],
        name="bias_expand",
    )(table, *bkts)


def _t5_bucket(dist):
    max_exact = NUM_BUCKETS // 2
    d = np.maximum(dist, 0)
    df = np.maximum(d, 1).astype(np.float32)
    ratio = np.log(df / np.float32(max_exact)) / np.float32(math.log(REL_MAX_DIST / max_exact))
    large = max_exact + (ratio * np.float32(NUM_BUCKETS - max_exact)).astype(np.int32)
    return np.where(d < max_exact, d, np.minimum(large, NUM_BUCKETS - 1))


def _masked_bucket(dist, valid):
    return np.where(valid, _t5_bucket(dist), -1).astype(np.int32)


def _head_norm(p, gain, slabs):
    lo = _lane_iota((p.shape[0], LANES)) < HEAD_DIM
    out = []
    for s in range(slabs):
        x = p[:, s * LANES:(s + 1) * LANES]
        sq = x * x
        s_lo = jnp.sum(jnp.where(lo, sq, 0.0), axis=-1, keepdims=True)
        s_hi = jnp.sum(jnp.where(lo, 0.0, sq), axis=-1, keepdims=True)
        ss = jnp.where(lo, s_lo, s_hi)
        out.append(x * lax.rsqrt(ss * (1.0 / HEAD_DIM) + EPS) * gain[:, s * LANES:(s + 1) * LANES])
    return out[0] if slabs == 1 else jnp.concatenate(out, axis=1)


def _in_proj_kernel(h_ref, g_ref, w_ref, gain_ref, cw_ref, cb_ref,
                    qa_ref, ka_ref, va_ref, qc_ref, kc_ref, vc_ref, z_ref, xbc_ref, dt_ref, tail_ref, xbuf,
                    *, conv_tiles):
    if conv_tiles:
        @pl.when(pl.program_id(0) % conv_tiles == 0)
        def _():
            xbuf[0:SUBLANES, :] = jnp.zeros((SUBLANES, CONV_DIM), F32)

    tm = h_ref.shape[0]
    x = h_ref[...]
    ms = jnp.mean(x * x, axis=-1, keepdims=True)
    u = (x * lax.rsqrt(ms + EPS) * g_ref[...]).astype(BF16)

    def proj(c0, c1):
        return _dot(u, w_ref[:, c0:c1])

    xbc = proj(COL_XBC, COL_DT)
    tail = xbc[tm - SUBLANES:tm, :]
    tail_ref[...] = tail
    if conv_tiles:
        xbuf[SUBLANES:SUBLANES + tm, :] = xbc
        y = cb_ref[...] + cw_ref[3:4, :] * xbc
        for kk in range(3):
            y = y + cw_ref[kk:kk + 1, :] * xbuf[SUBLANES - 3 + kk:SUBLANES - 3 + kk + tm, :]
        xbuf[0:SUBLANES, :] = tail
        xbc_ref[...] = y
    else:
        xbc_ref[...] = xbc
    groups = [COL_QA, COL_KA, COL_QC, COL_KC]
    raw = [proj(c0, c0 + 256) for c0 in groups]
    va_ref[...] = raw[1][:, COL_VA - COL_KA:]
    vc_ref[...] = proj(COL_VC, COL_Z)
    z_ref[...] = proj(COL_Z, COL_XBC)
    qa_ref[...] = _head_norm(raw[0], gain_ref[:, COL_QA:COL_QA + 256], 2)
    ka_ref[...] = _head_norm(raw[1], gain_ref[:, COL_KA:COL_KA + 256], 1)
    qc_ref[...] = _head_norm(raw[2], gain_ref[:, COL_QC:COL_QC + 256], 2)
    kc_ref[...] = _head_norm(raw[3], gain_ref[:, COL_KC:COL_KC + 256], 2)
    dt_ref[...] = jnp.zeros(dt_ref.shape, F32)
    dt_ref[:, 0:N_SSM_HEADS] = proj(COL_DT, IN_COLS)


def _in_proj(h, g, w, gain, cw8, cb, l, tm, conv_tiles):
    t = h.shape[0]
    widths = (256, 128, 128, 256, 256, 256, 512, 1024, 128)
    layer = lambda i: (l, 0, 0)
    n_tail = t // tm // conv_tiles if conv_tiles else t // tm
    per_seq = conv_tiles if conv_tiles else 1
    return pl.pallas_call(
        functools.partial(_in_proj_kernel, conv_tiles=conv_tiles),
        grid=(t // tm,),
        out_shape=[jax.ShapeDtypeStruct((t, w_), F32) for w_ in widths]
        + [jax.ShapeDtypeStruct((n_tail, SUBLANES, CONV_DIM), F32)],
        in_specs=[pl.BlockSpec((tm, D_MODEL), lambda i: (i, 0)),
                  pl.BlockSpec((None, 1, D_MODEL), layer),
                  pl.BlockSpec((None, D_MODEL, IN_COLS), layer),
                  pl.BlockSpec((None, 1, D_MODEL), layer),
                  pl.BlockSpec((None, SUBLANES, CONV_DIM), layer),
                  pl.BlockSpec((None, 1, CONV_DIM), layer)],
        out_specs=[pl.BlockSpec((tm, w_), lambda i: (i, 0)) for w_ in widths]
        + [pl.BlockSpec((None, SUBLANES, CONV_DIM), lambda i: (i // per_seq, 0, 0))],
        scratch_shapes=[pltpu.VMEM((SUBLANES + tm, CONV_DIM), F32)],
        compiler_params=_cparams(("arbitrary",)),
        name="in_proj",
    )(h, g, w, gain, cw8, cb)


def _out_mlp_kernel(hp_ref, oap_ref, obp_ref, ocp_ref, hs_ref, oas_ref, obs_ref, ocs_ref,
                    wo_ref, g_ref, wu_ref, wd_ref, op_ref, os_ref, *, prompt_tiles):
    def tile(h_ref, oa_ref, ob_ref, oc_ref, o_ref):
        acc = _dot(oa_ref[...].astype(BF16), wo_ref[0:256, :])
        acc += _dot(ob_ref[...].astype(BF16), wo_ref[256:768, :])
        acc += _dot(oc_ref[...].astype(BF16), wo_ref[768:1024, :])
        h2 = h_ref[...] + acc
        ms = jnp.mean(h2 * h2, axis=-1, keepdims=True)
        u = (h2 * lax.rsqrt(ms + EPS) * g_ref[...]).astype(BF16)
        ff_chunk = 1024
        mlp = jnp.zeros_like(h2)
        for c in range(D_FF // ff_chunk):
            a = _dot(u, wu_ref[:, c * ff_chunk:(c + 1) * ff_chunk])
            a = jnp.square(jnp.maximum(a, 0.0)).astype(BF16)
            mlp += _dot(a, wd_ref[c * ff_chunk:(c + 1) * ff_chunk, :])
        o_ref[...] = h2 + mlp

    is_prompt = pl.program_id(0) < prompt_tiles

    @pl.when(is_prompt)
    def _():
        tile(hp_ref, oap_ref, obp_ref, ocp_ref, op_ref)

    @pl.when(jnp.logical_not(is_prompt))
    def _():
        tile(hs_ref, oas_ref, obs_ref, ocs_ref, os_ref)


def _out_mlp(hp, oap, obp, ocp, hs, oas, obs, ocs, wo, g, wu, wd, l, tm):
    tp, ts = hp.shape[0], hs.shape[0]
    prompt_tiles = tp // tm
    layer = lambda i: (l, 0, 0)
    prow = lambda i: (jnp.minimum(i, prompt_tiles - 1), 0)
    srow = lambda i: (jnp.maximum(i - prompt_tiles, 0), 0)
    tok = lambda idx: [pl.BlockSpec((tm, w_), idx) for w_ in (D_MODEL, 256, 512, 256)]
    weights = lambda shape: pl.BlockSpec((None,) + shape, layer, pipeline_mode=pl.Buffered(1))
    return pl.pallas_call(
        functools.partial(_out_mlp_kernel, prompt_tiles=prompt_tiles),
        grid=(prompt_tiles + ts // tm,),
        out_shape=[jax.ShapeDtypeStruct((tp, D_MODEL), F32), jax.ShapeDtypeStruct((ts, D_MODEL), F32)],
        in_specs=tok(prow) + tok(srow) + [weights((D_MODEL, D_MODEL)),
                                          pl.BlockSpec((None, 1, D_MODEL), layer),
                                          weights((D_MODEL, D_FF)),
                                          weights((D_FF, D_MODEL))],
        out_specs=[pl.BlockSpec((tm, D_MODEL), prow), pl.BlockSpec((tm, D_MODEL), srow)],
        compiler_params=_cparams(("arbitrary",)),
        name="out_mlp",
    )(hp, oap, obp, ocp, hs, oas, obs, ocs, wo, g, wu, wd)


def _attn_a_prompt_kernel(sink_ref, q_ref, k_ref, v_ref, bias_ref, o_ref, kt_ref, vt_ref,
                          k_st, k_sw, v_st, v_sw, *, l, seq):
    lo = _lane_iota((BLOCK, LANES)) < HEAD_DIM
    k = k_ref[...]
    v = v_ref[...]
    for st in (k_st, k_sw, v_st, v_sw):
        st[0:BLOCK, :] = jnp.zeros((BLOCK, LANES), BF16)
    k_st[BLOCK:, :] = k.astype(BF16)
    v_st[BLOCK:, :] = v.astype(BF16)
    k_sw[BLOCK:, :] = pltpu.roll(k, HEAD_DIM, 1).astype(BF16)
    v_sw[BLOCK:, :] = pltpu.roll(v, HEAD_DIM, 1).astype(BF16)
    kt_ref[...] = k_ref[seq - WIN:seq, :].T
    vt_ref[...] = v_ref[seq - WIN:seq, :].T

    def body(g, carry):
        scores, where = [], []
        for u in range(A_UNROLL):
            b = g * A_UNROLL + u
            cur = pl.ds(pl.multiple_of(b * BLOCK, BLOCK), BLOCK)
            both = pl.ds(pl.multiple_of(b * BLOCK, BLOCK), 2 * BLOCK)
            variant = jnp.where(b == 0, 1, 0)
            where.append((cur, both))
            for h in range(4):
                slab = q_ref[cur, (h // 2) * LANES:(h // 2 + 1) * LANES]
                qm = jnp.where(lo if h % 2 == 0 else ~lo, slab, 0.0).astype(BF16)
                kk = k_st if h in (0, 3) else k_sw
                scores.append(_dot_t(qm, kk[both, :]) + bias_ref[h, variant])
        probs = []
        for i, s in enumerate(scores):
            sink = sink_ref[l, i % 4]
            m = jnp.maximum(jnp.max(jnp.maximum(s[:, 0:BLOCK], s[:, BLOCK:2 * BLOCK]), axis=-1, keepdims=True), sink)
            p = jnp.exp2(s - m)
            den = jnp.sum(p[:, 0:BLOCK] + p[:, BLOCK:2 * BLOCK], axis=-1, keepdims=True) + jnp.exp2(sink - m)
            probs.append((p.astype(BF16), den))
        for u, (cur, both) in enumerate(where):
            outs = []
            for h in range(4):
                p, den = probs[4 * u + h]
                vv = v_st if h in (0, 3) else v_sw
                outs.append(_dot(p, vv[both, :]) / den)
            o_ref[cur, 0:LANES] = jnp.where(lo, outs[0], outs[1])
            o_ref[cur, LANES:2 * LANES] = jnp.where(lo, outs[2], outs[3])
        return carry

    lax.fori_loop(0, seq // BLOCK // A_UNROLL, body, 0)


def _attn_a_prompt(q, k, v, bias, sinks, l, depth, n, seq, bufs):
    t = q.shape[0]
    n_in = 5
    body, xspecs, xargs, k_alias = _stacked(functools.partial(_attn_a_prompt_kernel, l=l, seq=seq), n_in, bufs)
    st_shape = jax.ShapeDtypeStruct((depth, n, LANES, WIN), F32)
    st_spec = pl.BlockSpec((None, None, LANES, WIN), lambda i: (l, i, 0, 0))
    tok = lambda w: pl.BlockSpec((seq, w), lambda i: (i, 0))
    return pl.pallas_call(
        body,
        grid=(n,),
        out_shape=[jax.ShapeDtypeStruct((t, 256), F32), st_shape, st_shape],
        in_specs=[pl.BlockSpec(memory_space=pltpu.SMEM), tok(256), tok(LANES), tok(LANES),
                  pl.BlockSpec((4, 2, BLOCK, 2 * BLOCK), lambda i: (0, 0, 0, 0))] + xspecs,
        out_specs=[tok(256), st_spec, st_spec],
        scratch_shapes=[pltpu.VMEM((BLOCK + seq, LANES), BF16)] * 4,
        input_output_aliases=_alias_map(n_in, k_alias, 3) if k_alias else {},
        compiler_params=_cparams(("parallel",)),
        name="attn_a_prompt",
    )(sinks, q, k, v, bias, *xargs)


def _attn_c_prompt_kernel(q_ref, k_ref, v_ref, bias_ref, o_ref, kt_ref, vt_ref, ob_scr, lse_scr,
                          q_st, k_st, v_st, *, seq):
    lane = _lane_iota((BLOCK, LANES))
    lo = lane < HEAD_DIM
    kt_ref[...] = k_ref[...].T
    vt_ref[...] = v_ref[...].T
    part = seq // C_STAGE
    for src, dst in ((q_ref, q_st), (k_ref, k_st), (v_ref, v_st)):
        for rho in range(C_STAGE):
            dst[rho * part:(rho + 1) * part, :] = src[pl.ds(rho, part, stride=C_STAGE), :]

    def blocks(br, staged, stride, where, has_prev):
        rows = lambda ref, s: ref[pl.ds(s, BLOCK, stride=stride), :]
        q_src, k_src, v_src = (q_st, k_st, v_st) if staged else (q_ref, k_ref, v_ref)
        scores, vals = [], []
        for qs, ps, first in where:
            q = rows(q_src, qs)
            if has_prev:
                k2 = jnp.concatenate([rows(k_src, ps), rows(k_src, qs)], axis=0).astype(BF16)
                vals.append(jnp.concatenate([rows(v_src, ps), rows(v_src, qs)], axis=0).astype(BF16))
                variant = jnp.where(first, 1, 0)
            else:
                k2 = rows(k_src, qs).astype(BF16)
                vals.append(rows(v_src, qs).astype(BF16))
            for hh in range(2):
                qm = jnp.where(lo if hh == 0 else ~lo, q, 0.0).astype(BF16)
                bias = bias_ref[br, hh, variant] if has_prev else bias_ref[br, hh, 0, :, BLOCK:2 * BLOCK]
                scores.append(_dot_t(qm, k2) + bias)
        probs = []
        for s in scores:
            sm = jnp.maximum(s[:, 0:BLOCK], s[:, BLOCK:2 * BLOCK]) if has_prev else s
            m = jnp.max(sm, axis=-1, keepdims=True)
            p = jnp.exp2(s - m)
            pm = p[:, 0:BLOCK] + p[:, BLOCK:2 * BLOCK] if has_prev else p
            probs.append((p.astype(BF16), m, jnp.sum(pm, axis=-1, keepdims=True)))
        for u, (qs, _, _) in enumerate(where):
            outs, lses = [], []
            for hh in range(2):
                p, m, den = probs[2 * u + hh]
                outs.append(_dot(p, vals[u]) / den)
                lses.append(jnp.broadcast_to(m + jnp.log2(den), (BLOCK, LANES)))
            ob_scr[br, pl.ds(qs, BLOCK, stride=stride), :] = jnp.where(lo, outs[0], outs[1])
            lse_scr[br, pl.ds(qs, BLOCK, stride=stride), :] = jnp.where(lo, lses[0], lses[1])

    for br, dil in enumerate(C_DILS):
        nb = seq // dil // BLOCK
        staged = dil % C_STAGE == 0
        stride = dil // C_STAGE if staged else dil

        def body(g, carry, br=br, dil=dil, nb=nb, staged=staged, stride=stride):
            where = []
            for u in range(C_UNROLL):
                i = g * C_UNROLL + u
                res = i % dil
                b = i // dil
                base = (res % C_STAGE) * part + res // C_STAGE if staged else res
                step = BLOCK * stride
                where.append((base + b * step, base + jnp.maximum(b - 1, 0) * step, b == 0))
            blocks(br, staged, stride, where, nb > 1)
            return carry

        lax.fori_loop(0, dil * nb // C_UNROLL, body, 0)

    def merge(i, carry):
        rho = i // (part // BLOCK)
        jb = i % (part // BLOCK)
        st = pl.ds(pl.multiple_of(i * BLOCK, BLOCK), BLOCK)
        nat = pl.ds(rho + C_STAGE * BLOCK * jb, BLOCK, stride=C_STAGE)
        ls, os = [], []
        for br, dil in enumerate(C_DILS):
            rows = st if dil % C_STAGE == 0 else nat
            ls.append(lse_scr[br, rows, :])
            os.append(ob_scr[br, rows, :])
        m = jnp.maximum(jnp.maximum(ls[0], ls[1]), ls[2])
        ws = [jnp.exp2(l_ - m) for l_ in ls]
        num = ws[0] * os[0] + ws[1] * os[1] + ws[2] * os[2]
        o_ref[nat, :] = num / (ws[0] + ws[1] + ws[2])
        return carry

    lax.fori_loop(0, seq // BLOCK, merge, 0)


def _attn_c_prompt(q, k, v, bias, l, depth, n, seq, bufs):
    t = q.shape[0]
    blk = pl.BlockSpec((seq, LANES), lambda i, hp: (i, hp))
    n_in = 4
    body, xspecs, xargs, k_alias = _stacked(functools.partial(_attn_c_prompt_kernel, seq=seq), n_in, bufs)
    st_shape = jax.ShapeDtypeStruct((depth, n, 256, seq), F32)
    st_spec = pl.BlockSpec((None, None, LANES, seq), lambda i, hp: (l, i, hp, 0))
    return pl.pallas_call(
        body,
        grid=(n, 2),
        out_shape=[jax.ShapeDtypeStruct((t, 256), F32), st_shape, st_shape],
        in_specs=[blk, blk, blk,
                  pl.BlockSpec((3, 2, 2, BLOCK, 2 * BLOCK), lambda i, hp: (0, hp, 0, 0, 0))] + xspecs,
        out_specs=[blk, st_spec, st_spec],
        scratch_shapes=[pltpu.VMEM((3, seq, LANES), F32), pltpu.VMEM((3, seq, LANES), F32)]
        + [pltpu.VMEM((seq, LANES), F32)] * 3,
        input_output_aliases=_alias_map(n_in, k_alias, 3) if k_alias else {},
        compiler_params=_cparams(("parallel", "parallel")),
        name="attn_c_prompt",
    )(q, k, v, bias, *xargs)


def _gated_group_norm(y, z, ng):
    gated = y * _silu(z)
    parts = []
    for grp in range(2):
        gsl = gated[:, grp * 256:(grp + 1) * 256]
        ms = jnp.mean(gsl * gsl, axis=-1, keepdims=True)
        parts.append(gsl * lax.rsqrt(ms + EPS))
    return jnp.concatenate(parts, axis=1) * ng


def _expand_heads(v, ex):
    hi = v.astype(BF16)
    r1 = v - hi.astype(F32)
    mid = r1.astype(BF16)
    lo = (r1 - mid.astype(F32)).astype(BF16)
    return _dot(hi, ex) + _dot(mid, ex) + _dot(lo, ex)


def _ssd_prompt_kernel(xbc_ref, z_ref, dt_ref, dtb_ref, alog_ref, dsk_ref, ng_ref, tri_ref, ex_ref,
                       ob_ref, hl_ref, hst):
    c = pl.program_id(1)

    @pl.when(c == 0)
    def _():
        hst[...] = jnp.zeros_like(hst)

    xa_all = _silu(xbc_ref[...])
    dt_all = _softplus(dt_ref[...] + dtb_ref[...])
    a_all = dt_all * (-jnp.exp(alog_ref[...]))
    row = lax.broadcasted_iota(jnp.int32, (BLOCK, BLOCK), 0)
    lane = _lane_iota((BLOCK, BLOCK))
    causal = lane <= row
    lo = lane < HEAD_DIM
    top = row < HEAD_DIM
    n_pairs = N_SSM_HEADS // 2
    for sub in range(SSD_SUB):
        rs = slice(sub * BLOCK, (sub + 1) * BLOCK)
        xa, dt = xa_all[rs], dt_all[rs]
        xs = xa[:, 0:D_INNER]
        acum = jnp.dot(tri_ref[...], a_all[rs], precision=HIGHEST, preferred_element_type=F32)
        bms = [xa[:, D_INNER + g * D_STATE:D_INNER + (g + 1) * D_STATE].astype(BF16) for g in range(2)]
        cms = [xa[:, D_INNER + (2 + g) * D_STATE:D_INNER + (3 + g) * D_STATE].astype(BF16) for g in range(2)]
        cbs = [_dot_t(cms[g], bms[g]) for g in range(2)]
        hps = [hst[2 * j:2 * j + 2].reshape(BLOCK, D_STATE) for j in range(n_pairs)]
        y_off = [_dot_t(cms[j // 2], hps[j].astype(BF16)) for j in range(n_pairs)]
        last = acum[BLOCK - 1:BLOCK, :]
        e_last = jnp.exp(last)
        w_all = _expand_heads(jnp.exp(last - acum) * dt, ex_ref[...])
        e_all = _expand_heads(jnp.exp(acum), ex_ref[...])
        xs_pairs = [xs[:, j * LANES:(j + 1) * LANES] for j in range(n_pairs)]
        for j in range(n_pairs):
            h0_, h1_ = 2 * j, 2 * j + 1
            xw = xs_pairs[j] * w_all[:, j * LANES:(j + 1) * LANES]
            s_new = _dot_t0(xw.astype(BF16), bms[j // 2])
            cd_pair = jnp.where(top, e_last[:, h0_:h0_ + 1], e_last[:, h1_:h1_ + 1])
            hst[h0_:h0_ + 2] = (hps[j] * cd_pair + s_new).reshape(2, HEAD_DIM, D_STATE)
        acum_t = acum.T
        dt_t = dt.T
        y_parts = []
        for j in range(n_pairs):
            xs_b = xs_pairs[j].astype(BF16)
            yd = []
            for hh in range(2):
                h = 2 * j + hh
                seg = acum[:, h:h + 1] - acum_t[h:h + 1, :]
                dec = jnp.exp(jnp.where(causal, seg, NEG)) * dt_t[h:h + 1, :]
                yd.append(_dot((cbs[j // 2] * dec).astype(BF16), xs_b))
            y_parts.append(jnp.where(lo, yd[0], yd[1]) + y_off[j] * e_all[:, j * LANES:(j + 1) * LANES])
        y = jnp.concatenate(y_parts, axis=1) + dsk_ref[...] * xs
        ob_ref[rs, :] = _gated_group_norm(y, z_ref[rs, :], ng_ref[...])

    @pl.when(c == pl.num_programs(1) - 1)
    def _():
        hl_ref[...] = hst[...]


def _ssd_prompt(xbc, z, dt, dtb, alog, dsk, ng, tri, ex, l, depth, n, nc, bufs):
    t = xbc.shape[0]
    nc = nc // SSD_SUB
    rows = SSD_SUB * BLOCK
    row = lambda i, c: (i * nc + c, 0)
    layer = lambda i, c: (l, 0, 0)
    n_in = 9
    body, xspecs, xargs, k_alias = _stacked(_ssd_prompt_kernel, n_in, bufs)
    st = (N_SSM_HEADS, HEAD_DIM, D_STATE)
    return pl.pallas_call(
        body,
        grid=(n, nc),
        out_shape=[jax.ShapeDtypeStruct((t, D_INNER), F32),
                   jax.ShapeDtypeStruct((depth, n) + st, F32)],
        in_specs=[pl.BlockSpec((rows, CONV_DIM), row),
                  pl.BlockSpec((rows, D_INNER), row),
                  pl.BlockSpec((rows, LANES), row),
                  pl.BlockSpec((None, 1, LANES), layer),
                  pl.BlockSpec((None, 1, LANES), layer),
                  pl.BlockSpec((None, 1, D_INNER), layer),
                  pl.BlockSpec((None, 1, D_INNER), layer),
                  pl.BlockSpec((BLOCK, BLOCK), lambda i, c: (0, 0)),
                  pl.BlockSpec((LANES, D_INNER), lambda i, c: (0, 0))] + xspecs,
        out_specs=[pl.BlockSpec((rows, D_INNER), row),
                   pl.BlockSpec((None, None) + st, lambda i, c: (l, i, 0, 0, 0))],
        scratch_shapes=[pltpu.VMEM(st, F32)],
        input_output_aliases=_alias_map(n_in, k_alias, 2) if k_alias else {},
        compiler_params=_cparams(("parallel", "arbitrary")),
        name="ssd_prompt",
    )(xbc, z, dt, dtb, alog, dsk, ng, tri, ex, *xargs)


def _attn_a_sample_kernel(sink_ref, q_ref, kn_ref, vn_ref, kt_ref, vt_ref, bc_ref, bn_ref, o_ref, *, nbs, l):
    rb = nbs * SROWS
    lo = _lane_iota((rb, LANES)) < HEAD_DIM
    s0, s1 = q_ref[:, 0:LANES], q_ref[:, LANES:2 * LANES]
    per_seq = lambda v: v.reshape(nbs, SROWS, LANES)
    qm = jnp.concatenate([
        per_seq(jnp.where(lo, s0, 0.0)),
        per_seq(jnp.where(lo, pltpu.roll(s0, HEAD_DIM, 1), 0.0)),
        per_seq(jnp.where(lo, 0.0, pltpu.roll(s1, HEAD_DIM, 1))),
        per_seq(jnp.where(lo, 0.0, s1))], axis=1).astype(BF16)
    kt = kt_ref[...].reshape(nbs, LANES, WIN).astype(BF16)
    vt = vt_ref[...].reshape(nbs, LANES, WIN).astype(BF16)
    kn = kn_ref[...].reshape(nbs, SROWS, LANES).astype(BF16)
    vn = vn_ref[...].reshape(nbs, SROWS, LANES).astype(BF16)
    s_c = jnp.einsum('nqd,ndk->nqk', qm, kt, preferred_element_type=F32) + bc_ref[...]
    s_n = jnp.einsum('nqd,nkd->nqk', qm, kn, preferred_element_type=F32) + bn_ref[...]
    hrow = lax.broadcasted_iota(jnp.int32, (nbs, 4 * SROWS, 1), 1) // SROWS
    sink = jnp.where(hrow == 0, sink_ref[l, 0], jnp.where(hrow == 1, sink_ref[l, 1],
                     jnp.where(hrow == 2, sink_ref[l, 2], sink_ref[l, 3])))
    m = jnp.maximum(jnp.max(s_c, axis=-1, keepdims=True), jnp.max(s_n, axis=-1, keepdims=True))
    m = jnp.maximum(m, sink)
    p_c = jnp.exp2(s_c - m)
    p_n = jnp.exp2(s_n - m)
    den = jnp.sum(p_c, axis=-1, keepdims=True) + jnp.sum(p_n, axis=-1, keepdims=True) + jnp.exp2(sink - m)
    o = (jnp.einsum('nqk,ndk->nqd', p_c.astype(BF16), vt, preferred_element_type=F32)
         + jnp.einsum('nqk,nkd->nqd', p_n.astype(BF16), vn, preferred_element_type=F32)) / den
    o0, o1, o2, o3 = (o[:, i * SROWS:(i + 1) * SROWS, :].reshape(rb, LANES) for i in range(4))
    o_ref[:, 0:LANES] = jnp.where(lo, o0, pltpu.roll(o1, HEAD_DIM, 1))
    o_ref[:, LANES:2 * LANES] = jnp.where(lo, pltpu.roll(o2, HEAD_DIM, 1), o3)


def _attn_a_sample(q8, k8, v8, cache_kt, cache_vt, bias_c, bias_n, sinks, l, nbs):
    ns = cache_kt.shape[1]
    tok = lambda i: (i, 0)
    cache = pl.BlockSpec((None, nbs, 2, HEAD_DIM, WIN), lambda i: (l, i, 0, 0, 0))
    c3 = lambda i: (0, 0, 0)
    return pl.pallas_call(
        functools.partial(_attn_a_sample_kernel, nbs=nbs, l=l),
        grid=(ns // nbs,),
        out_shape=jax.ShapeDtypeStruct((ns * SROWS, 256), F32),
        in_specs=[pl.BlockSpec(memory_space=pltpu.SMEM),
                  pl.BlockSpec((nbs * SROWS, 256), tok),
                  pl.BlockSpec((nbs * SROWS, LANES), tok),
                  pl.BlockSpec((nbs * SROWS, LANES), tok),
                  cache, cache,
                  pl.BlockSpec((1, 4 * SROWS, WIN), c3),
                  pl.BlockSpec((1, 4 * SROWS, SROWS), c3)],
        out_specs=pl.BlockSpec((nbs * SROWS, 256), tok),
        compiler_params=_cparams(("parallel",)),
        name="attn_a_sample",
    )(sinks, q8, k8, v8, cache_kt, cache_vt, bias_c, bias_n)


def _softmax_parts(parts):
    m = None
    for s in parts:
        mm = jnp.max(s, axis=-1, keepdims=True)
        m = mm if m is None else jnp.maximum(m, mm)
    ps = [jnp.exp2(s - m) for s in parts]
    den = None
    for p in ps:
        dd = jnp.sum(p, axis=-1, keepdims=True)
        den = dd if den is None else den + dd
    return ps, den, m + jnp.log2(den)


def _attn_c_sample_kernel(q_ref, kn_ref, vn_ref, kt_ref, vt_ref, b12_ref, b3_ref, bn_ref, o_ref, *, nbs, lc):
    lo = _lane_iota((SROWS, LANES)) < HEAD_DIM
    pr = 2 * SROWS
    units = [(n, j, slice(n * SROWS, (n + 1) * SROWS), slice(j * LANES, (j + 1) * LANES))
             for n in range(nbs) for j in range(2)]
    scores = []
    for n, j, rows, lanes in units:
        qp = q_ref[rows, lanes]
        lhs = jnp.concatenate([jnp.where(lo, qp, 0.0), jnp.where(lo, 0.0, qp)], axis=0).astype(BF16)
        ktp = kt_ref[n, 2 * j:2 * j + 2].reshape(LANES, lc).astype(BF16)
        scores.append((_dot(lhs, ktp), _dot_t(lhs, kn_ref[rows, lanes].astype(BF16))))
    probs = []
    for (n, j, rows, lanes), (s, sn) in zip(units, scores):
        s_near = s[:, lc - NEAR:]
        (p1, p1n), d1, l1 = _softmax_parts([s_near + b12_ref[0, j], sn + bn_ref[0, j]])
        (p2, p2n), d2, l2 = _softmax_parts([s_near + b12_ref[1, j], sn + bn_ref[1, j]])
        (p3, p3n), d3, l3 = _softmax_parts([s + b3_ref[j], sn + bn_ref[2, j]])
        m = jnp.maximum(jnp.maximum(l1, l2), l3)
        w1, w2, w3 = jnp.exp2(l1 - m), jnp.exp2(l2 - m), jnp.exp2(l3 - m)
        wsum = w1 + w2 + w3
        probs.append((jnp.concatenate([p1, p2], axis=0).astype(BF16), p3.astype(BF16),
                      (p1n.astype(BF16), p2n.astype(BF16), p3n.astype(BF16)),
                      (w1 / (d1 * wsum), w2 / (d2 * wsum), w3 / (d3 * wsum))))
    for (n, j, rows, lanes), (p12, p3, pn, wts) in zip(units, probs):
        vtp = vt_ref[n, 2 * j:2 * j + 2].reshape(LANES, lc).astype(BF16)
        vnp = vn_ref[rows, lanes].astype(BF16)
        o12 = _dot_t(p12, vtp[:, lc - NEAR:])
        o = ((o12[0:pr] + _dot(pn[0], vnp)) * wts[0] + (o12[pr:2 * pr] + _dot(pn[1], vnp)) * wts[1]
             + (_dot_t(p3, vtp) + _dot(pn[2], vnp)) * wts[2])
        o_ref[rows, lanes] = jnp.where(lo, o[0:SROWS], o[SROWS:pr])


def _attn_c_sample(q8, k8, v8, cache_kt, cache_vt, b12, b3, bn, l, nbs):
    ns, lc = cache_kt.shape[1], cache_kt.shape[4]
    tok = lambda i: (i, 0)
    cache = pl.BlockSpec((None, nbs, 4, HEAD_DIM, lc), lambda i: (l, i, 0, 0, 0))
    c3 = lambda i: (0, 0, 0)
    c4 = lambda i: (0, 0, 0, 0)
    return pl.pallas_call(
        functools.partial(_attn_c_sample_kernel, nbs=nbs, lc=lc),
        grid=(ns // nbs,),
        out_shape=jax.ShapeDtypeStruct((ns * SROWS, 256), F32),
        in_specs=[pl.BlockSpec((nbs * SROWS, 256), tok),
                  pl.BlockSpec((nbs * SROWS, 256), tok),
                  pl.BlockSpec((nbs * SROWS, 256), tok),
                  cache, cache,
                  pl.BlockSpec((2, 2, 2 * SROWS, NEAR), c4),
                  pl.BlockSpec((2, 2 * SROWS, lc), c3),
                  pl.BlockSpec((3, 2, 2 * SROWS, SROWS), c4)],
        out_specs=pl.BlockSpec((nbs * SROWS, 256), tok),
        compiler_params=_cparams(("parallel",)),
        name="attn_c_sample",
    )(q8, k8, v8, cache_kt, cache_vt, b12, b3, bn)


def _ssd_sample_kernel(xbc_ref, prev_ref, dt_ref, z_ref, h0_ref, cw_ref, cb_ref, dtb_ref, alog_ref, dsk_ref,
                       ex_ref, ng_ref, ob_ref, hn_ref, *, nbs):
    r = xbc_ref.shape[0]
    rid = lax.broadcasted_iota(jnp.int32, (r, 1), 0) % SROWS
    x = jnp.where(jnp.logical_and(rid >= 1, rid < SVALID), prev_ref[...], xbc_ref[...])
    y = cb_ref[...] + cw_ref[3:4, :] * x
    for d in range(1, 4):
        y = y + cw_ref[3 - d:4 - d, :] * pltpu.roll(x, d, 0)
    xa = _silu(y)
    xs = xa[:, 0:D_INNER]
    bm = xa[:, D_INNER:D_INNER + 2 * D_STATE]
    cm = xa[:, D_INNER + 2 * D_STATE:CONV_DIM]
    valid = rid >= SVALID
    dt = jnp.where(valid, _softplus(dt_ref[...] + dtb_ref[...]), 0.0)
    a = dt * (-jnp.exp(alog_ref[...]))
    acum = a
    rem = jnp.zeros_like(a)
    for d in range(1, 4):
        acum = acum + jnp.where(rid - d >= SVALID, pltpu.roll(a, d, 0), 0.0)
        rem = rem + jnp.where(rid + d < SROWS, pltpu.roll(a, r - d, 0), 0.0)
    ex = ex_ref[...]

    def expand(v):
        return _expand_heads(v, ex)

    lane = _lane_iota((r, LANES))
    y_acc = dsk_ref[...] * xs
    for d in range(4):
        ok = rid - d >= SVALID
        bsh = bm if d == 0 else pltpu.roll(bm, d, 0)
        cb0 = jnp.sum(cm[:, 0:D_STATE] * bsh[:, 0:D_STATE], axis=-1, keepdims=True)
        cb1 = jnp.sum(cm[:, D_STATE:] * bsh[:, D_STATE:], axis=-1, keepdims=True)
        cbh = jnp.where(lane < N_SSM_HEADS // 2, cb0, cb1)
        if d == 0:
            coef = cbh * dt
            xsh = xs
        else:
            dec = jnp.exp(jnp.where(ok, acum - pltpu.roll(acum, d, 0), NEG))
            coef = cbh * dec * pltpu.roll(dt, d, 0)
            xsh = pltpu.roll(xs, d, 0)
        y_acc = y_acc + expand(jnp.where(ok, coef, 0.0)) * xsh
    eac = expand(jnp.exp(acum))
    xw = (xs * expand(dt * jnp.exp(rem))).astype(BF16)
    cd = jnp.exp(acum + rem)
    bm, cm = bm.astype(BF16), cm.astype(BF16)

    yo = []
    for n in range(nbs):
        rows = slice(n * SROWS, (n + 1) * SROWS)
        last = (n + 1) * SROWS - 1
        parts = []
        for g in range(2):
            hp = h0_ref[n, 4 * g:4 * g + 4].reshape(4 * HEAD_DIM, D_STATE)
            parts.append(_dot_t(cm[rows, g * D_STATE:(g + 1) * D_STATE], hp.astype(BF16)))
            s_new = _dot_t0(xw[rows, g * 256:(g + 1) * 256], bm[rows, g * D_STATE:(g + 1) * D_STATE])
            for hh in range(4):
                h = 4 * g + hh
                hn_ref[n, h] = (hp[hh * HEAD_DIM:(hh + 1) * HEAD_DIM, :] * cd[last:last + 1, h:h + 1]
                                + s_new[hh * HEAD_DIM:(hh + 1) * HEAD_DIM, :])
        yo.append(jnp.concatenate(parts, axis=1))
    y_all = y_acc + eac * jnp.concatenate(yo, axis=0)
    ob_ref[...] = _gated_group_norm(y_all, z_ref[...], ng_ref[...])


def _ssd_sample(xbc8, prev8, dt8, z8, state, cw8, cb, dtb, alog, dsk, ex, ng, l, depth, nbs, bufs):
    ns = state.shape[1]
    rb = nbs * SROWS
    row = lambda i: (i, 0)
    layer = lambda i: (l, 0, 0)
    st_spec = pl.BlockSpec((None, nbs, N_SSM_HEADS, HEAD_DIM, D_STATE), lambda i: (l, i, 0, 0, 0))
    n_in = 12
    body, xspecs, xargs, k_alias = _stacked(functools.partial(_ssd_sample_kernel, nbs=nbs), n_in, bufs)
    return pl.pallas_call(
        body,
        grid=(ns // nbs,),
        out_shape=[jax.ShapeDtypeStruct((ns * SROWS, D_INNER), F32),
                   jax.ShapeDtypeStruct(state.shape, F32)],
        in_specs=[pl.BlockSpec((rb, CONV_DIM), row),
                  pl.BlockSpec((None, rb, CONV_DIM), lambda i: (l, i, 0)),
                  pl.BlockSpec((rb, LANES), row),
                  pl.BlockSpec((rb, D_INNER), row),
                  st_spec,
                  pl.BlockSpec((None, SUBLANES, CONV_DIM), layer),
                  pl.BlockSpec((None, 1, CONV_DIM), layer),
                  pl.BlockSpec((None, 1, LANES), layer),
                  pl.BlockSpec((None, 1, LANES), layer),
                  pl.BlockSpec((None, 1, D_INNER), layer),
                  pl.BlockSpec((LANES, D_INNER), lambda i: (0, 0)),
                  pl.BlockSpec((None, 1, D_INNER), layer)] + xspecs,
        out_specs=[pl.BlockSpec((rb, D_INNER), row), st_spec],
        input_output_aliases=_alias_map(n_in, k_alias, 2) if k_alias else {},
        compiler_params=_cparams(("parallel",)),
        name="ssd_sample",
    )(xbc8, prev8, dt8, z8, state, cw8, cb, dtb, alog, dsk, ex, ng, *xargs)


def _new_kv_kernel(ka_ref, va_ref, kc_ref, vc_ref, oka_ref, ova_ref, okc_ref, ovc_ref, *, ns):
    def emit(src, dst):
        for t in range(SROWS - SVALID):
            dst[t] = src[pl.ds(SVALID + t, ns, stride=SROWS), :].T

    emit(kc_ref, okc_ref)
    emit(vc_ref, ovc_ref)

    @pl.when(pl.program_id(1) == 0)
    def _():
        emit(ka_ref, oka_ref)
        emit(va_ref, ova_ref)


def _new_kv_states(ka, va, kc, vc, ns):
    depth = ka.shape[0]
    nt = SROWS - SVALID
    narrow_in = pl.BlockSpec((None, ns * SROWS, LANES), lambda l, j: (l, 0, 0))
    wide_in = pl.BlockSpec((None, ns * SROWS, LANES), lambda l, j: (l, 0, j))
    narrow_out = pl.BlockSpec((None, nt, LANES, ns), lambda l, j: (l, 0, 0, 0))
    wide_out = pl.BlockSpec((None, nt, LANES, ns), lambda l, j: (l, 0, j, 0))
    shape = lambda w: jax.ShapeDtypeStruct((depth, nt, w, ns), F32)
    return pl.pallas_call(
        functools.partial(_new_kv_kernel, ns=ns),
        grid=(depth, 2),
        out_shape=[shape(LANES), shape(LANES), shape(2 * LANES), shape(2 * LANES)],
        in_specs=[narrow_in, narrow_in, wide_in, wide_in],
        out_specs=[narrow_out, narrow_out, wide_out, wide_out],
        compiler_params=_cparams(("parallel", "arbitrary")),
        name="new_kv_states",
    )(ka, va, kc, vc)


def _prompt_bias_item(dil, col0):
    qi = np.arange(BLOCK)[:, None]
    kj = np.arange(2 * BLOCK)[None, :]
    dist = BLOCK + qi - kj
    ok = (dist >= 0) & (dist <= WIN)
    variants = [_masked_bucket(dist * dil, ok), _masked_bucket(dist * dil, ok & (kj >= BLOCK))]
    segs = [((2 * h + v) * BLOCK, (2 * h + v + 1) * BLOCK, col0 + h) for h in range(4) for v in range(2)]
    return np.concatenate(variants * 4, axis=0), segs


def _sample_bias_item(dist_of, ok_of, cols, col0):
    t = np.arange(SROWS)[:, None] - SVALID
    bkt = np.where(t >= 0, _masked_bucket(dist_of(t, cols), ok_of(t, cols)), 0).astype(np.int32)
    segs = [(h * SROWS, (h + 1) * SROWS, col0 + h) for h in range(4)]
    return np.concatenate([bkt] * 4, axis=0), segs


def _cache_bias_item(length, window, dil, span, col0):
    dist_of = lambda t, pos: length + t - pos
    ok_of = lambda t, pos: (dist_of(t, pos) >= 0) & (dist_of(t, pos) <= window) & (dist_of(t, pos) % dil == 0)
    return _sample_bias_item(dist_of, ok_of, np.arange(length - span, length)[None, :], col0)


def _new_rows_bias_item(same_token_only, col0):
    dist_of = lambda t, r: t - (r - SVALID)
    if same_token_only:
        ok_of = lambda t, r: (r >= SVALID) & (dist_of(t, r) == 0)
    else:
        ok_of = lambda t, r: (r >= SVALID) & (dist_of(t, r) >= 0)
    return _sample_bias_item(dist_of, ok_of, np.arange(SROWS)[None, :], col0)


Tiling = collections.namedtuple("Tiling", "tm_prompt tm_sample seqs_a seqs_c seqs_b")


def _tiling(ns):
    rows = ns * SROWS
    return Tiling(tm_prompt=512, tm_sample=min(512, rows), seqs_a=min(16, ns), seqs_c=min(4, ns),
                  seqs_b=min(16, ns))
def kernel(x_prompt, x_sample, cache_a_k, cache_a_v, cache_c_k, cache_c_v, state_ssm, state_conv,
           norm_mix_g, w_in, a_q_norm_g, a_k_norm_g, a_sinks, c_q_norm_g, c_k_norm_g, rel_bias,
           conv_w, conv_b, dt_bias, a_log, d_skip, ssm_norm_g, w_out, norm_mlp_g, w_up, w_down):
    depth = w_in.shape[0]
    n, seq, _ = x_prompt.shape
    ns, ts, _ = x_sample.shape
    lc = cache_c_k.shape[2]
    assert ts == SROWS - SVALID and seq % (BLOCK * 16) == 0
    assert cache_a_k.shape[2] == WIN and all(w <= lc for w, _ in C_BRANCHES)
    nb = seq // BLOCK

    w_in_b = w_in.astype(BF16)
    w_out_b, w_up_b, w_down_b = w_out.astype(BF16), w_up.astype(BF16), w_down.astype(BF16)
    ones = jnp.ones((depth, 128), F32)
    q_scale = ATTN_SCALE * LOG2E
    gain = jnp.concatenate([jnp.tile(a_q_norm_g, (1, 4)) * q_scale, jnp.tile(a_k_norm_g, (1, 2)), ones,
                            jnp.tile(c_q_norm_g, (1, 4)) * q_scale, jnp.tile(c_k_norm_g, (1, 4))], axis=1)
    rel_bias = rel_bias * LOG2E
    a_sinks = a_sinks * LOG2E
    tri = (jnp.arange(BLOCK)[None, :] <= jnp.arange(BLOCK)[:, None]).astype(F32)
    ex = (jnp.arange(LANES)[:, None] == (jnp.arange(D_INNER) // HEAD_DIM)[None, :]).astype(BF16)
    cw8 = jnp.pad(conv_w, ((0, 0), (0, SUBLANES - conv_w.shape[1]), (0, 0)))
    pad_h = lambda v: jnp.pad(v, ((0, 0), (0, LANES - N_SSM_HEADS)))[:, None, :]
    vec = lambda v: v[:, None, :]
    dtb, alog = pad_h(dt_bias), pad_h(a_log)
    dsk = vec(jnp.repeat(d_skip, HEAD_DIM, axis=1))
    g_mix, g_mlp, gain, cb, ng = vec(norm_mix_g), vec(norm_mlp_g), vec(gain), vec(conv_b), vec(ssm_norm_g)

    (w1, d1), (w2, d2), (w3, d3) = C_BRANCHES
    items = ([_prompt_bias_item(1, 0)] + [_prompt_bias_item(d, 4) for d in C_DILS]
             + [_cache_bias_item(WIN, WIN, 1, WIN, 0), _new_rows_bias_item(False, 0)]
             + [_cache_bias_item(lc, w1, d1, NEAR, 4), _cache_bias_item(lc, w2, d2, NEAR, 4),
                _cache_bias_item(lc, w3, d3, lc, 4), _new_rows_bias_item(False, 4), _new_rows_bias_item(True, 4)])
    pa, pc1, pc4, pc16, sa_c, sa_n, sc1, sc2, sc3, sn_all, sn_same = _expand_biases(rel_bias, items)
    by_variant = lambda b: b.reshape(4, 2, BLOCK, 2 * BLOCK)
    by_pair = lambda b: b.reshape(2, 2 * SROWS, b.shape[-1])
    bias_a = by_variant(pa)
    bias_c = jnp.stack([by_variant(pc1), by_variant(pc4), by_variant(pc16)])
    sbias_a_c, sbias_a_n = sa_c[None], sa_n[None]
    sb12 = jnp.stack([by_pair(sc1), by_pair(sc2)])
    sb3 = by_pair(sc3)
    sbn = jnp.stack([by_pair(sn_all), by_pair(sn_same), by_pair(sn_same)])

    cak_t, cav_t = cache_a_k.transpose(0, 1, 3, 4, 2), cache_a_v.transpose(0, 1, 3, 4, 2)
    cck_t, ccv_t = cache_c_k.transpose(0, 1, 3, 4, 2), cache_c_v.transpose(0, 1, 3, 4, 2)

    hp = x_prompt.reshape(n * seq, D_MODEL)
    hs = jnp.pad(x_sample, ((0, 0), (SVALID, 0), (0, 0))).reshape(ns * SROWS, D_MODEL)
    tl = _tiling(ns)
    prev8 = jnp.pad(state_conv, ((0, 0), (0, 0), (1, SROWS - 4), (0, 0))).reshape(depth, ns * SROWS, CONV_DIM)

    p_ak = p_ck = p_ssm = s_ssm = None
    p_conv, s_small = [], [[] for _ in range(5)]
    for l in range(depth):
        qa, ka, va, qc, kc, vc, z, xbc, dt, tail = _in_proj(hp, g_mix, w_in_b, gain, cw8, cb, l, tl.tm_prompt,
                                                            seq // tl.tm_prompt)
        oa_p, *p_ak = _attn_a_prompt(qa, ka, va, bias_a, a_sinks, l, depth, n, seq, p_ak)
        oc_p, *p_ck = _attn_c_prompt(qc, kc, vc, bias_c, l, depth, n, seq, p_ck)
        ob_p, *p_ssm = _ssd_prompt(xbc, z, dt, dtb, alog, dsk, ng, tri, ex, l, depth, n, nb, p_ssm)
        p_conv.append(tail)

        qa, ka, va, qc, kc, vc, z, xbc, dt, _ = _in_proj(hs, g_mix, w_in_b, gain, cw8, cb, l, tl.tm_sample, 0)
        oa = _attn_a_sample(qa, ka, va, cak_t, cav_t, sbias_a_c, sbias_a_n, a_sinks, l, tl.seqs_a)
        oc = _attn_c_sample(qc, kc, vc, cck_t, ccv_t, sb12, sb3, sbn, l, tl.seqs_c)
        ob, *s_ssm = _ssd_sample(xbc, prev8, dt, z, state_ssm, cw8, cb, dtb, alog, dsk, ex, ng, l, depth,
                                 tl.seqs_b, s_ssm)
        for dst, val in zip(s_small, (ka, va, kc, vc, xbc)):
            dst.append(val)

        hp, hs = _out_mlp(hp, oa_p, ob_p, oc_p, hs, oa, ob, oc, w_out_b, g_mlp, w_up_b, w_down_b, l,
                          math.gcd(tl.tm_prompt, ns * SROWS))

    y_prompt = hp.reshape(n, seq, D_MODEL)
    y_sample = hs.reshape(ns, SROWS, D_MODEL)[:, SVALID:]
    unminor = lambda buf, heads: buf.reshape(depth, n, heads, HEAD_DIM, -1).transpose(0, 1, 4, 2, 3)
    p_state = (unminor(p_ak[0], 2), unminor(p_ak[1], 2), unminor(p_ck[0], 4), unminor(p_ck[1], 4),
               p_ssm[0], jnp.stack(p_conv)[:, :, SUBLANES - 3:])
    new_kv = _new_kv_states(*(jnp.stack(v) for v in s_small[:4]), ns)
    s_kv = tuple(b.reshape(depth, SROWS - SVALID, hh, HEAD_DIM, ns).transpose(0, 4, 1, 2, 3)
                 for b, hh in zip(new_kv, (2, 2, 4, 4)))
    s_conv = jnp.stack([x.reshape(ns, SROWS, CONV_DIM)[:, SROWS - 3:] for x in s_small[4]])
    s_state = s_kv + (s_ssm[0], s_conv)
    return (y_prompt, y_sample) + p_state + s_state
```
